```python
import math
import jax, jax.numpy as jnp
from jax import lax
import numpy as np

D_MODEL = 1024
BATCH = 8
SEQ = 16384
DEPTH = 4

SSM_WIDTH = D_MODEL // 2
SSM_GROUP = 16
SSM_GROUPS = SSM_WIDTH // SSM_GROUP
SSM_STATE = 64
DT_MIN = 1e-3
DT_MAX = 1e-1
N_HEADS = 8
QK_NOPE = 64
QK_ROPE = 32
QK_DIM = QK_NOPE + QK_ROPE
V_DIM = 64
Q_LORA = 256
KV_LORA = 128
ROPE_BASE = 10000.0
Q_BLOCK = 128
D_FF = -(-8 * D_MODEL // (3 * 256)) * 256
N_MOD = 6
EPS = 1e-6
IN_SIZES = (SSM_WIDTH, Q_LORA, KV_LORA, QK_ROPE, D_MODEL, D_MODEL)
IN_COLS = SSM_WIDTH + Q_LORA + KV_LORA + QK_ROPE + 2 * D_MODEL

kernel_name = "hybrid_s5_mla_adaln_trunk"


def rms_norm(x, g):
    xf = x.astype(jnp.float32)
    y = xf * lax.rsqrt(jnp.mean(xf * xf, axis=-1, keepdims=True) + EPS)
    return (y * g.astype(jnp.float32)).astype(x.dtype)


def rope(x, cos, sin):
    x1, x2 = jnp.split(x, 2, axis=-1)
    return jnp.concatenate([x1 * cos - x2 * sin, x2 * cos + x1 * sin], axis=-1)


def _ssm_combine(left, right):
    a1r, a1i, b1r, b1i = left
    a2r, a2i, b2r, b2i = right
    return (a2r * a1r - a2i * a1i,
            a2r * a1i + a2i * a1r,
            a2r * b1r - a2i * b1i + b2r,
            a2r * b1i + a2i * b1r + b2i)


def s5_mixer(u, a_re, a_im, log_dt, b_re, b_im, c_re, c_im, d_skip, w_glu, b_glu):
    f32 = jnp.float32
    bsz, L, _ = u.shape
    dt = jnp.exp(log_dt.astype(f32))[:, None]
    ar = a_re.astype(f32)
    ai = a_im.astype(f32)
    mag = jnp.exp(ar * dt)
    abar_re = mag * jnp.cos(ai * dt)
    abar_im = mag * jnp.sin(ai * dt)
    den = ar * ar + ai * ai
    nr = abar_re - 1.0
    ni = abar_im
    coef_re = ((nr * ar + ni * ai) / den)[..., None]
    coef_im = ((ni * ar - nr * ai) / den)[..., None]
    br = b_re.astype(f32)
    bi = b_im.astype(f32)
    bbar_re = coef_re * br - coef_im * bi
    bbar_im = coef_re * bi + coef_im * br
    ug = u.reshape(bsz, L, SSM_GROUPS, SSM_GROUP).astype(f32)
    bu_re = jnp.einsum('blgm,gpm->blgp', ug, bbar_re)
    bu_im = jnp.einsum('blgm,gpm->blgp', ug, bbar_im)
    a_seq_re = jnp.broadcast_to(abar_re[None, None], (1, L, SSM_GROUPS, SSM_STATE))
    a_seq_im = jnp.broadcast_to(abar_im[None, None], (1, L, SSM_GROUPS, SSM_STATE))
    _, _, h_re, h_im = lax.associative_scan(
        _ssm_combine, (a_seq_re, a_seq_im, bu_re, bu_im), axis=1)
    y = (jnp.einsum('blgp,gmp->blgm', h_re, c_re.astype(f32))
         - jnp.einsum('blgp,gmp->blgm', h_im, c_im.astype(f32))
         + d_skip.astype(f32).reshape(SSM_GROUPS, SSM_GROUP) * ug)
    y = jax.nn.gelu(y.reshape(bsz, L, SSM_WIDTH)).astype(u.dtype)
    return y * jax.nn.sigmoid(y @ w_glu + b_glu)


def mla_mixer(cq_in, ckv_in, kr_in, positions, cos, sin, q_norm_g, w_uq, kv_norm_g, w_uk, w_uv):
    bsz, L, _ = cq_in.shape
    cq = rms_norm(cq_in, q_norm_g)
    q = (cq @ w_uq).reshape(bsz, L, N_HEADS, QK_DIM)
    q = jnp.concatenate([q[..., :QK_NOPE],
                         rope(q[..., QK_NOPE:], cos[:, :, None], sin[:, :, None])], axis=-1)
    ckv = rms_norm(ckv_in, kv_norm_g)
    k_nope = (ckv @ w_uk).reshape(bsz, L, N_HEADS, QK_NOPE)
    v = (ckv @ w_uv).reshape(bsz, L, N_HEADS, V_DIM)
    k_pe = rope(kr_in[:, :, None, :], cos[:, :, None], sin[:, :, None])
    k = jnp.concatenate([k_nope, jnp.broadcast_to(k_pe, (bsz, L, N_HEADS, QK_ROPE))], axis=-1)
    scale = QK_DIM ** -0.5
    nb = L // Q_BLOCK
    qb = q.reshape(bsz, nb, Q_BLOCK, N_HEADS, QK_DIM).transpose(1, 0, 2, 3, 4)
    pb = positions.reshape(bsz, nb, Q_BLOCK).transpose(1, 0, 2)
    neg = jnp.finfo(jnp.float32).min

    def attend_block(args):
        qi, pi = args
        s = jnp.einsum('bqhd,bkhd->bhqk', qi, k).astype(jnp.float32) * scale
        mask = positions[:, None, None, :] <= pi[:, None, :, None]
        p = jax.nn.softmax(jnp.where(mask, s, neg), axis=-1).astype(v.dtype)
        return jnp.einsum('bhqk,bkhd->bqhd', p, v)

    o = lax.map(attend_block, (qb, pb))
    return o.transpose(1, 0, 2, 3, 4).reshape(bsz, L, N_HEADS * V_DIM)


def _fwd_setup_inputs(seed: int = 0) -> dict:
    key = jax.random.key(seed)
    ks = jax.random.split(key, 32)
    nrm = jax.random.normal
    Lr = DEPTH
    G, P, M = SSM_GROUPS, SSM_STATE, SSM_GROUP
    x = nrm(ks[0], (BATCH, SEQ, D_MODEL), jnp.float32)
    c = nrm(ks[1], (BATCH, D_MODEL), jnp.float32)
    offset = jax.random.randint(ks[2], (BATCH, 1), 0, 4096, dtype=jnp.int32)
    positions = offset + jnp.arange(SEQ, dtype=jnp.int32)[None, :]
    n_idx = jnp.arange(P, dtype=jnp.float32)
    inv2 = 0.5 ** 0.5
    return {
        "x": x,
        "c": c,
        "positions": positions,
        "w_ada": nrm(ks[3], (Lr, D_MODEL, N_MOD * D_MODEL)) * (0.5 * D_MODEL ** -0.5),
        "b_ada": nrm(ks[4], (Lr, N_MOD * D_MODEL)) * 0.01,
        "norm1_g": 1.0 + 0.01 * nrm(ks[5], (Lr, D_MODEL)),
        "w_in": nrm(ks[6], (Lr, D_MODEL, IN_COLS)) * D_MODEL ** -0.5,
        "ssm_a_re": -0.5 * jnp.exp(0.02 * nrm(ks[7], (Lr, G, P))),
        "ssm_a_im": math.pi * n_idx[None, None, :] + 0.01 * nrm(ks[8], (Lr, G, P)),
        "ssm_log_dt": jax.random.uniform(ks[9], (Lr, G), jnp.float32, math.log(DT_MIN), math.log(DT_MAX)),
        "ssm_b_re": nrm(ks[10], (Lr, G, P, M)) * (inv2 * M ** -0.5),
        "ssm_b_im": nrm(ks[11], (Lr, G, P, M)) * (inv2 * M ** -0.5),
        "ssm_c_re": nrm(ks[12], (Lr, G, M, P)) * (inv2 * P ** -0.5),
        "ssm_c_im": nrm(ks[13], (Lr, G, M, P)) * (inv2 * P ** -0.5),
        "ssm_d": nrm(ks[14], (Lr, SSM_WIDTH)),
        "w_glu": nrm(ks[15], (Lr, SSM_WIDTH, SSM_WIDTH)) * SSM_WIDTH ** -0.5,
        "b_glu": nrm(ks[16], (Lr, SSM_WIDTH)) * 0.01,
        "w_a_out": nrm(ks[17], (Lr, SSM_WIDTH, D_MODEL)) * SSM_WIDTH ** -0.5,
        "q_norm_g": 1.0 + 0.01 * nrm(ks[18], (Lr, Q_LORA)),
        "w_uq": nrm(ks[19], (Lr, Q_LORA, N_HEADS * QK_DIM)) * Q_LORA ** -0.5,
        "kv_norm_g": 1.0 + 0.01 * nrm(ks[20], (Lr, KV_LORA)),
        "w_uk": nrm(ks[21], (Lr, KV_LORA, N_HEADS * QK_NOPE)) * KV_LORA ** -0.5,
        "w_uv": nrm(ks[22], (Lr, KV_LORA, N_HEADS * V_DIM)) * KV_LORA ** -0.5,
        "w_b_out": nrm(ks[23], (Lr, N_HEADS * V_DIM, D_MODEL)) * (N_HEADS * V_DIM) ** -0.5,
        "w_out": nrm(ks[24], (Lr, D_MODEL, D_MODEL)) * D_MODEL ** -0.5,
        "norm2_g": 1.0 + 0.01 * nrm(ks[25], (Lr, D_MODEL)),
        "w_gate": nrm(ks[26], (Lr, D_MODEL, D_FF)) * D_MODEL ** -0.5,
        "w_up": nrm(ks[27], (Lr, D_MODEL, D_FF)) * D_MODEL ** -0.5,
        "w_down": nrm(ks[28], (Lr, D_FF, D_MODEL)) * D_FF ** -0.5,
        "final_g": 1.0 + 0.01 * nrm(ks[29], (D_MODEL,)),
    }


def _fwd_reference(x, c, positions, w_ada, b_ada, norm1_g, w_in, ssm_a_re, ssm_a_im, ssm_log_dt,
              ssm_b_re, ssm_b_im, ssm_c_re, ssm_c_im, ssm_d, w_glu, b_glu, w_a_out,
              q_norm_g, w_uq, kv_norm_g, w_uk, w_uv, w_b_out, w_out, norm2_g,
              w_gate, w_up, w_down, final_g):
    inv_freq = ROPE_BASE ** (-jnp.arange(0, QK_ROPE, 2, dtype=jnp.float32) / QK_ROPE)
    ang = positions.astype(jnp.float32)[..., None] * inv_freq
    cos = jnp.cos(ang).astype(x.dtype)
    sin = jnp.sin(ang).astype(x.dtype)
    split_pts = [int(s) for s in np.cumsum(IN_SIZES)[:-1]]
    c_act = jax.nn.silu(c)
    for l in range(DEPTH):
        mod = (c_act @ w_ada[l] + b_ada[l])[:, None, :]
        sh1, sc1, g1, sh2, sc2, g2 = jnp.split(mod, N_MOD, axis=-1)
        h = rms_norm(x, norm1_g[l]) * (1.0 + sc1) + sh1
        z = h @ w_in[l]
        u, cq_in, ckv_in, kr_in, gate_a, gate_b = jnp.split(z, split_pts, axis=-1)
        y_a = s5_mixer(u, ssm_a_re[l], ssm_a_im[l], ssm_log_dt[l], ssm_b_re[l], ssm_b_im[l],
                       ssm_c_re[l], ssm_c_im[l], ssm_d[l], w_glu[l], b_glu[l]) @ w_a_out[l]
        y_b = mla_mixer(cq_in, ckv_in, kr_in, positions, cos, sin, q_norm_g[l], w_uq[l],
                        kv_norm_g[l], w_uk[l], w_uv[l]) @ w_b_out[l]
        merged = jax.nn.sigmoid(gate_a) * y_a + jax.nn.sigmoid(gate_b) * y_b
        x = x + g1 * (merged @ w_out[l])
        h = rms_norm(x, norm2_g[l]) * (1.0 + sc2) + sh2
        x = x + g2 * ((jax.nn.silu(h @ w_gate[l]) * (h @ w_up[l])) @ w_down[l])
    return rms_norm(x, final_g)


import jax as _jax
import jax.numpy as _jnp

TWIN_FORMAT = 'train_step'
FWD_PARAMS = ['x', 'c', 'positions', 'w_ada', 'b_ada', 'norm1_g', 'w_in', 'ssm_a_re', 'ssm_a_im', 'ssm_log_dt', 'ssm_b_re', 'ssm_b_im', 'ssm_c_re', 'ssm_c_im', 'ssm_d', 'w_glu', 'b_glu', 'w_a_out', 'q_norm_g', 'w_uq', 'kv_norm_g', 'w_uk', 'w_uv', 'w_b_out', 'w_out', 'norm2_g', 'w_gate', 'w_up', 'w_down', 'final_g']
TWIN_WEIGHTS = ['w_ada', 'b_ada', 'norm1_g', 'w_in', 'ssm_a_re', 'ssm_a_im', 'ssm_log_dt', 'ssm_b_re', 'ssm_b_im', 'ssm_c_re', 'ssm_c_im', 'ssm_d', 'w_glu', 'b_glu', 'w_a_out', 'q_norm_g', 'w_uq', 'kv_norm_g', 'w_uk', 'w_uv', 'w_b_out', 'w_out', 'norm2_g', 'w_gate', 'w_up', 'w_down', 'final_g']
TWIN_DIFF_INPUT = 'x'
TWIN_INPUTS = ['x', 'c', 'positions', 'w_ada', 'b_ada', 'norm1_g', 'w_in', 'ssm_a_re', 'ssm_a_im', 'ssm_log_dt', 'ssm_b_re', 'ssm_b_im', 'ssm_c_re', 'ssm_c_im', 'ssm_d', 'w_glu', 'b_glu', 'w_a_out', 'q_norm_g', 'w_uq', 'kv_norm_g', 'w_uk', 'w_uv', 'w_b_out', 'w_out', 'norm2_g', 'w_gate', 'w_up', 'w_down', 'final_g', 'loss_target', 'm_w_ada', 'm_b_ada', 'm_norm1_g', 'm_w_in', 'm_ssm_a_re', 'm_ssm_a_im', 'm_ssm_log_dt', 'm_ssm_b_re', 'm_ssm_b_im', 'm_ssm_c_re', 'm_ssm_c_im', 'm_ssm_d', 'm_w_glu', 'm_b_glu', 'm_w_a_out', 'm_q_norm_g', 'm_w_uq', 'm_kv_norm_g', 'm_w_uk', 'm_w_uv', 'm_w_b_out', 'm_w_out', 'm_norm2_g', 'm_w_gate', 'm_w_up', 'm_w_down', 'm_final_g', 'v_w_ada', 'v_b_ada', 'v_norm1_g', 'v_w_in', 'v_ssm_a_re', 'v_ssm_a_im', 'v_ssm_log_dt', 'v_ssm_b_re', 'v_ssm_b_im', 'v_ssm_c_re', 'v_ssm_c_im', 'v_ssm_d', 'v_w_glu', 'v_b_glu', 'v_w_a_out', 'v_q_norm_g', 'v_w_uq', 'v_kv_norm_g', 'v_w_uk', 'v_w_uv', 'v_w_b_out', 'v_w_out', 'v_norm2_g', 'v_w_gate', 'v_w_up', 'v_w_down', 'v_final_g']
TWIN_OUTPUTS = ['loss', 'grad_x', 'grad_w_ada', 'grad_b_ada', 'grad_norm1_g', 'grad_w_in', 'grad_ssm_a_re', 'grad_ssm_a_im', 'grad_ssm_log_dt', 'grad_ssm_b_re', 'grad_ssm_b_im', 'grad_ssm_c_re', 'grad_ssm_c_im', 'grad_ssm_d', 'grad_w_glu', 'grad_b_glu', 'grad_w_a_out', 'grad_q_norm_g', 'grad_w_uq', 'grad_kv_norm_g', 'grad_w_uk', 'grad_w_uv', 'grad_w_b_out', 'grad_w_out', 'grad_norm2_g', 'grad_w_gate', 'grad_w_up', 'grad_w_down', 'grad_final_g', 'delta_w_ada', 'delta_b_ada', 'delta_norm1_g', 'delta_w_in', 'delta_ssm_a_re', 'delta_ssm_a_im', 'delta_ssm_log_dt', 'delta_ssm_b_re', 'delta_ssm_b_im', 'delta_ssm_c_re', 'delta_ssm_c_im', 'delta_ssm_d', 'delta_w_glu', 'delta_b_glu', 'delta_w_a_out', 'delta_q_norm_g', 'delta_w_uq', 'delta_kv_norm_g', 'delta_w_uk', 'delta_w_uv', 'delta_w_b_out', 'delta_w_out', 'delta_norm2_g', 'delta_w_gate', 'delta_w_up', 'delta_w_down', 'delta_final_g', 'new_m_w_ada', 'new_m_b_ada', 'new_m_norm1_g', 'new_m_w_in', 'new_m_ssm_a_re', 'new_m_ssm_a_im', 'new_m_ssm_log_dt', 'new_m_ssm_b_re', 'new_m_ssm_b_im', 'new_m_ssm_c_re', 'new_m_ssm_c_im', 'new_m_ssm_d', 'new_m_w_glu', 'new_m_b_glu', 'new_m_w_a_out', 'new_m_q_norm_g', 'new_m_w_uq', 'new_m_kv_norm_g', 'new_m_w_uk', 'new_m_w_uv', 'new_m_w_b_out', 'new_m_w_out', 'new_m_norm2_g', 'new_m_w_gate', 'new_m_w_up', 'new_m_w_down', 'new_m_final_g', 'new_v_w_ada', 'new_v_b_ada', 'new_v_norm1_g', 'new_v_w_in', 'new_v_ssm_a_re', 'new_v_ssm_a_im', 'new_v_ssm_log_dt', 'new_v_ssm_b_re', 'new_v_ssm_b_im', 'new_v_ssm_c_re', 'new_v_ssm_c_im', 'new_v_ssm_d', 'new_v_w_glu', 'new_v_b_glu', 'new_v_w_a_out', 'new_v_q_norm_g', 'new_v_w_uq', 'new_v_kv_norm_g', 'new_v_w_uk', 'new_v_w_uv', 'new_v_w_b_out', 'new_v_w_out', 'new_v_norm2_g', 'new_v_w_gate', 'new_v_w_up', 'new_v_w_down', 'new_v_final_g']
TWIN_LEAF_KINDS = {'loss': 'loss', 'grad_x': 'grad_x', 'grad_w_ada': 'grad_w', 'grad_b_ada': 'grad_w', 'grad_norm1_g': 'grad_w', 'grad_w_in': 'grad_w', 'grad_ssm_a_re': 'grad_w', 'grad_ssm_a_im': 'grad_w', 'grad_ssm_log_dt': 'grad_w', 'grad_ssm_b_re': 'grad_w', 'grad_ssm_b_im': 'grad_w', 'grad_ssm_c_re': 'grad_w', 'grad_ssm_c_im': 'grad_w', 'grad_ssm_d': 'grad_w', 'grad_w_glu': 'grad_w', 'grad_b_glu': 'grad_w', 'grad_w_a_out': 'grad_w', 'grad_q_norm_g': 'grad_w', 'grad_w_uq': 'grad_w', 'grad_kv_norm_g': 'grad_w', 'grad_w_uk': 'grad_w', 'grad_w_uv': 'grad_w', 'grad_w_b_out': 'grad_w', 'grad_w_out': 'grad_w', 'grad_norm2_g': 'grad_w', 'grad_w_gate': 'grad_w', 'grad_w_up': 'grad_w', 'grad_w_down': 'grad_w', 'grad_final_g': 'grad_w', 'delta_w_ada': 'delta_w', 'delta_b_ada': 'delta_w', 'delta_norm1_g': 'delta_w', 'delta_w_in': 'delta_w', 'delta_ssm_a_re': 'delta_w', 'delta_ssm_a_im': 'delta_w', 'delta_ssm_log_dt': 'delta_w', 'delta_ssm_b_re': 'delta_w', 'delta_ssm_b_im': 'delta_w', 'delta_ssm_c_re': 'delta_w', 'delta_ssm_c_im': 'delta_w', 'delta_ssm_d': 'delta_w', 'delta_w_glu': 'delta_w', 'delta_b_glu': 'delta_w', 'delta_w_a_out': 'delta_w', 'delta_q_norm_g': 'delta_w', 'delta_w_uq': 'delta_w', 'delta_kv_norm_g': 'delta_w', 'delta_w_uk': 'delta_w', 'delta_w_uv': 'delta_w', 'delta_w_b_out': 'delta_w', 'delta_w_out': 'delta_w', 'delta_norm2_g': 'delta_w', 'delta_w_gate': 'delta_w', 'delta_w_up': 'delta_w', 'delta_w_down': 'delta_w', 'delta_final_g': 'delta_w', 'new_m_w_ada': 'new_m', 'new_m_b_ada': 'new_m', 'new_m_norm1_g': 'new_m', 'new_m_w_in': 'new_m', 'new_m_ssm_a_re': 'new_m', 'new_m_ssm_a_im': 'new_m', 'new_m_ssm_log_dt': 'new_m', 'new_m_ssm_b_re': 'new_m', 'new_m_ssm_b_im': 'new_m', 'new_m_ssm_c_re': 'new_m', 'new_m_ssm_c_im': 'new_m', 'new_m_ssm_d': 'new_m', 'new_m_w_glu': 'new_m', 'new_m_b_glu': 'new_m', 'new_m_w_a_out': 'new_m', 'new_m_q_norm_g': 'new_m', 'new_m_w_uq': 'new_m', 'new_m_kv_norm_g': 'new_m', 'new_m_w_uk': 'new_m', 'new_m_w_uv': 'new_m', 'new_m_w_b_out': 'new_m', 'new_m_w_out': 'new_m', 'new_m_norm2_g': 'new_m', 'new_m_w_gate': 'new_m', 'new_m_w_up': 'new_m', 'new_m_w_down': 'new_m', 'new_m_final_g': 'new_m', 'new_v_w_ada': 'new_v', 'new_v_b_ada': 'new_v', 'new_v_norm1_g': 'new_v', 'new_v_w_in': 'new_v', 'new_v_ssm_a_re': 'new_v', 'new_v_ssm_a_im': 'new_v', 'new_v_ssm_log_dt': 'new_v', 'new_v_ssm_b_re': 'new_v', 'new_v_ssm_b_im': 'new_v', 'new_v_ssm_c_re': 'new_v', 'new_v_ssm_c_im': 'new_v', 'new_v_ssm_d': 'new_v', 'new_v_w_glu': 'new_v', 'new_v_b_glu': 'new_v', 'new_v_w_a_out': 'new_v', 'new_v_q_norm_g': 'new_v', 'new_v_w_uq': 'new_v', 'new_v_kv_norm_g': 'new_v', 'new_v_w_uk': 'new_v', 'new_v_w_uv': 'new_v', 'new_v_w_b_out': 'new_v', 'new_v_w_out': 'new_v', 'new_v_norm2_g': 'new_v', 'new_v_w_gate': 'new_v', 'new_v_w_up': 'new_v', 'new_v_w_down': 'new_v', 'new_v_final_g': 'new_v'}


def _forward(args):
    return _fwd_reference(*[args[k] for k in FWD_PARAMS])


def _output_shape():
    def fwd():
        inp = _fwd_setup_inputs(0)
        return _fwd_reference(*[inp[k] for k in FWD_PARAMS])
    out = _jax.eval_shape(fwd)
    return out.shape, out.dtype

N_MICROBATCH = 1
ADAM_LR = 0.001
ADAM_B1 = 0.9
ADAM_B2 = 0.999
ADAM_EPS = 1e-08
ADAM_WD = 0.01
ADAM_STEP = 10
PER_EXAMPLE_BATCH_AXIS = {'x': 0, 'c': 0, 'positions': 0, 'loss_target': 0}
SHARED_INPUTS = []
_WEIGHT_DTYPES = {'w_ada': _jnp.float32, 'b_ada': _jnp.float32, 'norm1_g': _jnp.float32, 'w_in': _jnp.float32, 'ssm_a_re': _jnp.float32, 'ssm_a_im': _jnp.float32, 'ssm_log_dt': _jnp.float32, 'ssm_b_re': _jnp.float32, 'ssm_b_im': _jnp.float32, 'ssm_c_re': _jnp.float32, 'ssm_c_im': _jnp.float32, 'ssm_d': _jnp.float32, 'w_glu': _jnp.float32, 'b_glu': _jnp.float32, 'w_a_out': _jnp.float32, 'q_norm_g': _jnp.float32, 'w_uq': _jnp.float32, 'kv_norm_g': _jnp.float32, 'w_uk': _jnp.float32, 'w_uv': _jnp.float32, 'w_b_out': _jnp.float32, 'w_out': _jnp.float32, 'norm2_g': _jnp.float32, 'w_gate': _jnp.float32, 'w_up': _jnp.float32, 'w_down': _jnp.float32, 'final_g': _jnp.float32}
MOMENT_SCALE = {'w_ada': 7.659531e-02, 'b_ada': 1.273580e-01, 'norm1_g': 3.297301e-02, 'w_in': 2.216209e-02, 'ssm_a_re': 3.299595e-03, 'ssm_a_im': 2.827914e-03, 'ssm_log_dt': 2.043024e+00, 'ssm_b_re': 1.532211e-03, 'ssm_b_im': 1.541811e-03, 'ssm_c_re': 2.958759e-03, 'ssm_c_im': 3.082392e-03, 'ssm_d': 3.509326e-02, 'w_glu': 1.050929e-02, 'b_glu': 1.399607e-02, 'w_a_out': 2.322771e-02, 'q_norm_g': 2.124055e-02, 'w_uq': 1.205714e-02, 'kv_norm_g': 6.463139e-02, 'w_uk': 1.216570e-02, 'w_uv': 2.981654e-02, 'w_b_out': 2.117013e-02, 'w_out': 3.137740e-02, 'norm2_g': 1.096681e-01, 'w_gate': 4.653516e-02, 'w_up': 4.507309e-02, 'w_down': 7.469402e-02, 'final_g': 1.280266e+02}


def _to_microbatches(a, axis):
    t = _jnp.moveaxis(a, axis, 0)
    t = t.reshape((N_MICROBATCH, t.shape[0] // N_MICROBATCH) + t.shape[1:])
    return _jnp.moveaxis(t, 1, axis + 1)


def setup_inputs(seed: int = 0) -> dict:
    inp = _fwd_setup_inputs(seed)
    key = _jax.random.fold_in(_jax.random.key(seed), 7919)
    shape, _ = _output_shape()
    out = dict(inp)
    out["loss_target"] = _jax.random.normal(_jax.random.fold_in(key, 0), shape, _jnp.float32)
    for i, name in enumerate(TWIN_WEIGHTS):
        w = inp[name].astype(_jnp.float32)
        if MOMENT_SCALE is None:
            s = _jnp.sqrt(_jnp.mean(_jnp.square(w)) + 1e-30)
        else:
            s = MOMENT_SCALE[name]
        km, kv = _jax.random.split(_jax.random.fold_in(key, i + 1))
        out[name] = w
        out["m_" + name] = s * _jax.random.normal(km, w.shape, _jnp.float32)
        out["v_" + name] = (s * s) * _jax.random.uniform(kv, w.shape, _jnp.float32, 0.5, 1.5)
    if N_MICROBATCH > 1:
        for name, axis in PER_EXAMPLE_BATCH_AXIS.items():
            out[name] = _to_microbatches(out[name], axis)
    return {'x': out['x'], 'c': out['c'], 'positions': out['positions'], 'w_ada': out['w_ada'], 'b_ada': out['b_ada'], 'norm1_g': out['norm1_g'], 'w_in': out['w_in'], 'ssm_a_re': out['ssm_a_re'], 'ssm_a_im': out['ssm_a_im'], 'ssm_log_dt': out['ssm_log_dt'], 'ssm_b_re': out['ssm_b_re'], 'ssm_b_im': out['ssm_b_im'], 'ssm_c_re': out['ssm_c_re'], 'ssm_c_im': out['ssm_c_im'], 'ssm_d': out['ssm_d'], 'w_glu': out['w_glu'], 'b_glu': out['b_glu'], 'w_a_out': out['w_a_out'], 'q_norm_g': out['q_norm_g'], 'w_uq': out['w_uq'], 'kv_norm_g': out['kv_norm_g'], 'w_uk': out['w_uk'], 'w_uv': out['w_uv'], 'w_b_out': out['w_b_out'], 'w_out': out['w_out'], 'norm2_g': out['norm2_g'], 'w_gate': out['w_gate'], 'w_up': out['w_up'], 'w_down': out['w_down'], 'final_g': out['final_g'], 'loss_target': out['loss_target'], 'm_w_ada': out['m_w_ada'], 'm_b_ada': out['m_b_ada'], 'm_norm1_g': out['m_norm1_g'], 'm_w_in': out['m_w_in'], 'm_ssm_a_re': out['m_ssm_a_re'], 'm_ssm_a_im': out['m_ssm_a_im'], 'm_ssm_log_dt': out['m_ssm_log_dt'], 'm_ssm_b_re': out['m_ssm_b_re'], 'm_ssm_b_im': out['m_ssm_b_im'], 'm_ssm_c_re': out['m_ssm_c_re'], 'm_ssm_c_im': out['m_ssm_c_im'], 'm_ssm_d': out['m_ssm_d'], 'm_w_glu': out['m_w_glu'], 'm_b_glu': out['m_b_glu'], 'm_w_a_out': out['m_w_a_out'], 'm_q_norm_g': out['m_q_norm_g'], 'm_w_uq': out['m_w_uq'], 'm_kv_norm_g': out['m_kv_norm_g'], 'm_w_uk': out['m_w_uk'], 'm_w_uv': out['m_w_uv'], 'm_w_b_out': out['m_w_b_out'], 'm_w_out': out['m_w_out'], 'm_norm2_g': out['m_norm2_g'], 'm_w_gate': out['m_w_gate'], 'm_w_up': out['m_w_up'], 'm_w_down': out['m_w_down'], 'm_final_g': out['m_final_g'], 'v_w_ada': out['v_w_ada'], 'v_b_ada': out['v_b_ada'], 'v_norm1_g': out['v_norm1_g'], 'v_w_in': out['v_w_in'], 'v_ssm_a_re': out['v_ssm_a_re'], 'v_ssm_a_im': out['v_ssm_a_im'], 'v_ssm_log_dt': out['v_ssm_log_dt'], 'v_ssm_b_re': out['v_ssm_b_re'], 'v_ssm_b_im': out['v_ssm_b_im'], 'v_ssm_c_re': out['v_ssm_c_re'], 'v_ssm_c_im': out['v_ssm_c_im'], 'v_ssm_d': out['v_ssm_d'], 'v_w_glu': out['v_w_glu'], 'v_b_glu': out['v_b_glu'], 'v_w_a_out': out['v_w_a_out'], 'v_q_norm_g': out['v_q_norm_g'], 'v_w_uq': out['v_w_uq'], 'v_kv_norm_g': out['v_kv_norm_g'], 'v_w_uk': out['v_w_uk'], 'v_w_uv': out['v_w_uv'], 'v_w_b_out': out['v_w_b_out'], 'v_w_out': out['v_w_out'], 'v_norm2_g': out['v_norm2_g'], 'v_w_gate': out['v_w_gate'], 'v_w_up': out['v_w_up'], 'v_w_down': out['v_w_down'], 'v_final_g': out['v_final_g']}


def _loss(weights, diff, rest, loss_target):
    with _jax.named_scope("forward"):
        args = {**rest, TWIN_DIFF_INPUT: diff, **{k: w.astype(_WEIGHT_DTYPES[k]) for k, w in weights.items()}}
        y = _forward(args)
    with _jax.named_scope("loss_head"):
        err = _jnp.square(y.astype(_jnp.float32) - loss_target)
        return 0.5 * _jnp.sum(_jnp.mean(err, axis=-1)) if err.ndim else 0.5 * err


def _adamw(w, g, m, v):
    m = ADAM_B1 * m + (1.0 - ADAM_B1) * g
    v = ADAM_B2 * v + (1.0 - ADAM_B2) * _jnp.square(g)
    m_hat = m / (1.0 - ADAM_B1 ** ADAM_STEP)
    v_hat = v / (1.0 - ADAM_B2 ** ADAM_STEP)
    delta = -ADAM_LR * (m_hat / (_jnp.sqrt(v_hat) + ADAM_EPS) + ADAM_WD * w)
    return delta, m, v


def reference(x, c, positions, w_ada, b_ada, norm1_g, w_in, ssm_a_re, ssm_a_im, ssm_log_dt, ssm_b_re, ssm_b_im, ssm_c_re, ssm_c_im, ssm_d, w_glu, b_glu, w_a_out, q_norm_g, w_uq, kv_norm_g, w_uk, w_uv, w_b_out, w_out, norm2_g, w_gate, w_up, w_down, final_g, loss_target, m_w_ada, m_b_ada, m_norm1_g, m_w_in, m_ssm_a_re, m_ssm_a_im, m_ssm_log_dt, m_ssm_b_re, m_ssm_b_im, m_ssm_c_re, m_ssm_c_im, m_ssm_d, m_w_glu, m_b_glu, m_w_a_out, m_q_norm_g, m_w_uq, m_kv_norm_g, m_w_uk, m_w_uv, m_w_b_out, m_w_out, m_norm2_g, m_w_gate, m_w_up, m_w_down, m_final_g, v_w_ada, v_b_ada, v_norm1_g, v_w_in, v_ssm_a_re, v_ssm_a_im, v_ssm_log_dt, v_ssm_b_re, v_ssm_b_im, v_ssm_c_re, v_ssm_c_im, v_ssm_d, v_w_glu, v_b_glu, v_w_a_out, v_q_norm_g, v_w_uq, v_kv_norm_g, v_w_uk, v_w_uv, v_w_b_out, v_w_out, v_norm2_g, v_w_gate, v_w_up, v_w_down, v_final_g):
    given = dict(x=x, c=c, positions=positions, w_ada=w_ada, b_ada=b_ada, norm1_g=norm1_g, w_in=w_in, ssm_a_re=ssm_a_re, ssm_a_im=ssm_a_im, ssm_log_dt=ssm_log_dt, ssm_b_re=ssm_b_re, ssm_b_im=ssm_b_im, ssm_c_re=ssm_c_re, ssm_c_im=ssm_c_im, ssm_d=ssm_d, w_glu=w_glu, b_glu=b_glu, w_a_out=w_a_out, q_norm_g=q_norm_g, w_uq=w_uq, kv_norm_g=kv_norm_g, w_uk=w_uk, w_uv=w_uv, w_b_out=w_b_out, w_out=w_out, norm2_g=norm2_g, w_gate=w_gate, w_up=w_up, w_down=w_down, final_g=final_g, loss_target=loss_target, m_w_ada=m_w_ada, m_b_ada=m_b_ada, m_norm1_g=m_norm1_g, m_w_in=m_w_in, m_ssm_a_re=m_ssm_a_re, m_ssm_a_im=m_ssm_a_im, m_ssm_log_dt=m_ssm_log_dt, m_ssm_b_re=m_ssm_b_re, m_ssm_b_im=m_ssm_b_im, m_ssm_c_re=m_ssm_c_re, m_ssm_c_im=m_ssm_c_im, m_ssm_d=m_ssm_d, m_w_glu=m_w_glu, m_b_glu=m_b_glu, m_w_a_out=m_w_a_out, m_q_norm_g=m_q_norm_g, m_w_uq=m_w_uq, m_kv_norm_g=m_kv_norm_g, m_w_uk=m_w_uk, m_w_uv=m_w_uv, m_w_b_out=m_w_b_out, m_w_out=m_w_out, m_norm2_g=m_norm2_g, m_w_gate=m_w_gate, m_w_up=m_w_up, m_w_down=m_w_down, m_final_g=m_final_g, v_w_ada=v_w_ada, v_b_ada=v_b_ada, v_norm1_g=v_norm1_g, v_w_in=v_w_in, v_ssm_a_re=v_ssm_a_re, v_ssm_a_im=v_ssm_a_im, v_ssm_log_dt=v_ssm_log_dt, v_ssm_b_re=v_ssm_b_re, v_ssm_b_im=v_ssm_b_im, v_ssm_c_re=v_ssm_c_re, v_ssm_c_im=v_ssm_c_im, v_ssm_d=v_ssm_d, v_w_glu=v_w_glu, v_b_glu=v_b_glu, v_w_a_out=v_w_a_out, v_q_norm_g=v_q_norm_g, v_w_uq=v_w_uq, v_kv_norm_g=v_kv_norm_g, v_w_uk=v_w_uk, v_w_uv=v_w_uv, v_w_b_out=v_w_b_out, v_w_out=v_w_out, v_norm2_g=v_norm2_g, v_w_gate=v_w_gate, v_w_up=v_w_up, v_w_down=v_w_down, v_final_g=v_final_g)
    weights = {n: given[n] for n in TWIN_WEIGHTS}
    shared = {n: given[n] for n in SHARED_INPUTS}
    per_example = {n: given[n] for n in ['x', 'c', 'positions']}
    grad_fn = _jax.value_and_grad(_loss, argnums=(0, 1))

    def one_microbatch(ex, loss_target):
        ex = dict(ex)
        diff = ex.pop(TWIN_DIFF_INPUT)
        return grad_fn(weights, diff, {**shared, **ex}, loss_target)

    if N_MICROBATCH == 1:
        loss, (grad_w, grad_x) = one_microbatch(per_example, given["loss_target"])
    else:
        def body(carry, xs):
            loss_sum, grad_sum = carry
            l_k, (gw_k, gx_k) = one_microbatch(xs[0], xs[1])
            with _jax.named_scope("update"):
                return (loss_sum + l_k, _jax.tree.map(_jnp.add, grad_sum, gw_k)), gx_k

        init = (_jnp.zeros((), _jnp.float32), _jax.tree.map(_jnp.zeros_like, weights))
        (loss, grad_w), grad_x = _jax.lax.scan(body, init, (per_example, given["loss_target"]))
    with _jax.named_scope("update"):
        delta_w, new_m, new_v = {}, {}, {}
        for n in TWIN_WEIGHTS:
            delta_w[n], new_m[n], new_v[n] = _adamw(weights[n], grad_w[n], given["m_" + n], given["v_" + n])
    return (loss, grad_x, *[grad_w[n] for n in TWIN_WEIGHTS], *[delta_w[n] for n in TWIN_WEIGHTS],
            *[new_m[n] for n in TWIN_WEIGHTS], *[new_v[n] for n in TWIN_WEIGHTS])
```

```python
import functools
import math

import jax
import jax.numpy as jnp
from jax import lax
from jax.experimental import pallas as pl
from jax.experimental.pallas import tpu as pltpu

F32 = jnp.float32
BF16 = jnp.bfloat16

N_DEV = 8
LANES = 128
FLAT_W = 1024
VMEM_LIMIT = 48 * 1024 * 1024
QK_NOPE, QK_ROPE, V_DIM = 64, 32, 64
HEAD_PAD = LANES
ROPE_BASE = 10000.0
EPS = 1e-6
ADAM_LR, ADAM_B1, ADAM_B2, ADAM_EPS, ADAM_WD, ADAM_STEP = 0.001, 0.9, 0.999, 1e-08, 0.01, 10
NEG = float(jnp.finfo(jnp.float32).min)

WEIGHTS = ['w_ada', 'b_ada', 'norm1_g', 'w_in', 'ssm_a_re', 'ssm_a_im', 'ssm_log_dt', 'ssm_b_re', 'ssm_b_im',
           'ssm_c_re', 'ssm_c_im', 'ssm_d', 'w_glu', 'b_glu', 'w_a_out', 'q_norm_g', 'w_uq', 'kv_norm_g', 'w_uk',
           'w_uv', 'w_b_out', 'w_out', 'norm2_g', 'w_gate', 'w_up', 'w_down', 'final_g']
COL_SHARDED = ['w_in', 'w_a_out', 'w_uq', 'w_uk', 'w_uv', 'w_b_out', 'w_gate', 'w_up']
ROW_SHARDED = ['w_glu', 'w_out', 'w_down']
SHARDED = COL_SHARDED + ROW_SHARDED
SMALL = ['b_ada', 'norm1_g', 'ssm_a_re', 'ssm_a_im', 'ssm_log_dt', 'ssm_b_re', 'ssm_b_im', 'ssm_c_re', 'ssm_c_im',
         'ssm_d', 'b_glu', 'q_norm_g', 'kv_norm_g', 'norm2_g', 'final_g']


def _cparams(sem):
    return pltpu.CompilerParams(dimension_semantics=sem, vmem_limit_bytes=VMEM_LIMIT)


def _pick(dim, pref, quantum=LANES):
    if dim <= pref:
        return dim
    t = (pref // quantum) * quantum
    while t >= quantum:
        if dim % t == 0:
            return t
        t -= quantum
    return dim


def _mm(name, a, b, *, ta=False, out_dtype=F32, a_col=None, b_col=None, tm=512, tn=512, tk=512):
    a_off, a_w = a_col if a_col is not None else (0, a.shape[1])
    b_off, b_w = b_col if b_col is not None else (0, b.shape[1])
    if ta:
        kdim, m = a.shape[0], a_w
        assert b.shape[0] == kdim
    else:
        m, kdim = a.shape[0], a_w
        assert b.shape[0] == kdim, (name, a.shape, b.shape)
    n = b_w
    tm = _pick(m, tm, LANES if ta else 8)
    tn = _pick(n, tn)
    tk = _pick(kdim, tk, 8 if ta else LANES)
    nk = kdim // tk
    assert m % tm == 0 and n % tn == 0 and kdim % tk == 0, (name, m, n, kdim, tm, tn, tk)
    if ta:
        assert a_off % tm == 0 and b_off % tn == 0
        a_spec = pl.BlockSpec((tk, tm), lambda i, j, k: (k, i + a_off // tm))
        dims = (((0,), (0,)), ((), ()))
    else:
        assert a_off % tk == 0 and b_off % tn == 0
        a_spec = pl.BlockSpec((tm, tk), lambda i, j, k: (i, k + a_off // tk))
        dims = (((1,), (0,)), ((), ()))
    b_spec = pl.BlockSpec((tk, tn), lambda i, j, k: (k, j + b_off // tn))

    def body(a_ref, b_ref, o_ref, acc_ref):
        @pl.when(pl.program_id(2) == 0)
        def _():
            acc_ref[...] = jnp.zeros_like(acc_ref)

        acc_ref[...] += lax.dot_general(a_ref[...].astype(BF16), b_ref[...].astype(BF16), dims,
                                        preferred_element_type=F32)

        @pl.when(pl.program_id(2) == nk - 1)
        def _():
            o_ref[...] = acc_ref[...].astype(o_ref.dtype)

    return pl.pallas_call(
        body, name=name,
        out_shape=jax.ShapeDtypeStruct((m, n), out_dtype),
        grid=(m // tm, n // tn, nk),
        in_specs=[a_spec, b_spec],
        out_specs=pl.BlockSpec((tm, tn), lambda i, j, k: (i, j)),
        scratch_shapes=[pltpu.VMEM((tm, tn), F32)],
        compiler_params=_cparams(("parallel", "parallel", "arbitrary")),
    )(a, b)


def _rowwise(name, fn, rows, vecs, outs, reds=(), *, tm=256):
    rows = [(r, 0, r.shape[1]) if not isinstance(r, tuple) else r for r in rows]
    nrows = rows[0][0].shape[0]
    tm = _pick(nrows, tm, 8)
    assert nrows % tm == 0, (name, nrows, tm)
    nr, nv, no = len(rows), len(vecs), len(outs)
    in_specs = []
    for arr, off, w in rows:
        assert arr.shape[0] == nrows and off % w == 0, (name, arr.shape, off, w)
        in_specs.append(pl.BlockSpec((tm, w), functools.partial(lambda i, cb: (i, cb), cb=off // w)))
    for v in vecs:
        assert v.ndim == 2 and v.shape[0] == 1, (name, v.shape)
        in_specs.append(pl.BlockSpec(v.shape, lambda i: (0, 0)))
    out_shape = [jax.ShapeDtypeStruct((nrows, w), dt) for w, dt in outs]
    out_specs = [pl.BlockSpec((tm, w), lambda i: (i, 0)) for w, dt in outs]
    out_shape += [jax.ShapeDtypeStruct((1, w), F32) for w in reds]
    out_specs += [pl.BlockSpec((1, w), lambda i: (0, 0)) for w in reds]

    def body(*refs):
        rin, vin = refs[:nr], refs[nr:nr + nv]
        rout, rred = refs[nr + nv:nr + nv + no], refs[nr + nv + no:]
        res = fn(*[r[...] for r in rin], *[v[...] for v in vin])
        if not isinstance(res, (tuple, list)):
            res = (res,)
        assert len(res) == no + len(reds), (name, len(res))
        for r, val in zip(rout, res[:no]):
            r[...] = val.astype(r.dtype)
        if reds:
            @pl.when(pl.program_id(0) == 0)
            def _():
                for r in rred:
                    r[...] = jnp.zeros_like(r)

            for r, val in zip(rred, res[no:]):
                r[...] += val.astype(F32)

    res = pl.pallas_call(
        body, name=name, out_shape=out_shape, grid=(nrows // tm,),
        in_specs=in_specs, out_specs=out_specs,
        compiler_params=_cparams(("arbitrary",) if reds else ("parallel",)),
    )(*[r[0] for r in rows], *vecs)
    return res


def _f32(x):
    return x.astype(F32)


def _vjp_fn(f, n_in, n_cot, want):
    def fn(*args):
        ins = [_f32(a) for a in args[:n_in]]
        cots = tuple(_f32(a) for a in args[n_in:n_in + n_cot])
        _, vjp = jax.vjp(f, *ins)
        grads = vjp(cots if n_cot > 1 else cots[0])
        return tuple(grads[i] for i in want)
    return fn


def _f_rms(x, g):
    return (x * lax.rsqrt(jnp.mean(x * x, axis=-1, keepdims=True) + EPS)) * g


def _f_norm_mod(x, g, sc, sh):
    return _f_rms(x, g) * (1.0 + sc) + sh


def _f_gelu_in(ych, u, d):
    return jax.nn.gelu(ych + d * u)


def _f_glu(yg, pre, b):
    return yg * jax.nn.sigmoid(pre + b)


def _f_merge(ga, gb, ya, yb):
    return jax.nn.sigmoid(ga) * ya + jax.nn.sigmoid(gb) * yb


def _f_res(x, t, g):
    return x + g * t


def _f_swiglu(a, b):
    return jax.nn.silu(a) * b


def _f_rope_q(qa, qb, cos, sin):
    h = qa.shape[1] // HEAD_PAD
    return qa * jnp.tile(cos, (1, h)) + qb * jnp.tile(sin, (1, h))


def _f_rope_k(kn, kra, krb, cos, sin):
    h = kn.shape[1] // HEAD_PAD
    return kn + jnp.tile(kra * cos + krb * sin, (1, h))


def _scan_fwd(bu, a_re, a_im, *, tb=256, sc=512):
    L, gp2 = bu.shape
    gp = gp2 // 2
    tb = _pick(L, tb, 8)
    sc = _pick(gp, sc)
    nchunk = gp // sc

    def body(bu_ref, ar_ref, ai_ref, h_ref, carry):
        @pl.when(pl.program_id(0) == 0)
        def _():
            carry[...] = jnp.zeros_like(carry)

        for c in range(nchunk):
            re = pl.ds(c * sc, sc)
            im = pl.ds(gp + c * sc, sc)
            ar = ar_ref[:, re]
            ai = ai_ref[:, re]

            def step(t, hc):
                hr, hi = hc
                row = pl.ds(t, 1)
                nr = ar * hr - ai * hi + bu_ref[row, re]
                ni = ar * hi + ai * hr + bu_ref[row, im]
                h_ref[row, re] = nr
                h_ref[row, im] = ni
                return nr, ni

            hr, hi = lax.fori_loop(0, tb, step, (carry[0:1, re], carry[1:2, re]), unroll=8)
            carry[0:1, re] = hr
            carry[1:2, re] = hi

    return pl.pallas_call(
        body, name="s5_scan_fwd",
        out_shape=jax.ShapeDtypeStruct((L, gp2), F32),
        grid=(L // tb,),
        in_specs=[pl.BlockSpec((tb, gp2), lambda i: (i, 0)),
                  pl.BlockSpec((1, gp), lambda i: (0, 0)),
                  pl.BlockSpec((1, gp), lambda i: (0, 0))],
        out_specs=pl.BlockSpec((tb, gp2), lambda i: (i, 0)),
        scratch_shapes=[pltpu.VMEM((8, gp), F32)],
        compiler_params=_cparams(("arbitrary",)),
    )(bu, a_re, a_im)


def _scan_bwd(dh, h, a_re, a_im, *, tb=128, sc=512):
    L, gp2 = dh.shape
    gp = gp2 // 2
    tb = _pick(L, tb, 8)
    sc = _pick(gp, sc)
    nchunk = gp // sc
    nb = L // tb
    r8 = tb // 8

    def body(dh_ref, h_ref, hp_ref, ar_ref, ai_ref, g_ref, dar_ref, dai_ref, carry):
        i = pl.program_id(0)

        @pl.when(i == 0)
        def _():
            carry[...] = jnp.zeros_like(carry)
            dar_ref[...] = jnp.zeros_like(dar_ref)
            dai_ref[...] = jnp.zeros_like(dai_ref)

        first_block = (i == nb - 1)
        for c in range(nchunk):
            re = pl.ds(c * sc, sc)
            im = pl.ds(gp + c * sc, sc)
            ar = ar_ref[:, re]
            ai = ai_ref[:, re]

            def one(t, gr, gi, hpr, hpi, accr, acci):
                row = pl.ds(t, 1)
                ngr = dh_ref[row, re] + ar * gr + ai * gi
                ngi = dh_ref[row, im] - ai * gr + ar * gi
                g_ref[row, re] = ngr
                g_ref[row, im] = ngi
                accr = accr + ngr * hpr + ngi * hpi
                acci = acci + ngi * hpr - ngr * hpi
                return ngr, ngi, accr, acci

            def step(s, st):
                gr, gi, accr, acci = st
                t = tb - 1 - s
                prev = pl.ds(t - 1, 1)
                return one(t, gr, gi, h_ref[prev, re], h_ref[prev, im], accr, acci)

            zero = jnp.zeros((1, sc), F32)
            st = lax.fori_loop(0, tb - 1, step, (carry[0:1, re], carry[1:2, re], zero, zero), unroll=8)
            keep = jnp.where(first_block, 0.0, 1.0).astype(F32)
            hpr = hp_ref[7:8, re] * keep
            hpi = hp_ref[7:8, im] * keep
            gr, gi, accr, acci = one(0, st[0], st[1], hpr, hpi, st[2], st[3])
            carry[0:1, re] = gr
            carry[1:2, re] = gi
            dar_ref[:, re] += accr
            dai_ref[:, re] += acci

    rev = lambda i: (nb - 1 - i, 0)
    return pl.pallas_call(
        body, name="s5_scan_bwd",
        out_shape=[jax.ShapeDtypeStruct((L, gp2), F32),
                   jax.ShapeDtypeStruct((1, gp), F32), jax.ShapeDtypeStruct((1, gp), F32)],
        grid=(nb,),
        in_specs=[pl.BlockSpec((tb, gp2), rev),
                  pl.BlockSpec((tb, gp2), rev),
                  pl.BlockSpec((8, gp2), lambda i: (jnp.maximum((nb - 1 - i) * r8 - 1, 0), 0)),
                  pl.BlockSpec((1, gp), lambda i: (0, 0)),
                  pl.BlockSpec((1, gp), lambda i: (0, 0))],
        out_specs=[pl.BlockSpec((tb, gp2), rev),
                   pl.BlockSpec((1, gp), lambda i: (0, 0)),
                   pl.BlockSpec((1, gp), lambda i: (0, 0))],
        scratch_shapes=[pltpu.VMEM((8, gp), F32)],
        compiler_params=_cparams(("arbitrary",)),
    )(dh, h, h, a_re, a_im)


_NT = (((1,), (1,)), ((), ()))
_TN = (((0,), (0,)), ((), ()))


def _flash_fwd(q, k, v, pos_col, pos_row, *, heads, t, scale):
    L = q.shape[0]
    n = L // t

    def body(q_ref, k_ref, v_ref, pq_ref, pk_ref, o_ref, lse_ref, m_s, l_s, acc_s):
        qi, ki = pl.program_id(1), pl.program_id(2)

        @pl.when(ki == 0)
        def _():
            m_s[...] = jnp.full(m_s.shape, NEG, F32)
            l_s[...] = jnp.zeros_like(l_s)
            acc_s[...] = jnp.zeros_like(acc_s)

        @pl.when(ki <= qi)
        def _():
            s = lax.dot_general(q_ref[...], k_ref[...], _NT, preferred_element_type=F32) * scale
            s = jnp.where(pk_ref[...] <= pq_ref[...], s, NEG)
            m_prev = m_s[...]
            m_new = jnp.maximum(m_prev, jnp.max(s, axis=1, keepdims=True))
            p = jnp.exp(s - m_new)
            alpha = jnp.exp(m_prev - m_new)
            l_s[...] = alpha * l_s[...] + jnp.sum(p, axis=1, keepdims=True)
            acc_s[...] = alpha * acc_s[...] + jnp.dot(p.astype(BF16), v_ref[...], preferred_element_type=F32)
            m_s[...] = m_new

        @pl.when(ki == qi)
        def _():
            o_ref[...] = (acc_s[...] / l_s[...]).astype(o_ref.dtype)
            lse_ref[0] = m_s[...] + jnp.log(l_s[...])

    kv_map = lambda h, qi, ki: (jnp.minimum(ki, qi), h)
    v_map = lambda h, qi, ki: (jnp.minimum(ki, qi), h + heads)
    return pl.pallas_call(
        body, name="mla_flash_fwd",
        out_shape=[jax.ShapeDtypeStruct((L, heads * HEAD_PAD), BF16),
                   jax.ShapeDtypeStruct((heads, L, 1), F32)],
        grid=(heads, n, n),
        in_specs=[pl.BlockSpec((t, HEAD_PAD), lambda h, qi, ki: (qi, h)),
                  pl.BlockSpec((t, HEAD_PAD), kv_map),
                  pl.BlockSpec((t, HEAD_PAD), v_map),
                  pl.BlockSpec((t, 1), lambda h, qi, ki: (qi, 0)),
                  pl.BlockSpec((1, t), lambda h, qi, ki: (0, jnp.minimum(ki, qi)))],
        out_specs=[pl.BlockSpec((t, HEAD_PAD), lambda h, qi, ki: (qi, h)),
                   pl.BlockSpec((1, t, 1), lambda h, qi, ki: (h, qi, 0))],
        scratch_shapes=[pltpu.VMEM((t, 1), F32), pltpu.VMEM((t, 1), F32), pltpu.VMEM((t, HEAD_PAD), F32)],
        compiler_params=_cparams(("parallel", "parallel", "arbitrary")),
    )(q, k, v, pos_col, pos_row)


def _flash_delta(do, o, *, heads, t):
    L = do.shape[0]

    def body(do_ref, o_ref, d_ref):
        d_ref[0] = jnp.sum(do_ref[...].astype(F32) * o_ref[...].astype(F32), axis=1, keepdims=True)

    return pl.pallas_call(
        body, name="mla_flash_delta",
        out_shape=jax.ShapeDtypeStruct((heads, L, 1), F32),
        grid=(heads, L // t),
        in_specs=[pl.BlockSpec((t, HEAD_PAD), lambda h, i: (i, h)),
                  pl.BlockSpec((t, HEAD_PAD), lambda h, i: (i, h))],
        out_specs=pl.BlockSpec((1, t, 1), lambda h, i: (h, i, 0)),
        compiler_params=_cparams(("parallel", "parallel")),
    )(do, o)


def _flash_bwd(q, k, v, do, lse, delta, pos_col, pos_row, *, heads, t, scale):
    L = q.shape[0]
    n = L // t

    def body(q_ref, k_ref, v_ref, do_ref, lse_ref, dl_ref, pq_ref, pk_ref, dq_ref, dk_ref, dv_ref, dk_s, dv_s):
        ki, qi = pl.program_id(1), pl.program_id(2)

        @pl.when(qi == 0)
        def _():
            dk_s[...] = jnp.zeros_like(dk_s)
            dv_s[...] = jnp.zeros_like(dv_s)

        @pl.when(qi >= ki)
        def _():
            qb, kb, vb, dob = q_ref[...], k_ref[...], v_ref[...], do_ref[...]
            s = lax.dot_general(qb, kb, _NT, preferred_element_type=F32) * scale
            s = jnp.where(pk_ref[...] <= pq_ref[...], s, NEG)
            p = jnp.exp(s - lse_ref[0])
            dv_s[...] += lax.dot_general(p.astype(BF16), dob, _TN, preferred_element_type=F32)
            dp = lax.dot_general(dob, vb, _NT, preferred_element_type=F32)
            ds = (p * (dp - dl_ref[0]) * scale).astype(BF16)
            dk_s[...] += lax.dot_general(ds, qb, _TN, preferred_element_type=F32)
            dqb = jnp.dot(ds, kb, preferred_element_type=F32)
            rows = pl.ds(pl.multiple_of(qi * t, t), t)

            @pl.when(ki == 0)
            def _():
                dq_ref[rows, :] = dqb

            @pl.when(ki > 0)
            def _():
                dq_ref[rows, :] += dqb

        @pl.when(qi == n - 1)
        def _():
            dk_ref[...] = dk_s[...].astype(dk_ref.dtype)
            dv_ref[...] = dv_s[...].astype(dv_ref.dtype)

    q_map = lambda h, ki, qi: (jnp.maximum(qi, ki), h)
    stat_map = lambda h, ki, qi: (h, jnp.maximum(qi, ki), 0)
    kv_map = lambda h, ki, qi: (ki, h)
    v_map = lambda h, ki, qi: (ki, h + heads)
    return pl.pallas_call(
        body, name="mla_flash_bwd",
        out_shape=[jax.ShapeDtypeStruct((L, heads * HEAD_PAD), F32),
                   jax.ShapeDtypeStruct((L, heads * HEAD_PAD), BF16),
                   jax.ShapeDtypeStruct((L, heads * HEAD_PAD), BF16)],
        grid=(heads, n, n),
        in_specs=[pl.BlockSpec((t, HEAD_PAD), q_map),
                  pl.BlockSpec((t, HEAD_PAD), kv_map),
                  pl.BlockSpec((t, HEAD_PAD), v_map),
                  pl.BlockSpec((t, HEAD_PAD), q_map),
                  pl.BlockSpec((1, t, 1), stat_map),
                  pl.BlockSpec((1, t, 1), stat_map),
                  pl.BlockSpec((t, 1), lambda h, ki, qi: (jnp.maximum(qi, ki), 0)),
                  pl.BlockSpec((1, t), lambda h, ki, qi: (0, ki))],
        out_specs=[pl.BlockSpec((L, HEAD_PAD), lambda h, ki, qi: (0, h)),
                   pl.BlockSpec((t, HEAD_PAD), kv_map),
                   pl.BlockSpec((t, HEAD_PAD), kv_map)],
        scratch_shapes=[pltpu.VMEM((t, HEAD_PAD), F32), pltpu.VMEM((t, HEAD_PAD), F32)],
        compiler_params=_cparams(("parallel", "arbitrary", "arbitrary")),
    )(q, k, v, do, lse, delta, pos_col, pos_row)


def _peer(k):
    mx, my, mc = lax.axis_index("x"), lax.axis_index("y"), lax.axis_index("c")
    px = 1 - mx if (k >> 2) & 1 else mx
    py = 1 - my if (k >> 1) & 1 else my
    pc = 1 - mc if k & 1 else mc
    return (px, py, pc), 4 * px + 2 * py + pc


def _exchange(name, x, all_to_all):
    slab = x.shape[1:] if all_to_all else x.shape
    any_spec = pl.BlockSpec(memory_space=pl.ANY)

    def body(x_ref, o_ref, send_sems, recv_sems, local_sem):
        _, me = _peer(0)
        mine = x_ref.at[me] if all_to_all else x_ref
        local = pltpu.make_async_copy(mine, o_ref.at[me], local_sem)
        local.start()
        sends = []
        for k in range(1, N_DEV):
            dev, idx = _peer(k)
            cp = pltpu.make_async_remote_copy(
                src_ref=x_ref.at[idx] if all_to_all else x_ref, dst_ref=o_ref.at[me],
                send_sem=send_sems.at[k - 1], recv_sem=recv_sems.at[k - 1],
                device_id=dev, device_id_type=pl.DeviceIdType.MESH)
            cp.start()
            sends.append(cp)
        for k in range(1, N_DEV):
            dev, idx = _peer(k)
            pltpu.make_async_remote_copy(
                src_ref=mine, dst_ref=o_ref.at[idx],
                send_sem=send_sems.at[k - 1], recv_sem=recv_sems.at[k - 1],
                device_id=dev, device_id_type=pl.DeviceIdType.MESH).wait_recv()
        for cp in sends:
            cp.wait_send()
        local.wait()

    return pl.pallas_call(
        body, name=name,
        out_shape=jax.ShapeDtypeStruct((N_DEV,) + tuple(slab), x.dtype),
        in_specs=[any_spec], out_specs=any_spec,
        scratch_shapes=[pltpu.SemaphoreType.DMA((N_DEV - 1,)), pltpu.SemaphoreType.DMA((N_DEV - 1,)),
                        pltpu.SemaphoreType.DMA(())],
    )(x)


def _sum_slabs(name, x, *, tr=128):
    _, r, w = x.shape
    tr = _pick(r, tr, 8)

    def body(x_ref, o_ref):
        acc = x_ref[0]
        for j in range(1, N_DEV):
            acc = acc + x_ref[j]
        o_ref[...] = acc

    return pl.pallas_call(
        body, name=name, out_shape=jax.ShapeDtypeStruct((r, w), F32), grid=(r // tr,),
        in_specs=[pl.BlockSpec((N_DEV, tr, w), lambda i: (0, i, 0))],
        out_specs=pl.BlockSpec((tr, w), lambda i: (i, 0)),
        compiler_params=_cparams(("parallel",)),
    )(x)


def _flat_rows(n, row_quantum):
    q = row_quantum * FLAT_W
    return -(-n // q) * q // FLAT_W


def _pack(arrs, row_quantum, dtype=None):
    flat = jnp.concatenate([a.reshape(-1) for a in arrs])
    if dtype is not None:
        flat = flat.astype(dtype)
    rows = _flat_rows(flat.shape[0], row_quantum)
    flat = jnp.pad(flat, (0, rows * FLAT_W - flat.shape[0]))
    return flat.reshape(rows, FLAT_W)


def _unpack(flat2d, shapes, lead=()):
    flat = flat2d.reshape(lead + (-1,))
    out, off = [], 0
    for s in shapes:
        n = math.prod(s)
        out.append(flat[..., off:off + n].reshape(lead + tuple(s)))
        off += n
    return out


def _s5_operators(a_re, a_im, log_dt, b_re, b_im, c_re, c_im):
    g, p, m = b_re.shape
    dt = jnp.exp(log_dt)[:, None]
    mag = jnp.exp(a_re * dt)
    abar_re = mag * jnp.cos(a_im * dt)
    abar_im = mag * jnp.sin(a_im * dt)
    den = a_re * a_re + a_im * a_im
    nr = abar_re - 1.0
    ni = abar_im
    coef_re = ((nr * a_re + ni * a_im) / den)[..., None]
    coef_im = ((ni * a_re - nr * a_im) / den)[..., None]
    bbar_re = coef_re * b_re - coef_im * b_im
    bbar_im = coef_re * b_im + coef_im * b_re
    eye = jnp.eye(g, dtype=F32)
    bmat_re = jnp.einsum('gpm,gh->gmhp', bbar_re, eye).reshape(g * m, g * p)
    bmat_im = jnp.einsum('gpm,gh->gmhp', bbar_im, eye).reshape(g * m, g * p)
    bmat = jnp.concatenate([bmat_re, bmat_im], axis=1)
    cmat_re = jnp.einsum('gmp,gh->hpgm', c_re, eye).reshape(g * p, g * m)
    cmat_im = jnp.einsum('gmp,gh->hpgm', -c_im, eye).reshape(g * p, g * m)
    cmat = jnp.concatenate([cmat_re, cmat_im], axis=0)
    return abar_re.reshape(1, g * p), abar_im.reshape(1, g * p), bmat, cmat


def _rot_cols(w):
    half = w.shape[-1] // 2
    return jnp.concatenate([-w[..., half:], w[..., :half]], axis=-1)


def _layer_operators(w, dims):
    d, sw, ql, kvl, heads, dff = dims['d'], dims['sw'], dims['ql'], dims['kvl'], dims['heads'], dims['dff']
    w_in = w['w_in']
    o = 0
    parts = {}
    for nm, sz in (('u', sw), ('cq', ql), ('ckv', kvl), ('kr', QK_ROPE), ('ga', d), ('gb', d)):
        parts[nm] = w_in[:, o:o + sz]
        o += sz
    zpad = lambda n: jnp.zeros((d, n), w_in.dtype)
    kra = jnp.concatenate([zpad(QK_NOPE), parts['kr'], zpad(HEAD_PAD - QK_NOPE - QK_ROPE)], axis=1)
    krb = jnp.concatenate([zpad(QK_NOPE), _rot_cols(parts['kr']), zpad(HEAD_PAD - QK_NOPE - QK_ROPE)], axis=1)
    w_in_x = jnp.concatenate([parts['ga'], parts['gb'], parts['u'], parts['cq'], parts['ckv'], kra, krb], axis=1)

    wq = w['w_uq'].reshape(ql, heads, QK_NOPE + QK_ROPE)
    qz = lambda n: jnp.zeros((ql, heads, n), wq.dtype)
    wq_a = jnp.concatenate([wq, qz(HEAD_PAD - QK_NOPE - QK_ROPE)], axis=2)
    wq_b = jnp.concatenate([qz(QK_NOPE), _rot_cols(wq[:, :, QK_NOPE:]), qz(HEAD_PAD - QK_NOPE - QK_ROPE)], axis=2)
    wq_x = jnp.concatenate([wq_a.reshape(ql, -1), wq_b.reshape(ql, -1)], axis=1)

    kz = lambda n: jnp.zeros((kvl, heads, n), w['w_uk'].dtype)
    wk = jnp.concatenate([w['w_uk'].reshape(kvl, heads, QK_NOPE), kz(HEAD_PAD - QK_NOPE)], axis=2)
    wv = jnp.concatenate([w['w_uv'].reshape(kvl, heads, V_DIM), kz(HEAD_PAD - V_DIM)], axis=2)
    wkv_x = jnp.concatenate([wk.reshape(kvl, -1), wv.reshape(kvl, -1)], axis=1)

    wbo = w['w_b_out'].reshape(heads, V_DIM, d)
    wbo_x = jnp.concatenate([wbo, jnp.zeros((heads, HEAD_PAD - V_DIM, d), wbo.dtype)], axis=1).reshape(-1, d)
    wgu = jnp.concatenate([w['w_gate'], w['w_up']], axis=1)
    return dict(w_in=w_in_x, w_glu=w['w_glu'], w_a_out=w['w_a_out'], wq=wq_x, wkv=wkv_x, wbo=wbo_x,
                w_out=w['w_out'], wgu=wgu, w_down=w['w_down'])


def _gathered_to_full(gathered, shard_shapes):
    pieces = _unpack(gathered, [shard_shapes[n] for n in SHARDED], lead=(N_DEV,))
    full = {}
    for n, pc in zip(SHARDED, pieces):
        dep, r, c = pc.shape[1:]
        if n in COL_SHARDED:
            full[n] = pc.transpose(1, 2, 0, 3).reshape(dep, r, N_DEV * c)
        else:
            full[n] = pc.transpose(1, 0, 2, 3).reshape(dep, N_DEV * r, c)
    return full


def _layer_fwd(x, mod, ops, s5, small, rope, pos, dims):
    d, sw, ql, kvl, heads, dff = dims['d'], dims['sw'], dims['ql'], dims['kvl'], dims['heads'], dims['dff']
    zo = dims['zoff']
    hw = heads * HEAD_PAD
    cos, sin = rope
    sh1, sc1, g1, sh2, sc2, g2 = mod
    bf = lambda a: a.astype(BF16)
    sv = dict(x=x)
    (h1,) = _rowwise("norm1_fwd", _f_norm_mod, [x], [small['norm1_g'], sc1, sh1], [(d, BF16)])
    z = _mm("w_in_fwd", h1, bf(ops['w_in']))
    sv.update(h1=h1, z=z)
    a_re, a_im, bmat, cmat = s5
    bu = _mm("s5_bu_fwd", z, bf(bmat), a_col=(zo['u'], sw))
    hst = _scan_fwd(bu, a_re, a_im)
    ych = _mm("s5_y_fwd", hst, bf(cmat))
    (yg,) = _rowwise("s5_gelu_fwd", _f_gelu_in, [ych, (z, zo['u'], sw)], [small['ssm_d']], [(sw, F32)])
    pre = _mm("s5_glu_mm_fwd", yg, bf(ops['w_glu']))
    (s5o,) = _rowwise("s5_glu_fwd", _f_glu, [yg, pre], [small['b_glu']], [(sw, BF16)])
    ya = _mm("s5_out_fwd", s5o, bf(ops['w_a_out']))
    sv.update(hst=hst, ych=ych, yg=yg, pre=pre, s5o=s5o, ya=ya)
    (cq,) = _rowwise("q_norm_fwd", _f_rms, [(z, zo['cq'], ql)], [small['q_norm_g']], [(ql, BF16)])
    qab = _mm("q_up_fwd", cq, bf(ops['wq']))
    (q,) = _rowwise("q_rope_fwd", _f_rope_q, [(qab, 0, hw), (qab, hw, hw), cos, sin], [], [(hw, BF16)])
    (ckv,) = _rowwise("kv_norm_fwd", _f_rms, [(z, zo['ckv'], kvl)], [small['kv_norm_g']], [(kvl, BF16)])
    knv = _mm("kv_up_fwd", ckv, bf(ops['wkv']), out_dtype=BF16)
    (k,) = _rowwise("k_rope_fwd", _f_rope_k,
                    [(knv, 0, hw), (z, zo['kra'], HEAD_PAD), (z, zo['krb'], HEAD_PAD), cos, sin], [], [(hw, BF16)])
    o, lse = _flash_fwd(q, k, knv, pos[0], pos[1], heads=heads, t=dims['tq'], scale=dims['scale'])
    yb = _mm("mla_out_fwd", o, bf(ops['wbo']))
    sv.update(cq=cq, q=q, ckv=ckv, knv=knv, k=k, o=o, lse=lse, yb=yb)
    (merged,) = _rowwise("merge_fwd", _f_merge, [(z, zo['ga'], d), (z, zo['gb'], d), ya, yb], [], [(d, BF16)])
    t1 = _mm("w_out_fwd", merged, bf(ops['w_out']))
    (x1,) = _rowwise("res1_fwd", _f_res, [x, t1], [g1], [(d, F32)])
    sv.update(merged=merged, t1=t1, x1=x1)
    (h2,) = _rowwise("norm2_fwd", _f_norm_mod, [x1], [small['norm2_g'], sc2, sh2], [(d, BF16)])
    ab = _mm("ffn_up_fwd", h2, bf(ops['wgu']))
    (f,) = _rowwise("swiglu_fwd", _f_swiglu, [(ab, 0, dff), (ab, dff, dff)], [], [(dff, BF16)], tm=128)
    t2 = _mm("ffn_down_fwd", f, bf(ops['w_down']))
    (x2,) = _rowwise("res2_fwd", _f_res, [x1, t2], [g2], [(d, F32)])
    sv.update(h2=h2, ab=ab, f=f, t2=t2)
    return x2, sv


def _layer_bwd(dx2, sv, mod, ops, s5, small, rope, pos, dims):
    d, sw, ql, kvl, heads, dff = dims['d'], dims['sw'], dims['ql'], dims['kvl'], dims['heads'], dims['dff']
    zo = dims['zoff']
    hw = heads * HEAD_PAD
    cos, sin = rope
    sh1, sc1, g1, sh2, sc2, g2 = mod
    a_re, a_im, bmat, cmat = s5
    z = sv['z']
    tr = lambda a: a.T.astype(BF16)
    gops, gsm = {}, {}
    dt2, dg2 = _res_bwd("res2_bwd", sv['t2'], g2, dx2)
    gops['w_down'] = _mm("ffn_down_dw", sv['f'], dt2, ta=True)
    df = _mm("ffn_down_dx", dt2, tr(ops['w_down']), out_dtype=BF16)
    ab = sv['ab']
    (dab_a, dab_b) = _rowwise("swiglu_bwd", _vjp_fn(_f_swiglu, 2, 1, (0, 1)),
                              [(ab, 0, dff), (ab, dff, dff), df], [], [(dff, BF16), (dff, BF16)], tm=128)
    dab = jnp.concatenate([dab_a, dab_b], axis=1)
    gops['wgu'] = _mm("ffn_up_dw", sv['h2'], dab, ta=True)
    dh2 = _mm("ffn_up_dx", dab, tr(ops['wgu']))
    dx1, dn2, dsc2, dsh2 = _norm_mod_bwd("norm2_bwd", sv['x1'], small['norm2_g'], sc2, sh2, dh2, dx2)
    gsm['norm2_g'] = dn2
    dt1, dg1 = _res_bwd("res1_bwd", sv['t1'], g1, dx1)
    gops['w_out'] = _mm("w_out_dw", sv['merged'], dt1, ta=True)
    dmerged = _mm("w_out_dx", dt1, tr(ops['w_out']), out_dtype=BF16)
    dga, dgb, dya, dyb = _rowwise(
        "merge_bwd", _vjp_fn(_f_merge, 4, 1, (0, 1, 2, 3)),
        [(z, zo['ga'], d), (z, zo['gb'], d), sv['ya'], sv['yb'], dmerged], [],
        [(d, BF16), (d, BF16), (d, BF16), (d, BF16)])
    gops['wbo'] = _mm("mla_out_dw", sv['o'], dyb, ta=True)
    do = _mm("mla_out_dx", dyb, tr(ops['wbo']), out_dtype=BF16)
    delta = _flash_delta(do, sv['o'], heads=heads, t=dims['tq'])
    dq, dk, dv = _flash_bwd(sv['q'], sv['k'], sv['knv'], do, sv['lse'], delta, pos[0], pos[1],
                            heads=heads, t=dims['tq'], scale=dims['scale'])
    def k_bwd(dkb, cosb, sinb):
        dkb = _f32(dkb)
        dkpe = dkb[:, 0:HEAD_PAD]
        for h in range(1, heads):
            dkpe = dkpe + dkb[:, h * HEAD_PAD:(h + 1) * HEAD_PAD]
        return dkpe * cosb, dkpe * sinb
    dkra, dkrb = _rowwise("k_rope_bwd", k_bwd, [dk, cos, sin], [], [(HEAD_PAD, BF16), (HEAD_PAD, BF16)])
    dknv = jnp.concatenate([dk, dv], axis=1)
    gops['wkv'] = _mm("kv_up_dw", sv['ckv'], dknv, ta=True)
    dckv = _mm("kv_up_dx", dknv, tr(ops['wkv']))
    dckv_in, dkvg = _rms_bwd("kv_norm_bwd", z, zo['ckv'], kvl, small['kv_norm_g'], dckv)
    gsm['kv_norm_g'] = dkvg
    def q_bwd(dqb, cosb, sinb):
        dqb = _f32(dqb)
        return dqb * jnp.tile(cosb, (1, heads)), dqb * jnp.tile(sinb, (1, heads))
    dqa, dqb_ = _rowwise("q_rope_bwd", q_bwd, [dq, cos, sin], [], [(hw, BF16), (hw, BF16)])
    dqab = jnp.concatenate([dqa, dqb_], axis=1)
    gops['wq'] = _mm("q_up_dw", sv['cq'], dqab, ta=True)
    dcq = _mm("q_up_dx", dqab, tr(ops['wq']))
    dcq_in, dqg = _rms_bwd("q_norm_bwd", z, zo['cq'], ql, small['q_norm_g'], dcq)
    gsm['q_norm_g'] = dqg
    gops['w_a_out'] = _mm("s5_out_dw", sv['s5o'], dya, ta=True)
    ds5o = _mm("s5_out_dx", dya, tr(ops['w_a_out']))

    def glu_bwd(yg, pre, ds, b):
        _, vjp = jax.vjp(_f_glu, _f32(yg), _f32(pre), b)
        dyg, dpre, db = vjp(_f32(ds))
        return dyg, dpre, db
    dyg_a, dpre, dbglu = _rowwise("s5_glu_bwd", glu_bwd, [sv['yg'], sv['pre'], ds5o], [small['b_glu']],
                                  [(sw, F32), (sw, BF16)], [sw])
    gsm['b_glu'] = dbglu
    gops['w_glu'] = _mm("s5_glu_mm_dw", sv['yg'], dpre, ta=True)
    dyg_b = _mm("s5_glu_mm_dx", dpre, tr(ops['w_glu']))

    def gelu_bwd(ych, u, dya_, dyb_, dvec):
        _, vjp = jax.vjp(_f_gelu_in, _f32(ych), _f32(u), dvec)
        dych, du, dd = vjp(_f32(dya_) + _f32(dyb_))
        return dych, du, dd
    dy, du_skip, dssm_d = _rowwise("s5_gelu_bwd", gelu_bwd, [sv['ych'], (z, zo['u'], sw), dyg_a, dyg_b],
                                   [small['ssm_d']], [(sw, BF16), (sw, F32)], [sw])
    gsm['ssm_d'] = dssm_d
    g_cmat = _mm("s5_y_dw", sv['hst'], dy, ta=True)
    dh = _mm("s5_y_dx", dy, tr(cmat))
    gst, dar, dai = _scan_bwd(dh, sv['hst'], a_re, a_im)
    g_bmat = _mm("s5_bu_dw", z, gst, ta=True, a_col=(zo['u'], sw))
    du_scan = _mm("s5_bu_dx", gst, tr(bmat))
    (du,) = _rowwise("s5_du_sum", lambda a, b: _f32(a) + _f32(b), [du_skip, du_scan], [], [(sw, BF16)])
    gs5 = (dar, dai, g_bmat, g_cmat)
    dz = jnp.concatenate([dga, dgb, du, dcq_in, dckv_in, dkra, dkrb], axis=1)
    gops['w_in'] = _mm("w_in_dw", sv['h1'], dz, ta=True)
    dh1 = _mm("w_in_dx", dz, tr(ops['w_in']))
    dx, dn1, dsc1, dsh1 = _norm_mod_bwd("norm1_bwd", sv['x'], small['norm1_g'], sc1, sh1, dh1, dx1)
    gsm['norm1_g'] = dn1
    dmod = (dsh1, dsc1, dg1, dsh2, dsc2, dg2)
    return dx, gops, gs5, gsm, dmod


def _res_bwd(name, t, g, dxo):
    def fn(tb, db, gb):
        db = _f32(db)
        return gb * db, jnp.sum(db * _f32(tb), axis=0, keepdims=True)
    return _rowwise(name, fn, [t, dxo], [g], [(t.shape[1], BF16)], [t.shape[1]])


def _norm_mod_bwd(name, x, g, sc, sh, dh, dres):
    def fn(xb, dhb, dresb, gb, scb, shb):
        _, vjp = jax.vjp(_f_norm_mod, _f32(xb), gb, scb, shb)
        dx, dg, dsc, dsh = vjp(_f32(dhb))
        return dx + _f32(dresb), dg, dsc, dsh
    w = x.shape[1]
    return _rowwise(name, fn, [x, dh, dres], [g, sc, sh], [(w, F32)], [w, w, w])


def _rms_bwd(name, z, off, w, g, dy):
    def fn(xb, dyb, gb):
        _, vjp = jax.vjp(_f_rms, _f32(xb), gb)
        dx, dg = vjp(_f32(dyb))
        return dx, dg
    return _rowwise(name, fn, [(z, off, w), dy], [g], [(w, BF16)], [w])


def _adamw_fn(w, g, m, v):
    m = ADAM_B1 * m + (1.0 - ADAM_B1) * g
    v = ADAM_B2 * v + (1.0 - ADAM_B2) * jnp.square(g)
    m_hat = m / (1.0 - ADAM_B1 ** ADAM_STEP)
    v_hat = v / (1.0 - ADAM_B2 ** ADAM_STEP)
    delta = -ADAM_LR * (m_hat / (jnp.sqrt(v_hat) + ADAM_EPS) + ADAM_WD * w)
    return delta, m, v


def _step(p, mom_m, mom_v, x, c, positions, loss_target):
    depth, d = p['norm1_g'].shape
    L = x.shape[1]
    sw = p['ssm_d'].shape[1]
    groups, states, gwidth = p['ssm_b_re'].shape[1:]
    ql, kvl = p['q_norm_g'].shape[1], p['kv_norm_g'].shape[1]
    heads = p['w_uk'].shape[2] * N_DEV // QK_NOPE
    dff = p['w_gate'].shape[2] * N_DEV
    ada_w = p['w_ada'].shape[2]
    zoff, o = {}, 0
    for nm, sz in (('ga', d), ('gb', d), ('u', sw), ('cq', ql), ('ckv', kvl), ('kra', HEAD_PAD), ('krb', HEAD_PAD)):
        assert o % sz == 0, (nm, o, sz)
        zoff[nm] = o
        o += sz
    dims = dict(d=d, sw=sw, ql=ql, kvl=kvl, heads=heads, dff=dff, zoff=zoff,
                tq=512 if L >= 2048 else 128, scale=(QK_NOPE + QK_ROPE) ** -0.5)
    x = x.reshape(L, d)
    tgt = loss_target.reshape(L, d)

    posf = positions.reshape(L).astype(F32)
    inv_freq = ROPE_BASE ** (-jnp.arange(0, QK_ROPE, 2, dtype=F32) / QK_ROPE)
    ang = posf[:, None] * inv_freq
    cs, sn = jnp.cos(ang), jnp.sin(ang)
    padr = HEAD_PAD - QK_NOPE - QK_ROPE
    cos = jnp.concatenate([jnp.ones((L, QK_NOPE), F32), cs, cs, jnp.zeros((L, padr), F32)], axis=1)
    sin = jnp.concatenate([jnp.zeros((L, QK_NOPE), F32), sn, sn, jnp.zeros((L, padr), F32)], axis=1)
    rope = (cos, sin)
    pos = (positions.reshape(L, 1), positions.reshape(1, L))

    shard_shapes = {n: p[n].shape for n in SHARDED}
    gathered = _exchange("gather_weights", _pack([p[n] for n in SHARDED], 16, BF16), all_to_all=False)

    def make_ops(g2d):
        full = _gathered_to_full(g2d, shard_shapes)
        return [_layer_operators({n: full[n][l] for n in SHARDED}, dims) for l in range(depth)]

    ops, ops_vjp = jax.vjp(make_ops, gathered.astype(F32))

    def make_s5(sp):
        return [_s5_operators(*[sp[n][l] for n in ('ssm_a_re', 'ssm_a_im', 'ssm_log_dt', 'ssm_b_re', 'ssm_b_im',
                                                   'ssm_c_re', 'ssm_c_im')]) for l in range(depth)]

    s5_names = ('ssm_a_re', 'ssm_a_im', 'ssm_log_dt', 'ssm_b_re', 'ssm_b_im', 'ssm_c_re', 'ssm_c_im')
    s5ops, s5_vjp = jax.vjp(make_s5, {n: p[n] for n in s5_names})

    c_all = _exchange("gather_c", jnp.pad(c, ((0, 7), (0, 0))), all_to_all=False)[:, 0, :]
    (c_act,) = _rowwise("c_silu", lambda a: jax.nn.silu(a), [jnp.pad(c_all, ((0, 8), (0, 0)))], [], [(d, F32)])
    w_ada_cat = p['w_ada'].transpose(1, 0, 2).reshape(d, depth * ada_w)
    mod_cols = _mm("ada_fwd", c_act, w_ada_cat)[:N_DEV]
    mod_rows = _exchange("a2a_mod", _pack_slabs(mod_cols), all_to_all=True)
    mod_mine = _unpack_slabs(mod_rows, depth * ada_w).reshape(N_DEV, depth, ada_w)
    mod_mine = mod_mine.transpose(1, 0, 2).reshape(depth, N_DEV * ada_w)
    (mod_full,) = _rowwise("ada_bias", lambda a, b: a + b, [mod_mine, p['b_ada']], [], [(6 * d, F32)])
    mods = [tuple(mod_full[l:l + 1, i * d:(i + 1) * d] for i in range(6)) for l in range(depth)]

    saved = []
    xl = x
    for l in range(depth):
        small = {n: p[n][l:l + 1] for n in ('norm1_g', 'ssm_d', 'b_glu', 'q_norm_g', 'kv_norm_g', 'norm2_g')}
        xl, sv = _layer_fwd(xl, mods[l], ops[l], s5ops[l], small, rope, pos, dims)
        saved.append((sv, small))

    def final_fn(xb, tb, gb):
        def lossf(xv, gv):
            e = _f_rms(xv, gv) - tb
            per_row = 0.5 * jnp.mean(e * e, axis=-1, keepdims=True)
            return jnp.sum(per_row, axis=0, keepdims=True)
        lv, vjp = jax.vjp(lossf, xb, gb)
        dxb, dgb = vjp(jnp.ones((1, 1), F32))
        return dxb, jnp.broadcast_to(lv, (1, LANES)), dgb
    dx, loss_vec, dfinal_g = _rowwise("final_loss", final_fn, [xl, tgt], [p['final_g'].reshape(1, d)],
                                      [(d, F32)], [LANES, d])

    g_ops, g_s5, g_small, dmods = [None] * depth, [None] * depth, [None] * depth, [None] * depth
    for l in reversed(range(depth)):
        sv, small = saved[l]
        dx, g_ops[l], g_s5[l], g_small[l], dmods[l] = _layer_bwd(dx, sv, mods[l], ops[l], s5ops[l], small,
                                                                rope, pos, dims)
    grad_x = dx.reshape(1, L, d)

    dmod_mine = jnp.stack([jnp.concatenate(dm, axis=1)[0] for dm in dmods])
    dmod_slabs = dmod_mine.reshape(depth, N_DEV, ada_w).transpose(1, 0, 2).reshape(N_DEV, depth * ada_w)
    dmod_cols = _unpack_slabs(_exchange("a2a_dmod", _pack_slabs(dmod_slabs), all_to_all=True), depth * ada_w)
    g_ada = _mm("ada_dw", c_act, jnp.pad(dmod_cols, ((0, 8), (0, 0))), ta=True)
    g_w_ada = g_ada.reshape(d, depth, ada_w).transpose(1, 0, 2)

    (g_gathered,) = ops_vjp(g_ops)
    g_recv = _exchange("a2a_grads", g_gathered, all_to_all=True)
    g_shard = _sum_slabs("sum_grads", g_recv)
    g_sharded = dict(zip(SHARDED, _unpack(g_shard, [shard_shapes[n] for n in SHARDED])))

    (g_s5p,) = s5_vjp(g_s5)
    part = dict(g_s5p)
    part['b_ada'] = dmod_mine
    for n in ('norm1_g', 'ssm_d', 'b_glu', 'q_norm_g', 'kv_norm_g', 'norm2_g'):
        part[n] = jnp.concatenate([g_small[l][n] for l in range(depth)], axis=0)
    part['final_g'] = dfinal_g.reshape(d)
    small_shapes = [p[n].shape for n in SMALL] + [(1,)]
    small_pack = _pack([part[n] for n in SMALL] + [loss_vec[0, 0:1]], 8)
    small_sum = _sum_slabs("sum_small", _exchange("gather_small", small_pack, all_to_all=False))
    small_list = _unpack(small_sum, small_shapes)
    g_rep = dict(zip(SMALL, small_list[:-1]))
    loss = small_list[-1].reshape(())

    grads = dict(g_sharded)
    grads['w_ada'] = g_w_ada
    grads.update(g_rep)
    order = SHARDED + ['w_ada'] + SMALL
    shapes = [p[n].shape for n in order]
    flat = [_pack([src[n] for n in order], 256) for src in (p, grads, mom_m, mom_v)]
    delta_f, m_f, v_f = _rowwise("adamw", _adamw_fn, flat, [], [(FLAT_W, F32)] * 3)
    delta = dict(zip(order, _unpack(delta_f, shapes)))
    new_m = dict(zip(order, _unpack(m_f, shapes)))
    new_v = dict(zip(order, _unpack(v_f, shapes)))
    return (loss, grad_x, *[grads[n] for n in WEIGHTS], *[delta[n] for n in WEIGHTS],
            *[new_m[n] for n in WEIGHTS], *[new_v[n] for n in WEIGHTS])


def _pack_slabs(a):
    n = a.shape[1]
    rows = _flat_rows(n, 8)
    return jnp.pad(a, ((0, 0), (0, rows * FLAT_W - n))).reshape(N_DEV, rows, FLAT_W)


def _unpack_slabs(a, n):
    return a.reshape(N_DEV, -1)[:, :n]


def kernel(x, c, positions, w_ada, b_ada, norm1_g, w_in, ssm_a_re, ssm_a_im, ssm_log_dt, ssm_b_re, ssm_b_im, ssm_c_re, ssm_c_im, ssm_d, w_glu, b_glu, w_a_out, q_norm_g, w_uq, kv_norm_g, w_uk, w_uv, w_b_out, w_out, norm2_g, w_gate, w_up, w_down, final_g, loss_target, m_w_ada, m_b_ada, m_norm1_g, m_w_in, m_ssm_a_re, m_ssm_a_im, m_ssm_log_dt, m_ssm_b_re, m_ssm_b_im, m_ssm_c_re, m_ssm_c_im, m_ssm_d, m_w_glu, m_b_glu, m_w_a_out, m_q_norm_g, m_w_uq, m_kv_norm_g, m_w_uk, m_w_uv, m_w_b_out, m_w_out, m_norm2_g, m_w_gate, m_w_up, m_w_down, m_final_g, v_w_ada, v_b_ada, v_norm1_g, v_w_in, v_ssm_a_re, v_ssm_a_im, v_ssm_log_dt, v_ssm_b_re, v_ssm_b_im, v_ssm_c_re, v_ssm_c_im, v_ssm_d, v_w_glu, v_b_glu, v_w_a_out, v_q_norm_g, v_w_uq, v_kv_norm_g, v_w_uk, v_w_uv, v_w_b_out, v_w_out, v_norm2_g, v_w_gate, v_w_up, v_w_down, v_final_g):
    given = dict(locals())
    p = {n: given[n] for n in WEIGHTS}
    mom_m = {n: given["m_" + n] for n in WEIGHTS}
    mom_v = {n: given["v_" + n] for n in WEIGHTS}
    return _step(p, mom_m, mom_v, x, c, positions, loss_target)
```

```python
import functools
import math

import jax
import jax.numpy as jnp
from jax import lax
from jax.experimental import pallas as pl
from jax.experimental.pallas import tpu as pltpu

F32 = jnp.float32
BF16 = jnp.bfloat16

N_DEV = 8
LANES = 128
FLAT_W = 1024
VMEM_LIMIT = 48 * 1024 * 1024
QK_NOPE, QK_ROPE, V_DIM = 64, 32, 64
HEAD_PAD = LANES
ROPE_BASE = 10000.0
EPS = 1e-6
ADAM_LR, ADAM_B1, ADAM_B2, ADAM_EPS, ADAM_WD, ADAM_STEP = 0.001, 0.9, 0.999, 1e-08, 0.01, 10
NEG = float(jnp.finfo(jnp.float32).min)

WEIGHTS = ['w_ada', 'b_ada', 'norm1_g', 'w_in', 'ssm_a_re', 'ssm_a_im', 'ssm_log_dt', 'ssm_b_re', 'ssm_b_im',
           'ssm_c_re', 'ssm_c_im', 'ssm_d', 'w_glu', 'b_glu', 'w_a_out', 'q_norm_g', 'w_uq', 'kv_norm_g', 'w_uk',
           'w_uv', 'w_b_out', 'w_out', 'norm2_g', 'w_gate', 'w_up', 'w_down', 'final_g']
COL_SHARDED = ['w_in', 'w_a_out', 'w_uq', 'w_uk', 'w_uv', 'w_b_out', 'w_gate', 'w_up']
ROW_SHARDED = ['w_glu', 'w_out', 'w_down']
SHARDED = COL_SHARDED + ROW_SHARDED
SMALL = ['b_ada', 'norm1_g', 'ssm_a_re', 'ssm_a_im', 'ssm_log_dt', 'ssm_b_re', 'ssm_b_im', 'ssm_c_re', 'ssm_c_im',
         'ssm_d', 'b_glu', 'q_norm_g', 'kv_norm_g', 'norm2_g', 'final_g']


def _cparams(sem):
    return pltpu.CompilerParams(dimension_semantics=sem, vmem_limit_bytes=VMEM_LIMIT)


def _pick(dim, pref, quantum=LANES):
    if dim <= pref:
        return dim
    t = (pref // quantum) * quantum
    while t >= quantum:
        if dim % t == 0:
            return t
        t -= quantum
    return dim


def _mm(name, a, b, *, ta=False, out_dtype=F32, a_col=None, b_col=None, tm=1408, tn=1408, tk=None):
    tk = tk or (512 if ta else 1024)
    a_off, a_w = a_col if a_col is not None else (0, a.shape[1])
    b_off, b_w = b_col if b_col is not None else (0, b.shape[1])
    if ta:
        kdim, m = a.shape[0], a_w
        assert b.shape[0] == kdim
    else:
        m, kdim = a.shape[0], a_w
        assert b.shape[0] == kdim, (name, a.shape, b.shape)
    n = b_w
    tm = _pick(m, tm, LANES if ta else 8)
    tn = _pick(n, tn)
    tk = _pick(kdim, tk, 8 if ta else LANES)
    nk = kdim // tk
    assert m % tm == 0 and n % tn == 0 and kdim % tk == 0, (name, m, n, kdim, tm, tn, tk)
    if ta:
        assert a_off % tm == 0 and b_off % tn == 0
        a_spec = pl.BlockSpec((tk, tm), lambda i, j, k: (k, i + a_off // tm))
        dims = (((0,), (0,)), ((), ()))
    else:
        assert a_off % tk == 0 and b_off % tn == 0
        a_spec = pl.BlockSpec((tm, tk), lambda i, j, k: (i, k + a_off // tk))
        dims = (((1,), (0,)), ((), ()))
    b_spec = pl.BlockSpec((tk, tn), lambda i, j, k: (k, j + b_off // tn))

    def prod(a_ref, b_ref):
        return lax.dot_general(a_ref[...].astype(BF16), b_ref[...].astype(BF16), dims, preferred_element_type=F32)

    def body_one(a_ref, b_ref, o_ref):
        o_ref[...] = prod(a_ref, b_ref).astype(o_ref.dtype)

    def body_acc(a_ref, b_ref, o_ref, acc_ref):
        @pl.when(pl.program_id(2) == 0)
        def _():
            acc_ref[...] = jnp.zeros_like(acc_ref)

        acc_ref[...] += prod(a_ref, b_ref)

        @pl.when(pl.program_id(2) == nk - 1)
        def _():
            o_ref[...] = acc_ref[...].astype(o_ref.dtype)

    return pl.pallas_call(
        body_one if nk == 1 else body_acc, name=name,
        out_shape=jax.ShapeDtypeStruct((m, n), out_dtype),
        grid=(m // tm, n // tn, nk),
        in_specs=[a_spec, b_spec],
        out_specs=pl.BlockSpec((tm, tn), lambda i, j, k: (i, j)),
        scratch_shapes=[] if nk == 1 else [pltpu.VMEM((tm, tn), F32)],
        compiler_params=_cparams(("parallel", "parallel", "arbitrary")),
    )(a, b)


def _rowwise(name, fn, rows, vecs, outs, reds=(), *, tm=256):
    rows = [(r, 0, r.shape[1]) if not isinstance(r, tuple) else r for r in rows]
    nrows = rows[0][0].shape[0]
    tm = _pick(nrows, tm, 8)
    assert nrows % tm == 0, (name, nrows, tm)
    nr, nv, no = len(rows), len(vecs), len(outs)
    in_specs = []
    for arr, off, w in rows:
        assert arr.shape[0] == nrows and off % w == 0, (name, arr.shape, off, w)
        in_specs.append(pl.BlockSpec((tm, w), functools.partial(lambda i, cb: (i, cb), cb=off // w)))
    for v in vecs:
        assert v.ndim == 2 and v.shape[0] == 1, (name, v.shape)
        in_specs.append(pl.BlockSpec(v.shape, lambda i: (0, 0)))
    out_shape = [jax.ShapeDtypeStruct((nrows, w), dt) for w, dt in outs]
    out_specs = [pl.BlockSpec((tm, w), lambda i: (i, 0)) for w, dt in outs]
    out_shape += [jax.ShapeDtypeStruct((1, w), F32) for w in reds]
    out_specs += [pl.BlockSpec((1, w), lambda i: (0, 0)) for w in reds]

    def body(*refs):
        rin, vin = refs[:nr], refs[nr:nr + nv]
        rout, rred = refs[nr + nv:nr + nv + no], refs[nr + nv + no:]
        res = fn(*[r[...] for r in rin], *[v[...] for v in vin])
        if not isinstance(res, (tuple, list)):
            res = (res,)
        assert len(res) == no + len(reds), (name, len(res))
        for r, val in zip(rout, res[:no]):
            r[...] = val.astype(r.dtype)
        if reds:
            @pl.when(pl.program_id(0) == 0)
            def _():
                for r in rred:
                    r[...] = jnp.zeros_like(r)

            for r, val in zip(rred, res[no:]):
                r[...] += val.astype(F32)

    res = pl.pallas_call(
        body, name=name, out_shape=out_shape, grid=(nrows // tm,),
        in_specs=in_specs, out_specs=out_specs,
        compiler_params=_cparams(("arbitrary",) if reds else ("parallel",)),
    )(*[r[0] for r in rows], *vecs)
    return res


def _f32(x):
    return x.astype(F32)


def _vjp_fn(f, n_in, n_cot, want):
    def fn(*args):
        ins = [_f32(a) for a in args[:n_in]]
        cots = tuple(_f32(a) for a in args[n_in:n_in + n_cot])
        _, vjp = jax.vjp(f, *ins)
        grads = vjp(cots if n_cot > 1 else cots[0])
        return tuple(grads[i] for i in want)
    return fn


def _f_rms(x, g):
    return (x * lax.rsqrt(jnp.mean(x * x, axis=-1, keepdims=True) + EPS)) * g


def _f_norm_mod(x, g, sc, sh):
    return _f_rms(x, g) * (1.0 + sc) + sh


def _f_gelu_in(ych, u, d):
    return jax.nn.gelu(ych + d * u)


def _f_glu(yg, pre, b):
    return yg * jax.nn.sigmoid(pre + b)


def _f_merge(ga, gb, ya, yb):
    return jax.nn.sigmoid(ga) * ya + jax.nn.sigmoid(gb) * yb


def _f_res(x, t, g):
    return x + g * t


def _f_swiglu(a, b):
    return jax.nn.silu(a) * b


def _f_rope_q(qa, qb, cos, sin):
    h = qa.shape[1] // HEAD_PAD
    return qa * jnp.tile(cos, (1, h)) + qb * jnp.tile(sin, (1, h))


def _f_rope_k(kn, kra, krb, cos, sin):
    h = kn.shape[1] // HEAD_PAD
    return kn + jnp.tile(kra * cos + krb * sin, (1, h))


def _scan_fwd(bu, a_re, a_im, *, tb=256, sc=512):
    L, gp2 = bu.shape
    gp = gp2 // 2
    tb = _pick(L, tb, 8)
    sc = _pick(gp, sc)
    nchunk = gp // sc

    def body(bu_ref, ar_ref, ai_ref, h_ref, carry):
        @pl.when(pl.program_id(0) == 0)
        def _():
            carry[...] = jnp.zeros_like(carry)

        for c in range(nchunk):
            re = pl.ds(c * sc, sc)
            im = pl.ds(gp + c * sc, sc)
            ar = ar_ref[:, re]
            ai = ai_ref[:, re]

            def step(t, hc):
                hr, hi = hc
                row = pl.ds(t, 1)
                nr = ar * hr - ai * hi + bu_ref[row, re]
                ni = ar * hi + ai * hr + bu_ref[row, im]
                h_ref[row, re] = nr
                h_ref[row, im] = ni
                return nr, ni

            hr, hi = lax.fori_loop(0, tb, step, (carry[0:1, re], carry[1:2, re]), unroll=8)
            carry[0:1, re] = hr
            carry[1:2, re] = hi

    return pl.pallas_call(
        body, name="s5_scan_fwd",
        out_shape=jax.ShapeDtypeStruct((L, gp2), F32),
        grid=(L // tb,),
        in_specs=[pl.BlockSpec((tb, gp2), lambda i: (i, 0)),
                  pl.BlockSpec((1, gp), lambda i: (0, 0)),
                  pl.BlockSpec((1, gp), lambda i: (0, 0))],
        out_specs=pl.BlockSpec((tb, gp2), lambda i: (i, 0)),
        scratch_shapes=[pltpu.VMEM((8, gp), F32)],
        compiler_params=_cparams(("arbitrary",)),
    )(bu, a_re, a_im)


def _scan_bwd(dh, h, a_re, a_im, *, tb=128, sc=512):
    L, gp2 = dh.shape
    gp = gp2 // 2
    tb = _pick(L, tb, 8)
    sc = _pick(gp, sc)
    nchunk = gp // sc
    nb = L // tb
    r8 = tb // 8

    def body(dh_ref, h_ref, hp_ref, ar_ref, ai_ref, g_ref, dar_ref, dai_ref, carry):
        i = pl.program_id(0)

        @pl.when(i == 0)
        def _():
            carry[...] = jnp.zeros_like(carry)
            dar_ref[...] = jnp.zeros_like(dar_ref)
            dai_ref[...] = jnp.zeros_like(dai_ref)

        first_block = (i == nb - 1)
        for c in range(nchunk):
            re = pl.ds(c * sc, sc)
            im = pl.ds(gp + c * sc, sc)
            ar = ar_ref[:, re]
            ai = ai_ref[:, re]

            def one(t, gr, gi, hpr, hpi, accr, acci):
                row = pl.ds(t, 1)
                ngr = dh_ref[row, re] + ar * gr + ai * gi
                ngi = dh_ref[row, im] - ai * gr + ar * gi
                g_ref[row, re] = ngr
                g_ref[row, im] = ngi
                accr = accr + ngr * hpr + ngi * hpi
                acci = acci + ngi * hpr - ngr * hpi
                return ngr, ngi, accr, acci

            def step(s, st):
                gr, gi, accr, acci = st
                t = tb - 1 - s
                prev = pl.ds(t - 1, 1)
                return one(t, gr, gi, h_ref[prev, re], h_ref[prev, im], accr, acci)

            zero = jnp.zeros((1, sc), F32)
            st = lax.fori_loop(0, tb - 1, step, (carry[0:1, re], carry[1:2, re], zero, zero), unroll=8)
            keep = jnp.where(first_block, 0.0, 1.0).astype(F32)
            hpr = hp_ref[7:8, re] * keep
            hpi = hp_ref[7:8, im] * keep
            gr, gi, accr, acci = one(0, st[0], st[1], hpr, hpi, st[2], st[3])
            carry[0:1, re] = gr
            carry[1:2, re] = gi
            dar_ref[:, re] += accr
            dai_ref[:, re] += acci

    rev = lambda i: (nb - 1 - i, 0)
    return pl.pallas_call(
        body, name="s5_scan_bwd",
        out_shape=[jax.ShapeDtypeStruct((L, gp2), F32),
                   jax.ShapeDtypeStruct((1, gp), F32), jax.ShapeDtypeStruct((1, gp), F32)],
        grid=(nb,),
        in_specs=[pl.BlockSpec((tb, gp2), rev),
                  pl.BlockSpec((tb, gp2), rev),
                  pl.BlockSpec((8, gp2), lambda i: (jnp.maximum((nb - 1 - i) * r8 - 1, 0), 0)),
                  pl.BlockSpec((1, gp), lambda i: (0, 0)),
                  pl.BlockSpec((1, gp), lambda i: (0, 0))],
        out_specs=[pl.BlockSpec((tb, gp2), rev),
                   pl.BlockSpec((1, gp), lambda i: (0, 0)),
                   pl.BlockSpec((1, gp), lambda i: (0, 0))],
        scratch_shapes=[pltpu.VMEM((8, gp), F32)],
        compiler_params=_cparams(("arbitrary",)),
    )(dh, h, h, a_re, a_im)


_NT = (((1,), (1,)), ((), ()))
_TN = (((0,), (0,)), ((), ()))


def _causal(s, t):
    row = lax.broadcasted_iota(jnp.int32, (t, t), 0)
    col = lax.broadcasted_iota(jnp.int32, (t, t), 1)
    return jnp.where(col <= row, s, NEG)


def _flash_fwd(q, k, knv, *, heads, t):
    L = q.shape[0]
    n = L // t
    rep = t // LANES

    def body(q_ref, k_ref, v_ref, o_ref, lse_ref, m_s, acc_s):
        qi, ki = pl.program_id(1), pl.program_id(2)
        lane = lax.broadcasted_iota(jnp.int32, (t, HEAD_PAD), 1)

        @pl.when(ki == 0)
        def _():
            m_s[...] = jnp.full(m_s.shape, NEG, F32)
            acc_s[...] = jnp.zeros_like(acc_s)

        def update(diagonal):
            s = lax.dot_general(q_ref[...], k_ref[...], _NT, preferred_element_type=F32)
            if diagonal:
                s = _causal(s, t)
            m_prev = m_s[...]
            m_next = jnp.maximum(m_prev, jnp.max(s, axis=1, keepdims=True))
            p = jnp.exp(s - jnp.tile(m_next, (1, rep)))
            vb = jnp.where(lane == V_DIM, jnp.ones((), BF16), v_ref[...])
            acc_s[...] = jnp.exp(m_prev - m_next) * acc_s[...] + jnp.dot(p.astype(BF16), vb,
                                                                         preferred_element_type=F32)
            m_s[...] = m_next

        @pl.when(ki < qi)
        def _():
            update(False)

        @pl.when(ki == qi)
        def _():
            update(True)
            acc = acc_s[...]
            l = jnp.sum(jnp.where(lane == V_DIM, acc, 0.0), axis=1, keepdims=True)
            o_ref[...] = jnp.where(lane == V_DIM, 0.0, acc * (1.0 / l)).astype(o_ref.dtype)
            lse_ref[0] = jnp.max(m_s[...], axis=1, keepdims=True) + jnp.log(l)

    kv_map = lambda h, qi, ki: (jnp.minimum(ki, qi), h)
    v_map = lambda h, qi, ki: (jnp.minimum(ki, qi), h + heads)
    return pl.pallas_call(
        body, name="mla_flash_fwd",
        out_shape=[jax.ShapeDtypeStruct((L, heads * HEAD_PAD), BF16),
                   jax.ShapeDtypeStruct((heads, L, 1), F32)],
        grid=(heads, n, n),
        in_specs=[pl.BlockSpec((t, HEAD_PAD), lambda h, qi, ki: (qi, h)),
                  pl.BlockSpec((t, HEAD_PAD), kv_map),
                  pl.BlockSpec((t, HEAD_PAD), v_map)],
        out_specs=[pl.BlockSpec((t, HEAD_PAD), lambda h, qi, ki: (qi, h)),
                   pl.BlockSpec((1, t, 1), lambda h, qi, ki: (h, qi, 0))],
        scratch_shapes=[pltpu.VMEM((t, LANES), F32), pltpu.VMEM((t, HEAD_PAD), F32)],
        compiler_params=_cparams(("parallel", "parallel", "arbitrary")),
    )(q, k, knv)


def _flash_delta(do, o, *, heads, t):
    L = do.shape[0]

    def body(do_ref, o_ref, d_ref):
        d_ref[0] = jnp.sum(do_ref[...].astype(F32) * o_ref[...].astype(F32), axis=1, keepdims=True)

    return pl.pallas_call(
        body, name="mla_flash_delta",
        out_shape=jax.ShapeDtypeStruct((heads, L, 1), F32),
        grid=(heads, L // t),
        in_specs=[pl.BlockSpec((t, HEAD_PAD), lambda h, i: (i, h)),
                  pl.BlockSpec((t, HEAD_PAD), lambda h, i: (i, h))],
        out_specs=pl.BlockSpec((1, t, 1), lambda h, i: (h, i, 0)),
        compiler_params=_cparams(("parallel", "parallel")),
    )(do, o)


def _flash_bwd(q, k, knv, do, lse, delta, *, heads, t):
    L = q.shape[0]
    n = L // t

    def body(q_ref, k_ref, v_ref, do_ref, lse_ref, dl_ref, dq_ref, dk_ref, dv_ref, dk_s, dv_s):
        ki, qi = pl.program_id(1), pl.program_id(2)

        @pl.when(qi == 0)
        def _():
            dk_s[...] = jnp.zeros_like(dk_s)
            dv_s[...] = jnp.zeros_like(dv_s)

        def update(diagonal):
            qb, kb, vb, dob = q_ref[...], k_ref[...], v_ref[...], do_ref[...]
            s = lax.dot_general(qb, kb, _NT, preferred_element_type=F32)
            if diagonal:
                s = _causal(s, t)
            p = jnp.exp(s - lse_ref[0])
            dv_s[...] += lax.dot_general(p.astype(BF16), dob, _TN, preferred_element_type=F32)
            dp = lax.dot_general(dob, vb, _NT, preferred_element_type=F32)
            ds = (p * (dp - dl_ref[0])).astype(BF16)
            dk_s[...] += lax.dot_general(ds, qb, _TN, preferred_element_type=F32)
            dqb = jnp.dot(ds, kb, preferred_element_type=F32)
            rows = pl.ds(pl.multiple_of(qi * t, t), t)

            @pl.when(ki == 0)
            def _():
                dq_ref[rows, :] = dqb

            @pl.when(ki > 0)
            def _():
                dq_ref[rows, :] += dqb

        @pl.when(qi > ki)
        def _():
            update(False)

        @pl.when(qi == ki)
        def _():
            update(True)

        @pl.when(qi == n - 1)
        def _():
            dk_ref[...] = dk_s[...].astype(dk_ref.dtype)
            dv_ref[...] = dv_s[...].astype(dv_ref.dtype)

    q_map = lambda h, ki, qi: (jnp.maximum(qi, ki), h)
    stat_map = lambda h, ki, qi: (h, jnp.maximum(qi, ki), 0)
    kv_map = lambda h, ki, qi: (ki, h)
    v_map = lambda h, ki, qi: (ki, h + heads)
    return pl.pallas_call(
        body, name="mla_flash_bwd",
        out_shape=[jax.ShapeDtypeStruct((L, heads * HEAD_PAD), F32),
                   jax.ShapeDtypeStruct((L, heads * HEAD_PAD), BF16),
                   jax.ShapeDtypeStruct((L, heads * HEAD_PAD), BF16)],
        grid=(heads, n, n),
        in_specs=[pl.BlockSpec((t, HEAD_PAD), q_map),
                  pl.BlockSpec((t, HEAD_PAD), kv_map),
                  pl.BlockSpec((t, HEAD_PAD), v_map),
                  pl.BlockSpec((t, HEAD_PAD), q_map),
                  pl.BlockSpec((1, t, 1), stat_map),
                  pl.BlockSpec((1, t, 1), stat_map)],
        out_specs=[pl.BlockSpec((L, HEAD_PAD), lambda h, ki, qi: (0, h)),
                   pl.BlockSpec((t, HEAD_PAD), kv_map),
                   pl.BlockSpec((t, HEAD_PAD), kv_map)],
        scratch_shapes=[pltpu.VMEM((t, HEAD_PAD), F32), pltpu.VMEM((t, HEAD_PAD), F32)],
        compiler_params=_cparams(("parallel", "arbitrary", "arbitrary")),
    )(q, k, knv, do, lse, delta)


def _peer(k):
    mx, my, mc = lax.axis_index("x"), lax.axis_index("y"), lax.axis_index("c")
    px = 1 - mx if (k >> 2) & 1 else mx
    py = 1 - my if (k >> 1) & 1 else my
    pc = 1 - mc if k & 1 else mc
    return (px, py, pc), 4 * px + 2 * py + pc


def _exchange(name, xs, all_to_all):
    n = len(xs)
    any_spec = pl.BlockSpec(memory_space=pl.ANY)
    npeer = N_DEV - 1

    def body(*refs):
        x_refs, o_refs = refs[:n], refs[n:2 * n]
        send_sems, recv_sems, local_sems = refs[2 * n:]
        _, me = _peer(0)
        mine = [x.at[me] if all_to_all else x for x in x_refs]
        local = [pltpu.make_async_copy(mine[i], o_refs[i].at[me], local_sems.at[i]) for i in range(n)]
        for cp in local:
            cp.start()
        sends = []
        for k in range(1, N_DEV):
            dev, idx = _peer(k)
            for i in range(n):
                cp = pltpu.make_async_remote_copy(
                    src_ref=x_refs[i].at[idx] if all_to_all else x_refs[i], dst_ref=o_refs[i].at[me],
                    send_sem=send_sems.at[i * npeer + k - 1], recv_sem=recv_sems.at[i * npeer + k - 1],
                    device_id=dev, device_id_type=pl.DeviceIdType.MESH)
                cp.start()
                sends.append(cp)
        for k in range(1, N_DEV):
            dev, idx = _peer(k)
            for i in range(n):
                pltpu.make_async_remote_copy(
                    src_ref=mine[i], dst_ref=o_refs[i].at[idx],
                    send_sem=send_sems.at[i * npeer + k - 1], recv_sem=recv_sems.at[i * npeer + k - 1],
                    device_id=dev, device_id_type=pl.DeviceIdType.MESH).wait_recv()
        for cp in sends:
            cp.wait_send()
        for cp in local:
            cp.wait()

    return pl.pallas_call(
        body, name=name,
        out_shape=[jax.ShapeDtypeStruct((N_DEV,) + tuple(x.shape[1:] if all_to_all else x.shape), x.dtype)
                   for x in xs],
        in_specs=[any_spec] * n, out_specs=[any_spec] * n,
        scratch_shapes=[pltpu.SemaphoreType.DMA((n * npeer,)), pltpu.SemaphoreType.DMA((n * npeer,)),
                        pltpu.SemaphoreType.DMA((n,))],
    )(*xs)


def _sum_slabs(name, x, *, tr=128):
    _, r, w = x.shape
    tr = _pick(r, tr, 8)

    def body(x_ref, o_ref):
        acc = x_ref[0]
        for j in range(1, N_DEV):
            acc = acc + x_ref[j]
        o_ref[...] = acc

    return pl.pallas_call(
        body, name=name, out_shape=jax.ShapeDtypeStruct((r, w), F32), grid=(r // tr,),
        in_specs=[pl.BlockSpec((N_DEV, tr, w), lambda i: (0, i, 0))],
        out_specs=pl.BlockSpec((tr, w), lambda i: (i, 0)),
        compiler_params=_cparams(("parallel",)),
    )(x)


def _adamw_fn(w, g, m, v):
    m = ADAM_B1 * m + (1.0 - ADAM_B1) * g
    v = ADAM_B2 * v + (1.0 - ADAM_B2) * jnp.square(g)
    m_hat = m / (1.0 - ADAM_B1 ** ADAM_STEP)
    v_hat = v / (1.0 - ADAM_B2 ** ADAM_STEP)
    delta = -ADAM_LR * (m_hat / (jnp.sqrt(v_hat) + ADAM_EPS) + ADAM_WD * w)
    return delta, m, v


def _adamw_sharded(name, w, recv, m, v, *, tr=128):
    rows, c = w.shape
    tr = _pick(rows, tr, 8)

    def body(w_ref, r_ref, m_ref, v_ref, g_ref, d_ref, mo_ref, vo_ref):
        g = r_ref[0]
        for j in range(1, N_DEV):
            g = g + r_ref[j]
        d, mn, vn = _adamw_fn(w_ref[...], g, m_ref[...], v_ref[...])
        g_ref[...] = g
        d_ref[...] = d
        mo_ref[...] = mn
        vo_ref[...] = vn

    blk = pl.BlockSpec((tr, c), lambda i: (i, 0))
    return pl.pallas_call(
        body, name=name, out_shape=[jax.ShapeDtypeStruct((rows, c), F32)] * 4, grid=(rows // tr,),
        in_specs=[blk, pl.BlockSpec((N_DEV, tr, c), lambda i: (0, i, 0)), blk, blk],
        out_specs=[blk] * 4,
        compiler_params=_cparams(("parallel",)),
    )(w, recv, m, v)


def _piece_rows(shape):
    return -(-math.prod(shape) // (8 * FLAT_W)) * 8


def _pack_rows(arrs):
    out = []
    for a in arrs:
        flat = a.reshape(-1)
        rows = _piece_rows(a.shape)
        out.append(jnp.pad(flat, (0, rows * FLAT_W - flat.shape[0])).reshape(rows, FLAT_W))
    return jnp.concatenate(out, axis=0)


def _unpack_rows(packed, shapes):
    out, r0 = [], 0
    for s in shapes:
        rows = _piece_rows(s)
        out.append(packed[r0:r0 + rows].reshape(-1)[:math.prod(s)].reshape(s))
        r0 += rows
    return out


def _pack_slabs(a):
    n = a.shape[1]
    rows = _piece_rows((n,))
    return jnp.pad(a, ((0, 0), (0, rows * FLAT_W - n))).reshape(N_DEV, rows, FLAT_W)


def _unpack_slabs(a, n):
    return a.reshape(N_DEV, -1)[:, :n]


def _s5_operators(a_re, a_im, log_dt, b_re, b_im, c_re, c_im):
    g, p, m = b_re.shape
    dt = jnp.exp(log_dt)[:, None]
    mag = jnp.exp(a_re * dt)
    abar_re = mag * jnp.cos(a_im * dt)
    abar_im = mag * jnp.sin(a_im * dt)
    den = a_re * a_re + a_im * a_im
    nr = abar_re - 1.0
    ni = abar_im
    coef_re = ((nr * a_re + ni * a_im) / den)[..., None]
    coef_im = ((ni * a_re - nr * a_im) / den)[..., None]
    bbar_re = coef_re * b_re - coef_im * b_im
    bbar_im = coef_re * b_im + coef_im * b_re
    eye = jnp.eye(g, dtype=F32)
    bmat_re = jnp.einsum('gpm,gh->gmhp', bbar_re, eye).reshape(g * m, g * p)
    bmat_im = jnp.einsum('gpm,gh->gmhp', bbar_im, eye).reshape(g * m, g * p)
    bmat = jnp.concatenate([bmat_re, bmat_im], axis=1)
    cmat_re = jnp.einsum('gmp,gh->hpgm', c_re, eye).reshape(g * p, g * m)
    cmat_im = jnp.einsum('gmp,gh->hpgm', -c_im, eye).reshape(g * p, g * m)
    cmat = jnp.concatenate([cmat_re, cmat_im], axis=0)
    return abar_re.reshape(1, g * p), abar_im.reshape(1, g * p), bmat, cmat


def _rot_cols(w):
    half = w.shape[-1] // 2
    return jnp.concatenate([-w[..., half:], w[..., :half]], axis=-1)


def _layer_operators(w, dims):
    d, sw, ql, kvl, heads, dff = dims['d'], dims['sw'], dims['ql'], dims['kvl'], dims['heads'], dims['dff']
    w_in = w['w_in']
    o = 0
    parts = {}
    for nm, sz in (('u', sw), ('cq', ql), ('ckv', kvl), ('kr', QK_ROPE), ('ga', d), ('gb', d)):
        parts[nm] = w_in[:, o:o + sz]
        o += sz
    zpad = lambda n: jnp.zeros((d, n), w_in.dtype)
    kra = jnp.concatenate([zpad(QK_NOPE), parts['kr'], zpad(HEAD_PAD - QK_NOPE - QK_ROPE)], axis=1)
    krb = jnp.concatenate([zpad(QK_NOPE), _rot_cols(parts['kr']), zpad(HEAD_PAD - QK_NOPE - QK_ROPE)], axis=1)
    w_in_x = jnp.concatenate([parts['ga'], parts['gb'], parts['u'], parts['cq'], parts['ckv'], kra, krb], axis=1)

    wq = w['w_uq'].reshape(ql, heads, QK_NOPE + QK_ROPE)
    qz = lambda n: jnp.zeros((ql, heads, n), wq.dtype)
    wq_a = jnp.concatenate([wq, qz(HEAD_PAD - QK_NOPE - QK_ROPE)], axis=2)
    wq_b = jnp.concatenate([qz(QK_NOPE), _rot_cols(wq[:, :, QK_NOPE:]), qz(HEAD_PAD - QK_NOPE - QK_ROPE)], axis=2)
    wq_x = jnp.concatenate([wq_a.reshape(ql, -1), wq_b.reshape(ql, -1)], axis=1)

    kz = lambda n: jnp.zeros((kvl, heads, n), w['w_uk'].dtype)
    wk = jnp.concatenate([w['w_uk'].reshape(kvl, heads, QK_NOPE), kz(HEAD_PAD - QK_NOPE)], axis=2)
    wv = jnp.concatenate([w['w_uv'].reshape(kvl, heads, V_DIM), kz(HEAD_PAD - V_DIM)], axis=2)
    wkv_x = jnp.concatenate([wk.reshape(kvl, -1), wv.reshape(kvl, -1)], axis=1)

    wbo = w['w_b_out'].reshape(heads, V_DIM, d)
    wbo_x = jnp.concatenate([wbo, jnp.zeros((heads, HEAD_PAD - V_DIM, d), wbo.dtype)], axis=1).reshape(-1, d)
    wgu = jnp.concatenate([w['w_gate'], w['w_up']], axis=1)
    return dict(w_in=w_in_x, w_glu=w['w_glu'], w_a_out=w['w_a_out'], wq=wq_x, wkv=wkv_x, wbo=wbo_x,
                w_out=w['w_out'], wgu=wgu, w_down=w['w_down'])


def _gathered_to_full(gathered):
    full = {}
    for n, pc in zip(SHARDED, gathered):
        dep, r, c = pc.shape[1:]
        if n in COL_SHARDED:
            full[n] = pc.transpose(1, 2, 0, 3).reshape(dep, r, N_DEV * c)
        else:
            full[n] = pc.transpose(1, 0, 2, 3).reshape(dep, N_DEV * r, c)
    return full


def _layer_fwd(x, mod, ops, s5, small, rope, dims):
    d, sw, ql, kvl, heads, dff = dims['d'], dims['sw'], dims['ql'], dims['kvl'], dims['heads'], dims['dff']
    zo = dims['zoff']
    hw = heads * HEAD_PAD
    cos, sin = rope
    sh1, sc1, g1, sh2, sc2, g2 = mod
    bf = lambda a: a.astype(BF16)
    sv = dict(x=x)
    (h1,) = _rowwise("norm1_fwd", _f_norm_mod, [x], [small['norm1_g'], sc1, sh1], [(d, BF16)])
    z = _mm("w_in_fwd", h1, bf(ops['w_in']))
    sv.update(h1=h1, z=z)
    a_re, a_im, bmat, cmat = s5
    bu = _mm("s5_bu_fwd", z, bf(bmat), a_col=(zo['u'], sw))
    hst = _scan_fwd(bu, a_re, a_im)
    ych = _mm("s5_y_fwd", hst, bf(cmat))
    (yg,) = _rowwise("s5_gelu_fwd", _f_gelu_in, [ych, (z, zo['u'], sw)], [small['ssm_d']], [(sw, F32)])
    pre = _mm("s5_glu_mm_fwd", yg, bf(ops['w_glu']))
    (s5o,) = _rowwise("s5_glu_fwd", _f_glu, [yg, pre], [small['b_glu']], [(sw, BF16)])
    ya = _mm("s5_out_fwd", s5o, bf(ops['w_a_out']))
    sv.update(hst=hst, ych=ych, yg=yg, pre=pre, s5o=s5o, ya=ya)
    (cq,) = _rowwise("q_norm_fwd", _f_rms, [(z, zo['cq'], ql)], [small['q_norm_g']], [(ql, BF16)])
    qab = _mm("q_up_fwd", cq, bf(ops['wq']))
    scale = dims['scale']
    (q,) = _rowwise("q_rope_fwd", lambda a, b, cb, sb: _f_rope_q(a, b, cb, sb) * scale,
                    [(qab, 0, hw), (qab, hw, hw), cos, sin], [], [(hw, BF16)])
    (ckv,) = _rowwise("kv_norm_fwd", _f_rms, [(z, zo['ckv'], kvl)], [small['kv_norm_g']], [(kvl, BF16)])
    knv = _mm("kv_up_fwd", ckv, bf(ops['wkv']), out_dtype=BF16)
    (k,) = _rowwise("k_rope_fwd", _f_rope_k,
                    [(knv, 0, hw), (z, zo['kra'], HEAD_PAD), (z, zo['krb'], HEAD_PAD), cos, sin], [], [(hw, BF16)])
    o, lse = _flash_fwd(q, k, knv, heads=heads, t=dims['tq'])
    yb = _mm("mla_out_fwd", o, bf(ops['wbo']))
    sv.update(cq=cq, q=q, ckv=ckv, knv=knv, k=k, o=o, lse=lse, yb=yb)
    (merged,) = _rowwise("merge_fwd", _f_merge, [(z, zo['ga'], d), (z, zo['gb'], d), ya, yb], [], [(d, BF16)])
    t1 = _mm("w_out_fwd", merged, bf(ops['w_out']))
    (x1,) = _rowwise("res1_fwd", _f_res, [x, t1], [g1], [(d, F32)])
    sv.update(merged=merged, t1=t1, x1=x1)
    (h2,) = _rowwise("norm2_fwd", _f_norm_mod, [x1], [small['norm2_g'], sc2, sh2], [(d, BF16)])
    ab = _mm("ffn_up_fwd", h2, bf(ops['wgu']))
    (f,) = _rowwise("swiglu_fwd", _f_swiglu, [(ab, 0, dff), (ab, dff, dff)], [], [(dff, BF16)], tm=128)
    t2 = _mm("ffn_down_fwd", f, bf(ops['w_down']))
    (x2,) = _rowwise("res2_fwd", _f_res, [x1, t2], [g2], [(d, F32)])
    sv.update(h2=h2, ab=ab, f=f, t2=t2)
    return x2, sv


def _layer_bwd(dx2, sv, mod, ops, s5, small, rope, dims):
    d, sw, ql, kvl, heads, dff = dims['d'], dims['sw'], dims['ql'], dims['kvl'], dims['heads'], dims['dff']
    zo = dims['zoff']
    hw = heads * HEAD_PAD
    cos, sin = rope
    sh1, sc1, g1, sh2, sc2, g2 = mod
    a_re, a_im, bmat, cmat = s5
    z = sv['z']
    tr = lambda a: a.T.astype(BF16)
    gops, gsm = {}, {}
    dt2, dg2 = _res_bwd("res2_bwd", sv['t2'], g2, dx2)
    gops['w_down'] = _mm("ffn_down_dw", sv['f'], dt2, ta=True)
    df = _mm("ffn_down_dx", dt2, tr(ops['w_down']), out_dtype=BF16)
    ab = sv['ab']
    (dab_a, dab_b) = _rowwise("swiglu_bwd", _vjp_fn(_f_swiglu, 2, 1, (0, 1)),
                              [(ab, 0, dff), (ab, dff, dff), df], [], [(dff, BF16), (dff, BF16)], tm=128)
    dab = jnp.concatenate([dab_a, dab_b], axis=1)
    gops['wgu'] = _mm("ffn_up_dw", sv['h2'], dab, ta=True)
    dh2 = _mm("ffn_up_dx", dab, tr(ops['wgu']))
    dx1, dn2, dsc2, dsh2 = _norm_mod_bwd("norm2_bwd", sv['x1'], small['norm2_g'], sc2, sh2, dh2, dx2)
    gsm['norm2_g'] = dn2
    dt1, dg1 = _res_bwd("res1_bwd", sv['t1'], g1, dx1)
    gops['w_out'] = _mm("w_out_dw", sv['merged'], dt1, ta=True)
    dmerged = _mm("w_out_dx", dt1, tr(ops['w_out']), out_dtype=BF16)
    dga, dgb, dya, dyb = _rowwise(
        "merge_bwd", _vjp_fn(_f_merge, 4, 1, (0, 1, 2, 3)),
        [(z, zo['ga'], d), (z, zo['gb'], d), sv['ya'], sv['yb'], dmerged], [],
        [(d, BF16), (d, BF16), (d, BF16), (d, BF16)])
    gops['wbo'] = _mm("mla_out_dw", sv['o'], dyb, ta=True)
    do = _mm("mla_out_dx", dyb, tr(ops['wbo']), out_dtype=BF16)
    delta = _flash_delta(do, sv['o'], heads=heads, t=dims['tq'])
    dq, dk, dv = _flash_bwd(sv['q'], sv['k'], sv['knv'], do, sv['lse'], delta, heads=heads, t=dims['tq'])
    def k_bwd(dkb, cosb, sinb):
        dkb = _f32(dkb)
        dkpe = dkb[:, 0:HEAD_PAD]
        for h in range(1, heads):
            dkpe = dkpe + dkb[:, h * HEAD_PAD:(h + 1) * HEAD_PAD]
        return dkpe * cosb, dkpe * sinb
    dkra, dkrb = _rowwise("k_rope_bwd", k_bwd, [dk, cos, sin], [], [(HEAD_PAD, BF16), (HEAD_PAD, BF16)])
    dknv = jnp.concatenate([dk, dv], axis=1)
    gops['wkv'] = _mm("kv_up_dw", sv['ckv'], dknv, ta=True)
    dckv = _mm("kv_up_dx", dknv, tr(ops['wkv']))
    dckv_in, dkvg = _rms_bwd("kv_norm_bwd", z, zo['ckv'], kvl, small['kv_norm_g'], dckv)
    gsm['kv_norm_g'] = dkvg
    def q_bwd(dqb, cosb, sinb):
        dqb = _f32(dqb) * dims['scale']
        return dqb * jnp.tile(cosb, (1, heads)), dqb * jnp.tile(sinb, (1, heads))
    dqa, dqb_ = _rowwise("q_rope_bwd", q_bwd, [dq, cos, sin], [], [(hw, BF16), (hw, BF16)])
    dqab = jnp.concatenate([dqa, dqb_], axis=1)
    gops['wq'] = _mm("q_up_dw", sv['cq'], dqab, ta=True)
    dcq = _mm("q_up_dx", dqab, tr(ops['wq']))
    dcq_in, dqg = _rms_bwd("q_norm_bwd", z, zo['cq'], ql, small['q_norm_g'], dcq)
    gsm['q_norm_g'] = dqg
    gops['w_a_out'] = _mm("s5_out_dw", sv['s5o'], dya, ta=True)
    ds5o = _mm("s5_out_dx", dya, tr(ops['w_a_out']))

    def glu_bwd(yg, pre, ds, b):
        _, vjp = jax.vjp(_f_glu, _f32(yg), _f32(pre), b)
        dyg, dpre, db = vjp(_f32(ds))
        return dyg, dpre, db
    dyg_a, dpre, dbglu = _rowwise("s5_glu_bwd", glu_bwd, [sv['yg'], sv['pre'], ds5o], [small['b_glu']],
                                  [(sw, F32), (sw, BF16)], [sw])
    gsm['b_glu'] = dbglu
    gops['w_glu'] = _mm("s5_glu_mm_dw", sv['yg'], dpre, ta=True)
    dyg_b = _mm("s5_glu_mm_dx", dpre, tr(ops['w_glu']))

    def gelu_bwd(ych, u, dya_, dyb_, dvec):
        _, vjp = jax.vjp(_f_gelu_in, _f32(ych), _f32(u), dvec)
        dych, du, dd = vjp(_f32(dya_) + _f32(dyb_))
        return dych, du, dd
    dy, du_skip, dssm_d = _rowwise("s5_gelu_bwd", gelu_bwd, [sv['ych'], (z, zo['u'], sw), dyg_a, dyg_b],
                                   [small['ssm_d']], [(sw, BF16), (sw, F32)], [sw])
    gsm['ssm_d'] = dssm_d
    g_cmat = _mm("s5_y_dw", sv['hst'], dy, ta=True)
    dh = _mm("s5_y_dx", dy, tr(cmat))
    gst, dar, dai = _scan_bwd(dh, sv['hst'], a_re, a_im)
    g_bmat = _mm("s5_bu_dw", z, gst, ta=True, a_col=(zo['u'], sw))
    du_scan = _mm("s5_bu_dx", gst, tr(bmat))
    (du,) = _rowwise("s5_du_sum", lambda a, b: _f32(a) + _f32(b), [du_skip, du_scan], [], [(sw, BF16)])
    gs5 = (dar, dai, g_bmat, g_cmat)
    dz = jnp.concatenate([dga, dgb, du, dcq_in, dckv_in, dkra, dkrb], axis=1)
    gops['w_in'] = _mm("w_in_dw", sv['h1'], dz, ta=True)
    dh1 = _mm("w_in_dx", dz, tr(ops['w_in']))
    dx, dn1, dsc1, dsh1 = _norm_mod_bwd("norm1_bwd", sv['x'], small['norm1_g'], sc1, sh1, dh1, dx1)
    gsm['norm1_g'] = dn1
    dmod = (dsh1, dsc1, dg1, dsh2, dsc2, dg2)
    return dx, gops, gs5, gsm, dmod


def _res_bwd(name, t, g, dxo):
    def fn(tb, db, gb):
        db = _f32(db)
        return gb * db, jnp.sum(db * _f32(tb), axis=0, keepdims=True)
    return _rowwise(name, fn, [t, dxo], [g], [(t.shape[1], BF16)], [t.shape[1]])


def _norm_mod_bwd(name, x, g, sc, sh, dh, dres):
    def fn(xb, dhb, dresb, gb, scb, shb):
        _, vjp = jax.vjp(_f_norm_mod, _f32(xb), gb, scb, shb)
        dx, dg, dsc, dsh = vjp(_f32(dhb))
        return dx + _f32(dresb), dg, dsc, dsh
    w = x.shape[1]
    return _rowwise(name, fn, [x, dh, dres], [g, sc, sh], [(w, F32)], [w, w, w])


def _rms_bwd(name, z, off, w, g, dy):
    def fn(xb, dyb, gb):
        _, vjp = jax.vjp(_f_rms, _f32(xb), gb)
        dx, dg = vjp(_f32(dyb))
        return dx, dg
    return _rowwise(name, fn, [(z, off, w), dy], [g], [(w, BF16)], [w])


S5_NAMES = ('ssm_a_re', 'ssm_a_im', 'ssm_log_dt', 'ssm_b_re', 'ssm_b_im', 'ssm_c_re', 'ssm_c_im')
LAYER_VECS = ('norm1_g', 'ssm_d', 'b_glu', 'q_norm_g', 'kv_norm_g', 'norm2_g')


def _step(p, mom_m, mom_v, x, c, positions, loss_target):
    depth, d = p['norm1_g'].shape
    L = x.shape[1]
    sw = p['ssm_d'].shape[1]
    ql, kvl = p['q_norm_g'].shape[1], p['kv_norm_g'].shape[1]
    heads = p['w_uk'].shape[2] * N_DEV // QK_NOPE
    dff = p['w_gate'].shape[2] * N_DEV
    ada_w = p['w_ada'].shape[2]
    zoff, o = {}, 0
    for nm, sz in (('ga', d), ('gb', d), ('u', sw), ('cq', ql), ('ckv', kvl), ('kra', HEAD_PAD), ('krb', HEAD_PAD)):
        assert o % sz == 0, (nm, o, sz)
        zoff[nm] = o
        o += sz
    dims = dict(d=d, sw=sw, ql=ql, kvl=kvl, heads=heads, dff=dff, zoff=zoff,
                tq=512 if L >= 2048 else 128, scale=(QK_NOPE + QK_ROPE) ** -0.5)
    x = x.reshape(L, d)
    tgt = loss_target.reshape(L, d)

    posf = positions.reshape(L).astype(F32)
    inv_freq = ROPE_BASE ** (-jnp.arange(0, QK_ROPE, 2, dtype=F32) / QK_ROPE)
    ang = posf[:, None] * inv_freq
    cs, sn = jnp.cos(ang), jnp.sin(ang)
    padr = HEAD_PAD - QK_NOPE - QK_ROPE
    cos = jnp.concatenate([jnp.ones((L, QK_NOPE), F32), cs, cs, jnp.zeros((L, padr), F32)], axis=1)
    sin = jnp.concatenate([jnp.zeros((L, QK_NOPE), F32), sn, sn, jnp.zeros((L, padr), F32)], axis=1)
    rope = (cos, sin)

    gathered = _exchange("gather_weights", [p[n].astype(BF16) for n in SHARDED], all_to_all=False)

    def make_ops(gl):
        full = _gathered_to_full(gl)
        return [_layer_operators({n: full[n][l] for n in SHARDED}, dims) for l in range(depth)]

    ops, ops_vjp = jax.vjp(make_ops, [g.astype(F32) for g in gathered])

    def make_s5(sp):
        return [_s5_operators(*[sp[n][l] for n in S5_NAMES]) for l in range(depth)]

    s5ops, s5_vjp = jax.vjp(make_s5, {n: p[n] for n in S5_NAMES})

    (c_slabs,) = _exchange("gather_c", [jnp.pad(c, ((0, 7), (0, 0)))], all_to_all=False)
    c_all = c_slabs[:, 0, :]
    (c_act,) = _rowwise("c_silu", lambda a: jax.nn.silu(a), [jnp.pad(c_all, ((0, 8), (0, 0)))], [], [(d, F32)])
    w_ada_cat = p['w_ada'].transpose(1, 0, 2).reshape(d, depth * ada_w)
    mod_cols = _mm("ada_fwd", c_act, w_ada_cat)[:N_DEV]
    (mod_rows,) = _exchange("a2a_mod", [_pack_slabs(mod_cols)], all_to_all=True)
    mod_mine = _unpack_slabs(mod_rows, depth * ada_w).reshape(N_DEV, depth, ada_w)
    mod_mine = mod_mine.transpose(1, 0, 2).reshape(depth, N_DEV * ada_w)
    (mod_full,) = _rowwise("ada_bias", lambda a, b: a + b, [mod_mine, p['b_ada']], [], [(6 * d, F32)])
    mods = [tuple(mod_full[l:l + 1, i * d:(i + 1) * d] for i in range(6)) for l in range(depth)]

    saved = []
    xl = x
    for l in range(depth):
        small = {n: p[n][l:l + 1] for n in LAYER_VECS}
        xl, sv = _layer_fwd(xl, mods[l], ops[l], s5ops[l], small, rope, dims)
        saved.append((sv, small))

    def final_fn(xb, tb, gb):
        def lossf(xv, gv):
            e = _f_rms(xv, gv) - tb
            per_row = 0.5 * jnp.mean(e * e, axis=-1, keepdims=True)
            return jnp.sum(per_row, axis=0, keepdims=True)
        lv, vjp = jax.vjp(lossf, xb, gb)
        dxb, dgb = vjp(jnp.ones((1, 1), F32))
        return dxb, jnp.broadcast_to(lv, (1, LANES)), dgb
    dx, loss_vec, dfinal_g = _rowwise("final_loss", final_fn, [xl, tgt], [p['final_g'].reshape(1, d)],
                                      [(d, F32)], [LANES, d])

    g_ops, g_s5, g_small, dmods = [None] * depth, [None] * depth, [None] * depth, [None] * depth
    for l in reversed(range(depth)):
        sv, small = saved[l]
        dx, g_ops[l], g_s5[l], g_small[l], dmods[l] = _layer_bwd(dx, sv, mods[l], ops[l], s5ops[l], small,
                                                                rope, dims)
    grad_x = dx.reshape(1, L, d)

    dmod_mine = jnp.stack([jnp.concatenate(dm, axis=1)[0] for dm in dmods])
    dmod_slabs = dmod_mine.reshape(depth, N_DEV, ada_w).transpose(1, 0, 2).reshape(N_DEV, depth * ada_w)
    (dmod_recv,) = _exchange("a2a_dmod", [_pack_slabs(dmod_slabs)], all_to_all=True)
    dmod_cols = _unpack_slabs(dmod_recv, depth * ada_w)
    g_ada = _mm("ada_dw", c_act, jnp.pad(dmod_cols, ((0, 8), (0, 0))), ta=True)
    grads, delta, new_m, new_v = {}, {}, {}, {}
    grads['w_ada'] = g_ada.reshape(d, depth, ada_w).transpose(1, 0, 2)
    two_d = lambda a: a.reshape(-1, a.shape[-1])
    res = _rowwise("adamw_w_ada", _adamw_fn, [two_d(a) for a in (p['w_ada'], grads['w_ada'], mom_m['w_ada'],
                                                                  mom_v['w_ada'])], [], [(ada_w, F32)] * 3)
    delta['w_ada'], new_m['w_ada'], new_v['w_ada'] = [r.reshape(p['w_ada'].shape) for r in res]

    (g_slabs,) = ops_vjp(g_ops)
    g_recv = _exchange("a2a_grads", list(g_slabs), all_to_all=True)
    for n, rv in zip(SHARDED, g_recv):
        shp = p[n].shape
        res = _adamw_sharded("adamw_" + n, two_d(p[n]), rv.reshape(N_DEV, -1, shp[-1]), two_d(mom_m[n]),
                             two_d(mom_v[n]))
        grads[n], delta[n], new_m[n], new_v[n] = [r.reshape(shp) for r in res]

    (g_s5p,) = s5_vjp(g_s5)
    part = dict(g_s5p)
    part['b_ada'] = dmod_mine
    for n in LAYER_VECS:
        part[n] = jnp.concatenate([g_small[l][n] for l in range(depth)], axis=0)
    part['final_g'] = dfinal_g.reshape(d)
    small_shapes = [p[n].shape for n in SMALL] + [(1,)]
    (small_recv,) = _exchange("gather_small", [_pack_rows([part[n] for n in SMALL] + [loss_vec[0, 0:1]])],
                              all_to_all=False)
    small_sum = _sum_slabs("sum_small", small_recv)
    small_list = _unpack_rows(small_sum, small_shapes)
    grads.update(zip(SMALL, small_list[:-1]))
    loss = small_list[-1].reshape(())
    dummy = [jnp.zeros((1,), F32)]
    res = _rowwise("adamw_small", _adamw_fn,
                   [_pack_rows([src[n] for n in SMALL] + dummy) for src in (p, )] + [small_sum] +
                   [_pack_rows([src[n] for n in SMALL] + dummy) for src in (mom_m, mom_v)], [], [(FLAT_W, F32)] * 3)
    for dst, r in zip((delta, new_m, new_v), res):
        dst.update(zip(SMALL, _unpack_rows(r, small_shapes)[:-1]))
    return (loss, grad_x, *[grads[n] for n in WEIGHTS], *[delta[n] for n in WEIGHTS],
            *[new_m[n] for n in WEIGHTS], *[new_v[n] for n in WEIGHTS])


def kernel(x, c, positions, w_ada, b_ada, norm1_g, w_in, ssm_a_re, ssm_a_im, ssm_log_dt, ssm_b_re, ssm_b_im, ssm_c_re, ssm_c_im, ssm_d, w_glu, b_glu, w_a_out, q_norm_g, w_uq, kv_norm_g, w_uk, w_uv, w_b_out, w_out, norm2_g, w_gate, w_up, w_down, final_g, loss_target, m_w_ada, m_b_ada, m_norm1_g, m_w_in, m_ssm_a_re, m_ssm_a_im, m_ssm_log_dt, m_ssm_b_re, m_ssm_b_im, m_ssm_c_re, m_ssm_c_im, m_ssm_d, m_w_glu, m_b_glu, m_w_a_out, m_q_norm_g, m_w_uq, m_kv_norm_g, m_w_uk, m_w_uv, m_w_b_out, m_w_out, m_norm2_g, m_w_gate, m_w_up, m_w_down, m_final_g, v_w_ada, v_b_ada, v_norm1_g, v_w_in, v_ssm_a_re, v_ssm_a_im, v_ssm_log_dt, v_ssm_b_re, v_ssm_b_im, v_ssm_c_re, v_ssm_c_im, v_ssm_d, v_w_glu, v_b_glu, v_w_a_out, v_q_norm_g, v_w_uq, v_kv_norm_g, v_w_uk, v_w_uv, v_w_b_out, v_w_out, v_norm2_g, v_w_gate, v_w_up, v_w_down, v_final_g):
    given = dict(locals())
    p = {n: given[n] for n in WEIGHTS}
    mom_m = {n: given["m_" + n] for n in WEIGHTS}
    mom_v = {n: given["v_" + n] for n in WEIGHTS}
    return _step(p, mom_m, mom_v, x, c, positions, loss_target)
```

```python
import functools
import math

import jax
import jax.numpy as jnp
from jax import lax
from jax.experimental import pallas as pl
from jax.experimental.pallas import tpu as pltpu

F32 = jnp.float32
BF16 = jnp.bfloat16

N_DEV = 8
LANES = 128
FLAT_W = 1024
VMEM_LIMIT = 48 * 1024 * 1024
QK_NOPE, QK_ROPE, V_DIM = 64, 32, 64
HEAD_PAD = LANES
ROPE_BASE = 10000.0
EPS = 1e-6
ADAM_LR, ADAM_B1, ADAM_B2, ADAM_EPS, ADAM_WD, ADAM_STEP = 0.001, 0.9, 0.999, 1e-08, 0.01, 10
NEG = float(jnp.finfo(jnp.float32).min)

WEIGHTS = ['w_ada', 'b_ada', 'norm1_g', 'w_in', 'ssm_a_re', 'ssm_a_im', 'ssm_log_dt', 'ssm_b_re', 'ssm_b_im',
           'ssm_c_re', 'ssm_c_im', 'ssm_d', 'w_glu', 'b_glu', 'w_a_out', 'q_norm_g', 'w_uq', 'kv_norm_g', 'w_uk',
           'w_uv', 'w_b_out', 'w_out', 'norm2_g', 'w_gate', 'w_up', 'w_down', 'final_g']
COL_SHARDED = ['w_in', 'w_a_out', 'w_uq', 'w_uk', 'w_uv', 'w_b_out', 'w_gate', 'w_up']
ROW_SHARDED = ['w_glu', 'w_out', 'w_down']
SHARDED = COL_SHARDED + ROW_SHARDED
SMALL = ['b_ada', 'norm1_g', 'ssm_a_re', 'ssm_a_im', 'ssm_log_dt', 'ssm_b_re', 'ssm_b_im', 'ssm_c_re', 'ssm_c_im',
         'ssm_d', 'b_glu', 'q_norm_g', 'kv_norm_g', 'norm2_g', 'final_g']


def _cparams(sem):
    return pltpu.CompilerParams(dimension_semantics=sem, vmem_limit_bytes=VMEM_LIMIT)


def _pick(dim, pref, quantum=LANES):
    if dim <= pref:
        return dim
    t = (pref // quantum) * quantum
    while t >= quantum:
        if dim % t == 0:
            return t
        t -= quantum
    return dim


def _mm(name, a, b, *, ta=False, out_dtype=F32, a_col=None, b_col=None, tm=1408, tn=1408, tk=None):
    tk = tk or (512 if ta else 1024)
    a_off, a_w = a_col if a_col is not None else (0, a.shape[1])
    b_off, b_w = b_col if b_col is not None else (0, b.shape[1])
    if ta:
        kdim, m = a.shape[0], a_w
        assert b.shape[0] == kdim
    else:
        m, kdim = a.shape[0], a_w
        assert b.shape[0] == kdim, (name, a.shape, b.shape)
    n = b_w
    tm = _pick(m, tm, LANES if ta else 8)
    tn = _pick(n, tn)
    tk = _pick(kdim, tk, 8 if ta else LANES)
    nk = kdim // tk
    assert m % tm == 0 and n % tn == 0 and kdim % tk == 0, (name, m, n, kdim, tm, tn, tk)
    if ta:
        assert a_off % tm == 0 and b_off % tn == 0
        a_spec = pl.BlockSpec((tk, tm), lambda i, j, k: (k, i + a_off // tm))
        dims = (((0,), (0,)), ((), ()))
    else:
        assert a_off % tk == 0 and b_off % tn == 0
        a_spec = pl.BlockSpec((tm, tk), lambda i, j, k: (i, k + a_off // tk))
        dims = (((1,), (0,)), ((), ()))
    b_spec = pl.BlockSpec((tk, tn), lambda i, j, k: (k, j + b_off // tn))

    def prod(a_ref, b_ref):
        return lax.dot_general(a_ref[...].astype(BF16), b_ref[...].astype(BF16), dims, preferred_element_type=F32)

    def body_one(a_ref, b_ref, o_ref):
        o_ref[...] = prod(a_ref, b_ref).astype(o_ref.dtype)

    def body_acc(a_ref, b_ref, o_ref, acc_ref):
        @pl.when(pl.program_id(2) == 0)
        def _():
            acc_ref[...] = jnp.zeros_like(acc_ref)

        acc_ref[...] += prod(a_ref, b_ref)

        @pl.when(pl.program_id(2) == nk - 1)
        def _():
            o_ref[...] = acc_ref[...].astype(o_ref.dtype)

    return pl.pallas_call(
        body_one if nk == 1 else body_acc, name=name,
        out_shape=jax.ShapeDtypeStruct((m, n), out_dtype),
        grid=(m // tm, n // tn, nk),
        in_specs=[a_spec, b_spec],
        out_specs=pl.BlockSpec((tm, tn), lambda i, j, k: (i, j)),
        scratch_shapes=[] if nk == 1 else [pltpu.VMEM((tm, tn), F32)],
        compiler_params=_cparams(("parallel", "parallel", "arbitrary")),
    )(a, b)


def _rowwise(name, fn, rows, vecs, outs, reds=(), *, tm=256):
    rows = [(r, 0, r.shape[1]) if not isinstance(r, tuple) else r for r in rows]
    nrows = rows[0][0].shape[0]
    tm = _pick(nrows, tm, 8)
    assert nrows % tm == 0, (name, nrows, tm)
    nr, nv, no = len(rows), len(vecs), len(outs)
    in_specs = []
    for arr, off, w in rows:
        assert arr.shape[0] == nrows and off % w == 0, (name, arr.shape, off, w)
        in_specs.append(pl.BlockSpec((tm, w), functools.partial(lambda i, cb: (i, cb), cb=off // w)))
    for v in vecs:
        assert v.ndim == 2 and v.shape[0] == 1, (name, v.shape)
        in_specs.append(pl.BlockSpec(v.shape, lambda i: (0, 0)))
    out_shape = [jax.ShapeDtypeStruct((nrows, w), dt) for w, dt in outs]
    out_specs = [pl.BlockSpec((tm, w), lambda i: (i, 0)) for w, dt in outs]
    out_shape += [jax.ShapeDtypeStruct((1, w), F32) for w in reds]
    out_specs += [pl.BlockSpec((1, w), lambda i: (0, 0)) for w in reds]

    def body(*refs):
        rin, vin = refs[:nr], refs[nr:nr + nv]
        rout, rred = refs[nr + nv:nr + nv + no], refs[nr + nv + no:]
        res = fn(*[r[...] for r in rin], *[v[...] for v in vin])
        if not isinstance(res, (tuple, list)):
            res = (res,)
        assert len(res) == no + len(reds), (name, len(res))
        for r, val in zip(rout, res[:no]):
            r[...] = val.astype(r.dtype)
        if reds:
            @pl.when(pl.program_id(0) == 0)
            def _():
                for r in rred:
                    r[...] = jnp.zeros_like(r)

            for r, val in zip(rred, res[no:]):
                r[...] += val.astype(F32)

    res = pl.pallas_call(
        body, name=name, out_shape=out_shape, grid=(nrows // tm,),
        in_specs=in_specs, out_specs=out_specs,
        compiler_params=_cparams(("arbitrary",) if reds else ("parallel",)),
    )(*[r[0] for r in rows], *vecs)
    return res


def _f32(x):
    return x.astype(F32)


def _vjp_fn(f, n_in, n_cot, want):
    def fn(*args):
        ins = [_f32(a) for a in args[:n_in]]
        cots = tuple(_f32(a) for a in args[n_in:n_in + n_cot])
        _, vjp = jax.vjp(f, *ins)
        grads = vjp(cots if n_cot > 1 else cots[0])
        return tuple(grads[i] for i in want)
    return fn


def _f_rms(x, g):
    return (x * lax.rsqrt(jnp.mean(x * x, axis=-1, keepdims=True) + EPS)) * g


def _f_norm_mod(x, g, sc, sh):
    return _f_rms(x, g) * (1.0 + sc) + sh


def _f_gelu_in(ych, u, d):
    return jax.nn.gelu(ych + d * u)


def _f_glu(yg, pre, b):
    return yg * jax.nn.sigmoid(pre + b)


def _f_merge(ga, gb, ya, yb):
    return jax.nn.sigmoid(ga) * ya + jax.nn.sigmoid(gb) * yb


def _f_res(x, t, g):
    return x + g * t


def _f_swiglu(a, b):
    return jax.nn.silu(a) * b


def _f_rope_q(qa, qb, cos, sin):
    h = qa.shape[1] // HEAD_PAD
    return qa * jnp.tile(cos, (1, h)) + qb * jnp.tile(sin, (1, h))


def _f_rope_k(kn, kra, krb, cos, sin):
    h = kn.shape[1] // HEAD_PAD
    return kn + jnp.tile(kra * cos + krb * sin, (1, h))


SUBLANES = 8


def _cpow(ar, ai, log2n):
    for _ in range(log2n):
        ar, ai = ar * ar - ai * ai, 2.0 * ar * ai
    return ar, ai


def _to_segments(x):
    L, w = x.shape
    return x.reshape(SUBLANES, L // SUBLANES, w).transpose(1, 0, 2).reshape(L, w)


def _from_segments(x):
    L, w = x.shape
    return x.reshape(L // SUBLANES, SUBLANES, w).transpose(1, 0, 2).reshape(L, w)


def _scan(x, a_re, a_im, *, reverse, h=None, tb=256, sc=512):
    L, gp2 = x.shape
    gp = gp2 // 2
    seg = L // SUBLANES
    assert L % SUBLANES == 0 and seg & (seg - 1) == 0, L
    tb = _pick(L, tb, SUBLANES)
    nb, nt = L // tb, tb // SUBLANES
    sc = _pick(gp, sc)
    nchunk = gp // sc
    sign = -1.0 if reverse else 1.0
    blk = (lambda i: (nb - 1 - i, 0)) if reverse else (lambda i: (i, 0))
    order = (lambda s: nt - 1 - s) if reverse else (lambda s: s)
    tile = lambda s: pl.ds(pl.multiple_of(s * SUBLANES, SUBLANES), SUBLANES)
    vec = pl.BlockSpec((1, gp), lambda i: (0, 0))
    state = pl.BlockSpec((SUBLANES, gp2), lambda i: (0, 0))

    def coeffs(ar_ref, ai_ref, cols):
        ar1, ai1 = ar_ref[:, cols], sign * ai_ref[:, cols]
        return ar1, ai1, jnp.broadcast_to(ar1, (SUBLANES, sc)), jnp.broadcast_to(ai1, (SUBLANES, sc))

    def ends_body(x_ref, ar_ref, ai_ref, e_ref):
        @pl.when(pl.program_id(0) == 0)
        def _():
            e_ref[...] = jnp.zeros_like(e_ref)

        for c in range(nchunk):
            re, im = pl.ds(c * sc, sc), pl.ds(gp + c * sc, sc)
            _, _, ar, ai = coeffs(ar_ref, ai_ref, re)

            def step(s, st):
                sr, si = st
                rows = tile(order(s))
                return ar * sr - ai * si + x_ref[rows, re], ar * si + ai * sr + x_ref[rows, im]

            sr, si = lax.fori_loop(0, nt, step, (e_ref[:, re], e_ref[:, im]), unroll=4)
            e_ref[:, re] = sr
            e_ref[:, im] = si

    ends = pl.pallas_call(
        ends_body, name="s5_scan_ends_" + ("bwd" if reverse else "fwd"),
        out_shape=jax.ShapeDtypeStruct((SUBLANES, gp2), F32), grid=(nb,),
        in_specs=[pl.BlockSpec((tb, gp2), blk), vec, vec], out_specs=state,
        compiler_params=_cparams(("arbitrary",)),
    )(x, a_re, a_im)

    with_grad = h is not None

    def write_body(*refs):
        if with_grad:
            x_ref, e_ref, ar_ref, ai_ref, h_ref, hp_ref, hl_ref, o_ref, dar_ref, dai_ref, st_ref = refs
        else:
            x_ref, e_ref, ar_ref, ai_ref, o_ref, st_ref = refs
        i = pl.program_id(0)

        @pl.when(i == 0)
        def _():
            for c in range(nchunk):
                re, im = pl.ds(c * sc, sc), pl.ds(gp + c * sc, sc)
                ar1, ai1, _, _ = coeffs(ar_ref, ai_ref, re)
                pr, pi = _cpow(ar1, ai1, seg.bit_length() - 1)
                cr = jnp.zeros((1, sc), F32)
                ci = jnp.zeros((1, sc), F32)
                for j in (reversed(range(SUBLANES)) if reverse else range(SUBLANES)):
                    st_ref[j:j + 1, re] = cr
                    st_ref[j:j + 1, im] = ci
                    cr, ci = (e_ref[j:j + 1, re] + pr * cr - pi * ci, e_ref[j:j + 1, im] + pr * ci + pi * cr)
            if with_grad:
                dar_ref[...] = jnp.zeros_like(dar_ref)
                dai_ref[...] = jnp.zeros_like(dai_ref)

        sub = lax.broadcasted_iota(jnp.int32, (SUBLANES, sc), 0)
        for c in range(nchunk):
            re, im = pl.ds(c * sc, sc), pl.ds(gp + c * sc, sc)
            _, _, ar, ai = coeffs(ar_ref, ai_ref, re)

            def advance(rows, sr, si):
                sr, si = ar * sr - ai * si + x_ref[rows, re], ar * si + ai * sr + x_ref[rows, im]
                o_ref[rows, re] = sr
                o_ref[rows, im] = si
                return sr, si

            if not with_grad:
                def step(s, st):
                    return advance(tile(order(s)), *st)

                sr, si = lax.fori_loop(0, nt, step, (st_ref[:, re], st_ref[:, im]), unroll=4)
            else:
                def grad(sr, si, hpr, hpi, accr, acci):
                    return accr + sr * hpr + si * hpi, acci + si * hpr - sr * hpi

                def step(s, st):
                    sr, si, accr, acci = st
                    t = nt - 1 - s
                    sr, si = advance(tile(t), sr, si)
                    prev = tile(t - 1)
                    return (sr, si) + grad(sr, si, h_ref[prev, re], h_ref[prev, im], accr, acci)

                zero = jnp.zeros((SUBLANES, sc), F32)
                sr, si, accr, acci = lax.fori_loop(0, nt - 1, step, (st_ref[:, re], st_ref[:, im], zero, zero),
                                                   unroll=4)
                sr, si = advance(tile(0), sr, si)
                first = (i == nb - 1)
                wrap_r = jnp.where(sub == 0, 0.0, pltpu.roll(hl_ref[:, re], 1, 0))
                wrap_i = jnp.where(sub == 0, 0.0, pltpu.roll(hl_ref[:, im], 1, 0))
                hpr = jnp.where(first, wrap_r, hp_ref[:, re])
                hpi = jnp.where(first, wrap_i, hp_ref[:, im])
                accr, acci = grad(sr, si, hpr, hpi, accr, acci)
                dar_ref[:, re] += jnp.sum(accr, axis=0, keepdims=True)
                dai_ref[:, re] += jnp.sum(acci, axis=0, keepdims=True)
            st_ref[:, re] = sr
            st_ref[:, im] = si

    big = pl.BlockSpec((tb, gp2), blk)
    in_specs = [big, state, vec, vec]
    operands = [x, ends, a_re, a_im]
    out_shape = [jax.ShapeDtypeStruct((L, gp2), F32)]
    out_specs = [big]
    if with_grad:
        in_specs += [big,
                     pl.BlockSpec((SUBLANES, gp2), lambda i: (jnp.maximum((nb - 1 - i) * nt - 1, 0), 0)),
                     pl.BlockSpec((SUBLANES, gp2), lambda i: (seg - 1, 0))]
        operands += [h, h, h]
        out_shape += [jax.ShapeDtypeStruct((1, gp), F32)] * 2
        out_specs += [vec, vec]
    res = pl.pallas_call(
        write_body, name="s5_scan_" + ("bwd" if reverse else "fwd"),
        out_shape=out_shape, grid=(nb,), in_specs=in_specs, out_specs=out_specs,
        scratch_shapes=[pltpu.VMEM((SUBLANES, gp2), F32)],
        compiler_params=_cparams(("arbitrary",)),
    )(*operands)
    return res if with_grad else res[0]


_NT = (((1,), (1,)), ((), ()))
_TN = (((0,), (0,)), ((), ()))


def _causal(s, t):
    row = lax.broadcasted_iota(jnp.int32, (t, t), 0)
    col = lax.broadcasted_iota(jnp.int32, (t, t), 1)
    return jnp.where(col <= row, s, NEG)


def _pair_tables(n, k_major):
    if k_major:
        pairs = [(qi, ki) for ki in range(n) for qi in range(ki, n)]
    else:
        pairs = [(qi, ki) for qi in range(n) for ki in range(qi + 1)]
    return (jnp.asarray([p[0] for p in pairs], jnp.int32), jnp.asarray([p[1] for p in pairs], jnp.int32))


def _flash_fwd(q, k, knv, *, heads, t):
    L = q.shape[0]
    n = L // t
    rep = t // LANES
    qtab, ktab = _pair_tables(n, k_major=False)

    def body(qt_ref, kt_ref, q_ref, k_ref, v_ref, o_ref, lse_ref, m_s, acc_s):
        step = pl.program_id(1)
        qi, ki = qt_ref[step], kt_ref[step]
        lane = lax.broadcasted_iota(jnp.int32, (t, HEAD_PAD), 1)

        @pl.when(ki == 0)
        def _():
            m_s[...] = jnp.full(m_s.shape, NEG, F32)
            acc_s[...] = jnp.zeros_like(acc_s)

        def update(diagonal):
            s = lax.dot_general(q_ref[...], k_ref[...], _NT, preferred_element_type=F32)
            if diagonal:
                s = _causal(s, t)
            m_prev = m_s[...]
            m_next = jnp.maximum(m_prev, jnp.max(s, axis=1, keepdims=True))
            p = jnp.exp(s - jnp.tile(m_next, (1, rep)))
            vb = jnp.where(lane == V_DIM, jnp.ones((), BF16), v_ref[...])
            acc_s[...] = jnp.exp(m_prev - m_next) * acc_s[...] + jnp.dot(p.astype(BF16), vb,
                                                                         preferred_element_type=F32)
            m_s[...] = m_next

        @pl.when(ki < qi)
        def _():
            update(False)

        @pl.when(ki == qi)
        def _():
            update(True)
            acc = acc_s[...]
            l = jnp.sum(jnp.where(lane == V_DIM, acc, 0.0), axis=1, keepdims=True)
            o_ref[...] = jnp.where(lane == V_DIM, 0.0, acc * (1.0 / l)).astype(o_ref.dtype)
            lse_ref[0] = jnp.max(m_s[...], axis=1, keepdims=True) + jnp.log(l)

    q_map = lambda h, s, qt, kt: (qt[s], h)
    kv_map = lambda h, s, qt, kt: (kt[s], h)
    v_map = lambda h, s, qt, kt: (kt[s], h + heads)
    return pl.pallas_call(
        body, name="mla_flash_fwd",
        out_shape=[jax.ShapeDtypeStruct((L, heads * HEAD_PAD), BF16),
                   jax.ShapeDtypeStruct((heads, L, 1), F32)],
        grid_spec=pltpu.PrefetchScalarGridSpec(
            num_scalar_prefetch=2, grid=(heads, qtab.shape[0]),
            in_specs=[pl.BlockSpec((t, HEAD_PAD), q_map),
                      pl.BlockSpec((t, HEAD_PAD), kv_map),
                      pl.BlockSpec((t, HEAD_PAD), v_map)],
            out_specs=[pl.BlockSpec((t, HEAD_PAD), q_map),
                       pl.BlockSpec((1, t, 1), lambda h, s, qt, kt: (h, qt[s], 0))],
            scratch_shapes=[pltpu.VMEM((t, LANES), F32), pltpu.VMEM((t, HEAD_PAD), F32)]),
        compiler_params=_cparams(("parallel", "arbitrary")),
    )(qtab, ktab, q, k, knv)


def _flash_delta(do, o, *, heads, t):
    L = do.shape[0]

    def body(do_ref, o_ref, d_ref):
        d_ref[0] = jnp.sum(do_ref[...].astype(F32) * o_ref[...].astype(F32), axis=1, keepdims=True)

    return pl.pallas_call(
        body, name="mla_flash_delta",
        out_shape=jax.ShapeDtypeStruct((heads, L, 1), F32),
        grid=(heads, L // t),
        in_specs=[pl.BlockSpec((t, HEAD_PAD), lambda h, i: (i, h)),
                  pl.BlockSpec((t, HEAD_PAD), lambda h, i: (i, h))],
        out_specs=pl.BlockSpec((1, t, 1), lambda h, i: (h, i, 0)),
        compiler_params=_cparams(("parallel", "parallel")),
    )(do, o)


def _flash_bwd(q, k, knv, do, lse, delta, *, heads, t):
    L = q.shape[0]
    n = L // t
    qtab, ktab = _pair_tables(n, k_major=True)

    def body(qt_ref, kt_ref, q_ref, k_ref, v_ref, do_ref, lse_ref, dl_ref, dq_ref, dk_ref, dv_ref, dk_s, dv_s):
        step = pl.program_id(1)
        qi, ki = qt_ref[step], kt_ref[step]

        @pl.when(qi == ki)
        def _():
            dk_s[...] = jnp.zeros_like(dk_s)
            dv_s[...] = jnp.zeros_like(dv_s)

        def update(diagonal):
            qb, kb, vb, dob = q_ref[...], k_ref[...], v_ref[...], do_ref[...]
            s = lax.dot_general(qb, kb, _NT, preferred_element_type=F32)
            if diagonal:
                s = _causal(s, t)
            p = jnp.exp(s - lse_ref[0])
            dv_s[...] += lax.dot_general(p.astype(BF16), dob, _TN, preferred_element_type=F32)
            dp = lax.dot_general(dob, vb, _NT, preferred_element_type=F32)
            ds = (p * (dp - dl_ref[0])).astype(BF16)
            dk_s[...] += lax.dot_general(ds, qb, _TN, preferred_element_type=F32)
            dqb = jnp.dot(ds, kb, preferred_element_type=F32)
            rows = pl.ds(pl.multiple_of(qi * t, t), t)

            @pl.when(ki == 0)
            def _():
                dq_ref[rows, :] = dqb

            @pl.when(ki > 0)
            def _():
                dq_ref[rows, :] += dqb

        @pl.when(qi > ki)
        def _():
            update(False)

        @pl.when(qi == ki)
        def _():
            update(True)

        @pl.when(qi == n - 1)
        def _():
            dk_ref[...] = dk_s[...].astype(dk_ref.dtype)
            dv_ref[...] = dv_s[...].astype(dv_ref.dtype)

    q_map = lambda h, s, qt, kt: (qt[s], h)
    stat_map = lambda h, s, qt, kt: (h, qt[s], 0)
    kv_map = lambda h, s, qt, kt: (kt[s], h)
    v_map = lambda h, s, qt, kt: (kt[s], h + heads)
    return pl.pallas_call(
        body, name="mla_flash_bwd",
        out_shape=[jax.ShapeDtypeStruct((L, heads * HEAD_PAD), F32),
                   jax.ShapeDtypeStruct((L, heads * HEAD_PAD), BF16),
                   jax.ShapeDtypeStruct((L, heads * HEAD_PAD), BF16)],
        grid_spec=pltpu.PrefetchScalarGridSpec(
            num_scalar_prefetch=2, grid=(heads, qtab.shape[0]),
            in_specs=[pl.BlockSpec((t, HEAD_PAD), q_map),
                      pl.BlockSpec((t, HEAD_PAD), kv_map),
                      pl.BlockSpec((t, HEAD_PAD), v_map),
                      pl.BlockSpec((t, HEAD_PAD), q_map),
                      pl.BlockSpec((1, t, 1), stat_map),
                      pl.BlockSpec((1, t, 1), stat_map)],
            out_specs=[pl.BlockSpec((L, HEAD_PAD), lambda h, s, qt, kt: (0, h)),
                       pl.BlockSpec((t, HEAD_PAD), kv_map),
                       pl.BlockSpec((t, HEAD_PAD), kv_map)],
            scratch_shapes=[pltpu.VMEM((t, HEAD_PAD), F32), pltpu.VMEM((t, HEAD_PAD), F32)]),
        compiler_params=_cparams(("parallel", "arbitrary")),
    )(qtab, ktab, q, k, knv, do, lse, delta)


def _peer(k):
    mx, my, mc = lax.axis_index("x"), lax.axis_index("y"), lax.axis_index("c")
    px = 1 - mx if (k >> 2) & 1 else mx
    py = 1 - my if (k >> 1) & 1 else my
    pc = 1 - mc if k & 1 else mc
    return (px, py, pc), 4 * px + 2 * py + pc


def _exchange(name, xs, all_to_all):
    n = len(xs)
    any_spec = pl.BlockSpec(memory_space=pl.ANY)
    npeer = N_DEV - 1

    def body(*refs):
        x_refs, o_refs = refs[:n], refs[n:2 * n]
        send_sems, recv_sems, local_sems = refs[2 * n:]
        _, me = _peer(0)
        mine = [x.at[me] if all_to_all else x for x in x_refs]
        local = [pltpu.make_async_copy(mine[i], o_refs[i].at[me], local_sems.at[i]) for i in range(n)]
        for cp in local:
            cp.start()
        sends = []
        for k in range(1, N_DEV):
            dev, idx = _peer(k)
            for i in range(n):
                cp = pltpu.make_async_remote_copy(
                    src_ref=x_refs[i].at[idx] if all_to_all else x_refs[i], dst_ref=o_refs[i].at[me],
                    send_sem=send_sems.at[i * npeer + k - 1], recv_sem=recv_sems.at[i * npeer + k - 1],
                    device_id=dev, device_id_type=pl.DeviceIdType.MESH)
                cp.start()
                sends.append(cp)
        for k in range(1, N_DEV):
            dev, idx = _peer(k)
            for i in range(n):
                pltpu.make_async_remote_copy(
                    src_ref=mine[i], dst_ref=o_refs[i].at[idx],
                    send_sem=send_sems.at[i * npeer + k - 1], recv_sem=recv_sems.at[i * npeer + k - 1],
                    device_id=dev, device_id_type=pl.DeviceIdType.MESH).wait_recv()
        for cp in sends:
            cp.wait_send()
        for cp in local:
            cp.wait()

    return pl.pallas_call(
        body, name=name,
        out_shape=[jax.ShapeDtypeStruct((N_DEV,) + tuple(x.shape[1:] if all_to_all else x.shape), x.dtype)
                   for x in xs],
        in_specs=[any_spec] * n, out_specs=[any_spec] * n,
        scratch_shapes=[pltpu.SemaphoreType.DMA((n * npeer,)), pltpu.SemaphoreType.DMA((n * npeer,)),
                        pltpu.SemaphoreType.DMA((n,))],
    )(*xs)


def _sum_slabs(name, x, *, tr=128):
    _, r, w = x.shape
    tr = _pick(r, tr, 8)

    def body(x_ref, o_ref):
        acc = x_ref[0]
        for j in range(1, N_DEV):
            acc = acc + x_ref[j]
        o_ref[...] = acc

    return pl.pallas_call(
        body, name=name, out_shape=jax.ShapeDtypeStruct((r, w), F32), grid=(r // tr,),
        in_specs=[pl.BlockSpec((N_DEV, tr, w), lambda i: (0, i, 0))],
        out_specs=pl.BlockSpec((tr, w), lambda i: (i, 0)),
        compiler_params=_cparams(("parallel",)),
    )(x)


def _adamw_fn(w, g, m, v):
    m = ADAM_B1 * m + (1.0 - ADAM_B1) * g
    v = ADAM_B2 * v + (1.0 - ADAM_B2) * jnp.square(g)
    m_hat = m / (1.0 - ADAM_B1 ** ADAM_STEP)
    v_hat = v / (1.0 - ADAM_B2 ** ADAM_STEP)
    delta = -ADAM_LR * (m_hat / (jnp.sqrt(v_hat) + ADAM_EPS) + ADAM_WD * w)
    return delta, m, v


def _adamw_sharded(name, w, recv, m, v, *, tr=128):
    rows, c = w.shape
    tr = _pick(rows, tr, 8)

    def body(w_ref, r_ref, m_ref, v_ref, g_ref, d_ref, mo_ref, vo_ref):
        g = r_ref[0].astype(F32)
        for j in range(1, N_DEV):
            g = g + r_ref[j].astype(F32)
        d, mn, vn = _adamw_fn(w_ref[...], g, m_ref[...], v_ref[...])
        g_ref[...] = g
        d_ref[...] = d
        mo_ref[...] = mn
        vo_ref[...] = vn

    blk = pl.BlockSpec((tr, c), lambda i: (i, 0))
    return pl.pallas_call(
        body, name=name, out_shape=[jax.ShapeDtypeStruct((rows, c), F32)] * 4, grid=(rows // tr,),
        in_specs=[blk, pl.BlockSpec((N_DEV, tr, c), lambda i: (0, i, 0)), blk, blk],
        out_specs=[blk] * 4,
        compiler_params=_cparams(("parallel",)),
    )(w, recv, m, v)


def _piece_rows(shape):
    return -(-math.prod(shape) // (8 * FLAT_W)) * 8


def _pack_rows(arrs):
    out = []
    for a in arrs:
        flat = a.reshape(-1)
        rows = _piece_rows(a.shape)
        out.append(jnp.pad(flat, (0, rows * FLAT_W - flat.shape[0])).reshape(rows, FLAT_W))
    return jnp.concatenate(out, axis=0)


def _unpack_rows(packed, shapes):
    out, r0 = [], 0
    for s in shapes:
        rows = _piece_rows(s)
        out.append(packed[r0:r0 + rows].reshape(-1)[:math.prod(s)].reshape(s))
        r0 += rows
    return out


def _pack_slabs(a):
    n = a.shape[1]
    rows = _piece_rows((n,))
    return jnp.pad(a, ((0, 0), (0, rows * FLAT_W - n))).reshape(N_DEV, rows, FLAT_W)


def _unpack_slabs(a, n):
    return a.reshape(N_DEV, -1)[:, :n]


def _s5_operators(a_re, a_im, log_dt, b_re, b_im, c_re, c_im):
    g, p, m = b_re.shape
    dt = jnp.exp(log_dt)[:, None]
    mag = jnp.exp(a_re * dt)
    abar_re = mag * jnp.cos(a_im * dt)
    abar_im = mag * jnp.sin(a_im * dt)
    den = a_re * a_re + a_im * a_im
    nr = abar_re - 1.0
    ni = abar_im
    coef_re = ((nr * a_re + ni * a_im) / den)[..., None]
    coef_im = ((ni * a_re - nr * a_im) / den)[..., None]
    bbar_re = coef_re * b_re - coef_im * b_im
    bbar_im = coef_re * b_im + coef_im * b_re
    eye = jnp.eye(g, dtype=F32)
    bmat_re = jnp.einsum('gpm,gh->gmhp', bbar_re, eye).reshape(g * m, g * p)
    bmat_im = jnp.einsum('gpm,gh->gmhp', bbar_im, eye).reshape(g * m, g * p)
    bmat = jnp.concatenate([bmat_re, bmat_im], axis=1)
    cmat_re = jnp.einsum('gmp,gh->hpgm', c_re, eye).reshape(g * p, g * m)
    cmat_im = jnp.einsum('gmp,gh->hpgm', -c_im, eye).reshape(g * p, g * m)
    cmat = jnp.concatenate([cmat_re, cmat_im], axis=0)
    return abar_re.reshape(1, g * p), abar_im.reshape(1, g * p), bmat, cmat


def _rot_cols(w):
    half = w.shape[-1] // 2
    return jnp.concatenate([-w[..., half:], w[..., :half]], axis=-1)


def _layer_operators(w, dims):
    d, sw, ql, kvl, heads, dff = dims['d'], dims['sw'], dims['ql'], dims['kvl'], dims['heads'], dims['dff']
    w_in = w['w_in']
    o = 0
    parts = {}
    for nm, sz in (('u', sw), ('cq', ql), ('ckv', kvl), ('kr', QK_ROPE), ('ga', d), ('gb', d)):
        parts[nm] = w_in[:, o:o + sz]
        o += sz
    zpad = lambda n: jnp.zeros((d, n), w_in.dtype)
    kra = jnp.concatenate([zpad(QK_NOPE), parts['kr'], zpad(HEAD_PAD - QK_NOPE - QK_ROPE)], axis=1)
    krb = jnp.concatenate([zpad(QK_NOPE), _rot_cols(parts['kr']), zpad(HEAD_PAD - QK_NOPE - QK_ROPE)], axis=1)
    w_in_x = jnp.concatenate([parts['ga'], parts['gb'], parts['u'], parts['cq'], parts['ckv'], kra, krb], axis=1)

    wq = w['w_uq'].reshape(ql, heads, QK_NOPE + QK_ROPE)
    qz = lambda n: jnp.zeros((ql, heads, n), wq.dtype)
    wq_a = jnp.concatenate([wq, qz(HEAD_PAD - QK_NOPE - QK_ROPE)], axis=2)
    wq_b = jnp.concatenate([qz(QK_NOPE), _rot_cols(wq[:, :, QK_NOPE:]), qz(HEAD_PAD - QK_NOPE - QK_ROPE)], axis=2)
    wq_x = jnp.concatenate([wq_a.reshape(ql, -1), wq_b.reshape(ql, -1)], axis=1)

    kz = lambda n: jnp.zeros((kvl, heads, n), w['w_uk'].dtype)
    wk = jnp.concatenate([w['w_uk'].reshape(kvl, heads, QK_NOPE), kz(HEAD_PAD - QK_NOPE)], axis=2)
    wv = jnp.concatenate([w['w_uv'].reshape(kvl, heads, V_DIM), kz(HEAD_PAD - V_DIM)], axis=2)
    wkv_x = jnp.concatenate([wk.reshape(kvl, -1), wv.reshape(kvl, -1)], axis=1)

    wbo = w['w_b_out'].reshape(heads, V_DIM, d)
    wbo_x = jnp.concatenate([wbo, jnp.zeros((heads, HEAD_PAD - V_DIM, d), wbo.dtype)], axis=1).reshape(-1, d)
    wgu = jnp.concatenate([w['w_gate'], w['w_up']], axis=1)
    return dict(w_in=w_in_x, w_glu=w['w_glu'], w_a_out=w['w_a_out'], wq=wq_x, wkv=wkv_x, wbo=wbo_x,
                w_out=w['w_out'], wgu=wgu, w_down=w['w_down'])


def _gathered_to_full(gathered):
    full = {}
    for n, pc in zip(SHARDED, gathered):
        dep, r, c = pc.shape[1:]
        if n in COL_SHARDED:
            full[n] = pc.transpose(1, 2, 0, 3).reshape(dep, r, N_DEV * c)
        else:
            full[n] = pc.transpose(1, 0, 2, 3).reshape(dep, N_DEV * r, c)
    return full


def _layer_fwd(x, mod, ops, s5, small, rope, dims):
    d, sw, ql, kvl, heads, dff = dims['d'], dims['sw'], dims['ql'], dims['kvl'], dims['heads'], dims['dff']
    zo = dims['zoff']
    hw = heads * HEAD_PAD
    cos, sin = rope
    sh1, sc1, g1, sh2, sc2, g2 = mod
    bf = lambda a: a.astype(BF16)
    sv = dict(x=x)
    (h1,) = _rowwise("norm1_fwd", _f_norm_mod, [x], [small['norm1_g'], sc1, sh1], [(d, BF16)])
    z = _mm("w_in_fwd", h1, bf(ops['w_in']))
    sv.update(h1=h1, z=z)
    a_re, a_im, bmat, cmat = s5
    u_seg = _to_segments(z[:, zo['u']:zo['u'] + sw])
    bu = _mm("s5_bu_fwd", u_seg, bf(bmat))
    hst = _scan(bu, a_re, a_im, reverse=False)
    ych = _from_segments(_mm("s5_y_fwd", hst, bf(cmat)))
    (yg,) = _rowwise("s5_gelu_fwd", _f_gelu_in, [ych, (z, zo['u'], sw)], [small['ssm_d']], [(sw, F32)])
    pre = _mm("s5_glu_mm_fwd", yg, bf(ops['w_glu']))
    (s5o,) = _rowwise("s5_glu_fwd", _f_glu, [yg, pre], [small['b_glu']], [(sw, BF16)])
    ya = _mm("s5_out_fwd", s5o, bf(ops['w_a_out']))
    sv.update(u_seg=u_seg, hst=hst, ych=ych, yg=yg, pre=pre, s5o=s5o, ya=ya)
    (cq,) = _rowwise("q_norm_fwd", _f_rms, [(z, zo['cq'], ql)], [small['q_norm_g']], [(ql, BF16)])
    qab = _mm("q_up_fwd", cq, bf(ops['wq']))
    scale = dims['scale']
    (q,) = _rowwise("q_rope_fwd", lambda a, b, cb, sb: _f_rope_q(a, b, cb, sb) * scale,
                    [(qab, 0, hw), (qab, hw, hw), cos, sin], [], [(hw, BF16)])
    (ckv,) = _rowwise("kv_norm_fwd", _f_rms, [(z, zo['ckv'], kvl)], [small['kv_norm_g']], [(kvl, BF16)])
    knv = _mm("kv_up_fwd", ckv, bf(ops['wkv']), out_dtype=BF16)
    (k,) = _rowwise("k_rope_fwd", _f_rope_k,
                    [(knv, 0, hw), (z, zo['kra'], HEAD_PAD), (z, zo['krb'], HEAD_PAD), cos, sin], [], [(hw, BF16)])
    o, lse = _flash_fwd(q, k, knv, heads=heads, t=dims['tq'])
    yb = _mm("mla_out_fwd", o, bf(ops['wbo']))
    sv.update(cq=cq, q=q, ckv=ckv, knv=knv, k=k, o=o, lse=lse, yb=yb)
    (merged,) = _rowwise("merge_fwd", _f_merge, [(z, zo['ga'], d), (z, zo['gb'], d), ya, yb], [], [(d, BF16)])
    t1 = _mm("w_out_fwd", merged, bf(ops['w_out']))
    (x1,) = _rowwise("res1_fwd", _f_res, [x, t1], [g1], [(d, F32)])
    sv.update(merged=merged, t1=t1, x1=x1)
    (h2,) = _rowwise("norm2_fwd", _f_norm_mod, [x1], [small['norm2_g'], sc2, sh2], [(d, BF16)])
    ab = _mm("ffn_up_fwd", h2, bf(ops['wgu']))
    (f,) = _rowwise("swiglu_fwd", _f_swiglu, [(ab, 0, dff), (ab, dff, dff)], [], [(dff, BF16)], tm=128)
    t2 = _mm("ffn_down_fwd", f, bf(ops['w_down']))
    (x2,) = _rowwise("res2_fwd", _f_res, [x1, t2], [g2], [(d, F32)])
    sv.update(h2=h2, ab=ab, f=f, t2=t2)
    return x2, sv


def _layer_bwd(dx2, sv, mod, ops, s5, small, rope, dims):
    d, sw, ql, kvl, heads, dff = dims['d'], dims['sw'], dims['ql'], dims['kvl'], dims['heads'], dims['dff']
    zo = dims['zoff']
    hw = heads * HEAD_PAD
    cos, sin = rope
    sh1, sc1, g1, sh2, sc2, g2 = mod
    a_re, a_im, bmat, cmat = s5
    z = sv['z']
    tr = lambda a: a.T.astype(BF16)
    gops, gsm = {}, {}
    dt2, dg2 = _res_bwd("res2_bwd", sv['t2'], g2, dx2)
    gops['w_down'] = _mm("ffn_down_dw", sv['f'], dt2, ta=True)
    df = _mm("ffn_down_dx", dt2, tr(ops['w_down']), out_dtype=BF16)
    ab = sv['ab']
    (dab_a, dab_b) = _rowwise("swiglu_bwd", _vjp_fn(_f_swiglu, 2, 1, (0, 1)),
                              [(ab, 0, dff), (ab, dff, dff), df], [], [(dff, BF16), (dff, BF16)], tm=128)
    dab = jnp.concatenate([dab_a, dab_b], axis=1)
    gops['wgu'] = _mm("ffn_up_dw", sv['h2'], dab, ta=True)
    dh2 = _mm("ffn_up_dx", dab, tr(ops['wgu']))
    dx1, dn2, dsc2, dsh2 = _norm_mod_bwd("norm2_bwd", sv['x1'], small['norm2_g'], sc2, sh2, dh2, dx2)
    gsm['norm2_g'] = dn2
    dt1, dg1 = _res_bwd("res1_bwd", sv['t1'], g1, dx1)
    gops['w_out'] = _mm("w_out_dw", sv['merged'], dt1, ta=True)
    dmerged = _mm("w_out_dx", dt1, tr(ops['w_out']), out_dtype=BF16)
    dga, dgb, dya, dyb = _rowwise(
        "merge_bwd", _vjp_fn(_f_merge, 4, 1, (0, 1, 2, 3)),
        [(z, zo['ga'], d), (z, zo['gb'], d), sv['ya'], sv['yb'], dmerged], [],
        [(d, BF16), (d, BF16), (d, BF16), (d, BF16)])
    gops['wbo'] = _mm("mla_out_dw", sv['o'], dyb, ta=True)
    do = _mm("mla_out_dx", dyb, tr(ops['wbo']), out_dtype=BF16)
    delta = _flash_delta(do, sv['o'], heads=heads, t=dims['tq'])
    dq, dk, dv = _flash_bwd(sv['q'], sv['k'], sv['knv'], do, sv['lse'], delta, heads=heads, t=dims['tq'])
    def k_bwd(dkb, cosb, sinb):
        dkb = _f32(dkb)
        dkpe = dkb[:, 0:HEAD_PAD]
        for h in range(1, heads):
            dkpe = dkpe + dkb[:, h * HEAD_PAD:(h + 1) * HEAD_PAD]
        return dkpe * cosb, dkpe * sinb
    dkra, dkrb = _rowwise("k_rope_bwd", k_bwd, [dk, cos, sin], [], [(HEAD_PAD, BF16), (HEAD_PAD, BF16)])
    dknv = jnp.concatenate([dk, dv], axis=1)
    gops['wkv'] = _mm("kv_up_dw", sv['ckv'], dknv, ta=True)
    dckv = _mm("kv_up_dx", dknv, tr(ops['wkv']))
    dckv_in, dkvg = _rms_bwd("kv_norm_bwd", z, zo['ckv'], kvl, small['kv_norm_g'], dckv)
    gsm['kv_norm_g'] = dkvg
    def q_bwd(dqb, cosb, sinb):
        dqb = _f32(dqb) * dims['scale']
        return dqb * jnp.tile(cosb, (1, heads)), dqb * jnp.tile(sinb, (1, heads))
    dqa, dqb_ = _rowwise("q_rope_bwd", q_bwd, [dq, cos, sin], [], [(hw, BF16), (hw, BF16)])
    dqab = jnp.concatenate([dqa, dqb_], axis=1)
    gops['wq'] = _mm("q_up_dw", sv['cq'], dqab, ta=True)
    dcq = _mm("q_up_dx", dqab, tr(ops['wq']))
    dcq_in, dqg = _rms_bwd("q_norm_bwd", z, zo['cq'], ql, small['q_norm_g'], dcq)
    gsm['q_norm_g'] = dqg
    gops['w_a_out'] = _mm("s5_out_dw", sv['s5o'], dya, ta=True)
    ds5o = _mm("s5_out_dx", dya, tr(ops['w_a_out']))

    def glu_bwd(yg, pre, ds, b):
        _, vjp = jax.vjp(_f_glu, _f32(yg), _f32(pre), b)
        dyg, dpre, db = vjp(_f32(ds))
        return dyg, dpre, db
    dyg_a, dpre, dbglu = _rowwise("s5_glu_bwd", glu_bwd, [sv['yg'], sv['pre'], ds5o], [small['b_glu']],
                                  [(sw, F32), (sw, BF16)], [sw])
    gsm['b_glu'] = dbglu
    gops['w_glu'] = _mm("s5_glu_mm_dw", sv['yg'], dpre, ta=True)
    dyg_b = _mm("s5_glu_mm_dx", dpre, tr(ops['w_glu']))

    def gelu_bwd(ych, u, dya_, dyb_, dvec):
        _, vjp = jax.vjp(_f_gelu_in, _f32(ych), _f32(u), dvec)
        dych, du, dd = vjp(_f32(dya_) + _f32(dyb_))
        return dych, du, dd
    dy, du_skip, dssm_d = _rowwise("s5_gelu_bwd", gelu_bwd, [sv['ych'], (z, zo['u'], sw), dyg_a, dyg_b],
                                   [small['ssm_d']], [(sw, BF16), (sw, F32)], [sw])
    gsm['ssm_d'] = dssm_d
    dy_seg = _to_segments(dy)
    g_cmat = _mm("s5_y_dw", sv['hst'], dy_seg, ta=True)
    dh = _mm("s5_y_dx", dy_seg, tr(cmat))
    gst, dar, dai = _scan(dh, a_re, a_im, reverse=True, h=sv['hst'])
    g_bmat = _mm("s5_bu_dw", sv['u_seg'], gst, ta=True)
    du_scan = _from_segments(_mm("s5_bu_dx", gst, tr(bmat)))
    (du,) = _rowwise("s5_du_sum", lambda a, b: _f32(a) + _f32(b), [du_skip, du_scan], [], [(sw, BF16)])
    gs5 = (dar, dai, g_bmat, g_cmat)
    dz = jnp.concatenate([dga, dgb, du, dcq_in, dckv_in, dkra, dkrb], axis=1)
    gops['w_in'] = _mm("w_in_dw", sv['h1'], dz, ta=True)
    dh1 = _mm("w_in_dx", dz, tr(ops['w_in']))
    dx, dn1, dsc1, dsh1 = _norm_mod_bwd("norm1_bwd", sv['x'], small['norm1_g'], sc1, sh1, dh1, dx1)
    gsm['norm1_g'] = dn1
    dmod = (dsh1, dsc1, dg1, dsh2, dsc2, dg2)
    return dx, gops, gs5, gsm, dmod


def _res_bwd(name, t, g, dxo):
    def fn(tb, db, gb):
        db = _f32(db)
        return gb * db, jnp.sum(db * _f32(tb), axis=0, keepdims=True)
    return _rowwise(name, fn, [t, dxo], [g], [(t.shape[1], BF16)], [t.shape[1]])


def _norm_mod_bwd(name, x, g, sc, sh, dh, dres):
    def fn(xb, dhb, dresb, gb, scb, shb):
        _, vjp = jax.vjp(_f_norm_mod, _f32(xb), gb, scb, shb)
        dx, dg, dsc, dsh = vjp(_f32(dhb))
        return dx + _f32(dresb), dg, dsc, dsh
    w = x.shape[1]
    return _rowwise(name, fn, [x, dh, dres], [g, sc, sh], [(w, F32)], [w, w, w])


def _rms_bwd(name, z, off, w, g, dy):
    def fn(xb, dyb, gb):
        _, vjp = jax.vjp(_f_rms, _f32(xb), gb)
        dx, dg = vjp(_f32(dyb))
        return dx, dg
    return _rowwise(name, fn, [(z, off, w), dy], [g], [(w, BF16)], [w])


S5_NAMES = ('ssm_a_re', 'ssm_a_im', 'ssm_log_dt', 'ssm_b_re', 'ssm_b_im', 'ssm_c_re', 'ssm_c_im')
LAYER_VECS = ('norm1_g', 'ssm_d', 'b_glu', 'q_norm_g', 'kv_norm_g', 'norm2_g')


def _step(p, mom_m, mom_v, x, c, positions, loss_target):
    depth, d = p['norm1_g'].shape
    L = x.shape[1]
    sw = p['ssm_d'].shape[1]
    ql, kvl = p['q_norm_g'].shape[1], p['kv_norm_g'].shape[1]
    heads = p['w_uk'].shape[2] * N_DEV // QK_NOPE
    dff = p['w_gate'].shape[2] * N_DEV
    ada_w = p['w_ada'].shape[2]
    zoff, o = {}, 0
    for nm, sz in (('ga', d), ('gb', d), ('u', sw), ('cq', ql), ('ckv', kvl), ('kra', HEAD_PAD), ('krb', HEAD_PAD)):
        assert o % sz == 0, (nm, o, sz)
        zoff[nm] = o
        o += sz
    dims = dict(d=d, sw=sw, ql=ql, kvl=kvl, heads=heads, dff=dff, zoff=zoff,
                tq=1024 if L >= 4096 else 128, scale=(QK_NOPE + QK_ROPE) ** -0.5)
    x = x.reshape(L, d)
    tgt = loss_target.reshape(L, d)

    posf = positions.reshape(L).astype(F32)
    inv_freq = ROPE_BASE ** (-jnp.arange(0, QK_ROPE, 2, dtype=F32) / QK_ROPE)
    ang = posf[:, None] * inv_freq
    cs, sn = jnp.cos(ang), jnp.sin(ang)
    padr = HEAD_PAD - QK_NOPE - QK_ROPE
    cos = jnp.concatenate([jnp.ones((L, QK_NOPE), F32), cs, cs, jnp.zeros((L, padr), F32)], axis=1)
    sin = jnp.concatenate([jnp.zeros((L, QK_NOPE), F32), sn, sn, jnp.zeros((L, padr), F32)], axis=1)
    rope = (cos, sin)

    gathered = _exchange("gather_weights", [p[n].astype(BF16) for n in SHARDED], all_to_all=False)

    def make_ops(gl):
        full = _gathered_to_full(gl)
        return [_layer_operators({n: full[n][l] for n in SHARDED}, dims) for l in range(depth)]

    ops, ops_vjp = jax.vjp(make_ops, [g.astype(F32) for g in gathered])

    def make_s5(sp):
        return [_s5_operators(*[sp[n][l] for n in S5_NAMES]) for l in range(depth)]

    s5ops, s5_vjp = jax.vjp(make_s5, {n: p[n] for n in S5_NAMES})

    (c_slabs,) = _exchange("gather_c", [jnp.pad(c, ((0, 7), (0, 0)))], all_to_all=False)
    c_all = c_slabs[:, 0, :]
    (c_act,) = _rowwise("c_silu", lambda a: jax.nn.silu(a), [jnp.pad(c_all, ((0, 8), (0, 0)))], [], [(d, F32)])
    w_ada_cat = p['w_ada'].transpose(1, 0, 2).reshape(d, depth * ada_w)
    mod_cols = _mm("ada_fwd", c_act, w_ada_cat)[:N_DEV]
    (mod_rows,) = _exchange("a2a_mod", [_pack_slabs(mod_cols)], all_to_all=True)
    mod_mine = _unpack_slabs(mod_rows, depth * ada_w).reshape(N_DEV, depth, ada_w)
    mod_mine = mod_mine.transpose(1, 0, 2).reshape(depth, N_DEV * ada_w)
    (mod_full,) = _rowwise("ada_bias", lambda a, b: a + b, [mod_mine, p['b_ada']], [], [(6 * d, F32)])
    mods = [tuple(mod_full[l:l + 1, i * d:(i + 1) * d] for i in range(6)) for l in range(depth)]

    saved = []
    xl = x
    for l in range(depth):
        small = {n: p[n][l:l + 1] for n in LAYER_VECS}
        xl, sv = _layer_fwd(xl, mods[l], ops[l], s5ops[l], small, rope, dims)
        saved.append((sv, small))

    def final_fn(xb, tb, gb):
        def lossf(xv, gv):
            e = _f_rms(xv, gv) - tb
            per_row = 0.5 * jnp.mean(e * e, axis=-1, keepdims=True)
            return jnp.sum(per_row, axis=0, keepdims=True)
        lv, vjp = jax.vjp(lossf, xb, gb)
        dxb, dgb = vjp(jnp.ones((1, 1), F32))
        return dxb, jnp.broadcast_to(lv, (1, LANES)), dgb
    dx, loss_vec, dfinal_g = _rowwise("final_loss", final_fn, [xl, tgt], [p['final_g'].reshape(1, d)],
                                      [(d, F32)], [LANES, d])

    g_ops, g_s5, g_small, dmods = [None] * depth, [None] * depth, [None] * depth, [None] * depth
    for l in reversed(range(depth)):
        sv, small = saved[l]
        dx, g_ops[l], g_s5[l], g_small[l], dmods[l] = _layer_bwd(dx, sv, mods[l], ops[l], s5ops[l], small,
                                                                rope, dims)
    grad_x = dx.reshape(1, L, d)

    dmod_mine = jnp.stack([jnp.concatenate(dm, axis=1)[0] for dm in dmods])
    dmod_slabs = dmod_mine.reshape(depth, N_DEV, ada_w).transpose(1, 0, 2).reshape(N_DEV, depth * ada_w)
    (dmod_recv,) = _exchange("a2a_dmod", [_pack_slabs(dmod_slabs)], all_to_all=True)
    dmod_cols = _unpack_slabs(dmod_recv, depth * ada_w)
    g_ada = _mm("ada_dw", c_act, jnp.pad(dmod_cols, ((0, 8), (0, 0))), ta=True)
    grads, delta, new_m, new_v = {}, {}, {}, {}
    grads['w_ada'] = g_ada.reshape(d, depth, ada_w).transpose(1, 0, 2)
    two_d = lambda a: a.reshape(-1, a.shape[-1])
    res = _rowwise("adamw_w_ada", _adamw_fn, [two_d(a) for a in (p['w_ada'], grads['w_ada'], mom_m['w_ada'],
                                                                  mom_v['w_ada'])], [], [(ada_w, F32)] * 3)
    delta['w_ada'], new_m['w_ada'], new_v['w_ada'] = [r.reshape(p['w_ada'].shape) for r in res]

    (g_slabs,) = ops_vjp(g_ops)
    g_recv = _exchange("a2a_grads", [g.astype(BF16) for g in g_slabs], all_to_all=True)
    for n, rv in zip(SHARDED, g_recv):
        shp = p[n].shape
        res = _adamw_sharded("adamw_" + n, two_d(p[n]), rv.reshape(N_DEV, -1, shp[-1]), two_d(mom_m[n]),
                             two_d(mom_v[n]))
        grads[n], delta[n], new_m[n], new_v[n] = [r.reshape(shp) for r in res]

    (g_s5p,) = s5_vjp(g_s5)
    part = dict(g_s5p)
    part['b_ada'] = dmod_mine
    for n in LAYER_VECS:
        part[n] = jnp.concatenate([g_small[l][n] for l in range(depth)], axis=0)
    part['final_g'] = dfinal_g.reshape(d)
    small_shapes = [p[n].shape for n in SMALL] + [(1,)]
    (small_recv,) = _exchange("gather_small", [_pack_rows([part[n] for n in SMALL] + [loss_vec[0, 0:1]])],
                              all_to_all=False)
    small_sum = _sum_slabs("sum_small", small_recv)
    small_list = _unpack_rows(small_sum, small_shapes)
    grads.update(zip(SMALL, small_list[:-1]))
    loss = small_list[-1].reshape(())
    dummy = [jnp.zeros((1,), F32)]
    res = _rowwise("adamw_small", _adamw_fn,
                   [_pack_rows([src[n] for n in SMALL] + dummy) for src in (p, )] + [small_sum] +
                   [_pack_rows([src[n] for n in SMALL] + dummy) for src in (mom_m, mom_v)], [], [(FLAT_W, F32)] * 3)
    for dst, r in zip((delta, new_m, new_v), res):
        dst.update(zip(SMALL, _unpack_rows(r, small_shapes)[:-1]))
    return (loss, grad_x, *[grads[n] for n in WEIGHTS], *[delta[n] for n in WEIGHTS],
            *[new_m[n] for n in WEIGHTS], *[new_v[n] for n in WEIGHTS])


def kernel(x, c, positions, w_ada, b_ada, norm1_g, w_in, ssm_a_re, ssm_a_im, ssm_log_dt, ssm_b_re, ssm_b_im, ssm_c_re, ssm_c_im, ssm_d, w_glu, b_glu, w_a_out, q_norm_g, w_uq, kv_norm_g, w_uk, w_uv, w_b_out, w_out, norm2_g, w_gate, w_up, w_down, final_g, loss_target, m_w_ada, m_b_ada, m_norm1_g, m_w_in, m_ssm_a_re, m_ssm_a_im, m_ssm_log_dt, m_ssm_b_re, m_ssm_b_im, m_ssm_c_re, m_ssm_c_im, m_ssm_d, m_w_glu, m_b_glu, m_w_a_out, m_q_norm_g, m_w_uq, m_kv_norm_g, m_w_uk, m_w_uv, m_w_b_out, m_w_out, m_norm2_g, m_w_gate, m_w_up, m_w_down, m_final_g, v_w_ada, v_b_ada, v_norm1_g, v_w_in, v_ssm_a_re, v_ssm_a_im, v_ssm_log_dt, v_ssm_b_re, v_ssm_b_im, v_ssm_c_re, v_ssm_c_im, v_ssm_d, v_w_glu, v_b_glu, v_w_a_out, v_q_norm_g, v_w_uq, v_kv_norm_g, v_w_uk, v_w_uv, v_w_b_out, v_w_out, v_norm2_g, v_w_gate, v_w_up, v_w_down, v_final_g):
    given = dict(locals())
    p = {n: given[n] for n in WEIGHTS}
    mom_m = {n: given["m_" + n] for n in WEIGHTS}
    mom_v = {n: given["v_" + n] for n in WEIGHTS}
    return _step(p, mom_m, mom_v, x, c, positions, loss_target)
```

```python
import functools
import math

import jax
import jax.numpy as jnp
from jax import lax
from jax.experimental import pallas as pl
from jax.experimental.pallas import tpu as pltpu

F32 = jnp.float32
BF16 = jnp.bfloat16

N_DEV = 8
LANES = 128
FLAT_W = 1024
VMEM_LIMIT = 48 * 1024 * 1024
QK_NOPE, QK_ROPE, V_DIM = 64, 32, 64
HEAD_PAD = LANES
ROPE_BASE = 10000.0
EPS = 1e-6
ADAM_LR, ADAM_B1, ADAM_B2, ADAM_EPS, ADAM_WD, ADAM_STEP = 0.001, 0.9, 0.999, 1e-08, 0.01, 10
NEG = float(jnp.finfo(jnp.float32).min)

WEIGHTS = ['w_ada', 'b_ada', 'norm1_g', 'w_in', 'ssm_a_re', 'ssm_a_im', 'ssm_log_dt', 'ssm_b_re', 'ssm_b_im',
           'ssm_c_re', 'ssm_c_im', 'ssm_d', 'w_glu', 'b_glu', 'w_a_out', 'q_norm_g', 'w_uq', 'kv_norm_g', 'w_uk',
           'w_uv', 'w_b_out', 'w_out', 'norm2_g', 'w_gate', 'w_up', 'w_down', 'final_g']
COL_SHARDED = ['w_in', 'w_a_out', 'w_uq', 'w_uk', 'w_uv', 'w_b_out', 'w_gate', 'w_up']
ROW_SHARDED = ['w_glu', 'w_out', 'w_down']
SHARDED = COL_SHARDED + ROW_SHARDED
SMALL = ['b_ada', 'norm1_g', 'ssm_a_re', 'ssm_a_im', 'ssm_log_dt', 'ssm_b_re', 'ssm_b_im', 'ssm_c_re', 'ssm_c_im',
         'ssm_d', 'b_glu', 'q_norm_g', 'kv_norm_g', 'norm2_g', 'final_g']


def _cparams(sem):
    return pltpu.CompilerParams(dimension_semantics=sem, vmem_limit_bytes=VMEM_LIMIT)


def _pick(dim, pref, quantum=LANES):
    if dim <= pref:
        return dim
    t = (pref // quantum) * quantum
    while t >= quantum:
        if dim % t == 0:
            return t
        t -= quantum
    return dim


def _mm(name, a, b, *, ta=False, out_dtype=F32, a_col=None, b_col=None, tm=1408, tn=1408, tk=None):
    tk = tk or (512 if ta else 1024)
    a_off, a_w = a_col if a_col is not None else (0, a.shape[1])
    b_off, b_w = b_col if b_col is not None else (0, b.shape[1])
    if ta:
        kdim, m = a.shape[0], a_w
        assert b.shape[0] == kdim
    else:
        m, kdim = a.shape[0], a_w
        assert b.shape[0] == kdim, (name, a.shape, b.shape)
    n = b_w
    tm = _pick(m, tm, LANES if ta else 8)
    tn = _pick(n, tn)
    tk = _pick(kdim, tk, 8 if ta else LANES)
    nk = kdim // tk
    assert m % tm == 0 and n % tn == 0 and kdim % tk == 0, (name, m, n, kdim, tm, tn, tk)
    if ta:
        assert a_off % tm == 0 and b_off % tn == 0
        a_spec = pl.BlockSpec((tk, tm), lambda i, j, k: (k, i + a_off // tm))
        dims = (((0,), (0,)), ((), ()))
    else:
        assert a_off % tk == 0 and b_off % tn == 0
        a_spec = pl.BlockSpec((tm, tk), lambda i, j, k: (i, k + a_off // tk))
        dims = (((1,), (0,)), ((), ()))
    b_spec = pl.BlockSpec((tk, tn), lambda i, j, k: (k, j + b_off // tn))

    def prod(a_ref, b_ref):
        return lax.dot_general(a_ref[...].astype(BF16), b_ref[...].astype(BF16), dims, preferred_element_type=F32)

    def body_one(a_ref, b_ref, o_ref):
        o_ref[...] = prod(a_ref, b_ref).astype(o_ref.dtype)

    def body_acc(a_ref, b_ref, o_ref, acc_ref):
        @pl.when(pl.program_id(2) == 0)
        def _():
            acc_ref[...] = jnp.zeros_like(acc_ref)

        acc_ref[...] += prod(a_ref, b_ref)

        @pl.when(pl.program_id(2) == nk - 1)
        def _():
            o_ref[...] = acc_ref[...].astype(o_ref.dtype)

    return pl.pallas_call(
        body_one if nk == 1 else body_acc, name=name,
        out_shape=jax.ShapeDtypeStruct((m, n), out_dtype),
        grid=(m // tm, n // tn, nk),
        in_specs=[a_spec, b_spec],
        out_specs=pl.BlockSpec((tm, tn), lambda i, j, k: (i, j)),
        scratch_shapes=[] if nk == 1 else [pltpu.VMEM((tm, tn), F32)],
        compiler_params=_cparams(("parallel", "parallel", "arbitrary")),
    )(a, b)


def _mm_bd(name, a, b, nblk, *, ta=False, out_dtype=F32, tm=1024, tk=1024):
    ka = a.shape[1] // nblk
    if ta:
        rows, nb = a.shape[0], b.shape[1] // nblk
        assert b.shape[0] == rows and ka % LANES == 0 and nb % LANES == 0, (name, a.shape, b.shape)
        tk = _pick(rows, tk, 8)
        nk = rows // tk

        def body(a_ref, b_ref, o_ref, acc_ref):
            @pl.when(pl.program_id(1) == 0)
            def _():
                acc_ref[...] = jnp.zeros_like(acc_ref)

            acc_ref[...] += lax.dot_general(a_ref[...].astype(BF16), b_ref[...].astype(BF16), _TN,
                                            preferred_element_type=F32)

            @pl.when(pl.program_id(1) == nk - 1)
            def _():
                o_ref[...] = acc_ref[...].astype(o_ref.dtype)

        return pl.pallas_call(
            body, name=name, out_shape=jax.ShapeDtypeStruct((nblk * ka, nb), out_dtype), grid=(nblk, nk),
            in_specs=[pl.BlockSpec((tk, ka), lambda j, k: (k, j)), pl.BlockSpec((tk, nb), lambda j, k: (k, j))],
            out_specs=pl.BlockSpec((ka, nb), lambda j, k: (j, 0)),
            scratch_shapes=[pltpu.VMEM((ka, nb), F32)],
            compiler_params=_cparams(("parallel", "arbitrary")),
        )(a, b)
    m, nb = a.shape[0], b.shape[1]
    assert b.shape[0] == nblk * ka and ka % LANES == 0 and nb % LANES == 0, (name, a.shape, b.shape)
    tm = _pick(m, tm, 8)

    def body(a_ref, b_ref, o_ref):
        o_ref[...] = jnp.dot(a_ref[...].astype(BF16), b_ref[...].astype(BF16),
                             preferred_element_type=F32).astype(o_ref.dtype)

    return pl.pallas_call(
        body, name=name, out_shape=jax.ShapeDtypeStruct((m, nblk * nb), out_dtype), grid=(m // tm, nblk),
        in_specs=[pl.BlockSpec((tm, ka), lambda i, j: (i, j)), pl.BlockSpec((ka, nb), lambda i, j: (j, 0))],
        out_specs=pl.BlockSpec((tm, nb), lambda i, j: (i, j)),
        compiler_params=_cparams(("parallel", "parallel")),
    )(a, b)


def _rowwise(name, fn, rows, vecs, outs, reds=(), *, tm=256):
    rows = [(r, 0, r.shape[1]) if not isinstance(r, tuple) else r for r in rows]
    nrows = rows[0][0].shape[0]
    tm = _pick(nrows, tm, 8)
    assert nrows % tm == 0, (name, nrows, tm)
    nr, nv, no = len(rows), len(vecs), len(outs)
    in_specs = []
    for arr, off, w in rows:
        assert arr.shape[0] == nrows and off % w == 0, (name, arr.shape, off, w)
        in_specs.append(pl.BlockSpec((tm, w), functools.partial(lambda i, cb: (i, cb), cb=off // w)))
    for v in vecs:
        assert v.ndim == 2 and v.shape[0] == 1, (name, v.shape)
        in_specs.append(pl.BlockSpec(v.shape, lambda i: (0, 0)))
    out_shape = [jax.ShapeDtypeStruct((nrows, w), dt) for w, dt in outs]
    out_specs = [pl.BlockSpec((tm, w), lambda i: (i, 0)) for w, dt in outs]
    out_shape += [jax.ShapeDtypeStruct((1, w), F32) for w in reds]
    out_specs += [pl.BlockSpec((1, w), lambda i: (0, 0)) for w in reds]

    def body(*refs):
        rin, vin = refs[:nr], refs[nr:nr + nv]
        rout, rred = refs[nr + nv:nr + nv + no], refs[nr + nv + no:]
        res = fn(*[r[...] for r in rin], *[v[...] for v in vin])
        if not isinstance(res, (tuple, list)):
            res = (res,)
        assert len(res) == no + len(reds), (name, len(res))
        for r, val in zip(rout, res[:no]):
            r[...] = val.astype(r.dtype)
        if reds:
            @pl.when(pl.program_id(0) == 0)
            def _():
                for r in rred:
                    r[...] = jnp.zeros_like(r)

            for r, val in zip(rred, res[no:]):
                r[...] += val.astype(F32)

    res = pl.pallas_call(
        body, name=name, out_shape=out_shape, grid=(nrows // tm,),
        in_specs=in_specs, out_specs=out_specs,
        compiler_params=_cparams(("arbitrary",) if reds else ("parallel",)),
    )(*[r[0] for r in rows], *vecs)
    return res


def _f32(x):
    return x.astype(F32)


def _vjp_fn(f, n_in, n_cot, want):
    def fn(*args):
        ins = [_f32(a) for a in args[:n_in]]
        cots = tuple(_f32(a) for a in args[n_in:n_in + n_cot])
        _, vjp = jax.vjp(f, *ins)
        grads = vjp(cots if n_cot > 1 else cots[0])
        return tuple(grads[i] for i in want)
    return fn


def _f_rms(x, g):
    return (x * lax.rsqrt(jnp.mean(x * x, axis=-1, keepdims=True) + EPS)) * g


def _f_norm_mod(x, g, sc, sh):
    return _f_rms(x, g) * (1.0 + sc) + sh


def _f_gelu_in(ych, u, d):
    return jax.nn.gelu(ych + d * u)


def _f_glu(yg, pre, b):
    return yg * jax.nn.sigmoid(pre + b)


def _f_merge(ga, gb, ya, yb):
    return jax.nn.sigmoid(ga) * ya + jax.nn.sigmoid(gb) * yb


def _f_res(x, t, g):
    return x + g * t


def _f_swiglu(a, b):
    return jax.nn.silu(a) * b


def _f_rope_q(qa, qb, cos, sin):
    h = qa.shape[1] // HEAD_PAD
    return qa * jnp.tile(cos, (1, h)) + qb * jnp.tile(sin, (1, h))


def _f_rope_k(kn, kra, krb, cos, sin):
    h = kn.shape[1] // HEAD_PAD
    return kn + jnp.tile(kra * cos + krb * sin, (1, h))


SUBLANES = 8
S5_SUPER = 8


def _cpow(ar, ai, log2n):
    for _ in range(log2n):
        ar, ai = ar * ar - ai * ai, 2.0 * ar * ai
    return ar, ai


def _to_segments(x):
    L, w = x.shape
    return x.reshape(SUBLANES, L // SUBLANES, w).transpose(1, 0, 2).reshape(L, w)


def _from_segments(x):
    L, w = x.shape
    return x.reshape(L // SUBLANES, SUBLANES, w).transpose(1, 0, 2).reshape(L, w)


def _scan(x, a_re, a_im, *, reverse, sc, h=None, tb=256):
    L, gp2 = x.shape
    gp = gp2 // 2
    seg = L // SUBLANES
    assert L % SUBLANES == 0 and seg & (seg - 1) == 0, L
    tb = _pick(L, tb, SUBLANES)
    nb, nt = L // tb, tb // SUBLANES
    assert gp % sc == 0 and sc % LANES == 0, (gp, sc)
    nchunk = gp // sc
    sign = -1.0 if reverse else 1.0
    blk = (lambda i: (nb - 1 - i, 0)) if reverse else (lambda i: (i, 0))
    order = (lambda s: nt - 1 - s) if reverse else (lambda s: s)
    tile = lambda s: pl.ds(pl.multiple_of(s * SUBLANES, SUBLANES), SUBLANES)
    vec = pl.BlockSpec((1, gp), lambda i: (0, 0))
    state = pl.BlockSpec((SUBLANES, gp2), lambda i: (0, 0))

    def coeffs(ar_ref, ai_ref, cols):
        ar1, ai1 = ar_ref[:, cols], sign * ai_ref[:, cols]
        return ar1, ai1, jnp.broadcast_to(ar1, (SUBLANES, sc)), jnp.broadcast_to(ai1, (SUBLANES, sc))

    def ends_body(x_ref, ar_ref, ai_ref, e_ref):
        @pl.when(pl.program_id(0) == 0)
        def _():
            e_ref[...] = jnp.zeros_like(e_ref)

        for c in range(nchunk):
            cc, re, im = pl.ds(c * sc, sc), pl.ds(2 * c * sc, sc), pl.ds((2 * c + 1) * sc, sc)
            _, _, ar, ai = coeffs(ar_ref, ai_ref, cc)

            def step(s, st):
                sr, si = st
                rows = tile(order(s))
                return ar * sr - ai * si + x_ref[rows, re], ar * si + ai * sr + x_ref[rows, im]

            sr, si = lax.fori_loop(0, nt, step, (e_ref[:, re], e_ref[:, im]), unroll=4)
            e_ref[:, re] = sr
            e_ref[:, im] = si

    ends = pl.pallas_call(
        ends_body, name="s5_scan_ends_" + ("bwd" if reverse else "fwd"),
        out_shape=jax.ShapeDtypeStruct((SUBLANES, gp2), F32), grid=(nb,),
        in_specs=[pl.BlockSpec((tb, gp2), blk), vec, vec], out_specs=state,
        compiler_params=_cparams(("arbitrary",)),
    )(x, a_re, a_im)

    with_grad = h is not None

    def write_body(*refs):
        if with_grad:
            x_ref, e_ref, ar_ref, ai_ref, h_ref, hp_ref, hl_ref, o_ref, dar_ref, dai_ref, st_ref = refs
        else:
            x_ref, e_ref, ar_ref, ai_ref, o_ref, st_ref = refs
        i = pl.program_id(0)

        @pl.when(i == 0)
        def _():
            for c in range(nchunk):
                cc, re, im = pl.ds(c * sc, sc), pl.ds(2 * c * sc, sc), pl.ds((2 * c + 1) * sc, sc)
                ar1, ai1, _, _ = coeffs(ar_ref, ai_ref, cc)
                pr, pi = _cpow(ar1, ai1, seg.bit_length() - 1)
                cr = jnp.zeros((1, sc), F32)
                ci = jnp.zeros((1, sc), F32)
                for j in (reversed(range(SUBLANES)) if reverse else range(SUBLANES)):
                    st_ref[j:j + 1, re] = cr
                    st_ref[j:j + 1, im] = ci
                    cr, ci = (e_ref[j:j + 1, re] + pr * cr - pi * ci, e_ref[j:j + 1, im] + pr * ci + pi * cr)
            if with_grad:
                dar_ref[...] = jnp.zeros_like(dar_ref)
                dai_ref[...] = jnp.zeros_like(dai_ref)

        sub = lax.broadcasted_iota(jnp.int32, (SUBLANES, sc), 0)
        for c in range(nchunk):
            cc, re, im = pl.ds(c * sc, sc), pl.ds(2 * c * sc, sc), pl.ds((2 * c + 1) * sc, sc)
            _, _, ar, ai = coeffs(ar_ref, ai_ref, cc)

            def advance(rows, sr, si):
                sr, si = ar * sr - ai * si + x_ref[rows, re], ar * si + ai * sr + x_ref[rows, im]
                o_ref[rows, re] = sr
                o_ref[rows, im] = si
                return sr, si

            if not with_grad:
                def step(s, st):
                    return advance(tile(order(s)), *st)

                sr, si = lax.fori_loop(0, nt, step, (st_ref[:, re], st_ref[:, im]), unroll=4)
            else:
                def grad(sr, si, hpr, hpi, accr, acci):
                    return accr + sr * hpr + si * hpi, acci + si * hpr - sr * hpi

                def step(s, st):
                    sr, si, accr, acci = st
                    t = nt - 1 - s
                    sr, si = advance(tile(t), sr, si)
                    prev = tile(t - 1)
                    return (sr, si) + grad(sr, si, h_ref[prev, re], h_ref[prev, im], accr, acci)

                zero = jnp.zeros((SUBLANES, sc), F32)
                sr, si, accr, acci = lax.fori_loop(0, nt - 1, step, (st_ref[:, re], st_ref[:, im], zero, zero),
                                                   unroll=4)
                sr, si = advance(tile(0), sr, si)
                first = (i == nb - 1)
                wrap_r = jnp.where(sub == 0, 0.0, pltpu.roll(hl_ref[:, re], 1, 0))
                wrap_i = jnp.where(sub == 0, 0.0, pltpu.roll(hl_ref[:, im], 1, 0))
                hpr = jnp.where(first, wrap_r, hp_ref[:, re])
                hpi = jnp.where(first, wrap_i, hp_ref[:, im])
                accr, acci = grad(sr, si, hpr, hpi, accr, acci)
                dar_ref[:, cc] += jnp.sum(accr, axis=0, keepdims=True)
                dai_ref[:, cc] += jnp.sum(acci, axis=0, keepdims=True)
            st_ref[:, re] = sr
            st_ref[:, im] = si

    big = pl.BlockSpec((tb, gp2), blk)
    in_specs = [big, state, vec, vec]
    operands = [x, ends, a_re, a_im]
    out_shape = [jax.ShapeDtypeStruct((L, gp2), F32)]
    out_specs = [big]
    if with_grad:
        in_specs += [big,
                     pl.BlockSpec((SUBLANES, gp2), lambda i: (jnp.maximum((nb - 1 - i) * nt - 1, 0), 0)),
                     pl.BlockSpec((SUBLANES, gp2), lambda i: (seg - 1, 0))]
        operands += [h, h, h]
        out_shape += [jax.ShapeDtypeStruct((1, gp), F32)] * 2
        out_specs += [vec, vec]
    res = pl.pallas_call(
        write_body, name="s5_scan_" + ("bwd" if reverse else "fwd"),
        out_shape=out_shape, grid=(nb,), in_specs=in_specs, out_specs=out_specs,
        scratch_shapes=[pltpu.VMEM((SUBLANES, gp2), F32)],
        compiler_params=_cparams(("arbitrary",)),
    )(*operands)
    return res if with_grad else res[0]


_NT = (((1,), (1,)), ((), ()))
_TN = (((0,), (0,)), ((), ()))


def _causal(s, t, k_major=False):
    row = lax.broadcasted_iota(jnp.int32, (t, t), 0)
    col = lax.broadcasted_iota(jnp.int32, (t, t), 1)
    return jnp.where(row <= col if k_major else col <= row, s, NEG)


def _pair_tables(n, k_major):
    if k_major:
        pairs = [(qi, ki) for ki in range(n) for qi in range(ki, n)]
    else:
        pairs = [(qi, ki) for qi in range(n) for ki in range(qi + 1)]
    return (jnp.asarray([p[0] for p in pairs], jnp.int32), jnp.asarray([p[1] for p in pairs], jnp.int32))


def _flash_fwd(q, k, knv, *, heads, t):
    L = q.shape[0]
    n = L // t
    rep = t // LANES
    qtab, ktab = _pair_tables(n, k_major=False)

    def body(qt_ref, kt_ref, q_ref, k_ref, v_ref, o_ref, lse_ref, m_s, acc_s):
        step = pl.program_id(1)
        qi, ki = qt_ref[step], kt_ref[step]
        lane = lax.broadcasted_iota(jnp.int32, (t, HEAD_PAD), 1)

        @pl.when(ki == 0)
        def _():
            m_s[...] = jnp.full(m_s.shape, NEG, F32)
            acc_s[...] = jnp.zeros_like(acc_s)

        def update(diagonal):
            s = lax.dot_general(q_ref[...], k_ref[...], _NT, preferred_element_type=F32)
            if diagonal:
                s = _causal(s, t)
            m_prev = m_s[...]
            m_next = jnp.maximum(m_prev, jnp.max(s, axis=1, keepdims=True))
            p = jnp.exp(s - jnp.tile(m_next, (1, rep)))
            vb = jnp.where(lane == V_DIM, jnp.ones((), BF16), v_ref[...])
            acc_s[...] = jnp.exp(m_prev - m_next) * acc_s[...] + jnp.dot(p.astype(BF16), vb,
                                                                         preferred_element_type=F32)
            m_s[...] = m_next

        @pl.when(ki < qi)
        def _():
            update(False)

        @pl.when(ki == qi)
        def _():
            update(True)
            acc = acc_s[...]
            l = jnp.sum(jnp.where(lane == V_DIM, acc, 0.0), axis=1, keepdims=True)
            o_ref[...] = jnp.where(lane == V_DIM, 0.0, acc * (1.0 / l)).astype(o_ref.dtype)
            lse_ref[0] = jnp.max(m_s[...], axis=1, keepdims=True) + jnp.log(l)

    q_map = lambda h, s, qt, kt: (qt[s], h)
    kv_map = lambda h, s, qt, kt: (kt[s], h)
    v_map = lambda h, s, qt, kt: (kt[s], h + heads)
    return pl.pallas_call(
        body, name="mla_flash_fwd",
        out_shape=[jax.ShapeDtypeStruct((L, heads * HEAD_PAD), BF16),
                   jax.ShapeDtypeStruct((heads, L, 1), F32)],
        grid_spec=pltpu.PrefetchScalarGridSpec(
            num_scalar_prefetch=2, grid=(heads, qtab.shape[0]),
            in_specs=[pl.BlockSpec((t, HEAD_PAD), q_map),
                      pl.BlockSpec((t, HEAD_PAD), kv_map),
                      pl.BlockSpec((t, HEAD_PAD), v_map)],
            out_specs=[pl.BlockSpec((t, HEAD_PAD), q_map),
                       pl.BlockSpec((1, t, 1), lambda h, s, qt, kt: (h, qt[s], 0))],
            scratch_shapes=[pltpu.VMEM((t, LANES), F32), pltpu.VMEM((t, HEAD_PAD), F32)]),
        compiler_params=_cparams(("parallel", "arbitrary")),
    )(qtab, ktab, q, k, knv)


def _flash_delta(do, o, *, heads, t):
    L = do.shape[0]

    def body(do_ref, o_ref, d_ref):
        d_ref[0] = jnp.sum(do_ref[...].astype(F32) * o_ref[...].astype(F32), axis=1, keepdims=True)

    return pl.pallas_call(
        body, name="mla_flash_delta",
        out_shape=jax.ShapeDtypeStruct((heads, L, 1), F32),
        grid=(heads, L // t),
        in_specs=[pl.BlockSpec((t, HEAD_PAD), lambda h, i: (i, h)),
                  pl.BlockSpec((t, HEAD_PAD), lambda h, i: (i, h))],
        out_specs=pl.BlockSpec((1, t, 1), lambda h, i: (h, i, 0)),
        compiler_params=_cparams(("parallel", "parallel")),
    )(do, o)


def _flash_bwd(q, k, knv, do, lse, delta, *, heads, t):
    L = q.shape[0]
    n = L // t
    qtab, ktab = _pair_tables(n, k_major=True)

    def body(qt_ref, kt_ref, q_ref, k_ref, v_ref, do_ref, lse_ref, dl_ref, dq_ref, dk_ref, dv_ref, dk_s, dv_s):
        step = pl.program_id(1)
        qi, ki = qt_ref[step], kt_ref[step]

        @pl.when(qi == ki)
        def _():
            dk_s[...] = jnp.zeros_like(dk_s)
            dv_s[...] = jnp.zeros_like(dv_s)

        def update(diagonal):
            qb, kb, vb, dob = q_ref[...], k_ref[...], v_ref[...], do_ref[...]
            st = lax.dot_general(kb, qb, _NT, preferred_element_type=F32)
            if diagonal:
                st = _causal(st, t, k_major=True)
            pt = jnp.exp(st - lse_ref[0])
            dv_s[...] += jnp.dot(pt.astype(BF16), dob, preferred_element_type=F32)
            dpt = lax.dot_general(vb, dob, _NT, preferred_element_type=F32)
            dst = (pt * (dpt - dl_ref[0])).astype(BF16)
            dk_s[...] += jnp.dot(dst, qb, preferred_element_type=F32)
            dqb = lax.dot_general(dst, kb, _TN, preferred_element_type=F32)
            rows = pl.ds(pl.multiple_of(qi * t, t), t)

            @pl.when(ki == 0)
            def _():
                dq_ref[rows, :] = dqb

            @pl.when(ki > 0)
            def _():
                dq_ref[rows, :] += dqb

        @pl.when(qi > ki)
        def _():
            update(False)

        @pl.when(qi == ki)
        def _():
            update(True)

        @pl.when(qi == n - 1)
        def _():
            dk_ref[...] = dk_s[...].astype(dk_ref.dtype)
            dv_ref[...] = dv_s[...].astype(dv_ref.dtype)

    q_map = lambda h, s, qt, kt: (qt[s], h)
    stat_map = lambda h, s, qt, kt: (h, 0, qt[s])
    kv_map = lambda h, s, qt, kt: (kt[s], h)
    v_map = lambda h, s, qt, kt: (kt[s], h + heads)
    return pl.pallas_call(
        body, name="mla_flash_bwd",
        out_shape=[jax.ShapeDtypeStruct((L, heads * HEAD_PAD), F32),
                   jax.ShapeDtypeStruct((L, heads * HEAD_PAD), BF16),
                   jax.ShapeDtypeStruct((L, heads * HEAD_PAD), BF16)],
        grid_spec=pltpu.PrefetchScalarGridSpec(
            num_scalar_prefetch=2, grid=(heads, qtab.shape[0]),
            in_specs=[pl.BlockSpec((t, HEAD_PAD), q_map),
                      pl.BlockSpec((t, HEAD_PAD), kv_map),
                      pl.BlockSpec((t, HEAD_PAD), v_map),
                      pl.BlockSpec((t, HEAD_PAD), q_map),
                      pl.BlockSpec((1, 1, t), stat_map),
                      pl.BlockSpec((1, 1, t), stat_map)],
            out_specs=[pl.BlockSpec((L, HEAD_PAD), lambda h, s, qt, kt: (0, h)),
                       pl.BlockSpec((t, HEAD_PAD), kv_map),
                       pl.BlockSpec((t, HEAD_PAD), kv_map)],
            scratch_shapes=[pltpu.VMEM((t, HEAD_PAD), F32), pltpu.VMEM((t, HEAD_PAD), F32)]),
        compiler_params=_cparams(("parallel", "arbitrary")),
    )(qtab, ktab, q, k, knv, do, lse, delta)


def _peer(k):
    mx, my, mc = lax.axis_index("x"), lax.axis_index("y"), lax.axis_index("c")
    px = 1 - mx if (k >> 2) & 1 else mx
    py = 1 - my if (k >> 1) & 1 else my
    pc = 1 - mc if k & 1 else mc
    return (px, py, pc), 4 * px + 2 * py + pc


def _exchange(name, xs, all_to_all):
    n = len(xs)
    any_spec = pl.BlockSpec(memory_space=pl.ANY)
    npeer = N_DEV - 1

    def body(*refs):
        x_refs, o_refs = refs[:n], refs[n:2 * n]
        send_sems, recv_sems, local_sems = refs[2 * n:]
        _, me = _peer(0)
        mine = [x.at[me] if all_to_all else x for x in x_refs]
        local = [pltpu.make_async_copy(mine[i], o_refs[i].at[me], local_sems.at[i]) for i in range(n)]
        for cp in local:
            cp.start()
        sends = []
        for k in range(1, N_DEV):
            dev, idx = _peer(k)
            for i in range(n):
                cp = pltpu.make_async_remote_copy(
                    src_ref=x_refs[i].at[idx] if all_to_all else x_refs[i], dst_ref=o_refs[i].at[me],
                    send_sem=send_sems.at[i * npeer + k - 1], recv_sem=recv_sems.at[i * npeer + k - 1],
                    device_id=dev, device_id_type=pl.DeviceIdType.MESH)
                cp.start()
                sends.append(cp)
        for k in range(1, N_DEV):
            dev, idx = _peer(k)
            for i in range(n):
                pltpu.make_async_remote_copy(
                    src_ref=mine[i], dst_ref=o_refs[i].at[idx],
                    send_sem=send_sems.at[i * npeer + k - 1], recv_sem=recv_sems.at[i * npeer + k - 1],
                    device_id=dev, device_id_type=pl.DeviceIdType.MESH).wait_recv()
        for cp in sends:
            cp.wait_send()
        for cp in local:
            cp.wait()

    return pl.pallas_call(
        body, name=name,
        out_shape=[jax.ShapeDtypeStruct((N_DEV,) + tuple(x.shape[1:] if all_to_all else x.shape), x.dtype)
                   for x in xs],
        in_specs=[any_spec] * n, out_specs=[any_spec] * n,
        scratch_shapes=[pltpu.SemaphoreType.DMA((n * npeer,)), pltpu.SemaphoreType.DMA((n * npeer,)),
                        pltpu.SemaphoreType.DMA((n,))],
    )(*xs)


def _sum_slabs(name, x, *, tr=128):
    _, r, w = x.shape
    tr = _pick(r, tr, 8)

    def body(x_ref, o_ref):
        acc = x_ref[0]
        for j in range(1, N_DEV):
            acc = acc + x_ref[j]
        o_ref[...] = acc

    return pl.pallas_call(
        body, name=name, out_shape=jax.ShapeDtypeStruct((r, w), F32), grid=(r // tr,),
        in_specs=[pl.BlockSpec((N_DEV, tr, w), lambda i: (0, i, 0))],
        out_specs=pl.BlockSpec((tr, w), lambda i: (i, 0)),
        compiler_params=_cparams(("parallel",)),
    )(x)


def _adamw_fn(w, g, m, v):
    m = ADAM_B1 * m + (1.0 - ADAM_B1) * g
    v = ADAM_B2 * v + (1.0 - ADAM_B2) * jnp.square(g)
    m_hat = m / (1.0 - ADAM_B1 ** ADAM_STEP)
    v_hat = v / (1.0 - ADAM_B2 ** ADAM_STEP)
    delta = -ADAM_LR * (m_hat / (jnp.sqrt(v_hat) + ADAM_EPS) + ADAM_WD * w)
    return delta, m, v


def _adamw_sharded(name, w, recv, m, v, *, tr=128):
    rows, c = w.shape
    tr = _pick(rows, tr, 8)

    def body(w_ref, r_ref, m_ref, v_ref, g_ref, d_ref, mo_ref, vo_ref):
        g = r_ref[0].astype(F32)
        for j in range(1, N_DEV):
            g = g + r_ref[j].astype(F32)
        d, mn, vn = _adamw_fn(w_ref[...], g, m_ref[...], v_ref[...])
        g_ref[...] = g
        d_ref[...] = d
        mo_ref[...] = mn
        vo_ref[...] = vn

    blk = pl.BlockSpec((tr, c), lambda i: (i, 0))
    return pl.pallas_call(
        body, name=name, out_shape=[jax.ShapeDtypeStruct((rows, c), F32)] * 4, grid=(rows // tr,),
        in_specs=[blk, pl.BlockSpec((N_DEV, tr, c), lambda i: (0, i, 0)), blk, blk],
        out_specs=[blk] * 4,
        compiler_params=_cparams(("parallel",)),
    )(w, recv, m, v)


def _piece_rows(shape):
    return -(-math.prod(shape) // (8 * FLAT_W)) * 8


def _pack_rows(arrs):
    out = []
    for a in arrs:
        flat = a.reshape(-1)
        rows = _piece_rows(a.shape)
        out.append(jnp.pad(flat, (0, rows * FLAT_W - flat.shape[0])).reshape(rows, FLAT_W))
    return jnp.concatenate(out, axis=0)


def _unpack_rows(packed, shapes):
    out, r0 = [], 0
    for s in shapes:
        rows = _piece_rows(s)
        out.append(packed[r0:r0 + rows].reshape(-1)[:math.prod(s)].reshape(s))
        r0 += rows
    return out


def _pack_slabs(a):
    n = a.shape[1]
    rows = _piece_rows((n,))
    return jnp.pad(a, ((0, 0), (0, rows * FLAT_W - n))).reshape(N_DEV, rows, FLAT_W)


def _unpack_slabs(a, n):
    return a.reshape(N_DEV, -1)[:, :n]


def _s5_operators(a_re, a_im, log_dt, b_re, b_im, c_re, c_im):
    g, p, m = b_re.shape
    dt = jnp.exp(log_dt)[:, None]
    mag = jnp.exp(a_re * dt)
    abar_re = mag * jnp.cos(a_im * dt)
    abar_im = mag * jnp.sin(a_im * dt)
    den = a_re * a_re + a_im * a_im
    nr = abar_re - 1.0
    ni = abar_im
    coef_re = ((nr * a_re + ni * a_im) / den)[..., None]
    coef_im = ((ni * a_re - nr * a_im) / den)[..., None]
    bbar_re = coef_re * b_re - coef_im * b_im
    bbar_im = coef_re * b_im + coef_im * b_re
    sup, ns = S5_SUPER, g // S5_SUPER
    eye = jnp.eye(sup, dtype=F32)

    def b_blocks(bb):
        return jnp.einsum('cgpm,gh->cgmhp', bb.reshape(ns, sup, p, m), eye).reshape(ns, sup * m, sup * p)

    def c_blocks(cb):
        return jnp.einsum('cgmp,gh->chpgm', cb.reshape(ns, sup, m, p), eye).reshape(ns, sup * p, sup * m)

    bmat = jnp.concatenate([b_blocks(bbar_re), b_blocks(bbar_im)], axis=2).reshape(g * m, 2 * sup * p)
    cmat = jnp.concatenate([c_blocks(c_re), c_blocks(-c_im)], axis=1).reshape(ns * 2 * sup * p, sup * m)
    return abar_re.reshape(1, g * p), abar_im.reshape(1, g * p), bmat, cmat


def _rot_cols(w):
    half = w.shape[-1] // 2
    return jnp.concatenate([-w[..., half:], w[..., :half]], axis=-1)


def _layer_operators(w, dims):
    d, sw, ql, kvl, heads, dff = dims['d'], dims['sw'], dims['ql'], dims['kvl'], dims['heads'], dims['dff']
    w_in = w['w_in']
    o = 0
    parts = {}
    for nm, sz in (('u', sw), ('cq', ql), ('ckv', kvl), ('kr', QK_ROPE), ('ga', d), ('gb', d)):
        parts[nm] = w_in[:, o:o + sz]
        o += sz
    zpad = lambda n: jnp.zeros((d, n), w_in.dtype)
    kra = jnp.concatenate([zpad(QK_NOPE), parts['kr'], zpad(HEAD_PAD - QK_NOPE - QK_ROPE)], axis=1)
    krb = jnp.concatenate([zpad(QK_NOPE), _rot_cols(parts['kr']), zpad(HEAD_PAD - QK_NOPE - QK_ROPE)], axis=1)
    w_in_x = jnp.concatenate([parts['ga'], parts['gb'], parts['u'], parts['cq'], parts['ckv'], kra, krb], axis=1)

    wq = w['w_uq'].reshape(ql, heads, QK_NOPE + QK_ROPE)
    qz = lambda n: jnp.zeros((ql, heads, n), wq.dtype)
    wq_a = jnp.concatenate([wq, qz(HEAD_PAD - QK_NOPE - QK_ROPE)], axis=2)
    wq_b = jnp.concatenate([qz(QK_NOPE), _rot_cols(wq[:, :, QK_NOPE:]), qz(HEAD_PAD - QK_NOPE - QK_ROPE)], axis=2)
    wq_x = jnp.concatenate([wq_a.reshape(ql, -1), wq_b.reshape(ql, -1)], axis=1)

    kz = lambda n: jnp.zeros((kvl, heads, n), w['w_uk'].dtype)
    wk = jnp.concatenate([w['w_uk'].reshape(kvl, heads, QK_NOPE), kz(HEAD_PAD - QK_NOPE)], axis=2)
    wv = jnp.concatenate([w['w_uv'].reshape(kvl, heads, V_DIM), kz(HEAD_PAD - V_DIM)], axis=2)
    wkv_x = jnp.concatenate([wk.reshape(kvl, -1), wv.reshape(kvl, -1)], axis=1)

    wbo = w['w_b_out'].reshape(heads, V_DIM, d)
    wbo_x = jnp.concatenate([wbo, jnp.zeros((heads, HEAD_PAD - V_DIM, d), wbo.dtype)], axis=1).reshape(-1, d)
    wgu = jnp.concatenate([w['w_gate'], w['w_up']], axis=1)
    return dict(w_in=w_in_x, w_glu=w['w_glu'], w_a_out=w['w_a_out'], wq=wq_x, wkv=wkv_x, wbo=wbo_x,
                w_out=w['w_out'], wgu=wgu, w_down=w['w_down'])


def _gathered_to_full(gathered):
    full = {}
    for n, pc in zip(SHARDED, gathered):
        dep, r, c = pc.shape[1:]
        if n in COL_SHARDED:
            full[n] = pc.transpose(1, 2, 0, 3).reshape(dep, r, N_DEV * c)
        else:
            full[n] = pc.transpose(1, 0, 2, 3).reshape(dep, N_DEV * r, c)
    return full


def _layer_fwd(x, mod, ops, s5, small, rope, dims):
    d, sw, ql, kvl, heads, dff = dims['d'], dims['sw'], dims['ql'], dims['kvl'], dims['heads'], dims['dff']
    zo = dims['zoff']
    hw = heads * HEAD_PAD
    cos, sin = rope
    sh1, sc1, g1, sh2, sc2, g2 = mod
    bf = lambda a: a.astype(BF16)
    sv = dict(x=x)
    (h1,) = _rowwise("norm1_fwd", _f_norm_mod, [x], [small['norm1_g'], sc1, sh1], [(d, BF16)])
    z = _mm("w_in_fwd", h1, bf(ops['w_in']))
    sv.update(h1=h1, z=z)
    a_re, a_im, bmat, cmat = s5
    u_seg = _to_segments(z[:, zo['u']:zo['u'] + sw])
    nsup, sc = dims['nsup'], dims['s5_chunk']
    bu = _mm_bd("s5_bu_fwd", u_seg, bf(bmat), nsup)
    hst = _scan(bu, a_re, a_im, reverse=False, sc=sc)
    ych = _from_segments(_mm_bd("s5_y_fwd", hst, bf(cmat), nsup))
    (yg,) = _rowwise("s5_gelu_fwd", _f_gelu_in, [ych, (z, zo['u'], sw)], [small['ssm_d']], [(sw, F32)])
    pre = _mm("s5_glu_mm_fwd", yg, bf(ops['w_glu']))
    (s5o,) = _rowwise("s5_glu_fwd", _f_glu, [yg, pre], [small['b_glu']], [(sw, BF16)])
    ya = _mm("s5_out_fwd", s5o, bf(ops['w_a_out']))
    sv.update(u_seg=u_seg, hst=hst, ych=ych, yg=yg, pre=pre, s5o=s5o, ya=ya)
    (cq,) = _rowwise("q_norm_fwd", _f_rms, [(z, zo['cq'], ql)], [small['q_norm_g']], [(ql, BF16)])
    qab = _mm("q_up_fwd", cq, bf(ops['wq']))
    scale = dims['scale']
    (q,) = _rowwise("q_rope_fwd", lambda a, b, cb, sb: _f_rope_q(a, b, cb, sb) * scale,
                    [(qab, 0, hw), (qab, hw, hw), cos, sin], [], [(hw, BF16)])
    (ckv,) = _rowwise("kv_norm_fwd", _f_rms, [(z, zo['ckv'], kvl)], [small['kv_norm_g']], [(kvl, BF16)])
    knv = _mm("kv_up_fwd", ckv, bf(ops['wkv']), out_dtype=BF16)
    (k,) = _rowwise("k_rope_fwd", _f_rope_k,
                    [(knv, 0, hw), (z, zo['kra'], HEAD_PAD), (z, zo['krb'], HEAD_PAD), cos, sin], [], [(hw, BF16)])
    o, lse = _flash_fwd(q, k, knv, heads=heads, t=dims['tq'])
    yb = _mm("mla_out_fwd", o, bf(ops['wbo']))
    sv.update(cq=cq, q=q, ckv=ckv, knv=knv, k=k, o=o, lse=lse, yb=yb)
    (merged,) = _rowwise("merge_fwd", _f_merge, [(z, zo['ga'], d), (z, zo['gb'], d), ya, yb], [], [(d, BF16)])
    t1 = _mm("w_out_fwd", merged, bf(ops['w_out']))
    (x1,) = _rowwise("res1_fwd", _f_res, [x, t1], [g1], [(d, F32)])
    sv.update(merged=merged, t1=t1, x1=x1)
    (h2,) = _rowwise("norm2_fwd", _f_norm_mod, [x1], [small['norm2_g'], sc2, sh2], [(d, BF16)])
    ab = _mm("ffn_up_fwd", h2, bf(ops['wgu']))
    (f,) = _rowwise("swiglu_fwd", _f_swiglu, [(ab, 0, dff), (ab, dff, dff)], [], [(dff, BF16)], tm=128)
    t2 = _mm("ffn_down_fwd", f, bf(ops['w_down']))
    (x2,) = _rowwise("res2_fwd", _f_res, [x1, t2], [g2], [(d, F32)])
    sv.update(h2=h2, ab=ab, f=f, t2=t2)
    return x2, sv


def _layer_bwd(dx2, sv, mod, ops, s5, small, rope, dims):
    d, sw, ql, kvl, heads, dff = dims['d'], dims['sw'], dims['ql'], dims['kvl'], dims['heads'], dims['dff']
    zo = dims['zoff']
    hw = heads * HEAD_PAD
    cos, sin = rope
    sh1, sc1, g1, sh2, sc2, g2 = mod
    a_re, a_im, bmat, cmat = s5
    z = sv['z']
    tr = lambda a: a.T.astype(BF16)
    gops, gsm = {}, {}
    dt2, dg2 = _res_bwd("res2_bwd", sv['t2'], g2, dx2)
    gops['w_down'] = _mm("ffn_down_dw", sv['f'], dt2, ta=True)
    df = _mm("ffn_down_dx", dt2, tr(ops['w_down']), out_dtype=BF16)
    ab = sv['ab']
    (dab_a, dab_b) = _rowwise("swiglu_bwd", _vjp_fn(_f_swiglu, 2, 1, (0, 1)),
                              [(ab, 0, dff), (ab, dff, dff), df], [], [(dff, BF16), (dff, BF16)], tm=128)
    dab = jnp.concatenate([dab_a, dab_b], axis=1)
    gops['wgu'] = _mm("ffn_up_dw", sv['h2'], dab, ta=True)
    dh2 = _mm("ffn_up_dx", dab, tr(ops['wgu']))
    dx1, dn2, dsc2, dsh2 = _norm_mod_bwd("norm2_bwd", sv['x1'], small['norm2_g'], sc2, sh2, dh2, dx2)
    gsm['norm2_g'] = dn2
    dt1, dg1 = _res_bwd("res1_bwd", sv['t1'], g1, dx1)
    gops['w_out'] = _mm("w_out_dw", sv['merged'], dt1, ta=True)
    dmerged = _mm("w_out_dx", dt1, tr(ops['w_out']), out_dtype=BF16)
    dga, dgb, dya, dyb = _rowwise(
        "merge_bwd", _vjp_fn(_f_merge, 4, 1, (0, 1, 2, 3)),
        [(z, zo['ga'], d), (z, zo['gb'], d), sv['ya'], sv['yb'], dmerged], [],
        [(d, BF16), (d, BF16), (d, BF16), (d, BF16)])
    gops['wbo'] = _mm("mla_out_dw", sv['o'], dyb, ta=True)
    do = _mm("mla_out_dx", dyb, tr(ops['wbo']), out_dtype=BF16)
    delta = _flash_delta(do, sv['o'], heads=heads, t=dims['tq'])
    as_rows = lambda a: a.reshape(heads, 1, -1)
    dq, dk, dv = _flash_bwd(sv['q'], sv['k'], sv['knv'], do, as_rows(sv['lse']), as_rows(delta),
                            heads=heads, t=dims['tq'])
    def k_bwd(dkb, cosb, sinb):
        dkb = _f32(dkb)
        dkpe = dkb[:, 0:HEAD_PAD]
        for h in range(1, heads):
            dkpe = dkpe + dkb[:, h * HEAD_PAD:(h + 1) * HEAD_PAD]
        return dkpe * cosb, dkpe * sinb
    dkra, dkrb = _rowwise("k_rope_bwd", k_bwd, [dk, cos, sin], [], [(HEAD_PAD, BF16), (HEAD_PAD, BF16)])
    dknv = jnp.concatenate([dk, dv], axis=1)
    gops['wkv'] = _mm("kv_up_dw", sv['ckv'], dknv, ta=True)
    dckv = _mm("kv_up_dx", dknv, tr(ops['wkv']))
    dckv_in, dkvg = _rms_bwd("kv_norm_bwd", z, zo['ckv'], kvl, small['kv_norm_g'], dckv)
    gsm['kv_norm_g'] = dkvg
    def q_bwd(dqb, cosb, sinb):
        dqb = _f32(dqb) * dims['scale']
        return dqb * jnp.tile(cosb, (1, heads)), dqb * jnp.tile(sinb, (1, heads))
    dqa, dqb_ = _rowwise("q_rope_bwd", q_bwd, [dq, cos, sin], [], [(hw, BF16), (hw, BF16)])
    dqab = jnp.concatenate([dqa, dqb_], axis=1)
    gops['wq'] = _mm("q_up_dw", sv['cq'], dqab, ta=True)
    dcq = _mm("q_up_dx", dqab, tr(ops['wq']))
    dcq_in, dqg = _rms_bwd("q_norm_bwd", z, zo['cq'], ql, small['q_norm_g'], dcq)
    gsm['q_norm_g'] = dqg
    gops['w_a_out'] = _mm("s5_out_dw", sv['s5o'], dya, ta=True)
    ds5o = _mm("s5_out_dx", dya, tr(ops['w_a_out']))

    def glu_bwd(yg, pre, ds, b):
        _, vjp = jax.vjp(_f_glu, _f32(yg), _f32(pre), b)
        dyg, dpre, db = vjp(_f32(ds))
        return dyg, dpre, db
    dyg_a, dpre, dbglu = _rowwise("s5_glu_bwd", glu_bwd, [sv['yg'], sv['pre'], ds5o], [small['b_glu']],
                                  [(sw, F32), (sw, BF16)], [sw])
    gsm['b_glu'] = dbglu
    gops['w_glu'] = _mm("s5_glu_mm_dw", sv['yg'], dpre, ta=True)
    dyg_b = _mm("s5_glu_mm_dx", dpre, tr(ops['w_glu']))

    def gelu_bwd(ych, u, dya_, dyb_, dvec):
        _, vjp = jax.vjp(_f_gelu_in, _f32(ych), _f32(u), dvec)
        dych, du, dd = vjp(_f32(dya_) + _f32(dyb_))
        return dych, du, dd
    dy, du_skip, dssm_d = _rowwise("s5_gelu_bwd", gelu_bwd, [sv['ych'], (z, zo['u'], sw), dyg_a, dyg_b],
                                   [small['ssm_d']], [(sw, BF16), (sw, F32)], [sw])
    gsm['ssm_d'] = dssm_d
    dy_seg = _to_segments(dy)
    nsup, sc = dims['nsup'], dims['s5_chunk']
    tr_blocks = lambda a: a.reshape(nsup, -1, a.shape[1]).transpose(0, 2, 1).reshape(-1, a.shape[0] // nsup)
    g_cmat = _mm_bd("s5_y_dw", sv['hst'], dy_seg, nsup, ta=True)
    dh = _mm_bd("s5_y_dx", dy_seg, tr_blocks(cmat).astype(BF16), nsup)
    gst, dar, dai = _scan(dh, a_re, a_im, reverse=True, sc=sc, h=sv['hst'])
    g_bmat = _mm_bd("s5_bu_dw", sv['u_seg'], gst, nsup, ta=True)
    du_scan = _from_segments(_mm_bd("s5_bu_dx", gst, tr_blocks(bmat).astype(BF16), nsup))
    (du,) = _rowwise("s5_du_sum", lambda a, b: _f32(a) + _f32(b), [du_skip, du_scan], [], [(sw, BF16)])
    gs5 = (dar, dai, g_bmat, g_cmat)
    dz = jnp.concatenate([dga, dgb, du, dcq_in, dckv_in, dkra, dkrb], axis=1)
    gops['w_in'] = _mm("w_in_dw", sv['h1'], dz, ta=True)
    dh1 = _mm("w_in_dx", dz, tr(ops['w_in']))
    dx, dn1, dsc1, dsh1 = _norm_mod_bwd("norm1_bwd", sv['x'], small['norm1_g'], sc1, sh1, dh1, dx1)
    gsm['norm1_g'] = dn1
    dmod = (dsh1, dsc1, dg1, dsh2, dsc2, dg2)
    return dx, gops, gs5, gsm, dmod


def _res_bwd(name, t, g, dxo):
    def fn(tb, db, gb):
        db = _f32(db)
        return gb * db, jnp.sum(db * _f32(tb), axis=0, keepdims=True)
    return _rowwise(name, fn, [t, dxo], [g], [(t.shape[1], BF16)], [t.shape[1]])


def _norm_mod_bwd(name, x, g, sc, sh, dh, dres):
    def fn(xb, dhb, dresb, gb, scb, shb):
        _, vjp = jax.vjp(_f_norm_mod, _f32(xb), gb, scb, shb)
        dx, dg, dsc, dsh = vjp(_f32(dhb))
        return dx + _f32(dresb), dg, dsc, dsh
    w = x.shape[1]
    return _rowwise(name, fn, [x, dh, dres], [g, sc, sh], [(w, F32)], [w, w, w])


def _rms_bwd(name, z, off, w, g, dy):
    def fn(xb, dyb, gb):
        _, vjp = jax.vjp(_f_rms, _f32(xb), gb)
        dx, dg = vjp(_f32(dyb))
        return dx, dg
    return _rowwise(name, fn, [(z, off, w), dy], [g], [(w, BF16)], [w])


S5_NAMES = ('ssm_a_re', 'ssm_a_im', 'ssm_log_dt', 'ssm_b_re', 'ssm_b_im', 'ssm_c_re', 'ssm_c_im')
LAYER_VECS = ('norm1_g', 'ssm_d', 'b_glu', 'q_norm_g', 'kv_norm_g', 'norm2_g')


def _step(p, mom_m, mom_v, x, c, positions, loss_target):
    depth, d = p['norm1_g'].shape
    L = x.shape[1]
    sw = p['ssm_d'].shape[1]
    ql, kvl = p['q_norm_g'].shape[1], p['kv_norm_g'].shape[1]
    heads = p['w_uk'].shape[2] * N_DEV // QK_NOPE
    dff = p['w_gate'].shape[2] * N_DEV
    ada_w = p['w_ada'].shape[2]
    zoff, o = {}, 0
    for nm, sz in (('ga', d), ('gb', d), ('u', sw), ('cq', ql), ('ckv', kvl), ('kra', HEAD_PAD), ('krb', HEAD_PAD)):
        assert o % sz == 0, (nm, o, sz)
        zoff[nm] = o
        o += sz
    groups, states = p['ssm_a_re'].shape[1:]
    assert groups % S5_SUPER == 0 and p['ssm_b_re'].shape[3] * S5_SUPER == LANES
    dims = dict(d=d, sw=sw, ql=ql, kvl=kvl, heads=heads, dff=dff, zoff=zoff,
                nsup=groups // S5_SUPER, s5_chunk=S5_SUPER * states,
                tq=1024 if L >= 4096 else 128, scale=(QK_NOPE + QK_ROPE) ** -0.5)
    x = x.reshape(L, d)
    tgt = loss_target.reshape(L, d)

    posf = positions.reshape(L).astype(F32)
    inv_freq = ROPE_BASE ** (-jnp.arange(0, QK_ROPE, 2, dtype=F32) / QK_ROPE)
    ang = posf[:, None] * inv_freq
    cs, sn = jnp.cos(ang), jnp.sin(ang)
    padr = HEAD_PAD - QK_NOPE - QK_ROPE
    cos = jnp.concatenate([jnp.ones((L, QK_NOPE), F32), cs, cs, jnp.zeros((L, padr), F32)], axis=1)
    sin = jnp.concatenate([jnp.zeros((L, QK_NOPE), F32), sn, sn, jnp.zeros((L, padr), F32)], axis=1)
    rope = (cos, sin)

    gathered = _exchange("gather_weights", [p[n].astype(BF16) for n in SHARDED], all_to_all=False)

    def make_ops(gl):
        full = _gathered_to_full(gl)
        return [_layer_operators({n: full[n][l] for n in SHARDED}, dims) for l in range(depth)]

    ops, ops_vjp = jax.vjp(make_ops, [g.astype(F32) for g in gathered])

    def make_s5(sp):
        return [_s5_operators(*[sp[n][l] for n in S5_NAMES]) for l in range(depth)]

    s5ops, s5_vjp = jax.vjp(make_s5, {n: p[n] for n in S5_NAMES})

    (c_slabs,) = _exchange("gather_c", [jnp.pad(c, ((0, 7), (0, 0)))], all_to_all=False)
    c_all = c_slabs[:, 0, :]
    (c_act,) = _rowwise("c_silu", lambda a: jax.nn.silu(a), [jnp.pad(c_all, ((0, 8), (0, 0)))], [], [(d, F32)])
    w_ada_cat = p['w_ada'].transpose(1, 0, 2).reshape(d, depth * ada_w)
    mod_cols = _mm("ada_fwd", c_act, w_ada_cat)[:N_DEV]
    (mod_rows,) = _exchange("a2a_mod", [_pack_slabs(mod_cols)], all_to_all=True)
    mod_mine = _unpack_slabs(mod_rows, depth * ada_w).reshape(N_DEV, depth, ada_w)
    mod_mine = mod_mine.transpose(1, 0, 2).reshape(depth, N_DEV * ada_w)
    (mod_full,) = _rowwise("ada_bias", lambda a, b: a + b, [mod_mine, p['b_ada']], [], [(6 * d, F32)])
    mods = [tuple(mod_full[l:l + 1, i * d:(i + 1) * d] for i in range(6)) for l in range(depth)]

    saved = []
    xl = x
    for l in range(depth):
        small = {n: p[n][l:l + 1] for n in LAYER_VECS}
        xl, sv = _layer_fwd(xl, mods[l], ops[l], s5ops[l], small, rope, dims)
        saved.append((sv, small))

    def final_fn(xb, tb, gb):
        def lossf(xv, gv):
            e = _f_rms(xv, gv) - tb
            per_row = 0.5 * jnp.mean(e * e, axis=-1, keepdims=True)
            return jnp.sum(per_row, axis=0, keepdims=True)
        lv, vjp = jax.vjp(lossf, xb, gb)
        dxb, dgb = vjp(jnp.ones((1, 1), F32))
        return dxb, jnp.broadcast_to(lv, (1, LANES)), dgb
    dx, loss_vec, dfinal_g = _rowwise("final_loss", final_fn, [xl, tgt], [p['final_g'].reshape(1, d)],
                                      [(d, F32)], [LANES, d])

    g_ops, g_s5, g_small, dmods = [None] * depth, [None] * depth, [None] * depth, [None] * depth
    for l in reversed(range(depth)):
        sv, small = saved[l]
        dx, g_ops[l], g_s5[l], g_small[l], dmods[l] = _layer_bwd(dx, sv, mods[l], ops[l], s5ops[l], small,
                                                                rope, dims)
    grad_x = dx.reshape(1, L, d)

    dmod_mine = jnp.stack([jnp.concatenate(dm, axis=1)[0] for dm in dmods])
    dmod_slabs = dmod_mine.reshape(depth, N_DEV, ada_w).transpose(1, 0, 2).reshape(N_DEV, depth * ada_w)
    (dmod_recv,) = _exchange("a2a_dmod", [_pack_slabs(dmod_slabs)], all_to_all=True)
    dmod_cols = _unpack_slabs(dmod_recv, depth * ada_w)
    g_ada = _mm("ada_dw", c_act, jnp.pad(dmod_cols, ((0, 8), (0, 0))), ta=True)
    grads, delta, new_m, new_v = {}, {}, {}, {}
    grads['w_ada'] = g_ada.reshape(d, depth, ada_w).transpose(1, 0, 2)
    two_d = lambda a: a.reshape(-1, a.shape[-1])
    res = _rowwise("adamw_w_ada", _adamw_fn, [two_d(a) for a in (p['w_ada'], grads['w_ada'], mom_m['w_ada'],
                                                                  mom_v['w_ada'])], [], [(ada_w, F32)] * 3)
    delta['w_ada'], new_m['w_ada'], new_v['w_ada'] = [r.reshape(p['w_ada'].shape) for r in res]

    (g_slabs,) = ops_vjp(g_ops)
    g_recv = _exchange("a2a_grads", [g.astype(BF16) for g in g_slabs], all_to_all=True)
    for n, rv in zip(SHARDED, g_recv):
        shp = p[n].shape
        res = _adamw_sharded("adamw_" + n, two_d(p[n]), rv.reshape(N_DEV, -1, shp[-1]), two_d(mom_m[n]),
                             two_d(mom_v[n]))
        grads[n], delta[n], new_m[n], new_v[n] = [r.reshape(shp) for r in res]

    (g_s5p,) = s5_vjp(g_s5)
    part = dict(g_s5p)
    part['b_ada'] = dmod_mine
    for n in LAYER_VECS:
        part[n] = jnp.concatenate([g_small[l][n] for l in range(depth)], axis=0)
    part['final_g'] = dfinal_g.reshape(d)
    small_shapes = [p[n].shape for n in SMALL] + [(1,)]
    (small_recv,) = _exchange("gather_small", [_pack_rows([part[n] for n in SMALL] + [loss_vec[0, 0:1]])],
                              all_to_all=False)
    small_sum = _sum_slabs("sum_small", small_recv)
    small_list = _unpack_rows(small_sum, small_shapes)
    grads.update(zip(SMALL, small_list[:-1]))
    loss = small_list[-1].reshape(())
    dummy = [jnp.zeros((1,), F32)]
    res = _rowwise("adamw_small", _adamw_fn,
                   [_pack_rows([src[n] for n in SMALL] + dummy) for src in (p, )] + [small_sum] +
                   [_pack_rows([src[n] for n in SMALL] + dummy) for src in (mom_m, mom_v)], [], [(FLAT_W, F32)] * 3)
    for dst, r in zip((delta, new_m, new_v), res):
        dst.update(zip(SMALL, _unpack_rows(r, small_shapes)[:-1]))
    return (loss, grad_x, *[grads[n] for n in WEIGHTS], *[delta[n] for n in WEIGHTS],
            *[new_m[n] for n in WEIGHTS], *[new_v[n] for n in WEIGHTS])


def kernel(x, c, positions, w_ada, b_ada, norm1_g, w_in, ssm_a_re, ssm_a_im, ssm_log_dt, ssm_b_re, ssm_b_im, ssm_c_re, ssm_c_im, ssm_d, w_glu, b_glu, w_a_out, q_norm_g, w_uq, kv_norm_g, w_uk, w_uv, w_b_out, w_out, norm2_g, w_gate, w_up, w_down, final_g, loss_target, m_w_ada, m_b_ada, m_norm1_g, m_w_in, m_ssm_a_re, m_ssm_a_im, m_ssm_log_dt, m_ssm_b_re, m_ssm_b_im, m_ssm_c_re, m_ssm_c_im, m_ssm_d, m_w_glu, m_b_glu, m_w_a_out, m_q_norm_g, m_w_uq, m_kv_norm_g, m_w_uk, m_w_uv, m_w_b_out, m_w_out, m_norm2_g, m_w_gate, m_w_up, m_w_down, m_final_g, v_w_ada, v_b_ada, v_norm1_g, v_w_in, v_ssm_a_re, v_ssm_a_im, v_ssm_log_dt, v_ssm_b_re, v_ssm_b_im, v_ssm_c_re, v_ssm_c_im, v_ssm_d, v_w_glu, v_b_glu, v_w_a_out, v_q_norm_g, v_w_uq, v_kv_norm_g, v_w_uk, v_w_uv, v_w_b_out, v_w_out, v_norm2_g, v_w_gate, v_w_up, v_w_down, v_final_g):
    given = dict(locals())
    p = {n: given[n] for n in WEIGHTS}
    mom_m = {n: given["m_" + n] for n in WEIGHTS}
    mom_v = {n: given["v_" + n] for n in WEIGHTS}
    return _step(p, mom_m, mom_v, x, c, positions, loss_target)
```

```python
import functools
import math

import jax
import jax.numpy as jnp
from jax import lax
from jax.experimental import pallas as pl
from jax.experimental.pallas import tpu as pltpu

F32 = jnp.float32
BF16 = jnp.bfloat16

N_DEV = 8
LANES = 128
FLAT_W = 1024
VMEM_LIMIT = 48 * 1024 * 1024
QK_NOPE, QK_ROPE, V_DIM = 64, 32, 64
HEAD_PAD = LANES
ROPE_BASE = 10000.0
EPS = 1e-6
ADAM_LR, ADAM_B1, ADAM_B2, ADAM_EPS, ADAM_WD, ADAM_STEP = 0.001, 0.9, 0.999, 1e-08, 0.01, 10
NEG = float(jnp.finfo(jnp.float32).min)

WEIGHTS = ['w_ada', 'b_ada', 'norm1_g', 'w_in', 'ssm_a_re', 'ssm_a_im', 'ssm_log_dt', 'ssm_b_re', 'ssm_b_im',
           'ssm_c_re', 'ssm_c_im', 'ssm_d', 'w_glu', 'b_glu', 'w_a_out', 'q_norm_g', 'w_uq', 'kv_norm_g', 'w_uk',
           'w_uv', 'w_b_out', 'w_out', 'norm2_g', 'w_gate', 'w_up', 'w_down', 'final_g']
COL_SHARDED = ['w_in', 'w_a_out', 'w_uq', 'w_uk', 'w_uv', 'w_b_out', 'w_gate', 'w_up']
ROW_SHARDED = ['w_glu', 'w_out', 'w_down']
SHARDED = COL_SHARDED + ROW_SHARDED
SMALL = ['b_ada', 'norm1_g', 'ssm_a_re', 'ssm_a_im', 'ssm_log_dt', 'ssm_b_re', 'ssm_b_im', 'ssm_c_re', 'ssm_c_im',
         'ssm_d', 'b_glu', 'q_norm_g', 'kv_norm_g', 'norm2_g', 'final_g']


def _cparams(sem):
    return pltpu.CompilerParams(dimension_semantics=sem, vmem_limit_bytes=VMEM_LIMIT)


def _pick(dim, pref, quantum=LANES):
    if dim <= pref:
        return dim
    t = (pref // quantum) * quantum
    while t >= quantum:
        if dim % t == 0:
            return t
        t -= quantum
    return dim


def _mm(name, a, b, *, ta=False, out_dtype=F32, a_col=None, b_col=None, tm=1408, tn=1408, tk=None):
    tk = tk or (512 if ta else 1024)
    a_off, a_w = a_col if a_col is not None else (0, a.shape[1])
    b_off, b_w = b_col if b_col is not None else (0, b.shape[1])
    if ta:
        kdim, m = a.shape[0], a_w
        assert b.shape[0] == kdim
    else:
        m, kdim = a.shape[0], a_w
        assert b.shape[0] == kdim, (name, a.shape, b.shape)
    n = b_w
    tm = _pick(m, tm, LANES if ta else 8)
    tn = _pick(n, tn)
    tk = _pick(kdim, tk, 8 if ta else LANES)
    nk = kdim // tk
    assert m % tm == 0 and n % tn == 0 and kdim % tk == 0, (name, m, n, kdim, tm, tn, tk)
    if ta:
        assert a_off % tm == 0 and b_off % tn == 0
        a_spec = pl.BlockSpec((tk, tm), lambda i, j, k: (k, i + a_off // tm))
        dims = (((0,), (0,)), ((), ()))
    else:
        assert a_off % tk == 0 and b_off % tn == 0
        a_spec = pl.BlockSpec((tm, tk), lambda i, j, k: (i, k + a_off // tk))
        dims = (((1,), (0,)), ((), ()))
    b_spec = pl.BlockSpec((tk, tn), lambda i, j, k: (k, j + b_off // tn))

    def prod(a_ref, b_ref):
        return lax.dot_general(a_ref[...].astype(BF16), b_ref[...].astype(BF16), dims, preferred_element_type=F32)

    def body_one(a_ref, b_ref, o_ref):
        o_ref[...] = prod(a_ref, b_ref).astype(o_ref.dtype)

    def body_acc(a_ref, b_ref, o_ref, acc_ref):
        @pl.when(pl.program_id(2) == 0)
        def _():
            acc_ref[...] = jnp.zeros_like(acc_ref)

        acc_ref[...] += prod(a_ref, b_ref)

        @pl.when(pl.program_id(2) == nk - 1)
        def _():
            o_ref[...] = acc_ref[...].astype(o_ref.dtype)

    return pl.pallas_call(
        body_one if nk == 1 else body_acc, name=name,
        out_shape=jax.ShapeDtypeStruct((m, n), out_dtype),
        grid=(m // tm, n // tn, nk),
        in_specs=[a_spec, b_spec],
        out_specs=pl.BlockSpec((tm, tn), lambda i, j, k: (i, j)),
        scratch_shapes=[] if nk == 1 else [pltpu.VMEM((tm, tn), F32)],
        compiler_params=_cparams(("parallel", "parallel", "arbitrary")),
    )(a, b)


def _rowwise(name, fn, rows, vecs, outs, reds=(), *, tm=256):
    rows = [(r, 0, r.shape[1]) if not isinstance(r, tuple) else r for r in rows]
    nrows = rows[0][0].shape[0]
    tm = _pick(nrows, tm, 8)
    assert nrows % tm == 0, (name, nrows, tm)
    nr, nv, no = len(rows), len(vecs), len(outs)
    in_specs = []
    for arr, off, w in rows:
        assert arr.shape[0] == nrows and off % w == 0, (name, arr.shape, off, w)
        in_specs.append(pl.BlockSpec((tm, w), functools.partial(lambda i, cb: (i, cb), cb=off // w)))
    for v in vecs:
        assert v.ndim == 2 and v.shape[0] == 1, (name, v.shape)
        in_specs.append(pl.BlockSpec(v.shape, lambda i: (0, 0)))
    out_shape = [jax.ShapeDtypeStruct((nrows, w), dt) for w, dt in outs]
    out_specs = [pl.BlockSpec((tm, w), lambda i: (i, 0)) for w, dt in outs]
    out_shape += [jax.ShapeDtypeStruct((1, w), F32) for w in reds]
    out_specs += [pl.BlockSpec((1, w), lambda i: (0, 0)) for w in reds]

    def body(*refs):
        rin, vin = refs[:nr], refs[nr:nr + nv]
        rout, rred = refs[nr + nv:nr + nv + no], refs[nr + nv + no:]
        res = fn(*[r[...] for r in rin], *[v[...] for v in vin])
        if not isinstance(res, (tuple, list)):
            res = (res,)
        assert len(res) == no + len(reds), (name, len(res))
        for r, val in zip(rout, res[:no]):
            r[...] = val.astype(r.dtype)
        if reds:
            @pl.when(pl.program_id(0) == 0)
            def _():
                for r in rred:
                    r[...] = jnp.zeros_like(r)

            for r, val in zip(rred, res[no:]):
                r[...] += val.astype(F32)

    res = pl.pallas_call(
        body, name=name, out_shape=out_shape, grid=(nrows // tm,),
        in_specs=in_specs, out_specs=out_specs,
        compiler_params=_cparams(("arbitrary",) if reds else ("parallel",)),
    )(*[r[0] for r in rows], *vecs)
    return res


def _f32(x):
    return x.astype(F32)


def _vjp_fn(f, n_in, n_cot, want):
    def fn(*args):
        ins = [_f32(a) for a in args[:n_in]]
        cots = tuple(_f32(a) for a in args[n_in:n_in + n_cot])
        _, vjp = jax.vjp(f, *ins)
        grads = vjp(cots if n_cot > 1 else cots[0])
        return tuple(grads[i] for i in want)
    return fn


def _f_rms(x, g):
    return (x * lax.rsqrt(jnp.mean(x * x, axis=-1, keepdims=True) + EPS)) * g


def _f_norm_mod(x, g, sc, sh):
    return _f_rms(x, g) * (1.0 + sc) + sh


def _f_gelu_in(ych, u, d):
    return jax.nn.gelu(ych + d * u)


def _f_glu(yg, pre, b):
    return yg * jax.nn.sigmoid(pre + b)


def _f_merge(ga, gb, ya, yb):
    return jax.nn.sigmoid(ga) * ya + jax.nn.sigmoid(gb) * yb


def _f_res(x, t, g):
    return x + g * t


def _f_swiglu(a, b):
    return jax.nn.silu(a) * b


def _f_rope_q(qa, qb, cos, sin):
    h = qa.shape[1] // HEAD_PAD
    return qa * jnp.tile(cos, (1, h)) + qb * jnp.tile(sin, (1, h))


def _f_rope_k(kn, kra, krb, cos, sin):
    h = kn.shape[1] // HEAD_PAD
    return kn + jnp.tile(kra * cos + krb * sin, (1, h))


SUBLANES = 8
S5_SUPER = 8


def _cpow(ar, ai, log2n):
    for _ in range(log2n):
        ar, ai = ar * ar - ai * ai, 2.0 * ar * ai
    return ar, ai


def _to_segments(x):
    L, w = x.shape
    return x.reshape(SUBLANES, L // SUBLANES, w).transpose(1, 0, 2).reshape(L, w)


def _from_segments(x):
    L, w = x.shape
    return x.reshape(L // SUBLANES, SUBLANES, w).transpose(1, 0, 2).reshape(L, w)


def _s5_pass(name, xin, a_re, a_im, w_in, *, reverse, nsup, sc, ends=None, w_out=None, u=None, h=None, tb=512):
    L, sw = xin.shape
    gp2 = nsup * 2 * sc
    gp = gp2 // 2
    seg = L // SUBLANES
    assert sw == nsup * LANES and L % SUBLANES == 0 and seg & (seg - 1) == 0, (L, sw)
    tb = _pick(L, tb, SUBLANES)
    nb, nt = L // tb, tb // SUBLANES
    first_pass = ends is None
    sign = -1.0 if reverse else 1.0
    blk = (lambda i: (nb - 1 - i, 0)) if reverse else (lambda i: (i, 0))
    order = (lambda s: nt - 1 - s) if reverse else (lambda s: s)
    tile = lambda s: pl.ds(pl.multiple_of(s * SUBLANES, SUBLANES), SUBLANES)
    const = lambda shape: pl.BlockSpec(shape, lambda i: (0, 0))
    rows_in = pl.BlockSpec((tb, sw), blk)
    rows_st = pl.BlockSpec((tb, gp2), blk)

    def coeffs(ar_ref, ai_ref, cc):
        ar1, ai1 = ar_ref[:, cc], sign * ai_ref[:, cc]
        return ar1, ai1, jnp.broadcast_to(ar1, (SUBLANES, sc)), jnp.broadcast_to(ai1, (SUBLANES, sc))

    def drive(x_ref, w_ref, xs_ref, c):
        lanes = pl.ds(c * LANES, LANES)
        xs_ref[...] = jnp.dot(x_ref[:, lanes].astype(BF16), w_ref[lanes, :], preferred_element_type=F32)

    def body(*refs):
        it = iter(refs)
        x_ref, w_ref, ar_ref, ai_ref = next(it), next(it), next(it), next(it)
        if first_pass:
            e_ref, xs_ref = next(it), next(it)
        elif not reverse:
            e_ref, wo_ref, o_ref, y_ref, st_ref, xs_ref = (next(it) for _ in range(6))
        else:
            (e_ref, wo_ref, u_ref, h_ref, hp_ref, hl_ref, du_ref, gb_ref, gc_ref, dar_ref, dai_ref,
             st_ref, xs_ref, g_ref) = (next(it) for _ in range(14))
        i = pl.program_id(0)

        @pl.when(i == 0)
        def _():
            if first_pass:
                e_ref[...] = jnp.zeros_like(e_ref)
                return
            for c in range(nsup):
                cc, re, im = pl.ds(c * sc, sc), pl.ds(2 * c * sc, sc), pl.ds((2 * c + 1) * sc, sc)
                ar1, ai1, _, _ = coeffs(ar_ref, ai_ref, cc)
                pr, pi = _cpow(ar1, ai1, seg.bit_length() - 1)
                cr = jnp.zeros((1, sc), F32)
                ci = jnp.zeros((1, sc), F32)
                for j in (reversed(range(SUBLANES)) if reverse else range(SUBLANES)):
                    st_ref[j:j + 1, re] = cr
                    st_ref[j:j + 1, im] = ci
                    cr, ci = (e_ref[j:j + 1, re] + pr * cr - pi * ci, e_ref[j:j + 1, im] + pr * ci + pi * cr)
            if reverse:
                for r in (gb_ref, gc_ref, dar_ref, dai_ref):
                    r[...] = jnp.zeros_like(r)

        sub = lax.broadcasted_iota(jnp.int32, (SUBLANES, sc), 0)
        state = e_ref if first_pass else st_ref
        xr, xi = pl.ds(0, sc), pl.ds(sc, sc)
        for c in range(nsup):
            cc, re, im = pl.ds(c * sc, sc), pl.ds(2 * c * sc, sc), pl.ds((2 * c + 1) * sc, sc)
            both = pl.ds(2 * c * sc, 2 * sc)
            lanes = pl.ds(c * LANES, LANES)
            _, _, ar, ai = coeffs(ar_ref, ai_ref, cc)
            drive(x_ref, w_ref, xs_ref, c)

            def advance(rows, sr, si):
                return ar * sr - ai * si + xs_ref[rows, xr], ar * si + ai * sr + xs_ref[rows, xi]

            if first_pass:
                def step(s, st):
                    return advance(tile(order(s)), *st)

                sr, si = lax.fori_loop(0, nt, step, (state[:, re], state[:, im]), unroll=4)
            elif not reverse:
                def step(s, st):
                    rows = tile(s)
                    sr, si = advance(rows, *st)
                    o_ref[rows, re] = sr
                    o_ref[rows, im] = si
                    return sr, si

                sr, si = lax.fori_loop(0, nt, step, (state[:, re], state[:, im]), unroll=4)
                y_ref[:, lanes] = jnp.dot(o_ref[:, both].astype(BF16), wo_ref[both, :], preferred_element_type=F32)
            else:
                def emit(rows, sr, si):
                    sr, si = advance(rows, sr, si)
                    g_ref[rows, xr] = sr
                    g_ref[rows, xi] = si
                    return sr, si

                def grad(sr, si, hpr, hpi, accr, acci):
                    return accr + sr * hpr + si * hpi, acci + si * hpr - sr * hpi

                def step(s, st):
                    sr, si, accr, acci = st
                    t = nt - 1 - s
                    sr, si = emit(tile(t), sr, si)
                    prev = tile(t - 1)
                    return (sr, si) + grad(sr, si, h_ref[prev, re], h_ref[prev, im], accr, acci)

                zero = jnp.zeros((SUBLANES, sc), F32)
                sr, si, accr, acci = lax.fori_loop(0, nt - 1, step, (state[:, re], state[:, im], zero, zero),
                                                   unroll=4)
                sr, si = emit(tile(0), sr, si)
                first = (i == nb - 1)
                wrap_r = jnp.where(sub == 0, 0.0, pltpu.roll(hl_ref[:, re], 1, 0))
                wrap_i = jnp.where(sub == 0, 0.0, pltpu.roll(hl_ref[:, im], 1, 0))
                accr, acci = grad(sr, si, jnp.where(first, wrap_r, hp_ref[:, re]),
                                  jnp.where(first, wrap_i, hp_ref[:, im]), accr, acci)
                dar_ref[:, cc] += jnp.sum(accr, axis=0, keepdims=True)
                dai_ref[:, cc] += jnp.sum(acci, axis=0, keepdims=True)
                gb = g_ref[...].astype(BF16)
                du_ref[:, lanes] = jnp.dot(gb, wo_ref[both, :], preferred_element_type=F32)
                gb_ref[lanes, :] += lax.dot_general(u_ref[:, lanes].astype(BF16), gb, _TN,
                                                    preferred_element_type=F32)
                gc_ref[both, :] += lax.dot_general(h_ref[:, both].astype(BF16), x_ref[:, lanes].astype(BF16), _TN,
                                                   preferred_element_type=F32)
            state[:, re] = sr
            state[:, im] = si

    vec = const((1, gp))
    in_specs = [rows_in, const(w_in.shape), vec, vec]
    operands = [xin, w_in, a_re, a_im]
    xs_scratch = pltpu.VMEM((tb, 2 * sc), F32)
    st_scratch = pltpu.VMEM((SUBLANES, gp2), F32)
    if first_pass:
        out_shape = jax.ShapeDtypeStruct((SUBLANES, gp2), F32)
        out_specs = const((SUBLANES, gp2))
        scratch = [xs_scratch]
    elif not reverse:
        in_specs += [const((SUBLANES, gp2)), const(w_out.shape)]
        operands += [ends, w_out]
        out_shape = [jax.ShapeDtypeStruct((L, gp2), F32), jax.ShapeDtypeStruct((L, sw), F32)]
        out_specs = [rows_st, rows_in]
        scratch = [st_scratch, xs_scratch]
    else:
        in_specs += [const((SUBLANES, gp2)), const(w_out.shape), rows_in, rows_st,
                     pl.BlockSpec((SUBLANES, gp2), lambda i: (jnp.maximum((nb - 1 - i) * nt - 1, 0), 0)),
                     pl.BlockSpec((SUBLANES, gp2), lambda i: (seg - 1, 0))]
        operands += [ends, w_out, u, h, h, h]
        out_shape = [jax.ShapeDtypeStruct((L, sw), F32), jax.ShapeDtypeStruct((sw, 2 * sc), F32),
                     jax.ShapeDtypeStruct((gp2, LANES), F32), jax.ShapeDtypeStruct((1, gp), F32),
                     jax.ShapeDtypeStruct((1, gp), F32)]
        out_specs = [rows_in, const((sw, 2 * sc)), const((gp2, LANES)), vec, vec]
        scratch = [st_scratch, xs_scratch, pltpu.VMEM((tb, 2 * sc), F32)]
    return pl.pallas_call(
        body, name=name, out_shape=out_shape, grid=(nb,), in_specs=in_specs, out_specs=out_specs,
        scratch_shapes=scratch, compiler_params=_cparams(("arbitrary",)),
    )(*operands)


_NT = (((1,), (1,)), ((), ()))
_TN = (((0,), (0,)), ((), ()))


def _causal(s, t, k_major=False):
    row = lax.broadcasted_iota(jnp.int32, (t, t), 0)
    col = lax.broadcasted_iota(jnp.int32, (t, t), 1)
    return jnp.where(row <= col if k_major else col <= row, s, NEG)


def _pair_tables(n, k_major):
    if k_major:
        pairs = [(qi, ki) for ki in range(n) for qi in range(ki, n)]
    else:
        pairs = [(qi, ki) for qi in range(n) for ki in range(qi + 1)]
    return (jnp.asarray([p[0] for p in pairs], jnp.int32), jnp.asarray([p[1] for p in pairs], jnp.int32))


def _flash_fwd(q, k, knv, *, heads, t):
    L = q.shape[0]
    n = L // t
    rep = t // LANES
    qtab, ktab = _pair_tables(n, k_major=False)

    def body(qt_ref, kt_ref, q_ref, k_ref, v_ref, o_ref, lse_ref, m_s, acc_s):
        step = pl.program_id(1)
        qi, ki = qt_ref[step], kt_ref[step]
        lane = lax.broadcasted_iota(jnp.int32, (t, HEAD_PAD), 1)

        @pl.when(ki == 0)
        def _():
            m_s[...] = jnp.full(m_s.shape, NEG, F32)
            acc_s[...] = jnp.zeros_like(acc_s)

        def update(diagonal):
            s = lax.dot_general(q_ref[...], k_ref[...], _NT, preferred_element_type=F32)
            if diagonal:
                s = _causal(s, t)
            m_prev = m_s[...]
            m_next = jnp.maximum(m_prev, jnp.max(s, axis=1, keepdims=True))
            p = jnp.exp(s - jnp.tile(m_next, (1, rep)))
            vb = jnp.where(lane == V_DIM, jnp.ones((), BF16), v_ref[...])
            acc_s[...] = jnp.exp(m_prev - m_next) * acc_s[...] + jnp.dot(p.astype(BF16), vb,
                                                                         preferred_element_type=F32)
            m_s[...] = m_next

        @pl.when(ki < qi)
        def _():
            update(False)

        @pl.when(ki == qi)
        def _():
            update(True)
            acc = acc_s[...]
            l = jnp.sum(jnp.where(lane == V_DIM, acc, 0.0), axis=1, keepdims=True)
            o_ref[...] = jnp.where(lane == V_DIM, 0.0, acc * (1.0 / l)).astype(o_ref.dtype)
            lse_ref[0] = jnp.max(m_s[...], axis=1, keepdims=True) + jnp.log(l)

    q_map = lambda h, s, qt, kt: (qt[s], h)
    kv_map = lambda h, s, qt, kt: (kt[s], h)
    v_map = lambda h, s, qt, kt: (kt[s], h + heads)
    return pl.pallas_call(
        body, name="mla_flash_fwd",
        out_shape=[jax.ShapeDtypeStruct((L, heads * HEAD_PAD), BF16),
                   jax.ShapeDtypeStruct((heads, L, 1), F32)],
        grid_spec=pltpu.PrefetchScalarGridSpec(
            num_scalar_prefetch=2, grid=(heads, qtab.shape[0]),
            in_specs=[pl.BlockSpec((t, HEAD_PAD), q_map),
                      pl.BlockSpec((t, HEAD_PAD), kv_map),
                      pl.BlockSpec((t, HEAD_PAD), v_map)],
            out_specs=[pl.BlockSpec((t, HEAD_PAD), q_map),
                       pl.BlockSpec((1, t, 1), lambda h, s, qt, kt: (h, qt[s], 0))],
            scratch_shapes=[pltpu.VMEM((t, LANES), F32), pltpu.VMEM((t, HEAD_PAD), F32)]),
        compiler_params=_cparams(("parallel", "arbitrary")),
    )(qtab, ktab, q, k, knv)


def _flash_delta(do, o, *, heads, t):
    L = do.shape[0]

    def body(do_ref, o_ref, d_ref):
        d_ref[0] = jnp.sum(do_ref[...].astype(F32) * o_ref[...].astype(F32), axis=1, keepdims=True)

    return pl.pallas_call(
        body, name="mla_flash_delta",
        out_shape=jax.ShapeDtypeStruct((heads, L, 1), F32),
        grid=(heads, L // t),
        in_specs=[pl.BlockSpec((t, HEAD_PAD), lambda h, i: (i, h)),
                  pl.BlockSpec((t, HEAD_PAD), lambda h, i: (i, h))],
        out_specs=pl.BlockSpec((1, t, 1), lambda h, i: (h, i, 0)),
        compiler_params=_cparams(("parallel", "parallel")),
    )(do, o)


def _flash_bwd(q, k, knv, do, lse, delta, *, heads, t):
    L = q.shape[0]
    n = L // t
    qtab, ktab = _pair_tables(n, k_major=True)

    def body(qt_ref, kt_ref, q_ref, k_ref, v_ref, do_ref, lse_ref, dl_ref, dq_ref, dk_ref, dv_ref, dk_s, dv_s):
        step = pl.program_id(1)
        qi, ki = qt_ref[step], kt_ref[step]

        @pl.when(qi == ki)
        def _():
            dk_s[...] = jnp.zeros_like(dk_s)
            dv_s[...] = jnp.zeros_like(dv_s)

        def update(diagonal):
            qb, kb, vb, dob = q_ref[...], k_ref[...], v_ref[...], do_ref[...]
            st = lax.dot_general(kb, qb, _NT, preferred_element_type=F32)
            if diagonal:
                st = _causal(st, t, k_major=True)
            pt = jnp.exp(st - lse_ref[0])
            dv_s[...] += jnp.dot(pt.astype(BF16), dob, preferred_element_type=F32)
            dpt = lax.dot_general(vb, dob, _NT, preferred_element_type=F32)
            dst = (pt * (dpt - dl_ref[0])).astype(BF16)
            dk_s[...] += jnp.dot(dst, qb, preferred_element_type=F32)
            dqb = lax.dot_general(dst, kb, _TN, preferred_element_type=F32)
            rows = pl.ds(pl.multiple_of(qi * t, t), t)

            @pl.when(ki == 0)
            def _():
                dq_ref[rows, :] = dqb

            @pl.when(ki > 0)
            def _():
                dq_ref[rows, :] += dqb

        @pl.when(qi > ki)
        def _():
            update(False)

        @pl.when(qi == ki)
        def _():
            update(True)

        @pl.when(qi == n - 1)
        def _():
            dk_ref[...] = dk_s[...].astype(dk_ref.dtype)
            dv_ref[...] = dv_s[...].astype(dv_ref.dtype)

    q_map = lambda h, s, qt, kt: (qt[s], h)
    stat_map = lambda h, s, qt, kt: (h, 0, qt[s])
    kv_map = lambda h, s, qt, kt: (kt[s], h)
    v_map = lambda h, s, qt, kt: (kt[s], h + heads)
    return pl.pallas_call(
        body, name="mla_flash_bwd",
        out_shape=[jax.ShapeDtypeStruct((L, heads * HEAD_PAD), F32),
                   jax.ShapeDtypeStruct((L, heads * HEAD_PAD), BF16),
                   jax.ShapeDtypeStruct((L, heads * HEAD_PAD), BF16)],
        grid_spec=pltpu.PrefetchScalarGridSpec(
            num_scalar_prefetch=2, grid=(heads, qtab.shape[0]),
            in_specs=[pl.BlockSpec((t, HEAD_PAD), q_map),
                      pl.BlockSpec((t, HEAD_PAD), kv_map),
                      pl.BlockSpec((t, HEAD_PAD), v_map),
                      pl.BlockSpec((t, HEAD_PAD), q_map),
                      pl.BlockSpec((1, 1, t), stat_map),
                      pl.BlockSpec((1, 1, t), stat_map)],
            out_specs=[pl.BlockSpec((L, HEAD_PAD), lambda h, s, qt, kt: (0, h)),
                       pl.BlockSpec((t, HEAD_PAD), kv_map),
                       pl.BlockSpec((t, HEAD_PAD), kv_map)],
            scratch_shapes=[pltpu.VMEM((t, HEAD_PAD), F32), pltpu.VMEM((t, HEAD_PAD), F32)]),
        compiler_params=_cparams(("parallel", "arbitrary")),
    )(qtab, ktab, q, k, knv, do, lse, delta)


def _peer(k):
    mx, my, mc = lax.axis_index("x"), lax.axis_index("y"), lax.axis_index("c")
    px = 1 - mx if (k >> 2) & 1 else mx
    py = 1 - my if (k >> 1) & 1 else my
    pc = 1 - mc if k & 1 else mc
    return (px, py, pc), 4 * px + 2 * py + pc


def _exchange(name, xs, all_to_all):
    n = len(xs)
    any_spec = pl.BlockSpec(memory_space=pl.ANY)
    npeer = N_DEV - 1

    def body(*refs):
        x_refs, o_refs = refs[:n], refs[n:2 * n]
        send_sems, recv_sems, local_sems = refs[2 * n:]
        _, me = _peer(0)
        mine = [x.at[me] if all_to_all else x for x in x_refs]
        local = [pltpu.make_async_copy(mine[i], o_refs[i].at[me], local_sems.at[i]) for i in range(n)]
        for cp in local:
            cp.start()
        sends = []
        for k in range(1, N_DEV):
            dev, idx = _peer(k)
            for i in range(n):
                cp = pltpu.make_async_remote_copy(
                    src_ref=x_refs[i].at[idx] if all_to_all else x_refs[i], dst_ref=o_refs[i].at[me],
                    send_sem=send_sems.at[i * npeer + k - 1], recv_sem=recv_sems.at[i * npeer + k - 1],
                    device_id=dev, device_id_type=pl.DeviceIdType.MESH)
                cp.start()
                sends.append(cp)
        for k in range(1, N_DEV):
            dev, idx = _peer(k)
            for i in range(n):
                pltpu.make_async_remote_copy(
                    src_ref=mine[i], dst_ref=o_refs[i].at[idx],
                    send_sem=send_sems.at[i * npeer + k - 1], recv_sem=recv_sems.at[i * npeer + k - 1],
                    device_id=dev, device_id_type=pl.DeviceIdType.MESH).wait_recv()
        for cp in sends:
            cp.wait_send()
        for cp in local:
            cp.wait()

    return pl.pallas_call(
        body, name=name,
        out_shape=[jax.ShapeDtypeStruct((N_DEV,) + tuple(x.shape[1:] if all_to_all else x.shape), x.dtype)
                   for x in xs],
        in_specs=[any_spec] * n, out_specs=[any_spec] * n,
        scratch_shapes=[pltpu.SemaphoreType.DMA((n * npeer,)), pltpu.SemaphoreType.DMA((n * npeer,)),
                        pltpu.SemaphoreType.DMA((n,))],
    )(*xs)


def _sum_slabs(name, x, *, tr=128):
    _, r, w = x.shape
    tr = _pick(r, tr, 8)

    def body(x_ref, o_ref):
        acc = x_ref[0]
        for j in range(1, N_DEV):
            acc = acc + x_ref[j]
        o_ref[...] = acc

    return pl.pallas_call(
        body, name=name, out_shape=jax.ShapeDtypeStruct((r, w), F32), grid=(r // tr,),
        in_specs=[pl.BlockSpec((N_DEV, tr, w), lambda i: (0, i, 0))],
        out_specs=pl.BlockSpec((tr, w), lambda i: (i, 0)),
        compiler_params=_cparams(("parallel",)),
    )(x)


def _adamw_fn(w, g, m, v):
    m = ADAM_B1 * m + (1.0 - ADAM_B1) * g
    v = ADAM_B2 * v + (1.0 - ADAM_B2) * jnp.square(g)
    m_hat = m / (1.0 - ADAM_B1 ** ADAM_STEP)
    v_hat = v / (1.0 - ADAM_B2 ** ADAM_STEP)
    delta = -ADAM_LR * (m_hat / (jnp.sqrt(v_hat) + ADAM_EPS) + ADAM_WD * w)
    return delta, m, v


def _adamw_sharded(name, w, recv, m, v, *, tr=128):
    rows, c = w.shape
    tr = _pick(rows, tr, 8)

    def body(w_ref, r_ref, m_ref, v_ref, g_ref, d_ref, mo_ref, vo_ref):
        g = r_ref[0].astype(F32)
        for j in range(1, N_DEV):
            g = g + r_ref[j].astype(F32)
        d, mn, vn = _adamw_fn(w_ref[...], g, m_ref[...], v_ref[...])
        g_ref[...] = g
        d_ref[...] = d
        mo_ref[...] = mn
        vo_ref[...] = vn

    blk = pl.BlockSpec((tr, c), lambda i: (i, 0))
    return pl.pallas_call(
        body, name=name, out_shape=[jax.ShapeDtypeStruct((rows, c), F32)] * 4, grid=(rows // tr,),
        in_specs=[blk, pl.BlockSpec((N_DEV, tr, c), lambda i: (0, i, 0)), blk, blk],
        out_specs=[blk] * 4,
        compiler_params=_cparams(("parallel",)),
    )(w, recv, m, v)


def _piece_rows(shape):
    return -(-math.prod(shape) // (8 * FLAT_W)) * 8


def _pack_rows(arrs):
    out = []
    for a in arrs:
        flat = a.reshape(-1)
        rows = _piece_rows(a.shape)
        out.append(jnp.pad(flat, (0, rows * FLAT_W - flat.shape[0])).reshape(rows, FLAT_W))
    return jnp.concatenate(out, axis=0)


def _unpack_rows(packed, shapes):
    out, r0 = [], 0
    for s in shapes:
        rows = _piece_rows(s)
        out.append(packed[r0:r0 + rows].reshape(-1)[:math.prod(s)].reshape(s))
        r0 += rows
    return out


def _pack_slabs(a):
    n = a.shape[1]
    rows = _piece_rows((n,))
    return jnp.pad(a, ((0, 0), (0, rows * FLAT_W - n))).reshape(N_DEV, rows, FLAT_W)


def _unpack_slabs(a, n):
    return a.reshape(N_DEV, -1)[:, :n]


def _s5_operators(a_re, a_im, log_dt, b_re, b_im, c_re, c_im):
    g, p, m = b_re.shape
    dt = jnp.exp(log_dt)[:, None]
    mag = jnp.exp(a_re * dt)
    abar_re = mag * jnp.cos(a_im * dt)
    abar_im = mag * jnp.sin(a_im * dt)
    den = a_re * a_re + a_im * a_im
    nr = abar_re - 1.0
    ni = abar_im
    coef_re = ((nr * a_re + ni * a_im) / den)[..., None]
    coef_im = ((ni * a_re - nr * a_im) / den)[..., None]
    bbar_re = coef_re * b_re - coef_im * b_im
    bbar_im = coef_re * b_im + coef_im * b_re
    sup, ns = S5_SUPER, g // S5_SUPER
    eye = jnp.eye(sup, dtype=F32)

    def b_blocks(bb):
        return jnp.einsum('cgpm,gh->cgmhp', bb.reshape(ns, sup, p, m), eye).reshape(ns, sup * m, sup * p)

    def c_blocks(cb):
        return jnp.einsum('cgmp,gh->chpgm', cb.reshape(ns, sup, m, p), eye).reshape(ns, sup * p, sup * m)

    bmat = jnp.concatenate([b_blocks(bbar_re), b_blocks(bbar_im)], axis=2).reshape(g * m, 2 * sup * p)
    cmat = jnp.concatenate([c_blocks(c_re), c_blocks(-c_im)], axis=1).reshape(ns * 2 * sup * p, sup * m)
    return abar_re.reshape(1, g * p), abar_im.reshape(1, g * p), bmat, cmat


def _rot_cols(w):
    half = w.shape[-1] // 2
    return jnp.concatenate([-w[..., half:], w[..., :half]], axis=-1)


def _layer_operators(w, dims):
    d, sw, ql, kvl, heads, dff = dims['d'], dims['sw'], dims['ql'], dims['kvl'], dims['heads'], dims['dff']
    w_in = w['w_in']
    o = 0
    parts = {}
    for nm, sz in (('u', sw), ('cq', ql), ('ckv', kvl), ('kr', QK_ROPE), ('ga', d), ('gb', d)):
        parts[nm] = w_in[:, o:o + sz]
        o += sz
    zpad = lambda n: jnp.zeros((d, n), w_in.dtype)
    kra = jnp.concatenate([zpad(QK_NOPE), parts['kr'], zpad(HEAD_PAD - QK_NOPE - QK_ROPE)], axis=1)
    krb = jnp.concatenate([zpad(QK_NOPE), _rot_cols(parts['kr']), zpad(HEAD_PAD - QK_NOPE - QK_ROPE)], axis=1)
    w_in_x = jnp.concatenate([parts['ga'], parts['gb'], parts['u'], parts['cq'], parts['ckv'], kra, krb], axis=1)

    wq = w['w_uq'].reshape(ql, heads, QK_NOPE + QK_ROPE)
    qz = lambda n: jnp.zeros((ql, heads, n), wq.dtype)
    wq_a = jnp.concatenate([wq, qz(HEAD_PAD - QK_NOPE - QK_ROPE)], axis=2)
    wq_b = jnp.concatenate([qz(QK_NOPE), _rot_cols(wq[:, :, QK_NOPE:]), qz(HEAD_PAD - QK_NOPE - QK_ROPE)], axis=2)
    wq_x = jnp.concatenate([wq_a.reshape(ql, -1), wq_b.reshape(ql, -1)], axis=1)

    kz = lambda n: jnp.zeros((kvl, heads, n), w['w_uk'].dtype)
    wk = jnp.concatenate([w['w_uk'].reshape(kvl, heads, QK_NOPE), kz(HEAD_PAD - QK_NOPE)], axis=2)
    wv = jnp.concatenate([w['w_uv'].reshape(kvl, heads, V_DIM), kz(HEAD_PAD - V_DIM)], axis=2)
    wkv_x = jnp.concatenate([wk.reshape(kvl, -1), wv.reshape(kvl, -1)], axis=1)

    wbo = w['w_b_out'].reshape(heads, V_DIM, d)
    wbo_x = jnp.concatenate([wbo, jnp.zeros((heads, HEAD_PAD - V_DIM, d), wbo.dtype)], axis=1).reshape(-1, d)
    wgu = jnp.concatenate([w['w_gate'], w['w_up']], axis=1)
    return dict(w_in=w_in_x, w_glu=w['w_glu'], w_a_out=w['w_a_out'], wq=wq_x, wkv=wkv_x, wbo=wbo_x,
                w_out=w['w_out'], wgu=wgu, w_down=w['w_down'])


def _gathered_to_full(gathered):
    full = {}
    for n, pc in zip(SHARDED, gathered):
        dep, r, c = pc.shape[1:]
        if n in COL_SHARDED:
            full[n] = pc.transpose(1, 2, 0, 3).reshape(dep, r, N_DEV * c)
        else:
            full[n] = pc.transpose(1, 0, 2, 3).reshape(dep, N_DEV * r, c)
    return full


def _layer_fwd(x, mod, ops, s5, small, rope, dims):
    d, sw, ql, kvl, heads, dff = dims['d'], dims['sw'], dims['ql'], dims['kvl'], dims['heads'], dims['dff']
    zo = dims['zoff']
    hw = heads * HEAD_PAD
    cos, sin = rope
    sh1, sc1, g1, sh2, sc2, g2 = mod
    bf = lambda a: a.astype(BF16)
    sv = dict(x=x)
    (h1,) = _rowwise("norm1_fwd", _f_norm_mod, [x], [small['norm1_g'], sc1, sh1], [(d, BF16)])
    z = _mm("w_in_fwd", h1, bf(ops['w_in']))
    sv.update(h1=h1, z=z)
    a_re, a_im, bmat, cmat = s5
    u_seg = _to_segments(z[:, zo['u']:zo['u'] + sw])
    nsup, sc = dims['nsup'], dims['s5_chunk']
    s5kw = dict(nsup=nsup, sc=sc)
    ends = _s5_pass("s5_ends_fwd", u_seg, a_re, a_im, bf(bmat), reverse=False, **s5kw)
    hst, ych = _s5_pass("s5_scan_fwd", u_seg, a_re, a_im, bf(bmat), reverse=False, ends=ends, w_out=bf(cmat), **s5kw)
    ych = _from_segments(ych)
    (yg,) = _rowwise("s5_gelu_fwd", _f_gelu_in, [ych, (z, zo['u'], sw)], [small['ssm_d']], [(sw, F32)])
    pre = _mm("s5_glu_mm_fwd", yg, bf(ops['w_glu']))
    (s5o,) = _rowwise("s5_glu_fwd", _f_glu, [yg, pre], [small['b_glu']], [(sw, BF16)])
    ya = _mm("s5_out_fwd", s5o, bf(ops['w_a_out']), out_dtype=BF16)
    sv.update(u_seg=u_seg, hst=hst, ych=ych, yg=yg, pre=pre, s5o=s5o, ya=ya)
    (cq,) = _rowwise("q_norm_fwd", _f_rms, [(z, zo['cq'], ql)], [small['q_norm_g']], [(ql, BF16)])
    qab = _mm("q_up_fwd", cq, bf(ops['wq']))
    scale = dims['scale']
    (q,) = _rowwise("q_rope_fwd", lambda a, b, cb, sb: _f_rope_q(a, b, cb, sb) * scale,
                    [(qab, 0, hw), (qab, hw, hw), cos, sin], [], [(hw, BF16)])
    (ckv,) = _rowwise("kv_norm_fwd", _f_rms, [(z, zo['ckv'], kvl)], [small['kv_norm_g']], [(kvl, BF16)])
    knv = _mm("kv_up_fwd", ckv, bf(ops['wkv']), out_dtype=BF16)
    (k,) = _rowwise("k_rope_fwd", _f_rope_k,
                    [(knv, 0, hw), (z, zo['kra'], HEAD_PAD), (z, zo['krb'], HEAD_PAD), cos, sin], [], [(hw, BF16)])
    o, lse = _flash_fwd(q, k, knv, heads=heads, t=dims['tq'])
    yb = _mm("mla_out_fwd", o, bf(ops['wbo']), out_dtype=BF16)
    sv.update(cq=cq, q=q, ckv=ckv, knv=knv, k=k, o=o, lse=lse, yb=yb)
    (merged,) = _rowwise("merge_fwd", _f_merge, [(z, zo['ga'], d), (z, zo['gb'], d), ya, yb], [], [(d, BF16)])
    t1 = _mm("w_out_fwd", merged, bf(ops['w_out']))
    (x1,) = _rowwise("res1_fwd", _f_res, [x, t1], [g1], [(d, F32)])
    sv.update(merged=merged, t1=t1, x1=x1)
    (h2,) = _rowwise("norm2_fwd", _f_norm_mod, [x1], [small['norm2_g'], sc2, sh2], [(d, BF16)])
    ab = _mm("ffn_up_fwd", h2, bf(ops['wgu']), out_dtype=BF16)
    (f,) = _rowwise("swiglu_fwd", _f_swiglu, [(ab, 0, dff), (ab, dff, dff)], [], [(dff, BF16)], tm=128)
    t2 = _mm("ffn_down_fwd", f, bf(ops['w_down']))
    (x2,) = _rowwise("res2_fwd", _f_res, [x1, t2], [g2], [(d, F32)])
    sv.update(h2=h2, ab=ab, f=f, t2=t2)
    return x2, sv


def _layer_bwd(dx2, sv, mod, ops, s5, small, rope, dims):
    d, sw, ql, kvl, heads, dff = dims['d'], dims['sw'], dims['ql'], dims['kvl'], dims['heads'], dims['dff']
    zo = dims['zoff']
    hw = heads * HEAD_PAD
    cos, sin = rope
    sh1, sc1, g1, sh2, sc2, g2 = mod
    a_re, a_im, bmat, cmat = s5
    z = sv['z']
    tr = lambda a: a.T.astype(BF16)
    gops, gsm = {}, {}
    dt2, dg2 = _res_bwd("res2_bwd", sv['t2'], g2, dx2)
    gops['w_down'] = _mm("ffn_down_dw", sv['f'], dt2, ta=True)
    df = _mm("ffn_down_dx", dt2, tr(ops['w_down']), out_dtype=BF16)
    ab = sv['ab']
    (dab_a, dab_b) = _rowwise("swiglu_bwd", _vjp_fn(_f_swiglu, 2, 1, (0, 1)),
                              [(ab, 0, dff), (ab, dff, dff), df], [], [(dff, BF16), (dff, BF16)], tm=128)
    dab = jnp.concatenate([dab_a, dab_b], axis=1)
    gops['wgu'] = _mm("ffn_up_dw", sv['h2'], dab, ta=True)
    dh2 = _mm("ffn_up_dx", dab, tr(ops['wgu']))
    dx1, dn2, dsc2, dsh2 = _norm_mod_bwd("norm2_bwd", sv['x1'], small['norm2_g'], sc2, sh2, dh2, dx2)
    gsm['norm2_g'] = dn2
    dt1, dg1 = _res_bwd("res1_bwd", sv['t1'], g1, dx1)
    gops['w_out'] = _mm("w_out_dw", sv['merged'], dt1, ta=True)
    dmerged = _mm("w_out_dx", dt1, tr(ops['w_out']), out_dtype=BF16)
    dga, dgb, dya, dyb = _rowwise(
        "merge_bwd", _vjp_fn(_f_merge, 4, 1, (0, 1, 2, 3)),
        [(z, zo['ga'], d), (z, zo['gb'], d), sv['ya'], sv['yb'], dmerged], [],
        [(d, BF16), (d, BF16), (d, BF16), (d, BF16)])
    gops['wbo'] = _mm("mla_out_dw", sv['o'], dyb, ta=True)
    do = _mm("mla_out_dx", dyb, tr(ops['wbo']), out_dtype=BF16)
    delta = _flash_delta(do, sv['o'], heads=heads, t=dims['tq'])
    as_rows = lambda a: a.reshape(heads, 1, -1)
    dq, dk, dv = _flash_bwd(sv['q'], sv['k'], sv['knv'], do, as_rows(sv['lse']), as_rows(delta),
                            heads=heads, t=dims['tq'])
    def k_bwd(dkb, cosb, sinb):
        dkb = _f32(dkb)
        dkpe = dkb[:, 0:HEAD_PAD]
        for h in range(1, heads):
            dkpe = dkpe + dkb[:, h * HEAD_PAD:(h + 1) * HEAD_PAD]
        return dkpe * cosb, dkpe * sinb
    dkra, dkrb = _rowwise("k_rope_bwd", k_bwd, [dk, cos, sin], [], [(HEAD_PAD, BF16), (HEAD_PAD, BF16)])
    dknv = jnp.concatenate([dk, dv], axis=1)
    gops['wkv'] = _mm("kv_up_dw", sv['ckv'], dknv, ta=True)
    dckv = _mm("kv_up_dx", dknv, tr(ops['wkv']))
    dckv_in, dkvg = _rms_bwd("kv_norm_bwd", z, zo['ckv'], kvl, small['kv_norm_g'], dckv)
    gsm['kv_norm_g'] = dkvg
    def q_bwd(dqb, cosb, sinb):
        dqb = _f32(dqb) * dims['scale']
        return dqb * jnp.tile(cosb, (1, heads)), dqb * jnp.tile(sinb, (1, heads))
    dqa, dqb_ = _rowwise("q_rope_bwd", q_bwd, [dq, cos, sin], [], [(hw, BF16), (hw, BF16)])
    dqab = jnp.concatenate([dqa, dqb_], axis=1)
    gops['wq'] = _mm("q_up_dw", sv['cq'], dqab, ta=True)
    dcq = _mm("q_up_dx", dqab, tr(ops['wq']))
    dcq_in, dqg = _rms_bwd("q_norm_bwd", z, zo['cq'], ql, small['q_norm_g'], dcq)
    gsm['q_norm_g'] = dqg
    gops['w_a_out'] = _mm("s5_out_dw", sv['s5o'], dya, ta=True)
    ds5o = _mm("s5_out_dx", dya, tr(ops['w_a_out']))

    def glu_bwd(yg, pre, ds, b):
        _, vjp = jax.vjp(_f_glu, _f32(yg), _f32(pre), b)
        dyg, dpre, db = vjp(_f32(ds))
        return dyg, dpre, db
    dyg_a, dpre, dbglu = _rowwise("s5_glu_bwd", glu_bwd, [sv['yg'], sv['pre'], ds5o], [small['b_glu']],
                                  [(sw, F32), (sw, BF16)], [sw])
    gsm['b_glu'] = dbglu
    gops['w_glu'] = _mm("s5_glu_mm_dw", sv['yg'], dpre, ta=True)
    dyg_b = _mm("s5_glu_mm_dx", dpre, tr(ops['w_glu']))

    def gelu_bwd(ych, u, dya_, dyb_, dvec):
        _, vjp = jax.vjp(_f_gelu_in, _f32(ych), _f32(u), dvec)
        dych, du, dd = vjp(_f32(dya_) + _f32(dyb_))
        return dych, du, dd
    dy, du_skip, dssm_d = _rowwise("s5_gelu_bwd", gelu_bwd, [sv['ych'], (z, zo['u'], sw), dyg_a, dyg_b],
                                   [small['ssm_d']], [(sw, BF16), (sw, F32)], [sw])
    gsm['ssm_d'] = dssm_d
    dy_seg = _to_segments(dy)
    nsup, sc = dims['nsup'], dims['s5_chunk']
    tr_blocks = lambda a: a.reshape(nsup, -1, a.shape[1]).transpose(0, 2, 1).reshape(-1, a.shape[0] // nsup)
    s5kw = dict(nsup=nsup, sc=sc)
    c_t, b_t = tr_blocks(cmat).astype(BF16), tr_blocks(bmat).astype(BF16)
    ends = _s5_pass("s5_ends_bwd", dy_seg, a_re, a_im, c_t, reverse=True, **s5kw)
    du_scan, g_bmat, g_cmat, dar, dai = _s5_pass("s5_scan_bwd", dy_seg, a_re, a_im, c_t, reverse=True, ends=ends,
                                                 w_out=b_t, u=sv['u_seg'], h=sv['hst'], **s5kw)
    du_scan = _from_segments(du_scan)
    (du,) = _rowwise("s5_du_sum", lambda a, b: _f32(a) + _f32(b), [du_skip, du_scan], [], [(sw, BF16)])
    gs5 = (dar, dai, g_bmat, g_cmat)
    dz = jnp.concatenate([dga, dgb, du, dcq_in, dckv_in, dkra, dkrb], axis=1)
    gops['w_in'] = _mm("w_in_dw", sv['h1'], dz, ta=True)
    dh1 = _mm("w_in_dx", dz, tr(ops['w_in']))
    dx, dn1, dsc1, dsh1 = _norm_mod_bwd("norm1_bwd", sv['x'], small['norm1_g'], sc1, sh1, dh1, dx1)
    gsm['norm1_g'] = dn1
    dmod = (dsh1, dsc1, dg1, dsh2, dsc2, dg2)
    return dx, gops, gs5, gsm, dmod


def _res_bwd(name, t, g, dxo):
    def fn(tb, db, gb):
        db = _f32(db)
        return gb * db, jnp.sum(db * _f32(tb), axis=0, keepdims=True)
    return _rowwise(name, fn, [t, dxo], [g], [(t.shape[1], BF16)], [t.shape[1]])


def _norm_mod_bwd(name, x, g, sc, sh, dh, dres):
    def fn(xb, dhb, dresb, gb, scb, shb):
        _, vjp = jax.vjp(_f_norm_mod, _f32(xb), gb, scb, shb)
        dx, dg, dsc, dsh = vjp(_f32(dhb))
        return dx + _f32(dresb), dg, dsc, dsh
    w = x.shape[1]
    return _rowwise(name, fn, [x, dh, dres], [g, sc, sh], [(w, F32)], [w, w, w])


def _rms_bwd(name, z, off, w, g, dy):
    def fn(xb, dyb, gb):
        _, vjp = jax.vjp(_f_rms, _f32(xb), gb)
        dx, dg = vjp(_f32(dyb))
        return dx, dg
    return _rowwise(name, fn, [(z, off, w), dy], [g], [(w, BF16)], [w])


S5_NAMES = ('ssm_a_re', 'ssm_a_im', 'ssm_log_dt', 'ssm_b_re', 'ssm_b_im', 'ssm_c_re', 'ssm_c_im')
LAYER_VECS = ('norm1_g', 'ssm_d', 'b_glu', 'q_norm_g', 'kv_norm_g', 'norm2_g')


def _step(p, mom_m, mom_v, x, c, positions, loss_target):
    depth, d = p['norm1_g'].shape
    L = x.shape[1]
    sw = p['ssm_d'].shape[1]
    ql, kvl = p['q_norm_g'].shape[1], p['kv_norm_g'].shape[1]
    heads = p['w_uk'].shape[2] * N_DEV // QK_NOPE
    dff = p['w_gate'].shape[2] * N_DEV
    ada_w = p['w_ada'].shape[2]
    zoff, o = {}, 0
    for nm, sz in (('ga', d), ('gb', d), ('u', sw), ('cq', ql), ('ckv', kvl), ('kra', HEAD_PAD), ('krb', HEAD_PAD)):
        assert o % sz == 0, (nm, o, sz)
        zoff[nm] = o
        o += sz
    groups, states = p['ssm_a_re'].shape[1:]
    assert groups % S5_SUPER == 0 and p['ssm_b_re'].shape[3] * S5_SUPER == LANES
    dims = dict(d=d, sw=sw, ql=ql, kvl=kvl, heads=heads, dff=dff, zoff=zoff,
                nsup=groups // S5_SUPER, s5_chunk=S5_SUPER * states,
                tq=1024 if L >= 4096 else 128, scale=(QK_NOPE + QK_ROPE) ** -0.5)
    x = x.reshape(L, d)
    tgt = loss_target.reshape(L, d)

    posf = positions.reshape(L).astype(F32)
    inv_freq = ROPE_BASE ** (-jnp.arange(0, QK_ROPE, 2, dtype=F32) / QK_ROPE)
    ang = posf[:, None] * inv_freq
    cs, sn = jnp.cos(ang), jnp.sin(ang)
    padr = HEAD_PAD - QK_NOPE - QK_ROPE
    cos = jnp.concatenate([jnp.ones((L, QK_NOPE), F32), cs, cs, jnp.zeros((L, padr), F32)], axis=1)
    sin = jnp.concatenate([jnp.zeros((L, QK_NOPE), F32), sn, sn, jnp.zeros((L, padr), F32)], axis=1)
    rope = (cos, sin)

    gathered = _exchange("gather_weights", [p[n].astype(BF16) for n in SHARDED], all_to_all=False)

    def make_ops(gl):
        full = _gathered_to_full(gl)
        return [_layer_operators({n: full[n][l] for n in SHARDED}, dims) for l in range(depth)]

    ops, ops_vjp = jax.vjp(make_ops, [g.astype(F32) for g in gathered])

    def make_s5(sp):
        return [_s5_operators(*[sp[n][l] for n in S5_NAMES]) for l in range(depth)]

    s5ops, s5_vjp = jax.vjp(make_s5, {n: p[n] for n in S5_NAMES})

    (c_slabs,) = _exchange("gather_c", [jnp.pad(c, ((0, 7), (0, 0)))], all_to_all=False)
    c_all = c_slabs[:, 0, :]
    (c_act,) = _rowwise("c_silu", lambda a: jax.nn.silu(a), [jnp.pad(c_all, ((0, 8), (0, 0)))], [], [(d, F32)])
    w_ada_cat = p['w_ada'].transpose(1, 0, 2).reshape(d, depth * ada_w)
    mod_cols = _mm("ada_fwd", c_act, w_ada_cat)[:N_DEV]
    (mod_rows,) = _exchange("a2a_mod", [_pack_slabs(mod_cols)], all_to_all=True)
    mod_mine = _unpack_slabs(mod_rows, depth * ada_w).reshape(N_DEV, depth, ada_w)
    mod_mine = mod_mine.transpose(1, 0, 2).reshape(depth, N_DEV * ada_w)
    (mod_full,) = _rowwise("ada_bias", lambda a, b: a + b, [mod_mine, p['b_ada']], [], [(6 * d, F32)])
    mods = [tuple(mod_full[l:l + 1, i * d:(i + 1) * d] for i in range(6)) for l in range(depth)]

    saved = []
    xl = x
    for l in range(depth):
        small = {n: p[n][l:l + 1] for n in LAYER_VECS}
        xl, sv = _layer_fwd(xl, mods[l], ops[l], s5ops[l], small, rope, dims)
        saved.append((sv, small))

    def final_fn(xb, tb, gb):
        def lossf(xv, gv):
            e = _f_rms(xv, gv) - tb
            per_row = 0.5 * jnp.mean(e * e, axis=-1, keepdims=True)
            return jnp.sum(per_row, axis=0, keepdims=True)
        lv, vjp = jax.vjp(lossf, xb, gb)
        dxb, dgb = vjp(jnp.ones((1, 1), F32))
        return dxb, jnp.broadcast_to(lv, (1, LANES)), dgb
    dx, loss_vec, dfinal_g = _rowwise("final_loss", final_fn, [xl, tgt], [p['final_g'].reshape(1, d)],
                                      [(d, F32)], [LANES, d])

    g_ops, g_s5, g_small, dmods = [None] * depth, [None] * depth, [None] * depth, [None] * depth
    for l in reversed(range(depth)):
        sv, small = saved[l]
        dx, g_ops[l], g_s5[l], g_small[l], dmods[l] = _layer_bwd(dx, sv, mods[l], ops[l], s5ops[l], small,
                                                                rope, dims)
    grad_x = dx.reshape(1, L, d)

    dmod_mine = jnp.stack([jnp.concatenate(dm, axis=1)[0] for dm in dmods])
    dmod_slabs = dmod_mine.reshape(depth, N_DEV, ada_w).transpose(1, 0, 2).reshape(N_DEV, depth * ada_w)
    (dmod_recv,) = _exchange("a2a_dmod", [_pack_slabs(dmod_slabs)], all_to_all=True)
    dmod_cols = _unpack_slabs(dmod_recv, depth * ada_w)
    g_ada = _mm("ada_dw", c_act, jnp.pad(dmod_cols, ((0, 8), (0, 0))), ta=True)
    grads, delta, new_m, new_v = {}, {}, {}, {}
    grads['w_ada'] = g_ada.reshape(d, depth, ada_w).transpose(1, 0, 2)
    two_d = lambda a: a.reshape(-1, a.shape[-1])
    res = _rowwise("adamw_w_ada", _adamw_fn, [two_d(a) for a in (p['w_ada'], grads['w_ada'], mom_m['w_ada'],
                                                                  mom_v['w_ada'])], [], [(ada_w, F32)] * 3)
    delta['w_ada'], new_m['w_ada'], new_v['w_ada'] = [r.reshape(p['w_ada'].shape) for r in res]

    (g_slabs,) = ops_vjp(g_ops)
    g_recv = _exchange("a2a_grads", [g.astype(BF16) for g in g_slabs], all_to_all=True)
    for n, rv in zip(SHARDED, g_recv):
        shp = p[n].shape
        res = _adamw_sharded("adamw_" + n, two_d(p[n]), rv.reshape(N_DEV, -1, shp[-1]), two_d(mom_m[n]),
                             two_d(mom_v[n]))
        grads[n], delta[n], new_m[n], new_v[n] = [r.reshape(shp) for r in res]

    (g_s5p,) = s5_vjp(g_s5)
    part = dict(g_s5p)
    part['b_ada'] = dmod_mine
    for n in LAYER_VECS:
        part[n] = jnp.concatenate([g_small[l][n] for l in range(depth)], axis=0)
    part['final_g'] = dfinal_g.reshape(d)
    small_shapes = [p[n].shape for n in SMALL] + [(1,)]
    (small_recv,) = _exchange("gather_small", [_pack_rows([part[n] for n in SMALL] + [loss_vec[0, 0:1]])],
                              all_to_all=False)
    small_sum = _sum_slabs("sum_small", small_recv)
    small_list = _unpack_rows(small_sum, small_shapes)
    grads.update(zip(SMALL, small_list[:-1]))
    loss = small_list[-1].reshape(())
    dummy = [jnp.zeros((1,), F32)]
    res = _rowwise("adamw_small", _adamw_fn,
                   [_pack_rows([src[n] for n in SMALL] + dummy) for src in (p, )] + [small_sum] +
                   [_pack_rows([src[n] for n in SMALL] + dummy) for src in (mom_m, mom_v)], [], [(FLAT_W, F32)] * 3)
    for dst, r in zip((delta, new_m, new_v), res):
        dst.update(zip(SMALL, _unpack_rows(r, small_shapes)[:-1]))
    return (loss, grad_x, *[grads[n] for n in WEIGHTS], *[delta[n] for n in WEIGHTS],
            *[new_m[n] for n in WEIGHTS], *[new_v[n] for n in WEIGHTS])


def kernel(x, c, positions, w_ada, b_ada, norm1_g, w_in, ssm_a_re, ssm_a_im, ssm_log_dt, ssm_b_re, ssm_b_im, ssm_c_re, ssm_c_im, ssm_d, w_glu, b_glu, w_a_out, q_norm_g, w_uq, kv_norm_g, w_uk, w_uv, w_b_out, w_out, norm2_g, w_gate, w_up, w_down, final_g, loss_target, m_w_ada, m_b_ada, m_norm1_g, m_w_in, m_ssm_a_re, m_ssm_a_im, m_ssm_log_dt, m_ssm_b_re, m_ssm_b_im, m_ssm_c_re, m_ssm_c_im, m_ssm_d, m_w_glu, m_b_glu, m_w_a_out, m_q_norm_g, m_w_uq, m_kv_norm_g, m_w_uk, m_w_uv, m_w_b_out, m_w_out, m_norm2_g, m_w_gate, m_w_up, m_w_down, m_final_g, v_w_ada, v_b_ada, v_norm1_g, v_w_in, v_ssm_a_re, v_ssm_a_im, v_ssm_log_dt, v_ssm_b_re, v_ssm_b_im, v_ssm_c_re, v_ssm_c_im, v_ssm_d, v_w_glu, v_b_glu, v_w_a_out, v_q_norm_g, v_w_uq, v_kv_norm_g, v_w_uk, v_w_uv, v_w_b_out, v_w_out, v_norm2_g, v_w_gate, v_w_up, v_w_down, v_final_g):
    given = dict(locals())
    p = {n: given[n] for n in WEIGHTS}
    mom_m = {n: given["m_" + n] for n in WEIGHTS}
    mom_v = {n: given["v_" + n] for n in WEIGHTS}
    return _step(p, mom_m, mom_v, x, c, positions, loss_target)
```

```python
import functools
import math

import jax
import jax.numpy as jnp
from jax import lax
from jax.experimental import pallas as pl
from jax.experimental.pallas import tpu as pltpu

F32 = jnp.float32
BF16 = jnp.bfloat16

N_DEV = 8
LANES = 128
FLAT_W = 1024
VMEM_LIMIT = 48 * 1024 * 1024
MM_VMEM_BUDGET = 36 * 1024 * 1024
QK_NOPE, QK_ROPE, V_DIM = 64, 32, 64
HEAD_PAD = LANES
ROPE_BASE = 10000.0
EPS = 1e-6
ADAM_LR, ADAM_B1, ADAM_B2, ADAM_EPS, ADAM_WD, ADAM_STEP = 0.001, 0.9, 0.999, 1e-08, 0.01, 10
NEG = float(jnp.finfo(jnp.float32).min)

WEIGHTS = ['w_ada', 'b_ada', 'norm1_g', 'w_in', 'ssm_a_re', 'ssm_a_im', 'ssm_log_dt', 'ssm_b_re', 'ssm_b_im',
           'ssm_c_re', 'ssm_c_im', 'ssm_d', 'w_glu', 'b_glu', 'w_a_out', 'q_norm_g', 'w_uq', 'kv_norm_g', 'w_uk',
           'w_uv', 'w_b_out', 'w_out', 'norm2_g', 'w_gate', 'w_up', 'w_down', 'final_g']
COL_SHARDED = ['w_in', 'w_a_out', 'w_uq', 'w_uk', 'w_uv', 'w_b_out', 'w_gate', 'w_up']
ROW_SHARDED = ['w_glu', 'w_out', 'w_down']
SHARDED = COL_SHARDED + ROW_SHARDED
SMALL = ['b_ada', 'norm1_g', 'ssm_a_re', 'ssm_a_im', 'ssm_log_dt', 'ssm_b_re', 'ssm_b_im', 'ssm_c_re', 'ssm_c_im',
         'ssm_d', 'b_glu', 'q_norm_g', 'kv_norm_g', 'norm2_g', 'final_g']


def _cparams(sem):
    return pltpu.CompilerParams(dimension_semantics=sem, vmem_limit_bytes=VMEM_LIMIT)


def _pick(dim, pref, quantum=LANES):
    if dim <= pref:
        return dim
    t = (pref // quantum) * quantum
    while t >= quantum:
        if dim % t == 0:
            return t
        t -= quantum
    return dim


def _mm(name, a, b, *, ta=False, out_dtype=F32, a_col=None, b_col=None, tm=1408, tn=1408, tk=None):
    a_off, a_w = a_col if a_col is not None else (0, a.shape[1])
    b_off, b_w = b_col if b_col is not None else (0, b.shape[1])
    if ta:
        kdim, m = a.shape[0], a_w
        assert b.shape[0] == kdim
    else:
        m, kdim = a.shape[0], a_w
        assert b.shape[0] == kdim, (name, a.shape, b.shape)
    n = b_w
    tm = _pick(m, tm, LANES if ta else 8)
    tn = _pick(n, tn)
    if tk is None:
        def fits(tm_, tk_):
            return (tm_ * tn * (2 * jnp.dtype(out_dtype).itemsize + 4)
                    + tk_ * (2 * (tm_ * a.dtype.itemsize + tn * b.dtype.itemsize) + 2 * (tm_ + tn))) <= MM_VMEM_BUDGET

        tk = kdim
        while not fits(tm, tk):
            if not ta and tm % 16 == 0 and tm >= 1024 and fits(tm // 2, tk):
                tm //= 2
                break
            if tk % (2 * (8 if ta else LANES)):
                break
            tk //= 2
    tk = _pick(kdim, tk, 8 if ta else LANES)
    nk = kdim // tk
    assert m % tm == 0 and n % tn == 0 and kdim % tk == 0, (name, m, n, kdim, tm, tn, tk)
    if ta:
        assert a_off % tm == 0 and b_off % tn == 0
        a_spec = pl.BlockSpec((tk, tm), lambda i, j, k: (k, i + a_off // tm))
        dims = (((0,), (0,)), ((), ()))
    else:
        assert a_off % tk == 0 and b_off % tn == 0
        a_spec = pl.BlockSpec((tm, tk), lambda i, j, k: (i, k + a_off // tk))
        dims = (((1,), (0,)), ((), ()))
    b_spec = pl.BlockSpec((tk, tn), lambda i, j, k: (k, j + b_off // tn))

    def prod(a_ref, b_ref):
        return lax.dot_general(a_ref[...].astype(BF16), b_ref[...].astype(BF16), dims, preferred_element_type=F32)

    def body_one(a_ref, b_ref, o_ref):
        o_ref[...] = prod(a_ref, b_ref).astype(o_ref.dtype)

    def body_acc(a_ref, b_ref, o_ref, acc_ref):
        @pl.when(pl.program_id(2) == 0)
        def _():
            acc_ref[...] = jnp.zeros_like(acc_ref)

        acc_ref[...] += prod(a_ref, b_ref)

        @pl.when(pl.program_id(2) == nk - 1)
        def _():
            o_ref[...] = acc_ref[...].astype(o_ref.dtype)

    return pl.pallas_call(
        body_one if nk == 1 else body_acc, name=name,
        out_shape=jax.ShapeDtypeStruct((m, n), out_dtype),
        grid=(m // tm, n // tn, nk),
        in_specs=[a_spec, b_spec],
        out_specs=pl.BlockSpec((tm, tn), lambda i, j, k: (i, j)),
        scratch_shapes=[] if nk == 1 else [pltpu.VMEM((tm, tn), F32)],
        compiler_params=_cparams(("parallel", "parallel", "arbitrary")),
    )(a, b)


def _rowwise(name, fn, rows, vecs, outs, reds=(), *, tm=512):
    rows = [(r, 0, r.shape[1]) if not isinstance(r, tuple) else r for r in rows]
    nrows = rows[0][0].shape[0]
    tm = _pick(nrows, tm, 8)
    assert nrows % tm == 0, (name, nrows, tm)
    nr, nv, no = len(rows), len(vecs), len(outs)
    in_specs = []
    for arr, off, w in rows:
        assert arr.shape[0] == nrows and off % w == 0, (name, arr.shape, off, w)
        in_specs.append(pl.BlockSpec((tm, w), functools.partial(lambda i, cb: (i, cb), cb=off // w)))
    for v in vecs:
        assert v.ndim == 2 and v.shape[0] == 1, (name, v.shape)
        in_specs.append(pl.BlockSpec(v.shape, lambda i: (0, 0)))
    out_shape = [jax.ShapeDtypeStruct((nrows, w), dt) for w, dt in outs]
    out_specs = [pl.BlockSpec((tm, w), lambda i: (i, 0)) for w, dt in outs]
    out_shape += [jax.ShapeDtypeStruct((1, w), F32) for w in reds]
    out_specs += [pl.BlockSpec((1, w), lambda i: (0, 0)) for w in reds]

    def body(*refs):
        rin, vin = refs[:nr], refs[nr:nr + nv]
        rout, rred = refs[nr + nv:nr + nv + no], refs[nr + nv + no:]
        res = fn(*[r[...] for r in rin], *[v[...] for v in vin])
        if not isinstance(res, (tuple, list)):
            res = (res,)
        assert len(res) == no + len(reds), (name, len(res))
        for r, val in zip(rout, res[:no]):
            r[...] = val.astype(r.dtype)
        if reds:
            @pl.when(pl.program_id(0) == 0)
            def _():
                for r in rred:
                    r[...] = jnp.zeros_like(r)

            for r, val in zip(rred, res[no:]):
                r[...] += val.astype(F32)

    res = pl.pallas_call(
        body, name=name, out_shape=out_shape, grid=(nrows // tm,),
        in_specs=in_specs, out_specs=out_specs,
        compiler_params=_cparams(("arbitrary",) if reds else ("parallel",)),
    )(*[r[0] for r in rows], *vecs)
    return res


def _f32(x):
    return x.astype(F32)


def _vjp_fn(f, n_in, n_cot, want):
    def fn(*args):
        ins = [_f32(a) for a in args[:n_in]]
        cots = tuple(_f32(a) for a in args[n_in:n_in + n_cot])
        _, vjp = jax.vjp(f, *ins)
        grads = vjp(cots if n_cot > 1 else cots[0])
        return tuple(grads[i] for i in want)
    return fn


def _f_rms(x, g):
    return (x * lax.rsqrt(jnp.mean(x * x, axis=-1, keepdims=True) + EPS)) * g


def _f_norm_mod(x, g, sc, sh):
    return _f_rms(x, g) * (1.0 + sc) + sh


def _f_gelu_in(ych, u, d):
    return jax.nn.gelu(ych + d * u)


def _f_glu(yg, pre, b):
    return yg * jax.nn.sigmoid(pre + b)


def _f_merge(ga, gb, ya, yb):
    return jax.nn.sigmoid(ga) * ya + jax.nn.sigmoid(gb) * yb


def _f_res(x, t, g):
    return x + g * t


def _f_swiglu(a, b):
    return jax.nn.silu(a) * b


def _f_rope_q(qa, qb, cos, sin):
    h = qa.shape[1] // HEAD_PAD
    return qa * jnp.tile(cos, (1, h)) + qb * jnp.tile(sin, (1, h))


def _f_rope_k(kn, kra, krb, cos, sin):
    h = kn.shape[1] // HEAD_PAD
    return kn + jnp.tile(kra * cos + krb * sin, (1, h))


SUBLANES = 8
S5_SUPER = 8


def _cpow(ar, ai, log2n):
    for _ in range(log2n):
        ar, ai = ar * ar - ai * ai, 2.0 * ar * ai
    return ar, ai


def _to_segments(x):
    L, w = x.shape
    return x.reshape(SUBLANES, L // SUBLANES, w).transpose(1, 0, 2).reshape(L, w)


def _from_segments(x):
    L, w = x.shape
    return x.reshape(L // SUBLANES, SUBLANES, w).transpose(1, 0, 2).reshape(L, w)


def _s5_pass(name, xin, a_re, a_im, w_in, *, reverse, nsup, sc, ends=None, w_out=None, u=None, h=None, tb=512):
    L, sw = xin.shape
    gp2 = nsup * 2 * sc
    gp = gp2 // 2
    seg = L // SUBLANES
    assert sw == nsup * LANES and L % SUBLANES == 0 and seg & (seg - 1) == 0, (L, sw)
    tb = _pick(L, tb, SUBLANES)
    nb, nt = L // tb, tb // SUBLANES
    first_pass = ends is None
    sign = -1.0 if reverse else 1.0
    blk = (lambda i: (nb - 1 - i, 0)) if reverse else (lambda i: (i, 0))
    order = (lambda s: nt - 1 - s) if reverse else (lambda s: s)
    tile = lambda s: pl.ds(pl.multiple_of(s * SUBLANES, SUBLANES), SUBLANES)
    const = lambda shape: pl.BlockSpec(shape, lambda i: (0, 0))
    rows_in = pl.BlockSpec((tb, sw), blk)
    rows_st = pl.BlockSpec((tb, gp2), blk)

    def coeffs(ar_ref, ai_ref, cc):
        ar1, ai1 = ar_ref[:, cc], sign * ai_ref[:, cc]
        return ar1, ai1, jnp.broadcast_to(ar1, (SUBLANES, sc)), jnp.broadcast_to(ai1, (SUBLANES, sc))

    def drive(x_ref, w_ref, xs_ref, c):
        lanes = pl.ds(c * LANES, LANES)
        xs_ref[...] = jnp.dot(x_ref[:, lanes].astype(BF16), w_ref[lanes, :], preferred_element_type=F32)

    def body(*refs):
        it = iter(refs)
        x_ref, w_ref, ar_ref, ai_ref = next(it), next(it), next(it), next(it)
        if first_pass:
            e_ref, xs_ref = next(it), next(it)
        elif not reverse:
            e_ref, wo_ref, o_ref, y_ref, st_ref, xs_ref = (next(it) for _ in range(6))
        else:
            (e_ref, wo_ref, u_ref, h_ref, hp_ref, hl_ref, du_ref, gb_ref, gc_ref, dar_ref, dai_ref,
             st_ref, xs_ref, g_ref) = (next(it) for _ in range(14))
        i = pl.program_id(0)

        @pl.when(i == 0)
        def _():
            if first_pass:
                e_ref[...] = jnp.zeros_like(e_ref)
                return
            for c in range(nsup):
                cc, re, im = pl.ds(c * sc, sc), pl.ds(2 * c * sc, sc), pl.ds((2 * c + 1) * sc, sc)
                ar1, ai1, _, _ = coeffs(ar_ref, ai_ref, cc)
                pr, pi = _cpow(ar1, ai1, seg.bit_length() - 1)
                cr = jnp.zeros((1, sc), F32)
                ci = jnp.zeros((1, sc), F32)
                for j in (reversed(range(SUBLANES)) if reverse else range(SUBLANES)):
                    st_ref[j:j + 1, re] = cr
                    st_ref[j:j + 1, im] = ci
                    cr, ci = (e_ref[j:j + 1, re] + pr * cr - pi * ci, e_ref[j:j + 1, im] + pr * ci + pi * cr)
            if reverse:
                for r in (gb_ref, gc_ref, dar_ref, dai_ref):
                    r[...] = jnp.zeros_like(r)

        sub = lax.broadcasted_iota(jnp.int32, (SUBLANES, sc), 0)
        state = e_ref if first_pass else st_ref
        xr, xi = pl.ds(0, sc), pl.ds(sc, sc)
        for c in range(nsup):
            cc, re, im = pl.ds(c * sc, sc), pl.ds(2 * c * sc, sc), pl.ds((2 * c + 1) * sc, sc)
            both = pl.ds(2 * c * sc, 2 * sc)
            lanes = pl.ds(c * LANES, LANES)
            _, _, ar, ai = coeffs(ar_ref, ai_ref, cc)
            drive(x_ref, w_ref, xs_ref, c)

            def advance(rows, sr, si):
                return ar * sr - ai * si + xs_ref[rows, xr], ar * si + ai * sr + xs_ref[rows, xi]

            if first_pass:
                def step(s, st):
                    return advance(tile(order(s)), *st)

                sr, si = lax.fori_loop(0, nt, step, (state[:, re], state[:, im]), unroll=4)
            elif not reverse:
                def step(s, st):
                    rows = tile(s)
                    sr, si = advance(rows, *st)
                    o_ref[rows, re] = sr
                    o_ref[rows, im] = si
                    return sr, si

                sr, si = lax.fori_loop(0, nt, step, (state[:, re], state[:, im]), unroll=4)
                y_ref[:, lanes] = jnp.dot(o_ref[:, both].astype(BF16), wo_ref[both, :], preferred_element_type=F32)
            else:
                def emit(rows, sr, si):
                    sr, si = advance(rows, sr, si)
                    g_ref[rows, xr] = sr
                    g_ref[rows, xi] = si
                    return sr, si

                def grad(sr, si, hpr, hpi, accr, acci):
                    return accr + sr * hpr + si * hpi, acci + si * hpr - sr * hpi

                def step(s, st):
                    sr, si, accr, acci = st
                    t = nt - 1 - s
                    sr, si = emit(tile(t), sr, si)
                    prev = tile(t - 1)
                    return (sr, si) + grad(sr, si, h_ref[prev, re], h_ref[prev, im], accr, acci)

                zero = jnp.zeros((SUBLANES, sc), F32)
                sr, si, accr, acci = lax.fori_loop(0, nt - 1, step, (state[:, re], state[:, im], zero, zero),
                                                   unroll=4)
                sr, si = emit(tile(0), sr, si)
                first = (i == nb - 1)
                wrap_r = jnp.where(sub == 0, 0.0, pltpu.roll(hl_ref[:, re], 1, 0))
                wrap_i = jnp.where(sub == 0, 0.0, pltpu.roll(hl_ref[:, im], 1, 0))
                accr, acci = grad(sr, si, jnp.where(first, wrap_r, hp_ref[:, re]),
                                  jnp.where(first, wrap_i, hp_ref[:, im]), accr, acci)
                dar_ref[:, cc] += jnp.sum(accr, axis=0, keepdims=True)
                dai_ref[:, cc] += jnp.sum(acci, axis=0, keepdims=True)
                gb = g_ref[...].astype(BF16)
                du_ref[:, lanes] = jnp.dot(gb, wo_ref[both, :], preferred_element_type=F32)
                gb_ref[lanes, :] += lax.dot_general(u_ref[:, lanes].astype(BF16), gb, _TN,
                                                    preferred_element_type=F32)
                gc_ref[both, :] += lax.dot_general(h_ref[:, both].astype(BF16), x_ref[:, lanes].astype(BF16), _TN,
                                                   preferred_element_type=F32)
            state[:, re] = sr
            state[:, im] = si

    vec = const((1, gp))
    in_specs = [rows_in, const(w_in.shape), vec, vec]
    operands = [xin, w_in, a_re, a_im]
    xs_scratch = pltpu.VMEM((tb, 2 * sc), F32)
    st_scratch = pltpu.VMEM((SUBLANES, gp2), F32)
    if first_pass:
        out_shape = jax.ShapeDtypeStruct((SUBLANES, gp2), F32)
        out_specs = const((SUBLANES, gp2))
        scratch = [xs_scratch]
    elif not reverse:
        in_specs += [const((SUBLANES, gp2)), const(w_out.shape)]
        operands += [ends, w_out]
        out_shape = [jax.ShapeDtypeStruct((L, gp2), F32), jax.ShapeDtypeStruct((L, sw), F32)]
        out_specs = [rows_st, rows_in]
        scratch = [st_scratch, xs_scratch]
    else:
        in_specs += [const((SUBLANES, gp2)), const(w_out.shape), rows_in, rows_st,
                     pl.BlockSpec((SUBLANES, gp2), lambda i: (jnp.maximum((nb - 1 - i) * nt - 1, 0), 0)),
                     pl.BlockSpec((SUBLANES, gp2), lambda i: (seg - 1, 0))]
        operands += [ends, w_out, u, h, h, h]
        out_shape = [jax.ShapeDtypeStruct((L, sw), F32), jax.ShapeDtypeStruct((sw, 2 * sc), F32),
                     jax.ShapeDtypeStruct((gp2, LANES), F32), jax.ShapeDtypeStruct((1, gp), F32),
                     jax.ShapeDtypeStruct((1, gp), F32)]
        out_specs = [rows_in, const((sw, 2 * sc)), const((gp2, LANES)), vec, vec]
        scratch = [st_scratch, xs_scratch, pltpu.VMEM((tb, 2 * sc), F32)]
    return pl.pallas_call(
        body, name=name, out_shape=out_shape, grid=(nb,), in_specs=in_specs, out_specs=out_specs,
        scratch_shapes=scratch, compiler_params=_cparams(("arbitrary",)),
    )(*operands)


_NT = (((1,), (1,)), ((), ()))
_TN = (((0,), (0,)), ((), ()))


def _causal(s, t, k_major=False):
    row = lax.broadcasted_iota(jnp.int32, (t, t), 0)
    col = lax.broadcasted_iota(jnp.int32, (t, t), 1)
    return jnp.where(row <= col if k_major else col <= row, s, NEG)


def _pair_tables(n, k_major):
    if k_major:
        pairs = [(qi, ki) for ki in range(n) for qi in range(ki, n)]
    else:
        pairs = [(qi, ki) for qi in range(n) for ki in range(qi + 1)]
    return (jnp.asarray([p[0] for p in pairs], jnp.int32), jnp.asarray([p[1] for p in pairs], jnp.int32))


def _flash_fwd(q, k, knv, *, heads, t):
    L = q.shape[0]
    n = L // t
    rep = t // LANES
    qtab, ktab = _pair_tables(n, k_major=False)

    def body(qt_ref, kt_ref, q_ref, k_ref, v_ref, o_ref, lse_ref, m_s, acc_s):
        step = pl.program_id(1)
        qi, ki = qt_ref[step], kt_ref[step]
        lane = lax.broadcasted_iota(jnp.int32, (t, HEAD_PAD), 1)

        @pl.when(ki == 0)
        def _():
            m_s[...] = jnp.full(m_s.shape, NEG, F32)
            acc_s[...] = jnp.zeros_like(acc_s)

        def update(diagonal):
            s = lax.dot_general(q_ref[...], k_ref[...], _NT, preferred_element_type=F32)
            if diagonal:
                s = _causal(s, t)
            m_prev = m_s[...]
            m_next = jnp.maximum(m_prev, jnp.max(s, axis=1, keepdims=True))
            p = jnp.exp(s - jnp.tile(m_next, (1, rep)))
            vb = jnp.where(lane == V_DIM, jnp.ones((), BF16), v_ref[...])
            acc_s[...] = jnp.exp(m_prev - m_next) * acc_s[...] + jnp.dot(p.astype(BF16), vb,
                                                                         preferred_element_type=F32)
            m_s[...] = m_next

        @pl.when(ki < qi)
        def _():
            update(False)

        @pl.when(ki == qi)
        def _():
            update(True)
            acc = acc_s[...]
            l = jnp.sum(jnp.where(lane == V_DIM, acc, 0.0), axis=1, keepdims=True)
            o_ref[...] = jnp.where(lane == V_DIM, 0.0, acc * (1.0 / l)).astype(o_ref.dtype)
            lse_ref[0] = jnp.max(m_s[...], axis=1, keepdims=True) + jnp.log(l)

    q_map = lambda h, s, qt, kt: (qt[s], h)
    kv_map = lambda h, s, qt, kt: (kt[s], h)
    v_map = lambda h, s, qt, kt: (kt[s], h + heads)
    return pl.pallas_call(
        body, name="mla_flash_fwd",
        out_shape=[jax.ShapeDtypeStruct((L, heads * HEAD_PAD), BF16),
                   jax.ShapeDtypeStruct((heads, L, 1), F32)],
        grid_spec=pltpu.PrefetchScalarGridSpec(
            num_scalar_prefetch=2, grid=(heads, qtab.shape[0]),
            in_specs=[pl.BlockSpec((t, HEAD_PAD), q_map),
                      pl.BlockSpec((t, HEAD_PAD), kv_map),
                      pl.BlockSpec((t, HEAD_PAD), v_map)],
            out_specs=[pl.BlockSpec((t, HEAD_PAD), q_map),
                       pl.BlockSpec((1, t, 1), lambda h, s, qt, kt: (h, qt[s], 0))],
            scratch_shapes=[pltpu.VMEM((t, LANES), F32), pltpu.VMEM((t, HEAD_PAD), F32)]),
        compiler_params=_cparams(("parallel", "arbitrary")),
    )(qtab, ktab, q, k, knv)


def _flash_delta(do, o, *, heads, t):
    L = do.shape[0]

    def body(do_ref, o_ref, d_ref):
        d_ref[0] = jnp.sum(do_ref[...].astype(F32) * o_ref[...].astype(F32), axis=1, keepdims=True)

    return pl.pallas_call(
        body, name="mla_flash_delta",
        out_shape=jax.ShapeDtypeStruct((heads, L, 1), F32),
        grid=(heads, L // t),
        in_specs=[pl.BlockSpec((t, HEAD_PAD), lambda h, i: (i, h)),
                  pl.BlockSpec((t, HEAD_PAD), lambda h, i: (i, h))],
        out_specs=pl.BlockSpec((1, t, 1), lambda h, i: (h, i, 0)),
        compiler_params=_cparams(("parallel", "parallel")),
    )(do, o)


def _flash_bwd(q, k, knv, do, lse, delta, *, heads, t):
    L = q.shape[0]
    n = L // t
    qtab, ktab = _pair_tables(n, k_major=True)

    def body(qt_ref, kt_ref, q_ref, k_ref, v_ref, do_ref, lse_ref, dl_ref, dq_ref, dk_ref, dv_ref, dk_s, dv_s):
        step = pl.program_id(1)
        qi, ki = qt_ref[step], kt_ref[step]

        @pl.when(qi == ki)
        def _():
            dk_s[...] = jnp.zeros_like(dk_s)
            dv_s[...] = jnp.zeros_like(dv_s)

        def update(diagonal):
            qb, kb, vb, dob = q_ref[...], k_ref[...], v_ref[...], do_ref[...]
            st = lax.dot_general(kb, qb, _NT, preferred_element_type=F32)
            if diagonal:
                st = _causal(st, t, k_major=True)
            pt = jnp.exp(st - lse_ref[0])
            dv_s[...] += jnp.dot(pt.astype(BF16), dob, preferred_element_type=F32)
            dpt = lax.dot_general(vb, dob, _NT, preferred_element_type=F32)
            dst = (pt * (dpt - dl_ref[0])).astype(BF16)
            dk_s[...] += jnp.dot(dst, qb, preferred_element_type=F32)
            dqb = lax.dot_general(dst, kb, _TN, preferred_element_type=F32)
            rows = pl.ds(pl.multiple_of(qi * t, t), t)

            @pl.when(ki == 0)
            def _():
                dq_ref[rows, :] = dqb

            @pl.when(ki > 0)
            def _():
                dq_ref[rows, :] += dqb

        @pl.when(qi > ki)
        def _():
            update(False)

        @pl.when(qi == ki)
        def _():
            update(True)

        @pl.when(qi == n - 1)
        def _():
            dk_ref[...] = dk_s[...].astype(dk_ref.dtype)
            dv_ref[...] = dv_s[...].astype(dv_ref.dtype)

    q_map = lambda h, s, qt, kt: (qt[s], h)
    stat_map = lambda h, s, qt, kt: (h, 0, qt[s])
    kv_map = lambda h, s, qt, kt: (kt[s], h)
    v_map = lambda h, s, qt, kt: (kt[s], h + heads)
    return pl.pallas_call(
        body, name="mla_flash_bwd",
        out_shape=[jax.ShapeDtypeStruct((L, heads * HEAD_PAD), F32),
                   jax.ShapeDtypeStruct((L, heads * HEAD_PAD), BF16),
                   jax.ShapeDtypeStruct((L, heads * HEAD_PAD), BF16)],
        grid_spec=pltpu.PrefetchScalarGridSpec(
            num_scalar_prefetch=2, grid=(heads, qtab.shape[0]),
            in_specs=[pl.BlockSpec((t, HEAD_PAD), q_map),
                      pl.BlockSpec((t, HEAD_PAD), kv_map),
                      pl.BlockSpec((t, HEAD_PAD), v_map),
                      pl.BlockSpec((t, HEAD_PAD), q_map),
                      pl.BlockSpec((1, 1, t), stat_map),
                      pl.BlockSpec((1, 1, t), stat_map)],
            out_specs=[pl.BlockSpec((L, HEAD_PAD), lambda h, s, qt, kt: (0, h)),
                       pl.BlockSpec((t, HEAD_PAD), kv_map),
                       pl.BlockSpec((t, HEAD_PAD), kv_map)],
            scratch_shapes=[pltpu.VMEM((t, HEAD_PAD), F32), pltpu.VMEM((t, HEAD_PAD), F32)]),
        compiler_params=_cparams(("parallel", "arbitrary")),
    )(qtab, ktab, q, k, knv, do, lse, delta)


def _peer(k):
    mx, my, mc = lax.axis_index("x"), lax.axis_index("y"), lax.axis_index("c")
    px = 1 - mx if (k >> 2) & 1 else mx
    py = 1 - my if (k >> 1) & 1 else my
    pc = 1 - mc if k & 1 else mc
    return (px, py, pc), 4 * px + 2 * py + pc


def _exchange(name, xs, all_to_all):
    n = len(xs)
    any_spec = pl.BlockSpec(memory_space=pl.ANY)
    npeer = N_DEV - 1

    def body(*refs):
        x_refs, o_refs = refs[:n], refs[n:2 * n]
        send_sems, recv_sems, local_sems = refs[2 * n:]
        _, me = _peer(0)
        mine = [x.at[me] if all_to_all else x for x in x_refs]
        local = [pltpu.make_async_copy(mine[i], o_refs[i].at[me], local_sems.at[i]) for i in range(n)]
        for cp in local:
            cp.start()
        sends = []
        for k in range(1, N_DEV):
            dev, idx = _peer(k)
            for i in range(n):
                cp = pltpu.make_async_remote_copy(
                    src_ref=x_refs[i].at[idx] if all_to_all else x_refs[i], dst_ref=o_refs[i].at[me],
                    send_sem=send_sems.at[i * npeer + k - 1], recv_sem=recv_sems.at[i * npeer + k - 1],
                    device_id=dev, device_id_type=pl.DeviceIdType.MESH)
                cp.start()
                sends.append(cp)
        for k in range(1, N_DEV):
            dev, idx = _peer(k)
            for i in range(n):
                pltpu.make_async_remote_copy(
                    src_ref=mine[i], dst_ref=o_refs[i].at[idx],
                    send_sem=send_sems.at[i * npeer + k - 1], recv_sem=recv_sems.at[i * npeer + k - 1],
                    device_id=dev, device_id_type=pl.DeviceIdType.MESH).wait_recv()
        for cp in sends:
            cp.wait_send()
        for cp in local:
            cp.wait()

    return pl.pallas_call(
        body, name=name,
        out_shape=[jax.ShapeDtypeStruct((N_DEV,) + tuple(x.shape[1:] if all_to_all else x.shape), x.dtype)
                   for x in xs],
        in_specs=[any_spec] * n, out_specs=[any_spec] * n,
        scratch_shapes=[pltpu.SemaphoreType.DMA((n * npeer,)), pltpu.SemaphoreType.DMA((n * npeer,)),
                        pltpu.SemaphoreType.DMA((n,))],
    )(*xs)


def _gather_two_level(name, xs):
    n = len(xs)
    any_spec = pl.BlockSpec(memory_space=pl.ANY)
    nslot = N_DEV - 1
    chips = (2, 4, 6)

    def body(*refs):
        x_refs, o_refs = refs[:n], refs[n:2 * n]
        send_sems, recv_sems, local_sems = refs[2 * n:]
        _, me = _peer(0)
        sibling, sib_idx = _peer(1)

        def copy(i, slot, src, block, dev):
            return pltpu.make_async_remote_copy(
                src_ref=src, dst_ref=o_refs[i].at[block],
                send_sem=send_sems.at[i * nslot + slot], recv_sem=recv_sems.at[i * nslot + slot],
                device_id=dev, device_id_type=pl.DeviceIdType.MESH)

        local = [pltpu.make_async_copy(x_refs[i], o_refs[i].at[me], local_sems.at[i]) for i in range(n)]
        for cp in local:
            cp.start()
        sends = []
        for i in range(n):
            sends.append(copy(i, 0, x_refs[i], me, sibling))
            for j, k in enumerate(chips):
                sends.append(copy(i, 1 + j, x_refs[i], me, _peer(k)[0]))
        for cp in sends:
            cp.start()
        for j, k in enumerate(chips):
            _, idx = _peer(k)
            for i in range(n):
                copy(i, 1 + j, x_refs[i], idx, _peer(k)[0]).wait_recv()
                fwd = copy(i, 4 + j, o_refs[i].at[idx], idx, sibling)
                fwd.start()
                sends.append(fwd)
        for i in range(n):
            copy(i, 0, x_refs[i], sib_idx, sibling).wait_recv()
        for j, k in enumerate(chips):
            _, idx = _peer(k | 1)
            for i in range(n):
                copy(i, 4 + j, x_refs[i], idx, sibling).wait_recv()
        for cp in sends:
            cp.wait_send()
        for cp in local:
            cp.wait()

    return pl.pallas_call(
        body, name=name,
        out_shape=[jax.ShapeDtypeStruct((N_DEV,) + tuple(x.shape), x.dtype) for x in xs],
        in_specs=[any_spec] * n, out_specs=[any_spec] * n,
        scratch_shapes=[pltpu.SemaphoreType.DMA((n * nslot,)), pltpu.SemaphoreType.DMA((n * nslot,)),
                        pltpu.SemaphoreType.DMA((n,))],
    )(*xs)


def _sum_slabs(name, x, *, tr=128):
    _, r, w = x.shape
    tr = _pick(r, tr, 8)

    def body(x_ref, o_ref):
        acc = x_ref[0]
        for j in range(1, N_DEV):
            acc = acc + x_ref[j]
        o_ref[...] = acc

    return pl.pallas_call(
        body, name=name, out_shape=jax.ShapeDtypeStruct((r, w), F32), grid=(r // tr,),
        in_specs=[pl.BlockSpec((N_DEV, tr, w), lambda i: (0, i, 0))],
        out_specs=pl.BlockSpec((tr, w), lambda i: (i, 0)),
        compiler_params=_cparams(("parallel",)),
    )(x)


def _adamw_fn(w, g, m, v):
    m = ADAM_B1 * m + (1.0 - ADAM_B1) * g
    v = ADAM_B2 * v + (1.0 - ADAM_B2) * jnp.square(g)
    m_hat = m / (1.0 - ADAM_B1 ** ADAM_STEP)
    v_hat = v / (1.0 - ADAM_B2 ** ADAM_STEP)
    delta = -ADAM_LR * (m_hat / (jnp.sqrt(v_hat) + ADAM_EPS) + ADAM_WD * w)
    return delta, m, v


def _adamw_sharded(name, w, recv, m, v, *, tr=128):
    rows, c = w.shape
    tr = _pick(rows, tr, 8)

    def body(w_ref, r_ref, m_ref, v_ref, g_ref, d_ref, mo_ref, vo_ref):
        g = r_ref[0].astype(F32)
        for j in range(1, N_DEV):
            g = g + r_ref[j].astype(F32)
        d, mn, vn = _adamw_fn(w_ref[...], g, m_ref[...], v_ref[...])
        g_ref[...] = g
        d_ref[...] = d
        mo_ref[...] = mn
        vo_ref[...] = vn

    blk = pl.BlockSpec((tr, c), lambda i: (i, 0))
    return pl.pallas_call(
        body, name=name, out_shape=[jax.ShapeDtypeStruct((rows, c), F32)] * 4, grid=(rows // tr,),
        in_specs=[blk, pl.BlockSpec((N_DEV, tr, c), lambda i: (0, i, 0)), blk, blk],
        out_specs=[blk] * 4,
        compiler_params=_cparams(("parallel",)),
    )(w, recv, m, v)


def _piece_rows(shape):
    return -(-math.prod(shape) // (8 * FLAT_W)) * 8


def _pack_rows(arrs):
    out = []
    for a in arrs:
        flat = a.reshape(-1)
        rows = _piece_rows(a.shape)
        out.append(jnp.pad(flat, (0, rows * FLAT_W - flat.shape[0])).reshape(rows, FLAT_W))
    return jnp.concatenate(out, axis=0)


def _unpack_rows(packed, shapes):
    out, r0 = [], 0
    for s in shapes:
        rows = _piece_rows(s)
        out.append(packed[r0:r0 + rows].reshape(-1)[:math.prod(s)].reshape(s))
        r0 += rows
    return out


def _pack_slabs(a):
    n = a.shape[1]
    rows = _piece_rows((n,))
    return jnp.pad(a, ((0, 0), (0, rows * FLAT_W - n))).reshape(N_DEV, rows, FLAT_W)


def _unpack_slabs(a, n):
    return a.reshape(N_DEV, -1)[:, :n]


def _s5_operators(a_re, a_im, log_dt, b_re, b_im, c_re, c_im):
    g, p, m = b_re.shape
    dt = jnp.exp(log_dt)[:, None]
    mag = jnp.exp(a_re * dt)
    abar_re = mag * jnp.cos(a_im * dt)
    abar_im = mag * jnp.sin(a_im * dt)
    den = a_re * a_re + a_im * a_im
    nr = abar_re - 1.0
    ni = abar_im
    coef_re = ((nr * a_re + ni * a_im) / den)[..., None]
    coef_im = ((ni * a_re - nr * a_im) / den)[..., None]
    bbar_re = coef_re * b_re - coef_im * b_im
    bbar_im = coef_re * b_im + coef_im * b_re
    sup, ns = S5_SUPER, g // S5_SUPER
    eye = jnp.eye(sup, dtype=F32)

    def b_blocks(bb):
        return jnp.einsum('cgpm,gh->cgmhp', bb.reshape(ns, sup, p, m), eye).reshape(ns, sup * m, sup * p)

    def c_blocks(cb):
        return jnp.einsum('cgmp,gh->chpgm', cb.reshape(ns, sup, m, p), eye).reshape(ns, sup * p, sup * m)

    bmat = jnp.concatenate([b_blocks(bbar_re), b_blocks(bbar_im)], axis=2).reshape(g * m, 2 * sup * p)
    cmat = jnp.concatenate([c_blocks(c_re), c_blocks(-c_im)], axis=1).reshape(ns * 2 * sup * p, sup * m)
    return abar_re.reshape(1, g * p), abar_im.reshape(1, g * p), bmat, cmat


def _rot_cols(w):
    half = w.shape[-1] // 2
    return jnp.concatenate([-w[..., half:], w[..., :half]], axis=-1)


def _layer_operators(w, dims):
    d, sw, ql, kvl, heads, dff = dims['d'], dims['sw'], dims['ql'], dims['kvl'], dims['heads'], dims['dff']
    w_in = w['w_in']
    o = 0
    parts = {}
    for nm, sz in (('u', sw), ('cq', ql), ('ckv', kvl), ('kr', QK_ROPE), ('ga', d), ('gb', d)):
        parts[nm] = w_in[:, o:o + sz]
        o += sz
    zpad = lambda n: jnp.zeros((d, n), w_in.dtype)
    kra = jnp.concatenate([zpad(QK_NOPE), parts['kr'], zpad(HEAD_PAD - QK_NOPE - QK_ROPE)], axis=1)
    krb = jnp.concatenate([zpad(QK_NOPE), _rot_cols(parts['kr']), zpad(HEAD_PAD - QK_NOPE - QK_ROPE)], axis=1)
    w_in_x = jnp.concatenate([parts['ga'], parts['gb'], parts['u'], parts['cq'], parts['ckv'], kra, krb], axis=1)

    wq = w['w_uq'].reshape(ql, heads, QK_NOPE + QK_ROPE)
    qz = lambda n: jnp.zeros((ql, heads, n), wq.dtype)
    wq_a = jnp.concatenate([wq, qz(HEAD_PAD - QK_NOPE - QK_ROPE)], axis=2)
    wq_b = jnp.concatenate([qz(QK_NOPE), _rot_cols(wq[:, :, QK_NOPE:]), qz(HEAD_PAD - QK_NOPE - QK_ROPE)], axis=2)
    wq_x = jnp.concatenate([wq_a.reshape(ql, -1), wq_b.reshape(ql, -1)], axis=1)

    kz = lambda n: jnp.zeros((kvl, heads, n), w['w_uk'].dtype)
    wk = jnp.concatenate([w['w_uk'].reshape(kvl, heads, QK_NOPE), kz(HEAD_PAD - QK_NOPE)], axis=2)
    wv = jnp.concatenate([w['w_uv'].reshape(kvl, heads, V_DIM), kz(HEAD_PAD - V_DIM)], axis=2)
    wkv_x = jnp.concatenate([wk.reshape(kvl, -1), wv.reshape(kvl, -1)], axis=1)

    wbo = w['w_b_out'].reshape(heads, V_DIM, d)
    wbo_x = jnp.concatenate([wbo, jnp.zeros((heads, HEAD_PAD - V_DIM, d), wbo.dtype)], axis=1).reshape(-1, d)
    wgu = jnp.concatenate([w['w_gate'], w['w_up']], axis=1)
    return dict(w_in=w_in_x, w_glu=w['w_glu'], w_a_out=w['w_a_out'], wq=wq_x, wkv=wkv_x, wbo=wbo_x,
                w_out=w['w_out'], wgu=wgu, w_down=w['w_down'])


def _gathered_to_full(gathered):
    full = {}
    for n, pc in zip(SHARDED, gathered):
        dep, r, c = pc.shape[1:]
        if n in COL_SHARDED:
            full[n] = pc.transpose(1, 2, 0, 3).reshape(dep, r, N_DEV * c)
        else:
            full[n] = pc.transpose(1, 0, 2, 3).reshape(dep, N_DEV * r, c)
    return full


def _layer_fwd(x, mod, ops, s5, small, rope, dims):
    d, sw, ql, kvl, heads, dff = dims['d'], dims['sw'], dims['ql'], dims['kvl'], dims['heads'], dims['dff']
    zo = dims['zoff']
    hw = heads * HEAD_PAD
    cos, sin = rope
    sh1, sc1, g1, sh2, sc2, g2 = mod
    bf = lambda a: a.astype(BF16)
    sv = dict(x=x)
    (h1,) = _rowwise("norm1_fwd", _f_norm_mod, [x], [small['norm1_g'], sc1, sh1], [(d, BF16)])
    z = _mm("w_in_fwd", h1, bf(ops['w_in']))
    sv.update(h1=h1, z=z)
    a_re, a_im, bmat, cmat = s5
    u_seg = _to_segments(z[:, zo['u']:zo['u'] + sw])
    nsup, sc = dims['nsup'], dims['s5_chunk']
    s5kw = dict(nsup=nsup, sc=sc)
    ends = _s5_pass("s5_ends_fwd", u_seg, a_re, a_im, bf(bmat), reverse=False, **s5kw)
    hst, ych = _s5_pass("s5_scan_fwd", u_seg, a_re, a_im, bf(bmat), reverse=False, ends=ends, w_out=bf(cmat), **s5kw)
    ych = _from_segments(ych)
    (yg,) = _rowwise("s5_gelu_fwd", _f_gelu_in, [ych, (z, zo['u'], sw)], [small['ssm_d']], [(sw, F32)])
    pre = _mm("s5_glu_mm_fwd", yg, bf(ops['w_glu']))
    (s5o,) = _rowwise("s5_glu_fwd", _f_glu, [yg, pre], [small['b_glu']], [(sw, BF16)])
    ya = _mm("s5_out_fwd", s5o, bf(ops['w_a_out']), out_dtype=BF16)
    sv.update(u_seg=u_seg, hst=hst, ych=ych, yg=yg, pre=pre, s5o=s5o, ya=ya)
    (cq,) = _rowwise("q_norm_fwd", _f_rms, [(z, zo['cq'], ql)], [small['q_norm_g']], [(ql, BF16)])
    qab = _mm("q_up_fwd", cq, bf(ops['wq']))
    scale = dims['scale']
    (q,) = _rowwise("q_rope_fwd", lambda a, b, cb, sb: _f_rope_q(a, b, cb, sb) * scale,
                    [(qab, 0, hw), (qab, hw, hw), cos, sin], [], [(hw, BF16)])
    (ckv,) = _rowwise("kv_norm_fwd", _f_rms, [(z, zo['ckv'], kvl)], [small['kv_norm_g']], [(kvl, BF16)])
    knv = _mm("kv_up_fwd", ckv, bf(ops['wkv']), out_dtype=BF16)
    (k,) = _rowwise("k_rope_fwd", _f_rope_k,
                    [(knv, 0, hw), (z, zo['kra'], HEAD_PAD), (z, zo['krb'], HEAD_PAD), cos, sin], [], [(hw, BF16)])
    o, lse = _flash_fwd(q, k, knv, heads=heads, t=dims['tq'])
    yb = _mm("mla_out_fwd", o, bf(ops['wbo']), out_dtype=BF16)
    sv.update(cq=cq, q=q, ckv=ckv, knv=knv, k=k, o=o, lse=lse, yb=yb)
    (merged,) = _rowwise("merge_fwd", _f_merge, [(z, zo['ga'], d), (z, zo['gb'], d), ya, yb], [], [(d, BF16)])
    t1 = _mm("w_out_fwd", merged, bf(ops['w_out']))
    (x1,) = _rowwise("res1_fwd", _f_res, [x, t1], [g1], [(d, F32)])
    sv.update(merged=merged, t1=t1, x1=x1)
    (h2,) = _rowwise("norm2_fwd", _f_norm_mod, [x1], [small['norm2_g'], sc2, sh2], [(d, BF16)])
    ab = _mm("ffn_up_fwd", h2, bf(ops['wgu']), out_dtype=BF16)
    (f,) = _rowwise("swiglu_fwd", _f_swiglu, [(ab, 0, dff), (ab, dff, dff)], [], [(dff, BF16)], tm=256)
    t2 = _mm("ffn_down_fwd", f, bf(ops['w_down']))
    (x2,) = _rowwise("res2_fwd", _f_res, [x1, t2], [g2], [(d, F32)])
    sv.update(h2=h2, ab=ab, f=f, t2=t2)
    return x2, sv


def _layer_bwd(dx2, sv, mod, ops, s5, small, rope, dims):
    d, sw, ql, kvl, heads, dff = dims['d'], dims['sw'], dims['ql'], dims['kvl'], dims['heads'], dims['dff']
    zo = dims['zoff']
    hw = heads * HEAD_PAD
    cos, sin = rope
    sh1, sc1, g1, sh2, sc2, g2 = mod
    a_re, a_im, bmat, cmat = s5
    z = sv['z']
    tr = lambda a: a.T.astype(BF16)
    gops, gsm = {}, {}
    dt2, dg2 = _res_bwd("res2_bwd", sv['t2'], g2, dx2)
    gops['w_down'] = _mm("ffn_down_dw", sv['f'], dt2, ta=True)
    df = _mm("ffn_down_dx", dt2, tr(ops['w_down']), out_dtype=BF16)
    ab = sv['ab']
    (dab_a, dab_b) = _rowwise("swiglu_bwd", _vjp_fn(_f_swiglu, 2, 1, (0, 1)),
                              [(ab, 0, dff), (ab, dff, dff), df], [], [(dff, BF16), (dff, BF16)], tm=256)
    dab = jnp.concatenate([dab_a, dab_b], axis=1)
    gops['wgu'] = _mm("ffn_up_dw", sv['h2'], dab, ta=True)
    dh2 = _mm("ffn_up_dx", dab, tr(ops['wgu']))
    dx1, dn2, dsc2, dsh2 = _norm_mod_bwd("norm2_bwd", sv['x1'], small['norm2_g'], sc2, sh2, dh2, dx2)
    gsm['norm2_g'] = dn2
    dt1, dg1 = _res_bwd("res1_bwd", sv['t1'], g1, dx1)
    gops['w_out'] = _mm("w_out_dw", sv['merged'], dt1, ta=True)
    dmerged = _mm("w_out_dx", dt1, tr(ops['w_out']), out_dtype=BF16)
    dga, dgb, dya, dyb = _rowwise(
        "merge_bwd", _vjp_fn(_f_merge, 4, 1, (0, 1, 2, 3)),
        [(z, zo['ga'], d), (z, zo['gb'], d), sv['ya'], sv['yb'], dmerged], [],
        [(d, BF16), (d, BF16), (d, BF16), (d, BF16)])
    gops['wbo'] = _mm("mla_out_dw", sv['o'], dyb, ta=True)
    do = _mm("mla_out_dx", dyb, tr(ops['wbo']), out_dtype=BF16)
    delta = _flash_delta(do, sv['o'], heads=heads, t=dims['tq'])
    as_rows = lambda a: a.reshape(heads, 1, -1)
    dq, dk, dv = _flash_bwd(sv['q'], sv['k'], sv['knv'], do, as_rows(sv['lse']), as_rows(delta),
                            heads=heads, t=dims['tq'])
    def k_bwd(dkb, cosb, sinb):
        dkb = _f32(dkb)
        dkpe = dkb[:, 0:HEAD_PAD]
        for h in range(1, heads):
            dkpe = dkpe + dkb[:, h * HEAD_PAD:(h + 1) * HEAD_PAD]
        return dkpe * cosb, dkpe * sinb
    dkra, dkrb = _rowwise("k_rope_bwd", k_bwd, [dk, cos, sin], [], [(HEAD_PAD, BF16), (HEAD_PAD, BF16)])
    dknv = jnp.concatenate([dk, dv], axis=1)
    gops['wkv'] = _mm("kv_up_dw", sv['ckv'], dknv, ta=True)
    dckv = _mm("kv_up_dx", dknv, tr(ops['wkv']))
    dckv_in, dkvg = _rms_bwd("kv_norm_bwd", z, zo['ckv'], kvl, small['kv_norm_g'], dckv)
    gsm['kv_norm_g'] = dkvg
    def q_bwd(dqb, cosb, sinb):
        dqb = _f32(dqb) * dims['scale']
        return dqb * jnp.tile(cosb, (1, heads)), dqb * jnp.tile(sinb, (1, heads))
    dqa, dqb_ = _rowwise("q_rope_bwd", q_bwd, [dq, cos, sin], [], [(hw, BF16), (hw, BF16)])
    dqab = jnp.concatenate([dqa, dqb_], axis=1)
    gops['wq'] = _mm("q_up_dw", sv['cq'], dqab, ta=True)
    dcq = _mm("q_up_dx", dqab, tr(ops['wq']))
    dcq_in, dqg = _rms_bwd("q_norm_bwd", z, zo['cq'], ql, small['q_norm_g'], dcq)
    gsm['q_norm_g'] = dqg
    gops['w_a_out'] = _mm("s5_out_dw", sv['s5o'], dya, ta=True)
    ds5o = _mm("s5_out_dx", dya, tr(ops['w_a_out']))

    def glu_bwd(yg, pre, ds, b):
        _, vjp = jax.vjp(_f_glu, _f32(yg), _f32(pre), b)
        dyg, dpre, db = vjp(_f32(ds))
        return dyg, dpre, db
    dyg_a, dpre, dbglu = _rowwise("s5_glu_bwd", glu_bwd, [sv['yg'], sv['pre'], ds5o], [small['b_glu']],
                                  [(sw, F32), (sw, BF16)], [sw])
    gsm['b_glu'] = dbglu
    gops['w_glu'] = _mm("s5_glu_mm_dw", sv['yg'], dpre, ta=True)
    dyg_b = _mm("s5_glu_mm_dx", dpre, tr(ops['w_glu']))

    def gelu_bwd(ych, u, dya_, dyb_, dvec):
        _, vjp = jax.vjp(_f_gelu_in, _f32(ych), _f32(u), dvec)
        dych, du, dd = vjp(_f32(dya_) + _f32(dyb_))
        return dych, du, dd
    dy, du_skip, dssm_d = _rowwise("s5_gelu_bwd", gelu_bwd, [sv['ych'], (z, zo['u'], sw), dyg_a, dyg_b],
                                   [small['ssm_d']], [(sw, BF16), (sw, F32)], [sw])
    gsm['ssm_d'] = dssm_d
    dy_seg = _to_segments(dy)
    nsup, sc = dims['nsup'], dims['s5_chunk']
    tr_blocks = lambda a: a.reshape(nsup, -1, a.shape[1]).transpose(0, 2, 1).reshape(-1, a.shape[0] // nsup)
    s5kw = dict(nsup=nsup, sc=sc)
    c_t, b_t = tr_blocks(cmat).astype(BF16), tr_blocks(bmat).astype(BF16)
    ends = _s5_pass("s5_ends_bwd", dy_seg, a_re, a_im, c_t, reverse=True, **s5kw)
    du_scan, g_bmat, g_cmat, dar, dai = _s5_pass("s5_scan_bwd", dy_seg, a_re, a_im, c_t, reverse=True, ends=ends,
                                                 w_out=b_t, u=sv['u_seg'], h=sv['hst'], **s5kw)
    du_scan = _from_segments(du_scan)
    (du,) = _rowwise("s5_du_sum", lambda a, b: _f32(a) + _f32(b), [du_skip, du_scan], [], [(sw, BF16)])
    gs5 = (dar, dai, g_bmat, g_cmat)
    dz = jnp.concatenate([dga, dgb, du, dcq_in, dckv_in, dkra, dkrb], axis=1)
    gops['w_in'] = _mm("w_in_dw", sv['h1'], dz, ta=True)
    dh1 = _mm("w_in_dx", dz, tr(ops['w_in']))
    dx, dn1, dsc1, dsh1 = _norm_mod_bwd("norm1_bwd", sv['x'], small['norm1_g'], sc1, sh1, dh1, dx1)
    gsm['norm1_g'] = dn1
    dmod = (dsh1, dsc1, dg1, dsh2, dsc2, dg2)
    return dx, gops, gs5, gsm, dmod


def _res_bwd(name, t, g, dxo):
    def fn(tb, db, gb):
        db = _f32(db)
        return gb * db, jnp.sum(db * _f32(tb), axis=0, keepdims=True)
    return _rowwise(name, fn, [t, dxo], [g], [(t.shape[1], BF16)], [t.shape[1]])


def _norm_mod_bwd(name, x, g, sc, sh, dh, dres):
    def fn(xb, dhb, dresb, gb, scb, shb):
        _, vjp = jax.vjp(_f_norm_mod, _f32(xb), gb, scb, shb)
        dx, dg, dsc, dsh = vjp(_f32(dhb))
        return dx + _f32(dresb), dg, dsc, dsh
    w = x.shape[1]
    return _rowwise(name, fn, [x, dh, dres], [g, sc, sh], [(w, F32)], [w, w, w])


def _rms_bwd(name, z, off, w, g, dy):
    def fn(xb, dyb, gb):
        _, vjp = jax.vjp(_f_rms, _f32(xb), gb)
        dx, dg = vjp(_f32(dyb))
        return dx, dg
    return _rowwise(name, fn, [(z, off, w), dy], [g], [(w, BF16)], [w])


S5_NAMES = ('ssm_a_re', 'ssm_a_im', 'ssm_log_dt', 'ssm_b_re', 'ssm_b_im', 'ssm_c_re', 'ssm_c_im')
LAYER_VECS = ('norm1_g', 'ssm_d', 'b_glu', 'q_norm_g', 'kv_norm_g', 'norm2_g')


def _step(p, mom_m, mom_v, x, c, positions, loss_target):
    depth, d = p['norm1_g'].shape
    L = x.shape[1]
    sw = p['ssm_d'].shape[1]
    ql, kvl = p['q_norm_g'].shape[1], p['kv_norm_g'].shape[1]
    heads = p['w_uk'].shape[2] * N_DEV // QK_NOPE
    dff = p['w_gate'].shape[2] * N_DEV
    ada_w = p['w_ada'].shape[2]
    zoff, o = {}, 0
    for nm, sz in (('ga', d), ('gb', d), ('u', sw), ('cq', ql), ('ckv', kvl), ('kra', HEAD_PAD), ('krb', HEAD_PAD)):
        assert o % sz == 0, (nm, o, sz)
        zoff[nm] = o
        o += sz
    groups, states = p['ssm_a_re'].shape[1:]
    assert groups % S5_SUPER == 0 and p['ssm_b_re'].shape[3] * S5_SUPER == LANES
    dims = dict(d=d, sw=sw, ql=ql, kvl=kvl, heads=heads, dff=dff, zoff=zoff,
                nsup=groups // S5_SUPER, s5_chunk=S5_SUPER * states,
                tq=1024 if L >= 4096 else 128, scale=(QK_NOPE + QK_ROPE) ** -0.5)
    x = x.reshape(L, d)
    tgt = loss_target.reshape(L, d)

    posf = positions.reshape(L).astype(F32)
    inv_freq = ROPE_BASE ** (-jnp.arange(0, QK_ROPE, 2, dtype=F32) / QK_ROPE)
    ang = posf[:, None] * inv_freq
    cs, sn = jnp.cos(ang), jnp.sin(ang)
    padr = HEAD_PAD - QK_NOPE - QK_ROPE
    cos = jnp.concatenate([jnp.ones((L, QK_NOPE), F32), cs, cs, jnp.zeros((L, padr), F32)], axis=1)
    sin = jnp.concatenate([jnp.zeros((L, QK_NOPE), F32), sn, sn, jnp.zeros((L, padr), F32)], axis=1)
    rope = (cos, sin)

    gathered = _gather_two_level("gather_weights", [p[n].astype(BF16) for n in SHARDED])

    def make_ops(gl):
        full = _gathered_to_full(gl)
        return [_layer_operators({n: full[n][l] for n in SHARDED}, dims) for l in range(depth)]

    ops, ops_vjp = jax.vjp(make_ops, [g.astype(F32) for g in gathered])

    def make_s5(sp):
        return [_s5_operators(*[sp[n][l] for n in S5_NAMES]) for l in range(depth)]

    s5ops, s5_vjp = jax.vjp(make_s5, {n: p[n] for n in S5_NAMES})

    (c_slabs,) = _exchange("gather_c", [jnp.pad(c, ((0, 7), (0, 0)))], all_to_all=False)
    c_all = c_slabs[:, 0, :]
    (c_act,) = _rowwise("c_silu", lambda a: jax.nn.silu(a), [jnp.pad(c_all, ((0, 8), (0, 0)))], [], [(d, F32)])
    w_ada_cat = p['w_ada'].transpose(1, 0, 2).reshape(d, depth * ada_w)
    mod_cols = _mm("ada_fwd", c_act, w_ada_cat)[:N_DEV]
    (mod_rows,) = _exchange("a2a_mod", [_pack_slabs(mod_cols)], all_to_all=True)
    mod_mine = _unpack_slabs(mod_rows, depth * ada_w).reshape(N_DEV, depth, ada_w)
    mod_mine = mod_mine.transpose(1, 0, 2).reshape(depth, N_DEV * ada_w)
    (mod_full,) = _rowwise("ada_bias", lambda a, b: a + b, [mod_mine, p['b_ada']], [], [(6 * d, F32)])
    mods = [tuple(mod_full[l:l + 1, i * d:(i + 1) * d] for i in range(6)) for l in range(depth)]

    saved = []
    xl = x
    for l in range(depth):
        small = {n: p[n][l:l + 1] for n in LAYER_VECS}
        xl, sv = _layer_fwd(xl, mods[l], ops[l], s5ops[l], small, rope, dims)
        saved.append((sv, small))

    def final_fn(xb, tb, gb):
        def lossf(xv, gv):
            e = _f_rms(xv, gv) - tb
            per_row = 0.5 * jnp.mean(e * e, axis=-1, keepdims=True)
            return jnp.sum(per_row, axis=0, keepdims=True)
        lv, vjp = jax.vjp(lossf, xb, gb)
        dxb, dgb = vjp(jnp.ones((1, 1), F32))
        return dxb, jnp.broadcast_to(lv, (1, LANES)), dgb
    dx, loss_vec, dfinal_g = _rowwise("final_loss", final_fn, [xl, tgt], [p['final_g'].reshape(1, d)],
                                      [(d, F32)], [LANES, d])

    g_ops, g_s5, g_small, dmods = [None] * depth, [None] * depth, [None] * depth, [None] * depth
    for l in reversed(range(depth)):
        sv, small = saved[l]
        dx, g_ops[l], g_s5[l], g_small[l], dmods[l] = _layer_bwd(dx, sv, mods[l], ops[l], s5ops[l], small,
                                                                rope, dims)
    grad_x = dx.reshape(1, L, d)

    dmod_mine = jnp.stack([jnp.concatenate(dm, axis=1)[0] for dm in dmods])
    dmod_slabs = dmod_mine.reshape(depth, N_DEV, ada_w).transpose(1, 0, 2).reshape(N_DEV, depth * ada_w)
    (dmod_recv,) = _exchange("a2a_dmod", [_pack_slabs(dmod_slabs)], all_to_all=True)
    dmod_cols = _unpack_slabs(dmod_recv, depth * ada_w)
    g_ada = _mm("ada_dw", c_act, jnp.pad(dmod_cols, ((0, 8), (0, 0))), ta=True)
    grads, delta, new_m, new_v = {}, {}, {}, {}
    grads['w_ada'] = g_ada.reshape(d, depth, ada_w).transpose(1, 0, 2)
    two_d = lambda a: a.reshape(-1, a.shape[-1])
    res = _rowwise("adamw_w_ada", _adamw_fn, [two_d(a) for a in (p['w_ada'], grads['w_ada'], mom_m['w_ada'],
                                                                  mom_v['w_ada'])], [], [(ada_w, F32)] * 3)
    delta['w_ada'], new_m['w_ada'], new_v['w_ada'] = [r.reshape(p['w_ada'].shape) for r in res]

    (g_slabs,) = ops_vjp(g_ops)
    g_recv = _exchange("a2a_grads", [g.astype(BF16) for g in g_slabs], all_to_all=True)
    for n, rv in zip(SHARDED, g_recv):
        shp = p[n].shape
        res = _adamw_sharded("adamw_" + n, two_d(p[n]), rv.reshape(N_DEV, -1, shp[-1]), two_d(mom_m[n]),
                             two_d(mom_v[n]))
        grads[n], delta[n], new_m[n], new_v[n] = [r.reshape(shp) for r in res]

    (g_s5p,) = s5_vjp(g_s5)
    part = dict(g_s5p)
    part['b_ada'] = dmod_mine
    for n in LAYER_VECS:
        part[n] = jnp.concatenate([g_small[l][n] for l in range(depth)], axis=0)
    part['final_g'] = dfinal_g.reshape(d)
    small_shapes = [p[n].shape for n in SMALL] + [(1,)]
    (small_recv,) = _exchange("gather_small", [_pack_rows([part[n] for n in SMALL] + [loss_vec[0, 0:1]])],
                              all_to_all=False)
    small_sum = _sum_slabs("sum_small", small_recv)
    small_list = _unpack_rows(small_sum, small_shapes)
    grads.update(zip(SMALL, small_list[:-1]))
    loss = small_list[-1].reshape(())
    dummy = [jnp.zeros((1,), F32)]
    res = _rowwise("adamw_small", _adamw_fn,
                   [_pack_rows([src[n] for n in SMALL] + dummy) for src in (p, )] + [small_sum] +
                   [_pack_rows([src[n] for n in SMALL] + dummy) for src in (mom_m, mom_v)], [], [(FLAT_W, F32)] * 3)
    for dst, r in zip((delta, new_m, new_v), res):
        dst.update(zip(SMALL, _unpack_rows(r, small_shapes)[:-1]))
    return (loss, grad_x, *[grads[n] for n in WEIGHTS], *[delta[n] for n in WEIGHTS],
            *[new_m[n] for n in WEIGHTS], *[new_v[n] for n in WEIGHTS])


def kernel(x, c, positions, w_ada, b_ada, norm1_g, w_in, ssm_a_re, ssm_a_im, ssm_log_dt, ssm_b_re, ssm_b_im, ssm_c_re, ssm_c_im, ssm_d, w_glu, b_glu, w_a_out, q_norm_g, w_uq, kv_norm_g, w_uk, w_uv, w_b_out, w_out, norm2_g, w_gate, w_up, w_down, final_g, loss_target, m_w_ada, m_b_ada, m_norm1_g, m_w_in, m_ssm_a_re, m_ssm_a_im, m_ssm_log_dt, m_ssm_b_re, m_ssm_b_im, m_ssm_c_re, m_ssm_c_im, m_ssm_d, m_w_glu, m_b_glu, m_w_a_out, m_q_norm_g, m_w_uq, m_kv_norm_g, m_w_uk, m_w_uv, m_w_b_out, m_w_out, m_norm2_g, m_w_gate, m_w_up, m_w_down, m_final_g, v_w_ada, v_b_ada, v_norm1_g, v_w_in, v_ssm_a_re, v_ssm_a_im, v_ssm_log_dt, v_ssm_b_re, v_ssm_b_im, v_ssm_c_re, v_ssm_c_im, v_ssm_d, v_w_glu, v_b_glu, v_w_a_out, v_q_norm_g, v_w_uq, v_kv_norm_g, v_w_uk, v_w_uv, v_w_b_out, v_w_out, v_norm2_g, v_w_gate, v_w_up, v_w_down, v_final_g):
    given = dict(locals())
    p = {n: given[n] for n in WEIGHTS}
    mom_m = {n: given["m_" + n] for n in WEIGHTS}
    mom_v = {n: given["v_" + n] for n in WEIGHTS}
    return _step(p, mom_m, mom_v, x, c, positions, loss_target)
```

```python
import functools
import math

import jax
import jax.numpy as jnp
from jax import lax
from jax.experimental import pallas as pl
from jax.experimental.pallas import tpu as pltpu

F32 = jnp.float32
BF16 = jnp.bfloat16

N_DEV = 8
LANES = 128
FLAT_W = 1024
VMEM_LIMIT = 48 * 1024 * 1024
MM_VMEM_BUDGET = 36 * 1024 * 1024
QK_NOPE, QK_ROPE, V_DIM = 64, 32, 64
HEAD_PAD = LANES
ROPE_BASE = 10000.0
EPS = 1e-6
ADAM_LR, ADAM_B1, ADAM_B2, ADAM_EPS, ADAM_WD, ADAM_STEP = 0.001, 0.9, 0.999, 1e-08, 0.01, 10
NEG = float(jnp.finfo(jnp.float32).min)

WEIGHTS = ['w_ada', 'b_ada', 'norm1_g', 'w_in', 'ssm_a_re', 'ssm_a_im', 'ssm_log_dt', 'ssm_b_re', 'ssm_b_im',
           'ssm_c_re', 'ssm_c_im', 'ssm_d', 'w_glu', 'b_glu', 'w_a_out', 'q_norm_g', 'w_uq', 'kv_norm_g', 'w_uk',
           'w_uv', 'w_b_out', 'w_out', 'norm2_g', 'w_gate', 'w_up', 'w_down', 'final_g']
COL_SHARDED = ['w_in', 'w_a_out', 'w_uq', 'w_uk', 'w_uv', 'w_b_out', 'w_gate', 'w_up']
ROW_SHARDED = ['w_glu', 'w_out', 'w_down']
SHARDED = COL_SHARDED + ROW_SHARDED
SMALL = ['b_ada', 'norm1_g', 'ssm_a_re', 'ssm_a_im', 'ssm_log_dt', 'ssm_b_re', 'ssm_b_im', 'ssm_c_re', 'ssm_c_im',
         'ssm_d', 'b_glu', 'q_norm_g', 'kv_norm_g', 'norm2_g', 'final_g']


def _cparams(sem):
    return pltpu.CompilerParams(dimension_semantics=sem, vmem_limit_bytes=VMEM_LIMIT)


def _pick(dim, pref, quantum=LANES):
    if dim <= pref:
        return dim
    t = (pref // quantum) * quantum
    while t >= quantum:
        if dim % t == 0:
            return t
        t -= quantum
    return dim


def _mm(name, a, b, *, ta=False, out_dtype=F32, a_col=None, b_col=None, a_fn=None, tm=1408, tn=1408, tk=None):
    if a_fn is None:
        a_off, a_w = a_col if a_col is not None else (0, a.shape[1])
        a_parts = [(a, a_off, a_w)]
    else:
        a_parts = a
        a_w = a_parts[0][2]
        assert all(w == a_w for _, _, w in a_parts)
    a0 = a_parts[0][0]
    b_off, b_w = b_col if b_col is not None else (0, b.shape[1])
    if ta:
        kdim, m = a0.shape[0], a_w
        assert b.shape[0] == kdim
    else:
        m, kdim = a0.shape[0], a_w
        assert b.shape[0] == kdim, (name, a0.shape, b.shape)
    n = b_w
    tm = _pick(m, tm, LANES if ta else 8)
    tn = _pick(n, tn)
    a_bytes = sum(arr.dtype.itemsize for arr, _, _ in a_parts) + (8 if a_fn is not None else 0)
    if tk is None:
        def fits(tm_, tk_):
            return (tm_ * tn * (2 * jnp.dtype(out_dtype).itemsize + 4)
                    + tk_ * (2 * (tm_ * a_bytes + tn * b.dtype.itemsize) + 2 * (tm_ + tn))) <= MM_VMEM_BUDGET

        tk = kdim
        while not fits(tm, tk):
            if not ta and tm % 16 == 0 and tm >= 1024 and fits(tm // 2, tk):
                tm //= 2
                break
            if tk % (2 * (8 if ta else LANES)):
                break
            tk //= 2
    tk = _pick(kdim, tk, 8 if ta else LANES)
    nk = kdim // tk
    assert m % tm == 0 and n % tn == 0 and kdim % tk == 0, (name, m, n, kdim, tm, tn, tk)
    a_specs = []
    for _, off, _ in a_parts:
        if ta:
            assert off % tm == 0
            a_specs.append(pl.BlockSpec((tk, tm), functools.partial(lambda i, j, k, o: (k, i + o), o=off // tm)))
        else:
            assert off % tk == 0
            a_specs.append(pl.BlockSpec((tm, tk), functools.partial(lambda i, j, k, o: (i, k + o), o=off // tk)))
    dims = _TN if ta else (((1,), (0,)), ((), ()))
    assert b_off % tn == 0
    b_spec = pl.BlockSpec((tk, tn), lambda i, j, k: (k, j + b_off // tn))
    na = len(a_parts)

    def prod(refs):
        if a_fn is None:
            av = refs[0][...]
        else:
            av = a_fn(*[r[...].astype(F32) for r in refs[:na]])
        return lax.dot_general(av.astype(BF16), refs[na][...].astype(BF16), dims, preferred_element_type=F32)

    def body_one(*refs):
        refs[na + 1][...] = prod(refs).astype(refs[na + 1].dtype)

    def body_acc(*refs):
        o_ref, acc_ref = refs[na + 1], refs[na + 2]

        @pl.when(pl.program_id(2) == 0)
        def _():
            acc_ref[...] = jnp.zeros_like(acc_ref)

        acc_ref[...] += prod(refs)

        @pl.when(pl.program_id(2) == nk - 1)
        def _():
            o_ref[...] = acc_ref[...].astype(o_ref.dtype)

    return pl.pallas_call(
        body_one if nk == 1 else body_acc, name=name,
        out_shape=jax.ShapeDtypeStruct((m, n), out_dtype),
        grid=(m // tm, n // tn, nk),
        in_specs=a_specs + [b_spec],
        out_specs=pl.BlockSpec((tm, tn), lambda i, j, k: (i, j)),
        scratch_shapes=[] if nk == 1 else [pltpu.VMEM((tm, tn), F32)],
        compiler_params=_cparams(("parallel", "parallel", "arbitrary")),
    )(*[arr for arr, _, _ in a_parts], b)


def _rowwise(name, fn, rows, vecs, outs, reds=(), *, tm=512):
    rows = [(r, 0, r.shape[1]) if not isinstance(r, tuple) else r for r in rows]
    nrows = rows[0][0].shape[0]
    tm = _pick(nrows, tm, 8)
    assert nrows % tm == 0, (name, nrows, tm)
    nr, nv, no = len(rows), len(vecs), len(outs)
    in_specs = []
    for arr, off, w in rows:
        assert arr.shape[0] == nrows and off % w == 0, (name, arr.shape, off, w)
        in_specs.append(pl.BlockSpec((tm, w), functools.partial(lambda i, cb: (i, cb), cb=off // w)))
    for v in vecs:
        assert v.ndim == 2 and v.shape[0] == 1, (name, v.shape)
        in_specs.append(pl.BlockSpec(v.shape, lambda i: (0, 0)))
    out_shape = [jax.ShapeDtypeStruct((nrows, w), dt) for w, dt in outs]
    out_specs = [pl.BlockSpec((tm, w), lambda i: (i, 0)) for w, dt in outs]
    out_shape += [jax.ShapeDtypeStruct((1, w), F32) for w in reds]
    out_specs += [pl.BlockSpec((1, w), lambda i: (0, 0)) for w in reds]

    def body(*refs):
        rin, vin = refs[:nr], refs[nr:nr + nv]
        rout, rred = refs[nr + nv:nr + nv + no], refs[nr + nv + no:]
        res = fn(*[r[...].astype(F32) for r in rin], *[v[...] for v in vin])
        if not isinstance(res, (tuple, list)):
            res = (res,)
        assert len(res) == no + len(reds), (name, len(res))
        for r, val in zip(rout, res[:no]):
            r[...] = val.astype(r.dtype)
        if reds:
            @pl.when(pl.program_id(0) == 0)
            def _():
                for r in rred:
                    r[...] = jnp.zeros_like(r)

            for r, val in zip(rred, res[no:]):
                r[...] += val.astype(F32)

    res = pl.pallas_call(
        body, name=name, out_shape=out_shape, grid=(nrows // tm,),
        in_specs=in_specs, out_specs=out_specs,
        compiler_params=_cparams(("arbitrary",) if reds else ("parallel",)),
    )(*[r[0] for r in rows], *vecs)
    return res


def _f32(x):
    return x.astype(F32)


def _vjp_fn(f, n_in, n_cot, want):
    def fn(*args):
        ins = [_f32(a) for a in args[:n_in]]
        cots = tuple(_f32(a) for a in args[n_in:n_in + n_cot])
        _, vjp = jax.vjp(f, *ins)
        grads = vjp(cots if n_cot > 1 else cots[0])
        return tuple(grads[i] for i in want)
    return fn


def _f_rms(x, g):
    return (x * lax.rsqrt(jnp.mean(x * x, axis=-1, keepdims=True) + EPS)) * g


def _f_norm_mod(x, g, sc, sh):
    return _f_rms(x, g) * (1.0 + sc) + sh


def _f_gelu_in(ych, u, d):
    return jax.nn.gelu(ych + d * u)


def _f_glu(yg, pre, b):
    return yg * jax.nn.sigmoid(pre + b)


def _f_merge(ga, gb, ya, yb):
    return jax.nn.sigmoid(ga) * ya + jax.nn.sigmoid(gb) * yb


def _f_res(x, t, g):
    return x + g * t


def _f_swiglu(a, b):
    return jax.nn.silu(a) * b


def _f_rope_q(qa, qb, cos, sin):
    h = qa.shape[1] // HEAD_PAD
    return qa * jnp.tile(cos, (1, h)) + qb * jnp.tile(sin, (1, h))


def _f_rope_k(kn, kra, krb, cos, sin):
    h = kn.shape[1] // HEAD_PAD
    return kn + jnp.tile(kra * cos + krb * sin, (1, h))


SUBLANES = 8
S5_SUPER = 8


def _cpow(ar, ai, log2n):
    for _ in range(log2n):
        ar, ai = ar * ar - ai * ai, 2.0 * ar * ai
    return ar, ai


def _to_segments(x):
    L, w = x.shape
    return x.reshape(SUBLANES, L // SUBLANES, w).transpose(1, 0, 2).reshape(L, w)


def _from_segments(x):
    L, w = x.shape
    return x.reshape(L // SUBLANES, SUBLANES, w).transpose(1, 0, 2).reshape(L, w)


def _s5_pass(name, xin, a_re, a_im, w_in, *, reverse, nsup, sc, ends=None, w_out=None, u=None, h=None, tb=512):
    L, sw = xin.shape
    gp2 = nsup * 2 * sc
    gp = gp2 // 2
    seg = L // SUBLANES
    assert sw == nsup * LANES and L % SUBLANES == 0 and seg & (seg - 1) == 0, (L, sw)
    tb = _pick(L, tb, SUBLANES)
    nb, nt = L // tb, tb // SUBLANES
    first_pass = ends is None
    sign = -1.0 if reverse else 1.0
    blk = (lambda i: (nb - 1 - i, 0)) if reverse else (lambda i: (i, 0))
    order = (lambda s: nt - 1 - s) if reverse else (lambda s: s)
    tile = lambda s: pl.ds(pl.multiple_of(s * SUBLANES, SUBLANES), SUBLANES)
    const = lambda shape: pl.BlockSpec(shape, lambda i: (0, 0))
    rows_in = pl.BlockSpec((tb, sw), blk)
    rows_st = pl.BlockSpec((tb, gp2), blk)

    def coeffs(ar_ref, ai_ref, cc):
        ar1, ai1 = ar_ref[:, cc], sign * ai_ref[:, cc]
        return ar1, ai1, jnp.broadcast_to(ar1, (SUBLANES, sc)), jnp.broadcast_to(ai1, (SUBLANES, sc))

    def drive(x_ref, w_ref, xs_ref, c):
        lanes = pl.ds(c * LANES, LANES)
        xs_ref[...] = jnp.dot(x_ref[:, lanes].astype(BF16), w_ref[lanes, :], preferred_element_type=F32)

    def body(*refs):
        it = iter(refs)
        x_ref, w_ref, ar_ref, ai_ref = next(it), next(it), next(it), next(it)
        if first_pass:
            e_ref, xs_ref = next(it), next(it)
        elif not reverse:
            e_ref, wo_ref, o_ref, y_ref, st_ref, xs_ref = (next(it) for _ in range(6))
        else:
            (e_ref, wo_ref, u_ref, h_ref, hp_ref, hl_ref, du_ref, gb_ref, gc_ref, dar_ref, dai_ref,
             st_ref, xs_ref, g_ref) = (next(it) for _ in range(14))
        i = pl.program_id(0)

        @pl.when(i == 0)
        def _():
            if first_pass:
                e_ref[...] = jnp.zeros_like(e_ref)
                return
            for c in range(nsup):
                cc, re, im = pl.ds(c * sc, sc), pl.ds(2 * c * sc, sc), pl.ds((2 * c + 1) * sc, sc)
                ar1, ai1, _, _ = coeffs(ar_ref, ai_ref, cc)
                pr, pi = _cpow(ar1, ai1, seg.bit_length() - 1)
                cr = jnp.zeros((1, sc), F32)
                ci = jnp.zeros((1, sc), F32)
                for j in (reversed(range(SUBLANES)) if reverse else range(SUBLANES)):
                    st_ref[j:j + 1, re] = cr
                    st_ref[j:j + 1, im] = ci
                    cr, ci = (e_ref[j:j + 1, re] + pr * cr - pi * ci, e_ref[j:j + 1, im] + pr * ci + pi * cr)
            if reverse:
                for r in (gb_ref, gc_ref, dar_ref, dai_ref):
                    r[...] = jnp.zeros_like(r)

        sub = lax.broadcasted_iota(jnp.int32, (SUBLANES, sc), 0)
        state = e_ref if first_pass else st_ref
        xr, xi = pl.ds(0, sc), pl.ds(sc, sc)
        for c in range(nsup):
            cc, re, im = pl.ds(c * sc, sc), pl.ds(2 * c * sc, sc), pl.ds((2 * c + 1) * sc, sc)
            both = pl.ds(2 * c * sc, 2 * sc)
            lanes = pl.ds(c * LANES, LANES)
            _, _, ar, ai = coeffs(ar_ref, ai_ref, cc)
            drive(x_ref, w_ref, xs_ref, c)

            def advance(rows, sr, si):
                return ar * sr - ai * si + xs_ref[rows, xr], ar * si + ai * sr + xs_ref[rows, xi]

            if first_pass:
                def step(s, st):
                    return advance(tile(order(s)), *st)

                sr, si = lax.fori_loop(0, nt, step, (state[:, re], state[:, im]), unroll=4)
            elif not reverse:
                def step(s, st):
                    rows = tile(s)
                    sr, si = advance(rows, *st)
                    o_ref[rows, re] = sr
                    o_ref[rows, im] = si
                    return sr, si

                sr, si = lax.fori_loop(0, nt, step, (state[:, re], state[:, im]), unroll=4)
                y_ref[:, lanes] = jnp.dot(o_ref[:, both].astype(BF16), wo_ref[both, :], preferred_element_type=F32)
            else:
                def emit(rows, sr, si):
                    sr, si = advance(rows, sr, si)
                    g_ref[rows, xr] = sr
                    g_ref[rows, xi] = si
                    return sr, si

                def grad(sr, si, hpr, hpi, accr, acci):
                    return accr + sr * hpr + si * hpi, acci + si * hpr - sr * hpi

                def step(s, st):
                    sr, si, accr, acci = st
                    t = nt - 1 - s
                    sr, si = emit(tile(t), sr, si)
                    prev = tile(t - 1)
                    return (sr, si) + grad(sr, si, h_ref[prev, re], h_ref[prev, im], accr, acci)

                zero = jnp.zeros((SUBLANES, sc), F32)
                sr, si, accr, acci = lax.fori_loop(0, nt - 1, step, (state[:, re], state[:, im], zero, zero),
                                                   unroll=4)
                sr, si = emit(tile(0), sr, si)
                first = (i == nb - 1)
                wrap_r = jnp.where(sub == 0, 0.0, pltpu.roll(hl_ref[:, re], 1, 0))
                wrap_i = jnp.where(sub == 0, 0.0, pltpu.roll(hl_ref[:, im], 1, 0))
                accr, acci = grad(sr, si, jnp.where(first, wrap_r, hp_ref[:, re]),
                                  jnp.where(first, wrap_i, hp_ref[:, im]), accr, acci)
                dar_ref[:, cc] += jnp.sum(accr, axis=0, keepdims=True)
                dai_ref[:, cc] += jnp.sum(acci, axis=0, keepdims=True)
                gb = g_ref[...].astype(BF16)
                du_ref[:, lanes] = jnp.dot(gb, wo_ref[both, :], preferred_element_type=F32)
                gb_ref[lanes, :] += lax.dot_general(u_ref[:, lanes].astype(BF16), gb, _TN,
                                                    preferred_element_type=F32)
                gc_ref[both, :] += lax.dot_general(h_ref[:, both].astype(BF16), x_ref[:, lanes].astype(BF16), _TN,
                                                   preferred_element_type=F32)
            state[:, re] = sr
            state[:, im] = si

    vec = const((1, gp))
    in_specs = [rows_in, const(w_in.shape), vec, vec]
    operands = [xin, w_in, a_re, a_im]
    xs_scratch = pltpu.VMEM((tb, 2 * sc), F32)
    st_scratch = pltpu.VMEM((SUBLANES, gp2), F32)
    if first_pass:
        out_shape = jax.ShapeDtypeStruct((SUBLANES, gp2), F32)
        out_specs = const((SUBLANES, gp2))
        scratch = [xs_scratch]
    elif not reverse:
        in_specs += [const((SUBLANES, gp2)), const(w_out.shape)]
        operands += [ends, w_out]
        out_shape = [jax.ShapeDtypeStruct((L, gp2), F32), jax.ShapeDtypeStruct((L, sw), F32)]
        out_specs = [rows_st, rows_in]
        scratch = [st_scratch, xs_scratch]
    else:
        in_specs += [const((SUBLANES, gp2)), const(w_out.shape), rows_in, rows_st,
                     pl.BlockSpec((SUBLANES, gp2), lambda i: (jnp.maximum((nb - 1 - i) * nt - 1, 0), 0)),
                     pl.BlockSpec((SUBLANES, gp2), lambda i: (seg - 1, 0))]
        operands += [ends, w_out, u, h, h, h]
        out_shape = [jax.ShapeDtypeStruct((L, sw), F32), jax.ShapeDtypeStruct((sw, 2 * sc), F32),
                     jax.ShapeDtypeStruct((gp2, LANES), F32), jax.ShapeDtypeStruct((1, gp), F32),
                     jax.ShapeDtypeStruct((1, gp), F32)]
        out_specs = [rows_in, const((sw, 2 * sc)), const((gp2, LANES)), vec, vec]
        scratch = [st_scratch, xs_scratch, pltpu.VMEM((tb, 2 * sc), F32)]
    return pl.pallas_call(
        body, name=name, out_shape=out_shape, grid=(nb,), in_specs=in_specs, out_specs=out_specs,
        scratch_shapes=scratch, compiler_params=_cparams(("arbitrary",)),
    )(*operands)


_NT = (((1,), (1,)), ((), ()))
_TN = (((0,), (0,)), ((), ()))


def _causal(s, t, k_major=False):
    row = lax.broadcasted_iota(jnp.int32, (t, t), 0)
    col = lax.broadcasted_iota(jnp.int32, (t, t), 1)
    return jnp.where(row <= col if k_major else col <= row, s, NEG)


def _pair_tables(n, k_major):
    if k_major:
        pairs = [(qi, ki) for ki in range(n) for qi in range(ki, n)]
    else:
        pairs = [(qi, ki) for qi in range(n) for ki in range(qi + 1)]
    return (jnp.asarray([p[0] for p in pairs], jnp.int32), jnp.asarray([p[1] for p in pairs], jnp.int32))


def _flash_fwd(q, k, knv, *, heads, t):
    L = q.shape[0]
    n = L // t
    rep = t // LANES
    qtab, ktab = _pair_tables(n, k_major=False)

    def body(qt_ref, kt_ref, q_ref, k_ref, v_ref, o_ref, lse_ref, m_s, acc_s):
        step = pl.program_id(1)
        qi, ki = qt_ref[step], kt_ref[step]
        lane = lax.broadcasted_iota(jnp.int32, (t, HEAD_PAD), 1)

        @pl.when(ki == 0)
        def _():
            m_s[...] = jnp.full(m_s.shape, NEG, F32)
            acc_s[...] = jnp.zeros_like(acc_s)

        def update(diagonal):
            s = lax.dot_general(q_ref[...], k_ref[...], _NT, preferred_element_type=F32)
            if diagonal:
                s = _causal(s, t)
            m_prev = m_s[...]
            m_next = jnp.maximum(m_prev, jnp.max(s, axis=1, keepdims=True))
            p = jnp.exp(s - jnp.tile(m_next, (1, rep)))
            vb = jnp.where(lane == V_DIM, jnp.ones((), BF16), v_ref[...])
            acc_s[...] = jnp.exp(m_prev - m_next) * acc_s[...] + jnp.dot(p.astype(BF16), vb,
                                                                         preferred_element_type=F32)
            m_s[...] = m_next

        @pl.when(ki < qi)
        def _():
            update(False)

        @pl.when(ki == qi)
        def _():
            update(True)
            acc = acc_s[...]
            l = jnp.sum(jnp.where(lane == V_DIM, acc, 0.0), axis=1, keepdims=True)
            o_ref[...] = jnp.where(lane == V_DIM, 0.0, acc * (1.0 / l)).astype(o_ref.dtype)
            lse_ref[0] = jnp.max(m_s[...], axis=1, keepdims=True) + jnp.log(l)

    q_map = lambda h, s, qt, kt: (qt[s], h)
    kv_map = lambda h, s, qt, kt: (kt[s], h)
    v_map = lambda h, s, qt, kt: (kt[s], h + heads)
    return pl.pallas_call(
        body, name="mla_flash_fwd",
        out_shape=[jax.ShapeDtypeStruct((L, heads * HEAD_PAD), BF16),
                   jax.ShapeDtypeStruct((heads, L, 1), F32)],
        grid_spec=pltpu.PrefetchScalarGridSpec(
            num_scalar_prefetch=2, grid=(heads, qtab.shape[0]),
            in_specs=[pl.BlockSpec((t, HEAD_PAD), q_map),
                      pl.BlockSpec((t, HEAD_PAD), kv_map),
                      pl.BlockSpec((t, HEAD_PAD), v_map)],
            out_specs=[pl.BlockSpec((t, HEAD_PAD), q_map),
                       pl.BlockSpec((1, t, 1), lambda h, s, qt, kt: (h, qt[s], 0))],
            scratch_shapes=[pltpu.VMEM((t, LANES), F32), pltpu.VMEM((t, HEAD_PAD), F32)]),
        compiler_params=_cparams(("parallel", "arbitrary")),
    )(qtab, ktab, q, k, knv)


def _flash_delta(do, o, *, heads, t):
    L = do.shape[0]

    def body(do_ref, o_ref, d_ref):
        d_ref[0] = jnp.sum(do_ref[...].astype(F32) * o_ref[...].astype(F32), axis=1, keepdims=True)

    return pl.pallas_call(
        body, name="mla_flash_delta",
        out_shape=jax.ShapeDtypeStruct((heads, L, 1), F32),
        grid=(heads, L // t),
        in_specs=[pl.BlockSpec((t, HEAD_PAD), lambda h, i: (i, h)),
                  pl.BlockSpec((t, HEAD_PAD), lambda h, i: (i, h))],
        out_specs=pl.BlockSpec((1, t, 1), lambda h, i: (h, i, 0)),
        compiler_params=_cparams(("parallel", "parallel")),
    )(do, o)


def _flash_bwd(q, k, knv, do, lse, delta, *, heads, t):
    L = q.shape[0]
    n = L // t
    qtab, ktab = _pair_tables(n, k_major=True)

    def body(qt_ref, kt_ref, q_ref, k_ref, v_ref, do_ref, lse_ref, dl_ref, dq_ref, dk_ref, dv_ref, dk_s, dv_s):
        step = pl.program_id(1)
        qi, ki = qt_ref[step], kt_ref[step]

        @pl.when(qi == ki)
        def _():
            dk_s[...] = jnp.zeros_like(dk_s)
            dv_s[...] = jnp.zeros_like(dv_s)

        def update(diagonal):
            qb, kb, vb, dob = q_ref[...], k_ref[...], v_ref[...], do_ref[...]
            st = lax.dot_general(kb, qb, _NT, preferred_element_type=F32)
            if diagonal:
                st = _causal(st, t, k_major=True)
            pt = jnp.exp(st - lse_ref[0])
            dv_s[...] += jnp.dot(pt.astype(BF16), dob, preferred_element_type=F32)
            dpt = lax.dot_general(vb, dob, _NT, preferred_element_type=F32)
            dst = (pt * (dpt - dl_ref[0])).astype(BF16)
            dk_s[...] += jnp.dot(dst, qb, preferred_element_type=F32)
            dqb = lax.dot_general(dst, kb, _TN, preferred_element_type=F32)
            rows = pl.ds(pl.multiple_of(qi * t, t), t)

            @pl.when(ki == 0)
            def _():
                dq_ref[rows, :] = dqb

            @pl.when(ki > 0)
            def _():
                dq_ref[rows, :] += dqb

        @pl.when(qi > ki)
        def _():
            update(False)

        @pl.when(qi == ki)
        def _():
            update(True)

        @pl.when(qi == n - 1)
        def _():
            dk_ref[...] = dk_s[...].astype(dk_ref.dtype)
            dv_ref[...] = dv_s[...].astype(dv_ref.dtype)

    q_map = lambda h, s, qt, kt: (qt[s], h)
    stat_map = lambda h, s, qt, kt: (h, 0, qt[s])
    kv_map = lambda h, s, qt, kt: (kt[s], h)
    v_map = lambda h, s, qt, kt: (kt[s], h + heads)
    return pl.pallas_call(
        body, name="mla_flash_bwd",
        out_shape=[jax.ShapeDtypeStruct((L, heads * HEAD_PAD), F32),
                   jax.ShapeDtypeStruct((L, heads * HEAD_PAD), BF16),
                   jax.ShapeDtypeStruct((L, heads * HEAD_PAD), BF16)],
        grid_spec=pltpu.PrefetchScalarGridSpec(
            num_scalar_prefetch=2, grid=(heads, qtab.shape[0]),
            in_specs=[pl.BlockSpec((t, HEAD_PAD), q_map),
                      pl.BlockSpec((t, HEAD_PAD), kv_map),
                      pl.BlockSpec((t, HEAD_PAD), v_map),
                      pl.BlockSpec((t, HEAD_PAD), q_map),
                      pl.BlockSpec((1, 1, t), stat_map),
                      pl.BlockSpec((1, 1, t), stat_map)],
            out_specs=[pl.BlockSpec((L, HEAD_PAD), lambda h, s, qt, kt: (0, h)),
                       pl.BlockSpec((t, HEAD_PAD), kv_map),
                       pl.BlockSpec((t, HEAD_PAD), kv_map)],
            scratch_shapes=[pltpu.VMEM((t, HEAD_PAD), F32), pltpu.VMEM((t, HEAD_PAD), F32)]),
        compiler_params=_cparams(("parallel", "arbitrary")),
    )(qtab, ktab, q, k, knv, do, lse, delta)


def _peer(k):
    mx, my, mc = lax.axis_index("x"), lax.axis_index("y"), lax.axis_index("c")
    px = 1 - mx if (k >> 2) & 1 else mx
    py = 1 - my if (k >> 1) & 1 else my
    pc = 1 - mc if k & 1 else mc
    return (px, py, pc), 4 * px + 2 * py + pc


def _exchange(name, xs, all_to_all):
    n = len(xs)
    any_spec = pl.BlockSpec(memory_space=pl.ANY)
    npeer = N_DEV - 1

    def body(*refs):
        x_refs, o_refs = refs[:n], refs[n:2 * n]
        send_sems, recv_sems, local_sems = refs[2 * n:]
        _, me = _peer(0)
        mine = [x.at[me] if all_to_all else x for x in x_refs]
        local = [pltpu.make_async_copy(mine[i], o_refs[i].at[me], local_sems.at[i]) for i in range(n)]
        for cp in local:
            cp.start()
        sends = []
        for k in range(1, N_DEV):
            dev, idx = _peer(k)
            for i in range(n):
                cp = pltpu.make_async_remote_copy(
                    src_ref=x_refs[i].at[idx] if all_to_all else x_refs[i], dst_ref=o_refs[i].at[me],
                    send_sem=send_sems.at[i * npeer + k - 1], recv_sem=recv_sems.at[i * npeer + k - 1],
                    device_id=dev, device_id_type=pl.DeviceIdType.MESH)
                cp.start()
                sends.append(cp)
        for k in range(1, N_DEV):
            dev, idx = _peer(k)
            for i in range(n):
                pltpu.make_async_remote_copy(
                    src_ref=mine[i], dst_ref=o_refs[i].at[idx],
                    send_sem=send_sems.at[i * npeer + k - 1], recv_sem=recv_sems.at[i * npeer + k - 1],
                    device_id=dev, device_id_type=pl.DeviceIdType.MESH).wait_recv()
        for cp in sends:
            cp.wait_send()
        for cp in local:
            cp.wait()

    return pl.pallas_call(
        body, name=name,
        out_shape=[jax.ShapeDtypeStruct((N_DEV,) + tuple(x.shape[1:] if all_to_all else x.shape), x.dtype)
                   for x in xs],
        in_specs=[any_spec] * n, out_specs=[any_spec] * n,
        scratch_shapes=[pltpu.SemaphoreType.DMA((n * npeer,)), pltpu.SemaphoreType.DMA((n * npeer,)),
                        pltpu.SemaphoreType.DMA((n,))],
    )(*xs)


def _gather_two_level(name, xs):
    n = len(xs)
    any_spec = pl.BlockSpec(memory_space=pl.ANY)
    nslot = N_DEV - 1
    chips = (2, 4, 6)

    def body(*refs):
        x_refs, o_refs = refs[:n], refs[n:2 * n]
        send_sems, recv_sems, local_sems = refs[2 * n:]
        _, me = _peer(0)
        sibling, sib_idx = _peer(1)

        def copy(i, slot, src, block, dev):
            return pltpu.make_async_remote_copy(
                src_ref=src, dst_ref=o_refs[i].at[block],
                send_sem=send_sems.at[i * nslot + slot], recv_sem=recv_sems.at[i * nslot + slot],
                device_id=dev, device_id_type=pl.DeviceIdType.MESH)

        local = [pltpu.make_async_copy(x_refs[i], o_refs[i].at[me], local_sems.at[i]) for i in range(n)]
        for cp in local:
            cp.start()
        sends = []
        for i in range(n):
            sends.append(copy(i, 0, x_refs[i], me, sibling))
            for j, k in enumerate(chips):
                sends.append(copy(i, 1 + j, x_refs[i], me, _peer(k)[0]))
        for cp in sends:
            cp.start()
        for j, k in enumerate(chips):
            _, idx = _peer(k)
            for i in range(n):
                copy(i, 1 + j, x_refs[i], idx, _peer(k)[0]).wait_recv()
                fwd = copy(i, 4 + j, o_refs[i].at[idx], idx, sibling)
                fwd.start()
                sends.append(fwd)
        for i in range(n):
            copy(i, 0, x_refs[i], sib_idx, sibling).wait_recv()
        for j, k in enumerate(chips):
            _, idx = _peer(k | 1)
            for i in range(n):
                copy(i, 4 + j, x_refs[i], idx, sibling).wait_recv()
        for cp in sends:
            cp.wait_send()
        for cp in local:
            cp.wait()

    return pl.pallas_call(
        body, name=name,
        out_shape=[jax.ShapeDtypeStruct((N_DEV,) + tuple(x.shape), x.dtype) for x in xs],
        in_specs=[any_spec] * n, out_specs=[any_spec] * n,
        scratch_shapes=[pltpu.SemaphoreType.DMA((n * nslot,)), pltpu.SemaphoreType.DMA((n * nslot,)),
                        pltpu.SemaphoreType.DMA((n,))],
    )(*xs)


def _sum_slabs(name, x, *, tr=128):
    _, r, w = x.shape
    tr = _pick(r, tr, 8)

    def body(x_ref, o_ref):
        acc = x_ref[0]
        for j in range(1, N_DEV):
            acc = acc + x_ref[j]
        o_ref[...] = acc

    return pl.pallas_call(
        body, name=name, out_shape=jax.ShapeDtypeStruct((r, w), F32), grid=(r // tr,),
        in_specs=[pl.BlockSpec((N_DEV, tr, w), lambda i: (0, i, 0))],
        out_specs=pl.BlockSpec((tr, w), lambda i: (i, 0)),
        compiler_params=_cparams(("parallel",)),
    )(x)


def _adamw_fn(w, g, m, v):
    m = ADAM_B1 * m + (1.0 - ADAM_B1) * g
    v = ADAM_B2 * v + (1.0 - ADAM_B2) * jnp.square(g)
    m_hat = m / (1.0 - ADAM_B1 ** ADAM_STEP)
    v_hat = v / (1.0 - ADAM_B2 ** ADAM_STEP)
    delta = -ADAM_LR * (m_hat / (jnp.sqrt(v_hat) + ADAM_EPS) + ADAM_WD * w)
    return delta, m, v


def _adamw_sharded(name, w, recv, m, v, *, tr=128):
    rows, c = w.shape
    tr = _pick(rows, tr, 8)

    def body(w_ref, r_ref, m_ref, v_ref, g_ref, d_ref, mo_ref, vo_ref):
        g = r_ref[0].astype(F32)
        for j in range(1, N_DEV):
            g = g + r_ref[j].astype(F32)
        d, mn, vn = _adamw_fn(w_ref[...], g, m_ref[...], v_ref[...])
        g_ref[...] = g
        d_ref[...] = d
        mo_ref[...] = mn
        vo_ref[...] = vn

    blk = pl.BlockSpec((tr, c), lambda i: (i, 0))
    return pl.pallas_call(
        body, name=name, out_shape=[jax.ShapeDtypeStruct((rows, c), F32)] * 4, grid=(rows // tr,),
        in_specs=[blk, pl.BlockSpec((N_DEV, tr, c), lambda i: (0, i, 0)), blk, blk],
        out_specs=[blk] * 4,
        compiler_params=_cparams(("parallel",)),
    )(w, recv, m, v)


def _piece_rows(shape):
    return -(-math.prod(shape) // (8 * FLAT_W)) * 8


def _pack_rows(arrs):
    out = []
    for a in arrs:
        flat = a.reshape(-1)
        rows = _piece_rows(a.shape)
        out.append(jnp.pad(flat, (0, rows * FLAT_W - flat.shape[0])).reshape(rows, FLAT_W))
    return jnp.concatenate(out, axis=0)


def _unpack_rows(packed, shapes):
    out, r0 = [], 0
    for s in shapes:
        rows = _piece_rows(s)
        out.append(packed[r0:r0 + rows].reshape(-1)[:math.prod(s)].reshape(s))
        r0 += rows
    return out


def _pack_slabs(a):
    n = a.shape[1]
    rows = _piece_rows((n,))
    return jnp.pad(a, ((0, 0), (0, rows * FLAT_W - n))).reshape(N_DEV, rows, FLAT_W)


def _unpack_slabs(a, n):
    return a.reshape(N_DEV, -1)[:, :n]


def _s5_operators(a_re, a_im, log_dt, b_re, b_im, c_re, c_im):
    g, p, m = b_re.shape
    dt = jnp.exp(log_dt)[:, None]
    mag = jnp.exp(a_re * dt)
    abar_re = mag * jnp.cos(a_im * dt)
    abar_im = mag * jnp.sin(a_im * dt)
    den = a_re * a_re + a_im * a_im
    nr = abar_re - 1.0
    ni = abar_im
    coef_re = ((nr * a_re + ni * a_im) / den)[..., None]
    coef_im = ((ni * a_re - nr * a_im) / den)[..., None]
    bbar_re = coef_re * b_re - coef_im * b_im
    bbar_im = coef_re * b_im + coef_im * b_re
    sup, ns = S5_SUPER, g // S5_SUPER
    eye = jnp.eye(sup, dtype=F32)

    def b_blocks(bb):
        return jnp.einsum('cgpm,gh->cgmhp', bb.reshape(ns, sup, p, m), eye).reshape(ns, sup * m, sup * p)

    def c_blocks(cb):
        return jnp.einsum('cgmp,gh->chpgm', cb.reshape(ns, sup, m, p), eye).reshape(ns, sup * p, sup * m)

    bmat = jnp.concatenate([b_blocks(bbar_re), b_blocks(bbar_im)], axis=2).reshape(g * m, 2 * sup * p)
    cmat = jnp.concatenate([c_blocks(c_re), c_blocks(-c_im)], axis=1).reshape(ns * 2 * sup * p, sup * m)
    return abar_re.reshape(1, g * p), abar_im.reshape(1, g * p), bmat, cmat


def _rot_cols(w):
    half = w.shape[-1] // 2
    return jnp.concatenate([-w[..., half:], w[..., :half]], axis=-1)


def _layer_operators(w, dims):
    d, sw, ql, kvl, heads, dff = dims['d'], dims['sw'], dims['ql'], dims['kvl'], dims['heads'], dims['dff']
    w_in = w['w_in']
    o = 0
    parts = {}
    for nm, sz in (('u', sw), ('cq', ql), ('ckv', kvl), ('kr', QK_ROPE), ('ga', d), ('gb', d)):
        parts[nm] = w_in[:, o:o + sz]
        o += sz
    zpad = lambda n: jnp.zeros((d, n), w_in.dtype)
    kra = jnp.concatenate([zpad(QK_NOPE), parts['kr'], zpad(HEAD_PAD - QK_NOPE - QK_ROPE)], axis=1)
    krb = jnp.concatenate([zpad(QK_NOPE), _rot_cols(parts['kr']), zpad(HEAD_PAD - QK_NOPE - QK_ROPE)], axis=1)
    w_in_x = jnp.concatenate([parts['ga'], parts['gb'], parts['u'], parts['cq'], parts['ckv'], kra, krb], axis=1)

    wq = w['w_uq'].reshape(ql, heads, QK_NOPE + QK_ROPE)
    qz = lambda n: jnp.zeros((ql, heads, n), wq.dtype)
    wq_a = jnp.concatenate([wq, qz(HEAD_PAD - QK_NOPE - QK_ROPE)], axis=2)
    wq_b = jnp.concatenate([qz(QK_NOPE), _rot_cols(wq[:, :, QK_NOPE:]), qz(HEAD_PAD - QK_NOPE - QK_ROPE)], axis=2)
    wq_x = jnp.concatenate([wq_a.reshape(ql, -1), wq_b.reshape(ql, -1)], axis=1)

    kz = lambda n: jnp.zeros((kvl, heads, n), w['w_uk'].dtype)
    wk = jnp.concatenate([w['w_uk'].reshape(kvl, heads, QK_NOPE), kz(HEAD_PAD - QK_NOPE)], axis=2)
    wv = jnp.concatenate([w['w_uv'].reshape(kvl, heads, V_DIM), kz(HEAD_PAD - V_DIM)], axis=2)
    wkv_x = jnp.concatenate([wk.reshape(kvl, -1), wv.reshape(kvl, -1)], axis=1)

    wbo = w['w_b_out'].reshape(heads, V_DIM, d)
    wbo_x = jnp.concatenate([wbo, jnp.zeros((heads, HEAD_PAD - V_DIM, d), wbo.dtype)], axis=1).reshape(-1, d)
    wgu = jnp.concatenate([w['w_gate'], w['w_up']], axis=1)
    return dict(w_in=w_in_x, w_glu=w['w_glu'], w_a_out=w['w_a_out'], wq=wq_x, wkv=wkv_x, wbo=wbo_x,
                w_out=w['w_out'], wgu=wgu, w_down=w['w_down'])


def _gathered_to_full(gathered):
    full = {}
    for n, pc in zip(SHARDED, gathered):
        dep, r, c = pc.shape[1:]
        if n in COL_SHARDED:
            full[n] = pc.transpose(1, 2, 0, 3).reshape(dep, r, N_DEV * c)
        else:
            full[n] = pc.transpose(1, 0, 2, 3).reshape(dep, N_DEV * r, c)
    return full


def _layer_fwd(x, mod, ops, s5, small, rope, dims):
    d, sw, ql, kvl, heads, dff = dims['d'], dims['sw'], dims['ql'], dims['kvl'], dims['heads'], dims['dff']
    zo = dims['zoff']
    hw = heads * HEAD_PAD
    cos, sin = rope
    sh1, sc1, g1, sh2, sc2, g2 = mod
    bf = lambda a: a.astype(BF16)
    sv = dict(x=x)
    (h1,) = _rowwise("norm1_fwd", _f_norm_mod, [x], [small['norm1_g'], sc1, sh1], [(d, BF16)])
    z = _mm("w_in_fwd", h1, bf(ops['w_in']), out_dtype=BF16)
    sv.update(h1=h1, z=z)
    a_re, a_im, bmat, cmat = s5
    u_seg = _to_segments(z[:, zo['u']:zo['u'] + sw])
    nsup, sc = dims['nsup'], dims['s5_chunk']
    s5kw = dict(nsup=nsup, sc=sc)
    ends = _s5_pass("s5_ends_fwd", u_seg, a_re, a_im, bf(bmat), reverse=False, **s5kw)
    hst, ych = _s5_pass("s5_scan_fwd", u_seg, a_re, a_im, bf(bmat), reverse=False, ends=ends, w_out=bf(cmat), **s5kw)
    ych = _from_segments(ych)
    (yg,) = _rowwise("s5_gelu_fwd", _f_gelu_in, [ych, (z, zo['u'], sw)], [small['ssm_d']], [(sw, F32)])
    pre = _mm("s5_glu_mm_fwd", yg, bf(ops['w_glu']))
    (s5o,) = _rowwise("s5_glu_fwd", _f_glu, [yg, pre], [small['b_glu']], [(sw, BF16)])
    ya = _mm("s5_out_fwd", s5o, bf(ops['w_a_out']), out_dtype=BF16)
    sv.update(u_seg=u_seg, hst=hst, ych=ych, yg=yg, pre=pre, s5o=s5o, ya=ya)
    (cq,) = _rowwise("q_norm_fwd", _f_rms, [(z, zo['cq'], ql)], [small['q_norm_g']], [(ql, BF16)])
    qab = _mm("q_up_fwd", cq, bf(ops['wq']))
    scale = dims['scale']
    (q,) = _rowwise("q_rope_fwd", lambda a, b, cb, sb: _f_rope_q(a, b, cb, sb) * scale,
                    [(qab, 0, hw), (qab, hw, hw), cos, sin], [], [(hw, BF16)])
    (ckv,) = _rowwise("kv_norm_fwd", _f_rms, [(z, zo['ckv'], kvl)], [small['kv_norm_g']], [(kvl, BF16)])
    knv = _mm("kv_up_fwd", ckv, bf(ops['wkv']), out_dtype=BF16)
    (k,) = _rowwise("k_rope_fwd", _f_rope_k,
                    [(knv, 0, hw), (z, zo['kra'], HEAD_PAD), (z, zo['krb'], HEAD_PAD), cos, sin], [], [(hw, BF16)])
    o, lse = _flash_fwd(q, k, knv, heads=heads, t=dims['tq'])
    yb = _mm("mla_out_fwd", o, bf(ops['wbo']), out_dtype=BF16)
    sv.update(cq=cq, q=q, ckv=ckv, knv=knv, k=k, o=o, lse=lse, yb=yb)
    mix = [(z, zo['ga'], d), (z, zo['gb'], d), (ya, 0, d), (yb, 0, d)]
    t1 = _mm("w_out_fwd", mix, bf(ops['w_out']), a_fn=_f_merge)
    (x1,) = _rowwise("res1_fwd", _f_res, [x, t1], [g1], [(d, F32)])
    sv.update(t1=t1, x1=x1)
    (h2,) = _rowwise("norm2_fwd", _f_norm_mod, [x1], [small['norm2_g'], sc2, sh2], [(d, BF16)])
    ab = _mm("ffn_up_fwd", h2, bf(ops['wgu']), out_dtype=BF16)
    ffn_act = [(ab, 0, dff), (ab, dff, dff)]
    t2 = _mm("ffn_down_fwd", ffn_act, bf(ops['w_down']), a_fn=_f_swiglu)
    (x2,) = _rowwise("res2_fwd", _f_res, [x1, t2], [g2], [(d, F32)])
    sv.update(h2=h2, ab=ab, t2=t2)
    return x2, sv


def _layer_bwd(dx2, sv, mod, ops, s5, small, rope, dims):
    d, sw, ql, kvl, heads, dff = dims['d'], dims['sw'], dims['ql'], dims['kvl'], dims['heads'], dims['dff']
    zo = dims['zoff']
    hw = heads * HEAD_PAD
    cos, sin = rope
    sh1, sc1, g1, sh2, sc2, g2 = mod
    a_re, a_im, bmat, cmat = s5
    z = sv['z']
    tr = lambda a: a.T.astype(BF16)
    gops, gsm = {}, {}
    dt2, dg2 = _res_bwd("res2_bwd", sv['t2'], g2, dx2)
    gops['w_down'] = _mm("ffn_down_dw", [(sv['ab'], 0, dff), (sv['ab'], dff, dff)], dt2, ta=True, a_fn=_f_swiglu,
                         tn=512)
    df = _mm("ffn_down_dx", dt2, tr(ops['w_down']), out_dtype=BF16)
    ab = sv['ab']
    (dab_a, dab_b) = _rowwise("swiglu_bwd", _vjp_fn(_f_swiglu, 2, 1, (0, 1)),
                              [(ab, 0, dff), (ab, dff, dff), df], [], [(dff, BF16), (dff, BF16)], tm=256)
    dab = jnp.concatenate([dab_a, dab_b], axis=1)
    gops['wgu'] = _mm("ffn_up_dw", sv['h2'], dab, ta=True)
    dh2 = _mm("ffn_up_dx", dab, tr(ops['wgu']), out_dtype=BF16)
    dx1, dn2, dsc2, dsh2 = _norm_mod_bwd("norm2_bwd", sv['x1'], small['norm2_g'], sc2, sh2, dh2, dx2)
    gsm['norm2_g'] = dn2
    dt1, dg1 = _res_bwd("res1_bwd", sv['t1'], g1, dx1)
    gops['w_out'] = _mm("w_out_dw", [(z, zo['ga'], d), (z, zo['gb'], d), (sv['ya'], 0, d), (sv['yb'], 0, d)], dt1,
                        ta=True, a_fn=_f_merge)
    dmerged = _mm("w_out_dx", dt1, tr(ops['w_out']), out_dtype=BF16)
    dga, dgb, dya, dyb = _rowwise(
        "merge_bwd", _vjp_fn(_f_merge, 4, 1, (0, 1, 2, 3)),
        [(z, zo['ga'], d), (z, zo['gb'], d), sv['ya'], sv['yb'], dmerged], [],
        [(d, BF16), (d, BF16), (d, BF16), (d, BF16)])
    gops['wbo'] = _mm("mla_out_dw", sv['o'], dyb, ta=True)
    do = _mm("mla_out_dx", dyb, tr(ops['wbo']), out_dtype=BF16)
    delta = _flash_delta(do, sv['o'], heads=heads, t=dims['tq'])
    as_rows = lambda a: a.reshape(heads, 1, -1)
    dq, dk, dv = _flash_bwd(sv['q'], sv['k'], sv['knv'], do, as_rows(sv['lse']), as_rows(delta),
                            heads=heads, t=dims['tq'])
    def k_bwd(dkb, cosb, sinb):
        dkb = _f32(dkb)
        dkpe = dkb[:, 0:HEAD_PAD]
        for h in range(1, heads):
            dkpe = dkpe + dkb[:, h * HEAD_PAD:(h + 1) * HEAD_PAD]
        return dkpe * cosb, dkpe * sinb
    dkra, dkrb = _rowwise("k_rope_bwd", k_bwd, [dk, cos, sin], [], [(HEAD_PAD, BF16), (HEAD_PAD, BF16)])
    dknv = jnp.concatenate([dk, dv], axis=1)
    gops['wkv'] = _mm("kv_up_dw", sv['ckv'], dknv, ta=True)
    dckv = _mm("kv_up_dx", dknv, tr(ops['wkv']))
    dckv_in, dkvg = _rms_bwd("kv_norm_bwd", z, zo['ckv'], kvl, small['kv_norm_g'], dckv)
    gsm['kv_norm_g'] = dkvg
    def q_bwd(dqb, cosb, sinb):
        dqb = _f32(dqb) * dims['scale']
        return dqb * jnp.tile(cosb, (1, heads)), dqb * jnp.tile(sinb, (1, heads))
    dqa, dqb_ = _rowwise("q_rope_bwd", q_bwd, [dq, cos, sin], [], [(hw, BF16), (hw, BF16)])
    dqab = jnp.concatenate([dqa, dqb_], axis=1)
    gops['wq'] = _mm("q_up_dw", sv['cq'], dqab, ta=True)
    dcq = _mm("q_up_dx", dqab, tr(ops['wq']))
    dcq_in, dqg = _rms_bwd("q_norm_bwd", z, zo['cq'], ql, small['q_norm_g'], dcq)
    gsm['q_norm_g'] = dqg
    gops['w_a_out'] = _mm("s5_out_dw", sv['s5o'], dya, ta=True)
    ds5o = _mm("s5_out_dx", dya, tr(ops['w_a_out']))

    def glu_bwd(yg, pre, ds, b):
        _, vjp = jax.vjp(_f_glu, _f32(yg), _f32(pre), b)
        dyg, dpre, db = vjp(_f32(ds))
        return dyg, dpre, db
    dyg_a, dpre, dbglu = _rowwise("s5_glu_bwd", glu_bwd, [sv['yg'], sv['pre'], ds5o], [small['b_glu']],
                                  [(sw, F32), (sw, BF16)], [sw])
    gsm['b_glu'] = dbglu
    gops['w_glu'] = _mm("s5_glu_mm_dw", sv['yg'], dpre, ta=True)
    dyg_b = _mm("s5_glu_mm_dx", dpre, tr(ops['w_glu']))

    def gelu_bwd(ych, u, dya_, dyb_, dvec):
        _, vjp = jax.vjp(_f_gelu_in, _f32(ych), _f32(u), dvec)
        dych, du, dd = vjp(_f32(dya_) + _f32(dyb_))
        return dych, du, dd
    dy, du_skip, dssm_d = _rowwise("s5_gelu_bwd", gelu_bwd, [sv['ych'], (z, zo['u'], sw), dyg_a, dyg_b],
                                   [small['ssm_d']], [(sw, BF16), (sw, F32)], [sw])
    gsm['ssm_d'] = dssm_d
    dy_seg = _to_segments(dy)
    nsup, sc = dims['nsup'], dims['s5_chunk']
    tr_blocks = lambda a: a.reshape(nsup, -1, a.shape[1]).transpose(0, 2, 1).reshape(-1, a.shape[0] // nsup)
    s5kw = dict(nsup=nsup, sc=sc)
    c_t, b_t = tr_blocks(cmat).astype(BF16), tr_blocks(bmat).astype(BF16)
    ends = _s5_pass("s5_ends_bwd", dy_seg, a_re, a_im, c_t, reverse=True, **s5kw)
    du_scan, g_bmat, g_cmat, dar, dai = _s5_pass("s5_scan_bwd", dy_seg, a_re, a_im, c_t, reverse=True, ends=ends,
                                                 w_out=b_t, u=sv['u_seg'], h=sv['hst'], **s5kw)
    du_scan = _from_segments(du_scan)
    (du,) = _rowwise("s5_du_sum", lambda a, b: _f32(a) + _f32(b), [du_skip, du_scan], [], [(sw, BF16)])
    gs5 = (dar, dai, g_bmat, g_cmat)
    dz = jnp.concatenate([dga, dgb, du, dcq_in, dckv_in, dkra, dkrb], axis=1)
    gops['w_in'] = _mm("w_in_dw", sv['h1'], dz, ta=True)
    dh1 = _mm("w_in_dx", dz, tr(ops['w_in']), out_dtype=BF16)
    dx, dn1, dsc1, dsh1 = _norm_mod_bwd("norm1_bwd", sv['x'], small['norm1_g'], sc1, sh1, dh1, dx1)
    gsm['norm1_g'] = dn1
    dmod = (dsh1, dsc1, dg1, dsh2, dsc2, dg2)
    return dx, gops, gs5, gsm, dmod


def _res_bwd(name, t, g, dxo):
    def fn(tb, db, gb):
        db = _f32(db)
        return gb * db, jnp.sum(db * _f32(tb), axis=0, keepdims=True)
    return _rowwise(name, fn, [t, dxo], [g], [(t.shape[1], BF16)], [t.shape[1]])


def _norm_mod_bwd(name, x, g, sc, sh, dh, dres):
    def fn(xb, dhb, dresb, gb, scb, shb):
        _, vjp = jax.vjp(_f_norm_mod, _f32(xb), gb, scb, shb)
        dx, dg, dsc, dsh = vjp(_f32(dhb))
        return dx + _f32(dresb), dg, dsc, dsh
    w = x.shape[1]
    return _rowwise(name, fn, [x, dh, dres], [g, sc, sh], [(w, F32)], [w, w, w])


def _rms_bwd(name, z, off, w, g, dy):
    def fn(xb, dyb, gb):
        _, vjp = jax.vjp(_f_rms, _f32(xb), gb)
        dx, dg = vjp(_f32(dyb))
        return dx, dg
    return _rowwise(name, fn, [(z, off, w), dy], [g], [(w, BF16)], [w])


S5_NAMES = ('ssm_a_re', 'ssm_a_im', 'ssm_log_dt', 'ssm_b_re', 'ssm_b_im', 'ssm_c_re', 'ssm_c_im')
LAYER_VECS = ('norm1_g', 'ssm_d', 'b_glu', 'q_norm_g', 'kv_norm_g', 'norm2_g')


def _step(p, mom_m, mom_v, x, c, positions, loss_target):
    depth, d = p['norm1_g'].shape
    L = x.shape[1]
    sw = p['ssm_d'].shape[1]
    ql, kvl = p['q_norm_g'].shape[1], p['kv_norm_g'].shape[1]
    heads = p['w_uk'].shape[2] * N_DEV // QK_NOPE
    dff = p['w_gate'].shape[2] * N_DEV
    ada_w = p['w_ada'].shape[2]
    zoff, o = {}, 0
    for nm, sz in (('ga', d), ('gb', d), ('u', sw), ('cq', ql), ('ckv', kvl), ('kra', HEAD_PAD), ('krb', HEAD_PAD)):
        assert o % sz == 0, (nm, o, sz)
        zoff[nm] = o
        o += sz
    groups, states = p['ssm_a_re'].shape[1:]
    assert groups % S5_SUPER == 0 and p['ssm_b_re'].shape[3] * S5_SUPER == LANES
    dims = dict(d=d, sw=sw, ql=ql, kvl=kvl, heads=heads, dff=dff, zoff=zoff,
                nsup=groups // S5_SUPER, s5_chunk=S5_SUPER * states,
                tq=1024 if L >= 4096 else 128, scale=(QK_NOPE + QK_ROPE) ** -0.5)
    x = x.reshape(L, d)
    tgt = loss_target.reshape(L, d)

    posf = positions.reshape(L).astype(F32)
    inv_freq = ROPE_BASE ** (-jnp.arange(0, QK_ROPE, 2, dtype=F32) / QK_ROPE)
    ang = posf[:, None] * inv_freq
    cs, sn = jnp.cos(ang), jnp.sin(ang)
    padr = HEAD_PAD - QK_NOPE - QK_ROPE
    cos = jnp.concatenate([jnp.ones((L, QK_NOPE), F32), cs, cs, jnp.zeros((L, padr), F32)], axis=1)
    sin = jnp.concatenate([jnp.zeros((L, QK_NOPE), F32), sn, sn, jnp.zeros((L, padr), F32)], axis=1)
    rope = (cos, sin)

    gathered = _gather_two_level("gather_weights", [p[n].astype(BF16) for n in SHARDED])

    def make_ops(gl):
        return jax.vmap(lambda w: _layer_operators(w, dims))(_gathered_to_full(gl))

    ops_all, ops_vjp = jax.vjp(make_ops, [g.astype(F32) for g in gathered])
    ops = [{k: v[l] for k, v in ops_all.items()} for l in range(depth)]

    def make_s5(sp):
        return jax.vmap(_s5_operators)(*[sp[n] for n in S5_NAMES])

    s5_all, s5_vjp = jax.vjp(make_s5, {n: p[n] for n in S5_NAMES})
    s5ops = [tuple(a[l] for a in s5_all) for l in range(depth)]

    (c_slabs,) = _exchange("gather_c", [jnp.pad(c, ((0, 7), (0, 0)))], all_to_all=False)
    c_all = c_slabs[:, 0, :]
    (c_act,) = _rowwise("c_silu", lambda a: jax.nn.silu(a), [jnp.pad(c_all, ((0, 8), (0, 0)))], [], [(d, F32)])
    w_ada_cat = p['w_ada'].transpose(1, 0, 2).reshape(d, depth * ada_w)
    mod_cols = _mm("ada_fwd", c_act, w_ada_cat)[:N_DEV]
    (mod_rows,) = _exchange("a2a_mod", [_pack_slabs(mod_cols)], all_to_all=True)
    mod_mine = _unpack_slabs(mod_rows, depth * ada_w).reshape(N_DEV, depth, ada_w)
    mod_mine = mod_mine.transpose(1, 0, 2).reshape(depth, N_DEV * ada_w)
    (mod_full,) = _rowwise("ada_bias", lambda a, b: a + b, [mod_mine, p['b_ada']], [], [(6 * d, F32)])
    mods = [tuple(mod_full[l:l + 1, i * d:(i + 1) * d] for i in range(6)) for l in range(depth)]

    saved = []
    xl = x
    for l in range(depth):
        small = {n: p[n][l:l + 1] for n in LAYER_VECS}
        xl, sv = _layer_fwd(xl, mods[l], ops[l], s5ops[l], small, rope, dims)
        saved.append((sv, small))

    def final_fn(xb, tb, gb):
        def lossf(xv, gv):
            e = _f_rms(xv, gv) - tb
            per_row = 0.5 * jnp.mean(e * e, axis=-1, keepdims=True)
            return jnp.sum(per_row, axis=0, keepdims=True)
        lv, vjp = jax.vjp(lossf, xb, gb)
        dxb, dgb = vjp(jnp.ones((1, 1), F32))
        return dxb, jnp.broadcast_to(lv, (1, LANES)), dgb
    dx, loss_vec, dfinal_g = _rowwise("final_loss", final_fn, [xl, tgt], [p['final_g'].reshape(1, d)],
                                      [(d, F32)], [LANES, d])

    g_ops, g_s5, g_small, dmods = [None] * depth, [None] * depth, [None] * depth, [None] * depth
    for l in reversed(range(depth)):
        sv, small = saved[l]
        dx, g_ops[l], g_s5[l], g_small[l], dmods[l] = _layer_bwd(dx, sv, mods[l], ops[l], s5ops[l], small,
                                                                rope, dims)
    grad_x = dx.reshape(1, L, d)

    dmod_mine = jnp.stack([jnp.concatenate(dm, axis=1)[0] for dm in dmods])
    dmod_slabs = dmod_mine.reshape(depth, N_DEV, ada_w).transpose(1, 0, 2).reshape(N_DEV, depth * ada_w)
    (dmod_recv,) = _exchange("a2a_dmod", [_pack_slabs(dmod_slabs)], all_to_all=True)
    dmod_cols = _unpack_slabs(dmod_recv, depth * ada_w)
    g_ada = _mm("ada_dw", c_act, jnp.pad(dmod_cols, ((0, 8), (0, 0))), ta=True)
    grads, delta, new_m, new_v = {}, {}, {}, {}
    grads['w_ada'] = g_ada.reshape(d, depth, ada_w).transpose(1, 0, 2)
    two_d = lambda a: a.reshape(-1, a.shape[-1])
    res = _rowwise("adamw_w_ada", _adamw_fn, [two_d(a) for a in (p['w_ada'], grads['w_ada'], mom_m['w_ada'],
                                                                  mom_v['w_ada'])], [], [(ada_w, F32)] * 3)
    delta['w_ada'], new_m['w_ada'], new_v['w_ada'] = [r.reshape(p['w_ada'].shape) for r in res]

    (g_slabs,) = ops_vjp({k: jnp.stack([g[k] for g in g_ops]) for k in ops_all})
    g_recv = _exchange("a2a_grads", [g.astype(BF16) for g in g_slabs], all_to_all=True)
    for n, rv in zip(SHARDED, g_recv):
        shp = p[n].shape
        res = _adamw_sharded("adamw_" + n, two_d(p[n]), rv.reshape(N_DEV, -1, shp[-1]), two_d(mom_m[n]),
                             two_d(mom_v[n]))
        grads[n], delta[n], new_m[n], new_v[n] = [r.reshape(shp) for r in res]

    (g_s5p,) = s5_vjp(tuple(jnp.stack([g[i] for g in g_s5]) for i in range(len(s5_all))))
    part = dict(g_s5p)
    part['b_ada'] = dmod_mine
    for n in LAYER_VECS:
        part[n] = jnp.concatenate([g_small[l][n] for l in range(depth)], axis=0)
    part['final_g'] = dfinal_g.reshape(d)
    small_shapes = [p[n].shape for n in SMALL] + [(1,)]
    (small_recv,) = _exchange("gather_small", [_pack_rows([part[n] for n in SMALL] + [loss_vec[0, 0:1]])],
                              all_to_all=False)
    small_sum = _sum_slabs("sum_small", small_recv)
    small_list = _unpack_rows(small_sum, small_shapes)
    grads.update(zip(SMALL, small_list[:-1]))
    loss = small_list[-1].reshape(())
    dummy = [jnp.zeros((1,), F32)]
    res = _rowwise("adamw_small", _adamw_fn,
                   [_pack_rows([src[n] for n in SMALL] + dummy) for src in (p, )] + [small_sum] +
                   [_pack_rows([src[n] for n in SMALL] + dummy) for src in (mom_m, mom_v)], [], [(FLAT_W, F32)] * 3)
    for dst, r in zip((delta, new_m, new_v), res):
        dst.update(zip(SMALL, _unpack_rows(r, small_shapes)[:-1]))
    return (loss, grad_x, *[grads[n] for n in WEIGHTS], *[delta[n] for n in WEIGHTS],
            *[new_m[n] for n in WEIGHTS], *[new_v[n] for n in WEIGHTS])


def kernel(x, c, positions, w_ada, b_ada, norm1_g, w_in, ssm_a_re, ssm_a_im, ssm_log_dt, ssm_b_re, ssm_b_im, ssm_c_re, ssm_c_im, ssm_d, w_glu, b_glu, w_a_out, q_norm_g, w_uq, kv_norm_g, w_uk, w_uv, w_b_out, w_out, norm2_g, w_gate, w_up, w_down, final_g, loss_target, m_w_ada, m_b_ada, m_norm1_g, m_w_in, m_ssm_a_re, m_ssm_a_im, m_ssm_log_dt, m_ssm_b_re, m_ssm_b_im, m_ssm_c_re, m_ssm_c_im, m_ssm_d, m_w_glu, m_b_glu, m_w_a_out, m_q_norm_g, m_w_uq, m_kv_norm_g, m_w_uk, m_w_uv, m_w_b_out, m_w_out, m_norm2_g, m_w_gate, m_w_up, m_w_down, m_final_g, v_w_ada, v_b_ada, v_norm1_g, v_w_in, v_ssm_a_re, v_ssm_a_im, v_ssm_log_dt, v_ssm_b_re, v_ssm_b_im, v_ssm_c_re, v_ssm_c_im, v_ssm_d, v_w_glu, v_b_glu, v_w_a_out, v_q_norm_g, v_w_uq, v_kv_norm_g, v_w_uk, v_w_uv, v_w_b_out, v_w_out, v_norm2_g, v_w_gate, v_w_up, v_w_down, v_final_g):
    given = dict(locals())
    p = {n: given[n] for n in WEIGHTS}
    mom_m = {n: given["m_" + n] for n in WEIGHTS}
    mom_v = {n: given["v_" + n] for n in WEIGHTS}
    return _step(p, mom_m, mom_v, x, c, positions, loss_target)
```

```python
import functools
import math

import jax
import jax.numpy as jnp
from jax import lax
from jax.experimental import pallas as pl
from jax.experimental.pallas import tpu as pltpu

F32 = jnp.float32
BF16 = jnp.bfloat16

N_DEV = 8
LANES = 128
FLAT_W = 1024
VMEM_LIMIT = 48 * 1024 * 1024
MM_VMEM_BUDGET = 36 * 1024 * 1024
QK_NOPE, QK_ROPE, V_DIM = 64, 32, 64
HEAD_PAD = LANES
ROPE_BASE = 10000.0
EPS = 1e-6
ADAM_LR, ADAM_B1, ADAM_B2, ADAM_EPS, ADAM_WD, ADAM_STEP = 0.001, 0.9, 0.999, 1e-08, 0.01, 10
NEG = float(jnp.finfo(jnp.float32).min)

WEIGHTS = ['w_ada', 'b_ada', 'norm1_g', 'w_in', 'ssm_a_re', 'ssm_a_im', 'ssm_log_dt', 'ssm_b_re', 'ssm_b_im',
           'ssm_c_re', 'ssm_c_im', 'ssm_d', 'w_glu', 'b_glu', 'w_a_out', 'q_norm_g', 'w_uq', 'kv_norm_g', 'w_uk',
           'w_uv', 'w_b_out', 'w_out', 'norm2_g', 'w_gate', 'w_up', 'w_down', 'final_g']
COL_SHARDED = ['w_in', 'w_a_out', 'w_uq', 'w_uk', 'w_uv', 'w_b_out', 'w_gate', 'w_up']
ROW_SHARDED = ['w_glu', 'w_out', 'w_down']
SHARDED = COL_SHARDED + ROW_SHARDED
SMALL = ['b_ada', 'norm1_g', 'ssm_a_re', 'ssm_a_im', 'ssm_log_dt', 'ssm_b_re', 'ssm_b_im', 'ssm_c_re', 'ssm_c_im',
         'ssm_d', 'b_glu', 'q_norm_g', 'kv_norm_g', 'norm2_g', 'final_g']


def _cparams(sem):
    return pltpu.CompilerParams(dimension_semantics=sem, vmem_limit_bytes=VMEM_LIMIT)


def _pick(dim, pref, quantum=LANES):
    if dim <= pref:
        return dim
    t = (pref // quantum) * quantum
    while t >= quantum:
        if dim % t == 0:
            return t
        t -= quantum
    return dim


def _mm(name, a, b, *, ta=False, out_dtype=F32, a_col=None, b_col=None, a_fn=None, tm=1408, tn=1408, tk=None):
    if a_fn is None:
        a_off, a_w = a_col if a_col is not None else (0, a.shape[1])
        a_parts = [(a, a_off, a_w)]
    else:
        a_parts = a
        a_w = a_parts[0][2]
        assert all(w == a_w for _, _, w in a_parts)
    a0 = a_parts[0][0]
    b_off, b_w = b_col if b_col is not None else (0, b.shape[1])
    if ta:
        kdim, m = a0.shape[0], a_w
        assert b.shape[0] == kdim
    else:
        m, kdim = a0.shape[0], a_w
        assert b.shape[0] == kdim, (name, a0.shape, b.shape)
    n = b_w
    tm = _pick(m, tm, LANES if ta else 8)
    tn = _pick(n, tn)
    a_bytes = sum(arr.dtype.itemsize for arr, _, _ in a_parts) + (8 if a_fn is not None else 0)
    if tk is None:
        def fits(tm_, tk_):
            return (tm_ * tn * (2 * jnp.dtype(out_dtype).itemsize + 4)
                    + tk_ * (2 * (tm_ * a_bytes + tn * b.dtype.itemsize) + 2 * (tm_ + tn))) <= MM_VMEM_BUDGET

        tk = kdim
        while not fits(tm, tk):
            if not ta and tm % 16 == 0 and tm >= 1024 and fits(tm // 2, tk):
                tm //= 2
                break
            if tk % (2 * (8 if ta else LANES)):
                break
            tk //= 2
    tk = _pick(kdim, tk, 8 if ta else LANES)
    nk = kdim // tk
    assert m % tm == 0 and n % tn == 0 and kdim % tk == 0, (name, m, n, kdim, tm, tn, tk)
    a_specs = []
    for _, off, _ in a_parts:
        if ta:
            assert off % tm == 0
            a_specs.append(pl.BlockSpec((tk, tm), functools.partial(lambda i, j, k, o: (k, i + o), o=off // tm)))
        else:
            assert off % tk == 0
            a_specs.append(pl.BlockSpec((tm, tk), functools.partial(lambda i, j, k, o: (i, k + o), o=off // tk)))
    dims = _TN if ta else (((1,), (0,)), ((), ()))
    assert b_off % tn == 0
    b_spec = pl.BlockSpec((tk, tn), lambda i, j, k: (k, j + b_off // tn))
    na = len(a_parts)

    def prod(refs):
        if a_fn is None:
            av = refs[0][...]
        else:
            av = a_fn(*[r[...].astype(F32) for r in refs[:na]])
        return lax.dot_general(av.astype(BF16), refs[na][...].astype(BF16), dims, preferred_element_type=F32)

    def body_one(*refs):
        refs[na + 1][...] = prod(refs).astype(refs[na + 1].dtype)

    def body_acc(*refs):
        o_ref, acc_ref = refs[na + 1], refs[na + 2]

        @pl.when(pl.program_id(2) == 0)
        def _():
            acc_ref[...] = jnp.zeros_like(acc_ref)

        acc_ref[...] += prod(refs)

        @pl.when(pl.program_id(2) == nk - 1)
        def _():
            o_ref[...] = acc_ref[...].astype(o_ref.dtype)

    return pl.pallas_call(
        body_one if nk == 1 else body_acc, name=name,
        out_shape=jax.ShapeDtypeStruct((m, n), out_dtype),
        grid=(m // tm, n // tn, nk),
        in_specs=a_specs + [b_spec],
        out_specs=pl.BlockSpec((tm, tn), lambda i, j, k: (i, j)),
        scratch_shapes=[] if nk == 1 else [pltpu.VMEM((tm, tn), F32)],
        compiler_params=_cparams(("parallel", "parallel", "arbitrary")),
    )(*[arr for arr, _, _ in a_parts], b)


def _rowwise(name, fn, rows, vecs, outs, reds=(), *, tm=512):
    rows = [(r, 0, r.shape[1]) if not isinstance(r, tuple) else r for r in rows]
    nrows = rows[0][0].shape[0]
    tm = _pick(nrows, tm, 8)
    assert nrows % tm == 0, (name, nrows, tm)
    nr, nv, no = len(rows), len(vecs), len(outs)
    in_specs = []
    for arr, off, w in rows:
        assert arr.shape[0] == nrows and off % w == 0, (name, arr.shape, off, w)
        in_specs.append(pl.BlockSpec((tm, w), functools.partial(lambda i, cb: (i, cb), cb=off // w)))
    for v in vecs:
        assert v.ndim == 2 and v.shape[0] == 1, (name, v.shape)
        in_specs.append(pl.BlockSpec(v.shape, lambda i: (0, 0)))
    out_shape = [jax.ShapeDtypeStruct((nrows, w), dt) for w, dt in outs]
    out_specs = [pl.BlockSpec((tm, w), lambda i: (i, 0)) for w, dt in outs]
    out_shape += [jax.ShapeDtypeStruct((1, w), F32) for w in reds]
    out_specs += [pl.BlockSpec((1, w), lambda i: (0, 0)) for w in reds]

    def body(*refs):
        rin, vin = refs[:nr], refs[nr:nr + nv]
        rout, rred = refs[nr + nv:nr + nv + no], refs[nr + nv + no:]
        res = fn(*[r[...].astype(F32) for r in rin], *[v[...] for v in vin])
        if not isinstance(res, (tuple, list)):
            res = (res,)
        assert len(res) == no + len(reds), (name, len(res))
        for r, val in zip(rout, res[:no]):
            r[...] = val.astype(r.dtype)
        if reds:
            @pl.when(pl.program_id(0) == 0)
            def _():
                for r in rred:
                    r[...] = jnp.zeros_like(r)

            for r, val in zip(rred, res[no:]):
                r[...] += val.astype(F32)

    res = pl.pallas_call(
        body, name=name, out_shape=out_shape, grid=(nrows // tm,),
        in_specs=in_specs, out_specs=out_specs,
        compiler_params=_cparams(("arbitrary",) if reds else ("parallel",)),
    )(*[r[0] for r in rows], *vecs)
    return res


def _f32(x):
    return x.astype(F32)


def _vjp_fn(f, n_in, n_cot, want):
    def fn(*args):
        ins = [_f32(a) for a in args[:n_in]]
        cots = tuple(_f32(a) for a in args[n_in:n_in + n_cot])
        _, vjp = jax.vjp(f, *ins)
        grads = vjp(cots if n_cot > 1 else cots[0])
        return tuple(grads[i] for i in want)
    return fn


def _f_rms(x, g):
    return (x * lax.rsqrt(jnp.mean(x * x, axis=-1, keepdims=True) + EPS)) * g


def _f_norm_mod(x, g, sc, sh):
    return _f_rms(x, g) * (1.0 + sc) + sh


def _f_gelu_in(ych, u, d):
    return jax.nn.gelu(ych + d * u)


def _f_glu(yg, pre, b):
    return yg * jax.nn.sigmoid(pre + b)


def _f_merge(ga, gb, ya, yb):
    return jax.nn.sigmoid(ga) * ya + jax.nn.sigmoid(gb) * yb


def _f_res(x, t, g):
    return x + g * t


def _f_swiglu(a, b):
    return jax.nn.silu(a) * b


def _f_rope_q(qa, qb, cos, sin):
    h = qa.shape[1] // HEAD_PAD
    return qa * jnp.tile(cos, (1, h)) + qb * jnp.tile(sin, (1, h))


def _f_rope_k(kn, kra, krb, cos, sin):
    h = kn.shape[1] // HEAD_PAD
    return kn + jnp.tile(kra * cos + krb * sin, (1, h))


SUBLANES = 8
S5_SUPER = 8


def _cpow(ar, ai, log2n):
    for _ in range(log2n):
        ar, ai = ar * ar - ai * ai, 2.0 * ar * ai
    return ar, ai


def _to_segments(x):
    L, w = x.shape
    return x.reshape(SUBLANES, L // SUBLANES, w).transpose(1, 0, 2).reshape(L, w)


def _from_segments(x):
    L, w = x.shape
    return x.reshape(L // SUBLANES, SUBLANES, w).transpose(1, 0, 2).reshape(L, w)


def _s5_pass(name, xin, a_re, a_im, w_in, *, reverse, nsup, sc, ends=None, w_out=None, u=None, h=None, tb=512):
    L, sw = xin.shape
    gp2 = nsup * 2 * sc
    gp = gp2 // 2
    seg = L // SUBLANES
    assert sw == nsup * LANES and L % SUBLANES == 0 and seg & (seg - 1) == 0, (L, sw)
    tb = _pick(L, tb, SUBLANES)
    nb, nt = L // tb, tb // SUBLANES
    first_pass = ends is None
    sign = -1.0 if reverse else 1.0
    blk = (lambda i: (nb - 1 - i, 0)) if reverse else (lambda i: (i, 0))
    order = (lambda s: nt - 1 - s) if reverse else (lambda s: s)
    tile = lambda s: pl.ds(pl.multiple_of(s * SUBLANES, SUBLANES), SUBLANES)
    const = lambda shape: pl.BlockSpec(shape, lambda i: (0, 0))
    rows_in = pl.BlockSpec((tb, sw), blk)
    rows_st = pl.BlockSpec((tb, gp2), blk)

    def coeffs(ar_ref, ai_ref, cc):
        ar1, ai1 = ar_ref[:, cc], sign * ai_ref[:, cc]
        return ar1, ai1, jnp.broadcast_to(ar1, (SUBLANES, sc)), jnp.broadcast_to(ai1, (SUBLANES, sc))

    def drive(x_ref, w_ref, xs_ref, c):
        lanes = pl.ds(c * LANES, LANES)
        xs_ref[...] = jnp.dot(x_ref[:, lanes].astype(BF16), w_ref[lanes, :], preferred_element_type=F32)

    def body(*refs):
        it = iter(refs)
        x_ref, w_ref, ar_ref, ai_ref = next(it), next(it), next(it), next(it)
        if first_pass:
            e_ref, xs_ref = next(it), next(it)
        elif not reverse:
            e_ref, wo_ref, o_ref, y_ref, st_ref, xs_ref = (next(it) for _ in range(6))
        else:
            (e_ref, wo_ref, u_ref, h_ref, hp_ref, hl_ref, du_ref, gb_ref, gc_ref, dar_ref, dai_ref,
             st_ref, xs_ref, g_ref) = (next(it) for _ in range(14))
        i = pl.program_id(0)

        @pl.when(i == 0)
        def _():
            if first_pass:
                e_ref[...] = jnp.zeros_like(e_ref)
                return
            for c in range(nsup):
                cc, re, im = pl.ds(c * sc, sc), pl.ds(2 * c * sc, sc), pl.ds((2 * c + 1) * sc, sc)
                ar1, ai1, _, _ = coeffs(ar_ref, ai_ref, cc)
                pr, pi = _cpow(ar1, ai1, seg.bit_length() - 1)
                cr = jnp.zeros((1, sc), F32)
                ci = jnp.zeros((1, sc), F32)
                for j in (reversed(range(SUBLANES)) if reverse else range(SUBLANES)):
                    st_ref[j:j + 1, re] = cr
                    st_ref[j:j + 1, im] = ci
                    cr, ci = (e_ref[j:j + 1, re] + pr * cr - pi * ci, e_ref[j:j + 1, im] + pr * ci + pi * cr)
            if reverse:
                for r in (gb_ref, gc_ref, dar_ref, dai_ref):
                    r[...] = jnp.zeros_like(r)

        sub = lax.broadcasted_iota(jnp.int32, (SUBLANES, sc), 0)
        state = e_ref if first_pass else st_ref
        xr, xi = pl.ds(0, sc), pl.ds(sc, sc)
        for c in range(nsup):
            cc, re, im = pl.ds(c * sc, sc), pl.ds(2 * c * sc, sc), pl.ds((2 * c + 1) * sc, sc)
            both = pl.ds(2 * c * sc, 2 * sc)
            lanes = pl.ds(c * LANES, LANES)
            _, _, ar, ai = coeffs(ar_ref, ai_ref, cc)
            drive(x_ref, w_ref, xs_ref, c)

            def advance(rows, sr, si):
                return ar * sr - ai * si + xs_ref[rows, xr], ar * si + ai * sr + xs_ref[rows, xi]

            if first_pass:
                def step(s, st):
                    return advance(tile(order(s)), *st)

                sr, si = lax.fori_loop(0, nt, step, (state[:, re], state[:, im]), unroll=4)
            elif not reverse:
                def step(s, st):
                    rows = tile(s)
                    sr, si = advance(rows, *st)
                    o_ref[rows, re] = sr
                    o_ref[rows, im] = si
                    return sr, si

                sr, si = lax.fori_loop(0, nt, step, (state[:, re], state[:, im]), unroll=4)
                y_ref[:, lanes] = jnp.dot(o_ref[:, both].astype(BF16), wo_ref[both, :], preferred_element_type=F32)
            else:
                def emit(rows, sr, si):
                    sr, si = advance(rows, sr, si)
                    g_ref[rows, xr] = sr
                    g_ref[rows, xi] = si
                    return sr, si

                def grad(sr, si, hpr, hpi, accr, acci):
                    return accr + sr * hpr + si * hpi, acci + si * hpr - sr * hpi

                def step(s, st):
                    sr, si, accr, acci = st
                    t = nt - 1 - s
                    sr, si = emit(tile(t), sr, si)
                    prev = tile(t - 1)
                    return (sr, si) + grad(sr, si, h_ref[prev, re], h_ref[prev, im], accr, acci)

                zero = jnp.zeros((SUBLANES, sc), F32)
                sr, si, accr, acci = lax.fori_loop(0, nt - 1, step, (state[:, re], state[:, im], zero, zero),
                                                   unroll=4)
                sr, si = emit(tile(0), sr, si)
                first = (i == nb - 1)
                wrap_r = jnp.where(sub == 0, 0.0, pltpu.roll(hl_ref[:, re], 1, 0))
                wrap_i = jnp.where(sub == 0, 0.0, pltpu.roll(hl_ref[:, im], 1, 0))
                accr, acci = grad(sr, si, jnp.where(first, wrap_r, hp_ref[:, re]),
                                  jnp.where(first, wrap_i, hp_ref[:, im]), accr, acci)
                dar_ref[:, cc] += jnp.sum(accr, axis=0, keepdims=True)
                dai_ref[:, cc] += jnp.sum(acci, axis=0, keepdims=True)
                gb = g_ref[...].astype(BF16)
                du_ref[:, lanes] = jnp.dot(gb, wo_ref[both, :], preferred_element_type=F32)
                gb_ref[lanes, :] += lax.dot_general(u_ref[:, lanes].astype(BF16), gb, _TN,
                                                    preferred_element_type=F32)
                gc_ref[both, :] += lax.dot_general(h_ref[:, both].astype(BF16), x_ref[:, lanes].astype(BF16), _TN,
                                                   preferred_element_type=F32)
            state[:, re] = sr
            state[:, im] = si

    vec = const((1, gp))
    in_specs = [rows_in, const(w_in.shape), vec, vec]
    operands = [xin, w_in, a_re, a_im]
    xs_scratch = pltpu.VMEM((tb, 2 * sc), F32)
    st_scratch = pltpu.VMEM((SUBLANES, gp2), F32)
    if first_pass:
        out_shape = jax.ShapeDtypeStruct((SUBLANES, gp2), F32)
        out_specs = const((SUBLANES, gp2))
        scratch = [xs_scratch]
    elif not reverse:
        in_specs += [const((SUBLANES, gp2)), const(w_out.shape)]
        operands += [ends, w_out]
        out_shape = [jax.ShapeDtypeStruct((L, gp2), F32), jax.ShapeDtypeStruct((L, sw), F32)]
        out_specs = [rows_st, rows_in]
        scratch = [st_scratch, xs_scratch]
    else:
        in_specs += [const((SUBLANES, gp2)), const(w_out.shape), rows_in, rows_st,
                     pl.BlockSpec((SUBLANES, gp2), lambda i: (jnp.maximum((nb - 1 - i) * nt - 1, 0), 0)),
                     pl.BlockSpec((SUBLANES, gp2), lambda i: (seg - 1, 0))]
        operands += [ends, w_out, u, h, h, h]
        out_shape = [jax.ShapeDtypeStruct((L, sw), F32), jax.ShapeDtypeStruct((sw, 2 * sc), F32),
                     jax.ShapeDtypeStruct((gp2, LANES), F32), jax.ShapeDtypeStruct((1, gp), F32),
                     jax.ShapeDtypeStruct((1, gp), F32)]
        out_specs = [rows_in, const((sw, 2 * sc)), const((gp2, LANES)), vec, vec]
        scratch = [st_scratch, xs_scratch, pltpu.VMEM((tb, 2 * sc), F32)]
    return pl.pallas_call(
        body, name=name, out_shape=out_shape, grid=(nb,), in_specs=in_specs, out_specs=out_specs,
        scratch_shapes=scratch, compiler_params=_cparams(("arbitrary",)),
    )(*operands)


_NT = (((1,), (1,)), ((), ()))
_TN = (((0,), (0,)), ((), ()))


def _causal(s, t, k_major=False):
    row = lax.broadcasted_iota(jnp.int32, (t, t), 0)
    col = lax.broadcasted_iota(jnp.int32, (t, t), 1)
    return jnp.where(row <= col if k_major else col <= row, s, NEG)


def _pair_tables(n, k_major):
    if k_major:
        pairs = [(qi, ki) for ki in range(n) for qi in range(ki, n)]
    else:
        pairs = [(qi, ki) for qi in range(n) for ki in range(qi + 1)]
    return (jnp.asarray([p[0] for p in pairs], jnp.int32), jnp.asarray([p[1] for p in pairs], jnp.int32))


def _flash_fwd(q, k, knv, *, heads, t):
    L = q.shape[0]
    n = L // t
    rep = t // LANES
    qtab, ktab = _pair_tables(n, k_major=False)

    def body(qt_ref, kt_ref, q_ref, k_ref, v_ref, o_ref, lse_ref, m_s, acc_s):
        step = pl.program_id(1)
        qi, ki = qt_ref[step], kt_ref[step]
        lane = lax.broadcasted_iota(jnp.int32, (t, HEAD_PAD), 1)

        @pl.when(ki == 0)
        def _():
            m_s[...] = jnp.full(m_s.shape, NEG, F32)
            acc_s[...] = jnp.zeros_like(acc_s)

        def update(diagonal):
            s = lax.dot_general(q_ref[...], k_ref[...], _NT, preferred_element_type=F32)
            if diagonal:
                s = _causal(s, t)
            m_prev = m_s[...]
            m_next = jnp.maximum(m_prev, jnp.max(s, axis=1, keepdims=True))
            p = jnp.exp(s - jnp.tile(m_next, (1, rep)))
            vb = jnp.where(lane == V_DIM, jnp.ones((), BF16), v_ref[...])
            acc_s[...] = jnp.exp(m_prev - m_next) * acc_s[...] + jnp.dot(p.astype(BF16), vb,
                                                                         preferred_element_type=F32)
            m_s[...] = m_next

        @pl.when(ki < qi)
        def _():
            update(False)

        @pl.when(ki == qi)
        def _():
            update(True)
            acc = acc_s[...]
            l = jnp.sum(jnp.where(lane == V_DIM, acc, 0.0), axis=1, keepdims=True)
            o_ref[...] = jnp.where(lane == V_DIM, 0.0, acc * (1.0 / l)).astype(o_ref.dtype)
            lse_ref[0] = jnp.max(m_s[...], axis=1, keepdims=True) + jnp.log(l)

    q_map = lambda h, s, qt, kt: (qt[s], h)
    kv_map = lambda h, s, qt, kt: (kt[s], h)
    v_map = lambda h, s, qt, kt: (kt[s], h + heads)
    return pl.pallas_call(
        body, name="mla_flash_fwd",
        out_shape=[jax.ShapeDtypeStruct((L, heads * HEAD_PAD), BF16),
                   jax.ShapeDtypeStruct((heads, L, 1), F32)],
        grid_spec=pltpu.PrefetchScalarGridSpec(
            num_scalar_prefetch=2, grid=(heads, qtab.shape[0]),
            in_specs=[pl.BlockSpec((t, HEAD_PAD), q_map),
                      pl.BlockSpec((t, HEAD_PAD), kv_map),
                      pl.BlockSpec((t, HEAD_PAD), v_map)],
            out_specs=[pl.BlockSpec((t, HEAD_PAD), q_map),
                       pl.BlockSpec((1, t, 1), lambda h, s, qt, kt: (h, qt[s], 0))],
            scratch_shapes=[pltpu.VMEM((t, LANES), F32), pltpu.VMEM((t, HEAD_PAD), F32)]),
        compiler_params=_cparams(("parallel", "arbitrary")),
    )(qtab, ktab, q, k, knv)


def _flash_delta(do, o, *, heads, t):
    L = do.shape[0]

    hw = heads * HEAD_PAD

    def body(do_ref, o_ref, d_ref):
        prod = do_ref[...].astype(F32) * o_ref[...].astype(F32)
        for h in range(heads):
            d_ref[h] = jnp.sum(prod[:, h * HEAD_PAD:(h + 1) * HEAD_PAD], axis=1, keepdims=True)

    return pl.pallas_call(
        body, name="mla_flash_delta",
        out_shape=jax.ShapeDtypeStruct((heads, L, 1), F32),
        grid=(L // t,),
        in_specs=[pl.BlockSpec((t, hw), lambda i: (i, 0)),
                  pl.BlockSpec((t, hw), lambda i: (i, 0))],
        out_specs=pl.BlockSpec((heads, t, 1), lambda i: (0, i, 0)),
        compiler_params=_cparams(("parallel",)),
    )(do, o)


def _flash_bwd(q, k, knv, do, lse, delta, *, heads, t):
    L = q.shape[0]
    n = L // t
    qtab, ktab = _pair_tables(n, k_major=True)

    def body(qt_ref, kt_ref, q_ref, k_ref, v_ref, do_ref, lse_ref, dl_ref, dq_ref, dk_ref, dv_ref, dk_s, dv_s):
        step = pl.program_id(1)
        qi, ki = qt_ref[step], kt_ref[step]

        @pl.when(qi == ki)
        def _():
            dk_s[...] = jnp.zeros_like(dk_s)
            dv_s[...] = jnp.zeros_like(dv_s)

        def update(diagonal):
            qb, kb, vb, dob = q_ref[...], k_ref[...], v_ref[...], do_ref[...]
            st = lax.dot_general(kb, qb, _NT, preferred_element_type=F32)
            if diagonal:
                st = _causal(st, t, k_major=True)
            pt = jnp.exp(st - lse_ref[0])
            dv_s[...] += jnp.dot(pt.astype(BF16), dob, preferred_element_type=F32)
            dpt = lax.dot_general(vb, dob, _NT, preferred_element_type=F32)
            dst = (pt * (dpt - dl_ref[0])).astype(BF16)
            dk_s[...] += jnp.dot(dst, qb, preferred_element_type=F32)
            dqb = lax.dot_general(dst, kb, _TN, preferred_element_type=F32)
            rows = pl.ds(pl.multiple_of(qi * t, t), t)

            @pl.when(ki == 0)
            def _():
                dq_ref[rows, :] = dqb

            @pl.when(ki > 0)
            def _():
                dq_ref[rows, :] += dqb

        @pl.when(qi > ki)
        def _():
            update(False)

        @pl.when(qi == ki)
        def _():
            update(True)

        @pl.when(qi == n - 1)
        def _():
            dk_ref[...] = dk_s[...].astype(dk_ref.dtype)
            dv_ref[...] = dv_s[...].astype(dv_ref.dtype)

    q_map = lambda h, s, qt, kt: (qt[s], h)
    stat_map = lambda h, s, qt, kt: (h, 0, qt[s])
    kv_map = lambda h, s, qt, kt: (kt[s], h)
    v_map = lambda h, s, qt, kt: (kt[s], h + heads)
    return pl.pallas_call(
        body, name="mla_flash_bwd",
        out_shape=[jax.ShapeDtypeStruct((L, heads * HEAD_PAD), F32),
                   jax.ShapeDtypeStruct((L, heads * HEAD_PAD), BF16),
                   jax.ShapeDtypeStruct((L, heads * HEAD_PAD), BF16)],
        grid_spec=pltpu.PrefetchScalarGridSpec(
            num_scalar_prefetch=2, grid=(heads, qtab.shape[0]),
            in_specs=[pl.BlockSpec((t, HEAD_PAD), q_map),
                      pl.BlockSpec((t, HEAD_PAD), kv_map),
                      pl.BlockSpec((t, HEAD_PAD), v_map),
                      pl.BlockSpec((t, HEAD_PAD), q_map),
                      pl.BlockSpec((1, 1, t), stat_map),
                      pl.BlockSpec((1, 1, t), stat_map)],
            out_specs=[pl.BlockSpec((L, HEAD_PAD), lambda h, s, qt, kt: (0, h)),
                       pl.BlockSpec((t, HEAD_PAD), kv_map),
                       pl.BlockSpec((t, HEAD_PAD), kv_map)],
            scratch_shapes=[pltpu.VMEM((t, HEAD_PAD), F32), pltpu.VMEM((t, HEAD_PAD), F32)]),
        compiler_params=_cparams(("parallel", "arbitrary")),
    )(qtab, ktab, q, k, knv, do, lse, delta)


def _peer(k):
    mx, my, mc = lax.axis_index("x"), lax.axis_index("y"), lax.axis_index("c")
    px = 1 - mx if (k >> 2) & 1 else mx
    py = 1 - my if (k >> 1) & 1 else my
    pc = 1 - mc if k & 1 else mc
    return (px, py, pc), 4 * px + 2 * py + pc


def _exchange(name, xs, all_to_all):
    n = len(xs)
    any_spec = pl.BlockSpec(memory_space=pl.ANY)
    npeer = N_DEV - 1

    def body(*refs):
        x_refs, o_refs = refs[:n], refs[n:2 * n]
        send_sems, recv_sems, local_sems = refs[2 * n:]
        _, me = _peer(0)
        mine = [x.at[me] if all_to_all else x for x in x_refs]
        local = [pltpu.make_async_copy(mine[i], o_refs[i].at[me], local_sems.at[i]) for i in range(n)]
        for cp in local:
            cp.start()
        sends = []
        for k in range(1, N_DEV):
            dev, idx = _peer(k)
            for i in range(n):
                cp = pltpu.make_async_remote_copy(
                    src_ref=x_refs[i].at[idx] if all_to_all else x_refs[i], dst_ref=o_refs[i].at[me],
                    send_sem=send_sems.at[i * npeer + k - 1], recv_sem=recv_sems.at[i * npeer + k - 1],
                    device_id=dev, device_id_type=pl.DeviceIdType.MESH)
                cp.start()
                sends.append(cp)
        for k in range(1, N_DEV):
            dev, idx = _peer(k)
            for i in range(n):
                pltpu.make_async_remote_copy(
                    src_ref=mine[i], dst_ref=o_refs[i].at[idx],
                    send_sem=send_sems.at[i * npeer + k - 1], recv_sem=recv_sems.at[i * npeer + k - 1],
                    device_id=dev, device_id_type=pl.DeviceIdType.MESH).wait_recv()
        for cp in sends:
            cp.wait_send()
        for cp in local:
            cp.wait()

    return pl.pallas_call(
        body, name=name,
        out_shape=[jax.ShapeDtypeStruct((N_DEV,) + tuple(x.shape[1:] if all_to_all else x.shape), x.dtype)
                   for x in xs],
        in_specs=[any_spec] * n, out_specs=[any_spec] * n,
        scratch_shapes=[pltpu.SemaphoreType.DMA((n * npeer,)), pltpu.SemaphoreType.DMA((n * npeer,)),
                        pltpu.SemaphoreType.DMA((n,))],
    )(*xs)


def _gather_two_level(name, xs):
    n = len(xs)
    any_spec = pl.BlockSpec(memory_space=pl.ANY)
    nslot = N_DEV - 1
    chips = (2, 4, 6)

    def body(*refs):
        x_refs, o_refs = refs[:n], refs[n:2 * n]
        send_sems, recv_sems, local_sems = refs[2 * n:]
        _, me = _peer(0)
        sibling, sib_idx = _peer(1)

        def copy(i, slot, src, block, dev):
            return pltpu.make_async_remote_copy(
                src_ref=src, dst_ref=o_refs[i].at[block],
                send_sem=send_sems.at[i * nslot + slot], recv_sem=recv_sems.at[i * nslot + slot],
                device_id=dev, device_id_type=pl.DeviceIdType.MESH)

        local = [pltpu.make_async_copy(x_refs[i], o_refs[i].at[me], local_sems.at[i]) for i in range(n)]
        for cp in local:
            cp.start()
        sends = []
        for i in range(n):
            sends.append(copy(i, 0, x_refs[i], me, sibling))
            for j, k in enumerate(chips):
                sends.append(copy(i, 1 + j, x_refs[i], me, _peer(k)[0]))
        for cp in sends:
            cp.start()
        for j, k in enumerate(chips):
            _, idx = _peer(k)
            for i in range(n):
                copy(i, 1 + j, x_refs[i], idx, _peer(k)[0]).wait_recv()
                fwd = copy(i, 4 + j, o_refs[i].at[idx], idx, sibling)
                fwd.start()
                sends.append(fwd)
        for i in range(n):
            copy(i, 0, x_refs[i], sib_idx, sibling).wait_recv()
        for j, k in enumerate(chips):
            _, idx = _peer(k | 1)
            for i in range(n):
                copy(i, 4 + j, x_refs[i], idx, sibling).wait_recv()
        for cp in sends:
            cp.wait_send()
        for cp in local:
            cp.wait()

    return pl.pallas_call(
        body, name=name,
        out_shape=[jax.ShapeDtypeStruct((N_DEV,) + tuple(x.shape), x.dtype) for x in xs],
        in_specs=[any_spec] * n, out_specs=[any_spec] * n,
        scratch_shapes=[pltpu.SemaphoreType.DMA((n * nslot,)), pltpu.SemaphoreType.DMA((n * nslot,)),
                        pltpu.SemaphoreType.DMA((n,))],
    )(*xs)


def _sum_slabs(name, x, *, tr=128):
    _, r, w = x.shape
    tr = _pick(r, tr, 8)

    def body(x_ref, o_ref):
        acc = x_ref[0]
        for j in range(1, N_DEV):
            acc = acc + x_ref[j]
        o_ref[...] = acc

    return pl.pallas_call(
        body, name=name, out_shape=jax.ShapeDtypeStruct((r, w), F32), grid=(r // tr,),
        in_specs=[pl.BlockSpec((N_DEV, tr, w), lambda i: (0, i, 0))],
        out_specs=pl.BlockSpec((tr, w), lambda i: (i, 0)),
        compiler_params=_cparams(("parallel",)),
    )(x)


def _adamw_fn(w, g, m, v):
    m = ADAM_B1 * m + (1.0 - ADAM_B1) * g
    v = ADAM_B2 * v + (1.0 - ADAM_B2) * jnp.square(g)
    m_hat = m / (1.0 - ADAM_B1 ** ADAM_STEP)
    v_hat = v / (1.0 - ADAM_B2 ** ADAM_STEP)
    delta = -ADAM_LR * (m_hat / (jnp.sqrt(v_hat) + ADAM_EPS) + ADAM_WD * w)
    return delta, m, v


def _adamw_sharded(name, w, recv, m, v, *, tr=128):
    rows, c = w.shape
    tr = _pick(rows, tr, 8)

    def body(w_ref, r_ref, m_ref, v_ref, g_ref, d_ref, mo_ref, vo_ref):
        g = r_ref[0].astype(F32)
        for j in range(1, N_DEV):
            g = g + r_ref[j].astype(F32)
        d, mn, vn = _adamw_fn(w_ref[...], g, m_ref[...], v_ref[...])
        g_ref[...] = g
        d_ref[...] = d
        mo_ref[...] = mn
        vo_ref[...] = vn

    blk = pl.BlockSpec((tr, c), lambda i: (i, 0))
    return pl.pallas_call(
        body, name=name, out_shape=[jax.ShapeDtypeStruct((rows, c), F32)] * 4, grid=(rows // tr,),
        in_specs=[blk, pl.BlockSpec((N_DEV, tr, c), lambda i: (0, i, 0)), blk, blk],
        out_specs=[blk] * 4,
        compiler_params=_cparams(("parallel",)),
    )(w, recv, m, v)


def _piece_rows(shape):
    return -(-math.prod(shape) // (8 * FLAT_W)) * 8


def _pack_rows(arrs):
    out = []
    for a in arrs:
        flat = a.reshape(-1)
        rows = _piece_rows(a.shape)
        out.append(jnp.pad(flat, (0, rows * FLAT_W - flat.shape[0])).reshape(rows, FLAT_W))
    return jnp.concatenate(out, axis=0)


def _unpack_rows(packed, shapes):
    out, r0 = [], 0
    for s in shapes:
        rows = _piece_rows(s)
        out.append(packed[r0:r0 + rows].reshape(-1)[:math.prod(s)].reshape(s))
        r0 += rows
    return out


def _pack_slabs(a):
    n = a.shape[1]
    rows = _piece_rows((n,))
    return jnp.pad(a, ((0, 0), (0, rows * FLAT_W - n))).reshape(N_DEV, rows, FLAT_W)


def _unpack_slabs(a, n):
    return a.reshape(N_DEV, -1)[:, :n]


def _s5_operators(a_re, a_im, log_dt, b_re, b_im, c_re, c_im):
    g, p, m = b_re.shape
    dt = jnp.exp(log_dt)[:, None]
    mag = jnp.exp(a_re * dt)
    abar_re = mag * jnp.cos(a_im * dt)
    abar_im = mag * jnp.sin(a_im * dt)
    den = a_re * a_re + a_im * a_im
    nr = abar_re - 1.0
    ni = abar_im
    coef_re = ((nr * a_re + ni * a_im) / den)[..., None]
    coef_im = ((ni * a_re - nr * a_im) / den)[..., None]
    bbar_re = coef_re * b_re - coef_im * b_im
    bbar_im = coef_re * b_im + coef_im * b_re
    sup, ns = S5_SUPER, g // S5_SUPER
    eye = jnp.eye(sup, dtype=F32)

    def b_blocks(bb):
        return jnp.einsum('cgpm,gh->cgmhp', bb.reshape(ns, sup, p, m), eye).reshape(ns, sup * m, sup * p)

    def c_blocks(cb):
        return jnp.einsum('cgmp,gh->chpgm', cb.reshape(ns, sup, m, p), eye).reshape(ns, sup * p, sup * m)

    bmat = jnp.concatenate([b_blocks(bbar_re), b_blocks(bbar_im)], axis=2).reshape(g * m, 2 * sup * p)
    cmat = jnp.concatenate([c_blocks(c_re), c_blocks(-c_im)], axis=1).reshape(ns * 2 * sup * p, sup * m)
    return abar_re.reshape(1, g * p), abar_im.reshape(1, g * p), bmat, cmat


def _rot_cols(w):
    half = w.shape[-1] // 2
    return jnp.concatenate([-w[..., half:], w[..., :half]], axis=-1)


def _layer_operators(w, dims):
    d, sw, ql, kvl, heads, dff = dims['d'], dims['sw'], dims['ql'], dims['kvl'], dims['heads'], dims['dff']
    w_in = w['w_in']
    o = 0
    parts = {}
    for nm, sz in (('u', sw), ('cq', ql), ('ckv', kvl), ('kr', QK_ROPE), ('ga', d), ('gb', d)):
        parts[nm] = w_in[:, o:o + sz]
        o += sz
    zpad = lambda n: jnp.zeros((d, n), w_in.dtype)
    kra = jnp.concatenate([zpad(QK_NOPE), parts['kr'], zpad(HEAD_PAD - QK_NOPE - QK_ROPE)], axis=1)
    krb = jnp.concatenate([zpad(QK_NOPE), _rot_cols(parts['kr']), zpad(HEAD_PAD - QK_NOPE - QK_ROPE)], axis=1)
    w_in_x = jnp.concatenate([parts['ga'], parts['gb'], parts['u'], parts['cq'], parts['ckv'], kra, krb], axis=1)

    wq = w['w_uq'].reshape(ql, heads, QK_NOPE + QK_ROPE)
    qz = lambda n: jnp.zeros((ql, heads, n), wq.dtype)
    wq_a = jnp.concatenate([wq, qz(HEAD_PAD - QK_NOPE - QK_ROPE)], axis=2)
    wq_b = jnp.concatenate([qz(QK_NOPE), _rot_cols(wq[:, :, QK_NOPE:]), qz(HEAD_PAD - QK_NOPE - QK_ROPE)], axis=2)
    wq_x = jnp.concatenate([wq_a.reshape(ql, -1), wq_b.reshape(ql, -1)], axis=1)

    kz = lambda n: jnp.zeros((kvl, heads, n), w['w_uk'].dtype)
    wk = jnp.concatenate([w['w_uk'].reshape(kvl, heads, QK_NOPE), kz(HEAD_PAD - QK_NOPE)], axis=2)
    wv = jnp.concatenate([w['w_uv'].reshape(kvl, heads, V_DIM), kz(HEAD_PAD - V_DIM)], axis=2)
    wkv_x = jnp.concatenate([wk.reshape(kvl, -1), wv.reshape(kvl, -1)], axis=1)

    wbo = w['w_b_out'].reshape(heads, V_DIM, d)
    wbo_x = jnp.concatenate([wbo, jnp.zeros((heads, HEAD_PAD - V_DIM, d), wbo.dtype)], axis=1).reshape(-1, d)
    wgu = jnp.concatenate([w['w_gate'], w['w_up']], axis=1)
    return dict(w_in=w_in_x, w_glu=w['w_glu'], w_a_out=w['w_a_out'], wq=wq_x, wkv=wkv_x, wbo=wbo_x,
                w_out=w['w_out'], wgu=wgu, w_down=w['w_down'])


def _gathered_to_full(gathered):
    full = {}
    for n, pc in zip(SHARDED, gathered):
        dep, r, c = pc.shape[1:]
        if n in COL_SHARDED:
            full[n] = pc.transpose(1, 2, 0, 3).reshape(dep, r, N_DEV * c)
        else:
            full[n] = pc.transpose(1, 0, 2, 3).reshape(dep, N_DEV * r, c)
    return full


def _layer_fwd(x, mod, ops, s5, small, rope, dims):
    d, sw, ql, kvl, heads, dff = dims['d'], dims['sw'], dims['ql'], dims['kvl'], dims['heads'], dims['dff']
    zo = dims['zoff']
    hw = heads * HEAD_PAD
    cos, sin = rope
    sh1, sc1, g1, sh2, sc2, g2 = mod
    bf = lambda a: a.astype(BF16)
    sv = dict(x=x)
    (h1,) = _rowwise("norm1_fwd", _f_norm_mod, [x], [small['norm1_g'], sc1, sh1], [(d, BF16)])
    z = _mm("w_in_fwd", h1, bf(ops['w_in']), out_dtype=BF16)
    sv.update(h1=h1, z=z)
    a_re, a_im, bmat, cmat = s5
    u_seg = _to_segments(z[:, zo['u']:zo['u'] + sw])
    nsup, sc = dims['nsup'], dims['s5_chunk']
    s5kw = dict(nsup=nsup, sc=sc)
    ends = _s5_pass("s5_ends_fwd", u_seg, a_re, a_im, bf(bmat), reverse=False, **s5kw)
    hst, ych = _s5_pass("s5_scan_fwd", u_seg, a_re, a_im, bf(bmat), reverse=False, ends=ends, w_out=bf(cmat), **s5kw)
    ych = _from_segments(ych)
    (yg,) = _rowwise("s5_gelu_fwd", _f_gelu_in, [ych, (z, zo['u'], sw)], [small['ssm_d']], [(sw, F32)])
    pre = _mm("s5_glu_mm_fwd", yg, bf(ops['w_glu']))
    (s5o,) = _rowwise("s5_glu_fwd", _f_glu, [yg, pre], [small['b_glu']], [(sw, BF16)])
    ya = _mm("s5_out_fwd", s5o, bf(ops['w_a_out']), out_dtype=BF16)
    sv.update(u_seg=u_seg, hst=hst, ych=ych, yg=yg, pre=pre, s5o=s5o, ya=ya)
    (cq,) = _rowwise("q_norm_fwd", _f_rms, [(z, zo['cq'], ql)], [small['q_norm_g']], [(ql, BF16)])
    qab = _mm("q_up_fwd", cq, bf(ops['wq']))
    scale = dims['scale']
    (q,) = _rowwise("q_rope_fwd", lambda a, b, cb, sb: _f_rope_q(a, b, cb, sb) * scale,
                    [(qab, 0, hw), (qab, hw, hw), cos, sin], [], [(hw, BF16)])
    (ckv,) = _rowwise("kv_norm_fwd", _f_rms, [(z, zo['ckv'], kvl)], [small['kv_norm_g']], [(kvl, BF16)])
    knv = _mm("kv_up_fwd", ckv, bf(ops['wkv']), out_dtype=BF16)
    (k,) = _rowwise("k_rope_fwd", _f_rope_k,
                    [(knv, 0, hw), (z, zo['kra'], HEAD_PAD), (z, zo['krb'], HEAD_PAD), cos, sin], [], [(hw, BF16)])
    o, lse = _flash_fwd(q, k, knv, heads=heads, t=dims['tq'])
    yb = _mm("mla_out_fwd", o, bf(ops['wbo']), out_dtype=BF16)
    sv.update(cq=cq, q=q, ckv=ckv, knv=knv, k=k, o=o, lse=lse, yb=yb)
    mix = [(z, zo['ga'], d), (z, zo['gb'], d), (ya, 0, d), (yb, 0, d)]
    t1 = _mm("w_out_fwd", mix, bf(ops['w_out']), a_fn=_f_merge)
    (x1,) = _rowwise("res1_fwd", _f_res, [x, t1], [g1], [(d, F32)])
    sv.update(t1=t1, x1=x1)
    (h2,) = _rowwise("norm2_fwd", _f_norm_mod, [x1], [small['norm2_g'], sc2, sh2], [(d, BF16)])
    ab = _mm("ffn_up_fwd", h2, bf(ops['wgu']), out_dtype=BF16)
    ffn_act = [(ab, 0, dff), (ab, dff, dff)]
    t2 = _mm("ffn_down_fwd", ffn_act, bf(ops['w_down']), a_fn=_f_swiglu, tm=256, tk=dff)
    (x2,) = _rowwise("res2_fwd", _f_res, [x1, t2], [g2], [(d, F32)])
    sv.update(h2=h2, ab=ab, t2=t2)
    return x2, sv


def _layer_bwd(dx2, sv, mod, ops, s5, small, rope, dims):
    d, sw, ql, kvl, heads, dff = dims['d'], dims['sw'], dims['ql'], dims['kvl'], dims['heads'], dims['dff']
    zo = dims['zoff']
    hw = heads * HEAD_PAD
    cos, sin = rope
    sh1, sc1, g1, sh2, sc2, g2 = mod
    a_re, a_im, bmat, cmat = s5
    z = sv['z']
    tr = lambda a: a.T.astype(BF16)
    gops, gsm = {}, {}
    dt2, dg2 = _res_bwd("res2_bwd", sv['t2'], g2, dx2)
    gops['w_down'] = _mm("ffn_down_dw", [(sv['ab'], 0, dff), (sv['ab'], dff, dff)], dt2, ta=True, a_fn=_f_swiglu,
                         tm=256, tk=2048)
    df = _mm("ffn_down_dx", dt2, tr(ops['w_down']), out_dtype=BF16)
    ab = sv['ab']
    swiglu_vjp = _vjp_fn(_f_swiglu, 2, 1, (0, 1))
    (dab,) = _rowwise("swiglu_bwd", lambda a, b, c: jnp.concatenate(swiglu_vjp(a, b, c), axis=1),
                      [(ab, 0, dff), (ab, dff, dff), df], [], [(2 * dff, BF16)], tm=256)
    gops['wgu'] = _mm("ffn_up_dw", sv['h2'], dab, ta=True)
    dh2 = _mm("ffn_up_dx", dab, tr(ops['wgu']), out_dtype=BF16)
    dx1, dn2, dsc2, dsh2 = _norm_mod_bwd("norm2_bwd", sv['x1'], small['norm2_g'], sc2, sh2, dh2, dx2)
    gsm['norm2_g'] = dn2
    dt1, dg1 = _res_bwd("res1_bwd", sv['t1'], g1, dx1)
    gops['w_out'] = _mm("w_out_dw", [(z, zo['ga'], d), (z, zo['gb'], d), (sv['ya'], 0, d), (sv['yb'], 0, d)], dt1,
                        ta=True, a_fn=_f_merge)
    dmerged = _mm("w_out_dx", dt1, tr(ops['w_out']), out_dtype=BF16)
    dga, dgb, dya, dyb = _rowwise(
        "merge_bwd", _vjp_fn(_f_merge, 4, 1, (0, 1, 2, 3)),
        [(z, zo['ga'], d), (z, zo['gb'], d), sv['ya'], sv['yb'], dmerged], [],
        [(d, BF16), (d, BF16), (d, BF16), (d, BF16)])
    gops['wbo'] = _mm("mla_out_dw", sv['o'], dyb, ta=True)
    do = _mm("mla_out_dx", dyb, tr(ops['wbo']), out_dtype=BF16)
    delta = _flash_delta(do, sv['o'], heads=heads, t=dims['tq'])
    as_rows = lambda a: a.reshape(heads, 1, -1)
    dq, dk, dv = _flash_bwd(sv['q'], sv['k'], sv['knv'], do, as_rows(sv['lse']), as_rows(delta),
                            heads=heads, t=dims['tq'])
    def k_bwd(dkb, cosb, sinb):
        dkb = _f32(dkb)
        dkpe = dkb[:, 0:HEAD_PAD]
        for h in range(1, heads):
            dkpe = dkpe + dkb[:, h * HEAD_PAD:(h + 1) * HEAD_PAD]
        return dkpe * cosb, dkpe * sinb
    dkra, dkrb = _rowwise("k_rope_bwd", k_bwd, [dk, cos, sin], [], [(HEAD_PAD, BF16), (HEAD_PAD, BF16)])
    dknv = jnp.concatenate([dk, dv], axis=1)
    gops['wkv'] = _mm("kv_up_dw", sv['ckv'], dknv, ta=True)
    dckv = _mm("kv_up_dx", dknv, tr(ops['wkv']))
    dckv_in, dkvg = _rms_bwd("kv_norm_bwd", z, zo['ckv'], kvl, small['kv_norm_g'], dckv)
    gsm['kv_norm_g'] = dkvg
    def q_bwd(dqb, cosb, sinb):
        dqb = _f32(dqb) * dims['scale']
        return jnp.concatenate([dqb * jnp.tile(cosb, (1, heads)), dqb * jnp.tile(sinb, (1, heads))], axis=1)
    (dqab,) = _rowwise("q_rope_bwd", q_bwd, [dq, cos, sin], [], [(2 * hw, BF16)])
    gops['wq'] = _mm("q_up_dw", sv['cq'], dqab, ta=True)
    dcq = _mm("q_up_dx", dqab, tr(ops['wq']))
    dcq_in, dqg = _rms_bwd("q_norm_bwd", z, zo['cq'], ql, small['q_norm_g'], dcq)
    gsm['q_norm_g'] = dqg
    gops['w_a_out'] = _mm("s5_out_dw", sv['s5o'], dya, ta=True)
    ds5o = _mm("s5_out_dx", dya, tr(ops['w_a_out']))

    def glu_bwd(yg, pre, ds, b):
        _, vjp = jax.vjp(_f_glu, _f32(yg), _f32(pre), b)
        dyg, dpre, db = vjp(_f32(ds))
        return dyg, dpre, db
    dyg_a, dpre, dbglu = _rowwise("s5_glu_bwd", glu_bwd, [sv['yg'], sv['pre'], ds5o], [small['b_glu']],
                                  [(sw, F32), (sw, BF16)], [sw])
    gsm['b_glu'] = dbglu
    gops['w_glu'] = _mm("s5_glu_mm_dw", sv['yg'], dpre, ta=True)
    dyg_b = _mm("s5_glu_mm_dx", dpre, tr(ops['w_glu']))

    def gelu_bwd(ych, u, dya_, dyb_, dvec):
        _, vjp = jax.vjp(_f_gelu_in, _f32(ych), _f32(u), dvec)
        dych, du, dd = vjp(_f32(dya_) + _f32(dyb_))
        return dych, du, dd
    dy, du_skip, dssm_d = _rowwise("s5_gelu_bwd", gelu_bwd, [sv['ych'], (z, zo['u'], sw), dyg_a, dyg_b],
                                   [small['ssm_d']], [(sw, BF16), (sw, F32)], [sw])
    gsm['ssm_d'] = dssm_d
    dy_seg = _to_segments(dy)
    nsup, sc = dims['nsup'], dims['s5_chunk']
    tr_blocks = lambda a: a.reshape(nsup, -1, a.shape[1]).transpose(0, 2, 1).reshape(-1, a.shape[0] // nsup)
    s5kw = dict(nsup=nsup, sc=sc)
    c_t, b_t = tr_blocks(cmat).astype(BF16), tr_blocks(bmat).astype(BF16)
    ends = _s5_pass("s5_ends_bwd", dy_seg, a_re, a_im, c_t, reverse=True, **s5kw)
    du_scan, g_bmat, g_cmat, dar, dai = _s5_pass("s5_scan_bwd", dy_seg, a_re, a_im, c_t, reverse=True, ends=ends,
                                                 w_out=b_t, u=sv['u_seg'], h=sv['hst'], **s5kw)
    du_scan = _from_segments(du_scan)
    (du,) = _rowwise("s5_du_sum", lambda a, b: _f32(a) + _f32(b), [du_skip, du_scan], [], [(sw, BF16)])
    gs5 = (dar, dai, g_bmat, g_cmat)
    dz = jnp.concatenate([dga, dgb, du, dcq_in, dckv_in, dkra, dkrb], axis=1)
    gops['w_in'] = _mm("w_in_dw", sv['h1'], dz, ta=True)
    dh1 = _mm("w_in_dx", dz, tr(ops['w_in']), out_dtype=BF16)
    dx, dn1, dsc1, dsh1 = _norm_mod_bwd("norm1_bwd", sv['x'], small['norm1_g'], sc1, sh1, dh1, dx1)
    gsm['norm1_g'] = dn1
    dmod = (dsh1, dsc1, dg1, dsh2, dsc2, dg2)
    return dx, gops, gs5, gsm, dmod


def _res_bwd(name, t, g, dxo):
    def fn(tb, db, gb):
        db = _f32(db)
        return gb * db, jnp.sum(db * _f32(tb), axis=0, keepdims=True)
    return _rowwise(name, fn, [t, dxo], [g], [(t.shape[1], BF16)], [t.shape[1]])


def _norm_mod_bwd(name, x, g, sc, sh, dh, dres):
    def fn(xb, dhb, dresb, gb, scb, shb):
        _, vjp = jax.vjp(_f_norm_mod, _f32(xb), gb, scb, shb)
        dx, dg, dsc, dsh = vjp(_f32(dhb))
        return dx + _f32(dresb), dg, dsc, dsh
    w = x.shape[1]
    return _rowwise(name, fn, [x, dh, dres], [g, sc, sh], [(w, F32)], [w, w, w])


def _rms_bwd(name, z, off, w, g, dy):
    def fn(xb, dyb, gb):
        _, vjp = jax.vjp(_f_rms, _f32(xb), gb)
        dx, dg = vjp(_f32(dyb))
        return dx, dg
    return _rowwise(name, fn, [(z, off, w), dy], [g], [(w, BF16)], [w])


S5_NAMES = ('ssm_a_re', 'ssm_a_im', 'ssm_log_dt', 'ssm_b_re', 'ssm_b_im', 'ssm_c_re', 'ssm_c_im')
LAYER_VECS = ('norm1_g', 'ssm_d', 'b_glu', 'q_norm_g', 'kv_norm_g', 'norm2_g')


def _step(p, mom_m, mom_v, x, c, positions, loss_target):
    depth, d = p['norm1_g'].shape
    L = x.shape[1]
    sw = p['ssm_d'].shape[1]
    ql, kvl = p['q_norm_g'].shape[1], p['kv_norm_g'].shape[1]
    heads = p['w_uk'].shape[2] * N_DEV // QK_NOPE
    dff = p['w_gate'].shape[2] * N_DEV
    ada_w = p['w_ada'].shape[2]
    zoff, o = {}, 0
    for nm, sz in (('ga', d), ('gb', d), ('u', sw), ('cq', ql), ('ckv', kvl), ('kra', HEAD_PAD), ('krb', HEAD_PAD)):
        assert o % sz == 0, (nm, o, sz)
        zoff[nm] = o
        o += sz
    groups, states = p['ssm_a_re'].shape[1:]
    assert groups % S5_SUPER == 0 and p['ssm_b_re'].shape[3] * S5_SUPER == LANES
    dims = dict(d=d, sw=sw, ql=ql, kvl=kvl, heads=heads, dff=dff, zoff=zoff,
                nsup=groups // S5_SUPER, s5_chunk=S5_SUPER * states,
                tq=1024 if L >= 4096 else 128, scale=(QK_NOPE + QK_ROPE) ** -0.5)
    x = x.reshape(L, d)
    tgt = loss_target.reshape(L, d)

    posf = positions.reshape(L).astype(F32)
    inv_freq = ROPE_BASE ** (-jnp.arange(0, QK_ROPE, 2, dtype=F32) / QK_ROPE)
    ang = posf[:, None] * inv_freq
    cs, sn = jnp.cos(ang), jnp.sin(ang)
    padr = HEAD_PAD - QK_NOPE - QK_ROPE
    cos = jnp.concatenate([jnp.ones((L, QK_NOPE), F32), cs, cs, jnp.zeros((L, padr), F32)], axis=1)
    sin = jnp.concatenate([jnp.zeros((L, QK_NOPE), F32), sn, sn, jnp.zeros((L, padr), F32)], axis=1)
    rope = (cos, sin)

    gathered = _gather_two_level("gather_weights", [p[n].astype(BF16) for n in SHARDED])

    def make_ops(gl):
        return jax.vmap(lambda w: _layer_operators(w, dims))(_gathered_to_full(gl))

    ops_all, ops_vjp = jax.vjp(make_ops, [g.astype(F32) for g in gathered])
    ops = [{k: v[l] for k, v in ops_all.items()} for l in range(depth)]

    def make_s5(sp):
        return jax.vmap(_s5_operators)(*[sp[n] for n in S5_NAMES])

    s5_all, s5_vjp = jax.vjp(make_s5, {n: p[n] for n in S5_NAMES})
    s5ops = [tuple(a[l] for a in s5_all) for l in range(depth)]

    (c_slabs,) = _exchange("gather_c", [jnp.pad(c, ((0, 7), (0, 0)))], all_to_all=False)
    c_all = c_slabs[:, 0, :]
    (c_act,) = _rowwise("c_silu", lambda a: jax.nn.silu(a), [jnp.pad(c_all, ((0, 8), (0, 0)))], [], [(d, F32)])
    w_ada_cat = p['w_ada'].transpose(1, 0, 2).reshape(d, depth * ada_w)
    mod_cols = _mm("ada_fwd", c_act, w_ada_cat)[:N_DEV]
    (mod_rows,) = _exchange("a2a_mod", [_pack_slabs(mod_cols)], all_to_all=True)
    mod_mine = _unpack_slabs(mod_rows, depth * ada_w).reshape(N_DEV, depth, ada_w)
    mod_mine = mod_mine.transpose(1, 0, 2).reshape(depth, N_DEV * ada_w)
    (mod_full,) = _rowwise("ada_bias", lambda a, b: a + b, [mod_mine, p['b_ada']], [], [(6 * d, F32)])
    mods = [tuple(mod_full[l:l + 1, i * d:(i + 1) * d] for i in range(6)) for l in range(depth)]

    saved = []
    xl = x
    for l in range(depth):
        small = {n: p[n][l:l + 1] for n in LAYER_VECS}
        xl, sv = _layer_fwd(xl, mods[l], ops[l], s5ops[l], small, rope, dims)
        saved.append((sv, small))

    def final_fn(xb, tb, gb):
        def lossf(xv, gv):
            e = _f_rms(xv, gv) - tb
            per_row = 0.5 * jnp.mean(e * e, axis=-1, keepdims=True)
            return jnp.sum(per_row, axis=0, keepdims=True)
        lv, vjp = jax.vjp(lossf, xb, gb)
        dxb, dgb = vjp(jnp.ones((1, 1), F32))
        return dxb, jnp.broadcast_to(lv, (1, LANES)), dgb
    dx, loss_vec, dfinal_g = _rowwise("final_loss", final_fn, [xl, tgt], [p['final_g'].reshape(1, d)],
                                      [(d, F32)], [LANES, d])

    g_ops, g_s5, g_small, dmods = [None] * depth, [None] * depth, [None] * depth, [None] * depth
    for l in reversed(range(depth)):
        sv, small = saved[l]
        dx, g_ops[l], g_s5[l], g_small[l], dmods[l] = _layer_bwd(dx, sv, mods[l], ops[l], s5ops[l], small,
                                                                rope, dims)
    grad_x = dx.reshape(1, L, d)

    dmod_mine = jnp.stack([jnp.concatenate(dm, axis=1)[0] for dm in dmods])
    dmod_slabs = dmod_mine.reshape(depth, N_DEV, ada_w).transpose(1, 0, 2).reshape(N_DEV, depth * ada_w)
    (dmod_recv,) = _exchange("a2a_dmod", [_pack_slabs(dmod_slabs)], all_to_all=True)
    dmod_cols = _unpack_slabs(dmod_recv, depth * ada_w)
    g_ada = _mm("ada_dw", c_act, jnp.pad(dmod_cols, ((0, 8), (0, 0))), ta=True)
    grads, delta, new_m, new_v = {}, {}, {}, {}
    grads['w_ada'] = g_ada.reshape(d, depth, ada_w).transpose(1, 0, 2)
    two_d = lambda a: a.reshape(-1, a.shape[-1])
    res = _rowwise("adamw_w_ada", _adamw_fn, [two_d(a) for a in (p['w_ada'], grads['w_ada'], mom_m['w_ada'],
                                                                  mom_v['w_ada'])], [], [(ada_w, F32)] * 3)
    delta['w_ada'], new_m['w_ada'], new_v['w_ada'] = [r.reshape(p['w_ada'].shape) for r in res]

    (g_slabs,) = ops_vjp({k: jnp.stack([g[k] for g in g_ops]) for k in ops_all})
    g_recv = _exchange("a2a_grads", [g.astype(BF16) for g in g_slabs], all_to_all=True)
    for n, rv in zip(SHARDED, g_recv):
        shp = p[n].shape
        res = _adamw_sharded("adamw_" + n, two_d(p[n]), rv.reshape(N_DEV, -1, shp[-1]), two_d(mom_m[n]),
                             two_d(mom_v[n]))
        grads[n], delta[n], new_m[n], new_v[n] = [r.reshape(shp) for r in res]

    (g_s5p,) = s5_vjp(tuple(jnp.stack([g[i] for g in g_s5]) for i in range(len(s5_all))))
    part = dict(g_s5p)
    part['b_ada'] = dmod_mine
    for n in LAYER_VECS:
        part[n] = jnp.concatenate([g_small[l][n] for l in range(depth)], axis=0)
    part['final_g'] = dfinal_g.reshape(d)
    small_shapes = [p[n].shape for n in SMALL] + [(1,)]
    (small_recv,) = _exchange("gather_small", [_pack_rows([part[n] for n in SMALL] + [loss_vec[0, 0:1]])],
                              all_to_all=False)
    small_sum = _sum_slabs("sum_small", small_recv)
    small_list = _unpack_rows(small_sum, small_shapes)
    grads.update(zip(SMALL, small_list[:-1]))
    loss = small_list[-1].reshape(())
    dummy = [jnp.zeros((1,), F32)]
    res = _rowwise("adamw_small", _adamw_fn,
                   [_pack_rows([src[n] for n in SMALL] + dummy) for src in (p, )] + [small_sum] +
                   [_pack_rows([src[n] for n in SMALL] + dummy) for src in (mom_m, mom_v)], [], [(FLAT_W, F32)] * 3)
    for dst, r in zip((delta, new_m, new_v), res):
        dst.update(zip(SMALL, _unpack_rows(r, small_shapes)[:-1]))
    return (loss, grad_x, *[grads[n] for n in WEIGHTS], *[delta[n] for n in WEIGHTS],
            *[new_m[n] for n in WEIGHTS], *[new_v[n] for n in WEIGHTS])


def kernel(x, c, positions, w_ada, b_ada, norm1_g, w_in, ssm_a_re, ssm_a_im, ssm_log_dt, ssm_b_re, ssm_b_im, ssm_c_re, ssm_c_im, ssm_d, w_glu, b_glu, w_a_out, q_norm_g, w_uq, kv_norm_g, w_uk, w_uv, w_b_out, w_out, norm2_g, w_gate, w_up, w_down, final_g, loss_target, m_w_ada, m_b_ada, m_norm1_g, m_w_in, m_ssm_a_re, m_ssm_a_im, m_ssm_log_dt, m_ssm_b_re, m_ssm_b_im, m_ssm_c_re, m_ssm_c_im, m_ssm_d, m_w_glu, m_b_glu, m_w_a_out, m_q_norm_g, m_w_uq, m_kv_norm_g, m_w_uk, m_w_uv, m_w_b_out, m_w_out, m_norm2_g, m_w_gate, m_w_up, m_w_down, m_final_g, v_w_ada, v_b_ada, v_norm1_g, v_w_in, v_ssm_a_re, v_ssm_a_im, v_ssm_log_dt, v_ssm_b_re, v_ssm_b_im, v_ssm_c_re, v_ssm_c_im, v_ssm_d, v_w_glu, v_b_glu, v_w_a_out, v_q_norm_g, v_w_uq, v_kv_norm_g, v_w_uk, v_w_uv, v_w_b_out, v_w_out, v_norm2_g, v_w_gate, v_w_up, v_w_down, v_final_g):
    given = dict(locals())
    p = {n: given[n] for n in WEIGHTS}
    mom_m = {n: given["m_" + n] for n in WEIGHTS}
    mom_v = {n: given["v_" + n] for n in WEIGHTS}
    return _step(p, mom_m, mom_v, x, c, positions, loss_target)
```

```python
import functools
import math

import jax
import jax.numpy as jnp
from jax import lax
from jax.experimental import pallas as pl
from jax.experimental.pallas import tpu as pltpu

F32 = jnp.float32
BF16 = jnp.bfloat16

N_DEV = 8
LANES = 128
FLAT_W = 1024
VMEM_LIMIT = 48 * 1024 * 1024
MM_VMEM_BUDGET = 36 * 1024 * 1024
QK_NOPE, QK_ROPE, V_DIM = 64, 32, 64
HEAD_PAD = LANES
ROPE_BASE = 10000.0
EPS = 1e-6
ADAM_LR, ADAM_B1, ADAM_B2, ADAM_EPS, ADAM_WD, ADAM_STEP = 0.001, 0.9, 0.999, 1e-08, 0.01, 10
NEG = float(jnp.finfo(jnp.float32).min)

WEIGHTS = ['w_ada', 'b_ada', 'norm1_g', 'w_in', 'ssm_a_re', 'ssm_a_im', 'ssm_log_dt', 'ssm_b_re', 'ssm_b_im',
           'ssm_c_re', 'ssm_c_im', 'ssm_d', 'w_glu', 'b_glu', 'w_a_out', 'q_norm_g', 'w_uq', 'kv_norm_g', 'w_uk',
           'w_uv', 'w_b_out', 'w_out', 'norm2_g', 'w_gate', 'w_up', 'w_down', 'final_g']
COL_SHARDED = ['w_in', 'w_a_out', 'w_uq', 'w_uk', 'w_uv', 'w_b_out', 'w_gate', 'w_up']
ROW_SHARDED = ['w_glu', 'w_out', 'w_down']
SHARDED = COL_SHARDED + ROW_SHARDED
SMALL = ['b_ada', 'norm1_g', 'ssm_a_re', 'ssm_a_im', 'ssm_log_dt', 'ssm_b_re', 'ssm_b_im', 'ssm_c_re', 'ssm_c_im',
         'ssm_d', 'b_glu', 'q_norm_g', 'kv_norm_g', 'norm2_g', 'final_g']


def _cparams(sem):
    return pltpu.CompilerParams(dimension_semantics=sem, vmem_limit_bytes=VMEM_LIMIT)


def _pick(dim, pref, quantum=LANES):
    if dim <= pref:
        return dim
    t = (pref // quantum) * quantum
    while t >= quantum:
        if dim % t == 0:
            return t
        t -= quantum
    return dim


def _mm(name, a, b, *, ta=False, out_dtype=F32, a_col=None, b_col=None, a_fn=None, keep_a=None, tm=1408, tn=1408,
        tk=None):
    if a_fn is None:
        a_off, a_w = a_col if a_col is not None else (0, a.shape[1])
        a_parts = [(a, a_off, a_w)]
    else:
        a_parts = a
        a_w = a_parts[0][2]
        assert all(w == a_w for _, _, w in a_parts)
    a0 = a_parts[0][0]
    b_off, b_w = b_col if b_col is not None else (0, b.shape[1])
    if ta:
        kdim, m = a0.shape[0], a_w
        assert b.shape[0] == kdim
    else:
        m, kdim = a0.shape[0], a_w
        assert b.shape[0] == kdim, (name, a0.shape, b.shape)
    n = b_w
    tm = _pick(m, tm, LANES if ta else 8)
    tn = _pick(n, tn)
    a_bytes = sum(arr.dtype.itemsize for arr, _, _ in a_parts) + (8 if a_fn is not None else 0)
    if tk is None:
        def fits(tm_, tk_):
            return (tm_ * tn * (2 * jnp.dtype(out_dtype).itemsize + 4)
                    + tk_ * (2 * (tm_ * a_bytes + tn * b.dtype.itemsize) + 2 * (tm_ + tn))) <= MM_VMEM_BUDGET

        tk = kdim
        while not fits(tm, tk):
            if not ta and tm % 16 == 0 and tm >= 1024 and fits(tm // 2, tk):
                tm //= 2
                break
            if tk % (2 * (8 if ta else LANES)):
                break
            tk //= 2
    tk = _pick(kdim, tk, 8 if ta else LANES)
    nk = kdim // tk
    assert m % tm == 0 and n % tn == 0 and kdim % tk == 0, (name, m, n, kdim, tm, tn, tk)
    a_specs = []
    for _, off, _ in a_parts:
        if ta:
            assert off % tm == 0
            a_specs.append(pl.BlockSpec((tk, tm), functools.partial(lambda i, j, k, o: (k, i + o), o=off // tm)))
        else:
            assert off % tk == 0
            a_specs.append(pl.BlockSpec((tm, tk), functools.partial(lambda i, j, k, o: (i, k + o), o=off // tk)))
    dims = _TN if ta else (((1,), (0,)), ((), ()))
    assert b_off % tn == 0
    b_spec = pl.BlockSpec((tk, tn), lambda i, j, k: (k, j + b_off // tn))
    na = len(a_parts)

    def prod(refs):
        if a_fn is None:
            av = refs[0][...]
        else:
            av = a_fn(*[r[...].astype(F32) for r in refs[:na]])
        return lax.dot_general(av.astype(BF16), refs[na][...].astype(BF16), dims, preferred_element_type=F32)

    def body_one(*refs):
        refs[na + 1][...] = prod(refs).astype(refs[na + 1].dtype)

    def body_keep(*refs):
        av = a_fn(*[r[...].astype(F32) for r in refs[:na]]).astype(keep_a)
        refs[na + 2][...] = av
        refs[na + 1][...] = lax.dot_general(av.astype(BF16), refs[na][...].astype(BF16), dims,
                                            preferred_element_type=F32).astype(refs[na + 1].dtype)

    def body_acc(*refs):
        o_ref, acc_ref = refs[na + 1], refs[na + 2]

        @pl.when(pl.program_id(2) == 0)
        def _():
            acc_ref[...] = jnp.zeros_like(acc_ref)

        acc_ref[...] += prod(refs)

        @pl.when(pl.program_id(2) == nk - 1)
        def _():
            o_ref[...] = acc_ref[...].astype(o_ref.dtype)

    out_shape = jax.ShapeDtypeStruct((m, n), out_dtype)
    out_specs = pl.BlockSpec((tm, tn), lambda i, j, k: (i, j))
    if keep_a is not None:
        assert a_fn is not None and not ta and nk == 1 and n == tn, (name, nk, n, tn)
        out_shape = [out_shape, jax.ShapeDtypeStruct((m, kdim), keep_a)]
        out_specs = [out_specs, pl.BlockSpec((tm, tk), lambda i, j, k: (i, k))]
    return pl.pallas_call(
        body_keep if keep_a is not None else (body_one if nk == 1 else body_acc), name=name,
        out_shape=out_shape,
        grid=(m // tm, n // tn, nk),
        in_specs=a_specs + [b_spec],
        out_specs=out_specs,
        scratch_shapes=[] if nk == 1 else [pltpu.VMEM((tm, tn), F32)],
        compiler_params=_cparams(("parallel", "parallel", "arbitrary")),
    )(*[arr for arr, _, _ in a_parts], b)


def _rowwise(name, fn, rows, vecs, outs, reds=(), *, tm=512):
    rows = [(r, 0, r.shape[1]) if not isinstance(r, tuple) else r for r in rows]
    nrows = rows[0][0].shape[0]
    tm = _pick(nrows, tm, 8)
    assert nrows % tm == 0, (name, nrows, tm)
    nr, nv, no = len(rows), len(vecs), len(outs)
    in_specs = []
    for arr, off, w in rows:
        assert arr.shape[0] == nrows and off % w == 0, (name, arr.shape, off, w)
        in_specs.append(pl.BlockSpec((tm, w), functools.partial(lambda i, cb: (i, cb), cb=off // w)))
    for v in vecs:
        assert v.ndim == 2 and v.shape[0] == 1, (name, v.shape)
        in_specs.append(pl.BlockSpec(v.shape, lambda i: (0, 0)))
    out_shape = [jax.ShapeDtypeStruct((nrows, w), dt) for w, dt in outs]
    out_specs = [pl.BlockSpec((tm, w), lambda i: (i, 0)) for w, dt in outs]
    out_shape += [jax.ShapeDtypeStruct((1, w), F32) for w in reds]
    out_specs += [pl.BlockSpec((1, w), lambda i: (0, 0)) for w in reds]

    def body(*refs):
        rin, vin = refs[:nr], refs[nr:nr + nv]
        rout, rred = refs[nr + nv:nr + nv + no], refs[nr + nv + no:]
        res = fn(*[r[...].astype(F32) for r in rin], *[v[...] for v in vin])
        if not isinstance(res, (tuple, list)):
            res = (res,)
        assert len(res) == no + len(reds), (name, len(res))
        for r, val in zip(rout, res[:no]):
            r[...] = val.astype(r.dtype)
        if reds:
            @pl.when(pl.program_id(0) == 0)
            def _():
                for r in rred:
                    r[...] = jnp.zeros_like(r)

            for r, val in zip(rred, res[no:]):
                r[...] += val.astype(F32)

    res = pl.pallas_call(
        body, name=name, out_shape=out_shape, grid=(nrows // tm,),
        in_specs=in_specs, out_specs=out_specs,
        compiler_params=_cparams(("arbitrary",) if reds else ("parallel",)),
    )(*[r[0] for r in rows], *vecs)
    return res


def _f32(x):
    return x.astype(F32)


def _vjp_fn(f, n_in, n_cot, want):
    def fn(*args):
        ins = [_f32(a) for a in args[:n_in]]
        cots = tuple(_f32(a) for a in args[n_in:n_in + n_cot])
        _, vjp = jax.vjp(f, *ins)
        grads = vjp(cots if n_cot > 1 else cots[0])
        return tuple(grads[i] for i in want)
    return fn


def _f_rms(x, g):
    return (x * lax.rsqrt(jnp.mean(x * x, axis=-1, keepdims=True) + EPS)) * g


def _f_norm_mod(x, g, sc, sh):
    return _f_rms(x, g) * (1.0 + sc) + sh


def _f_gelu_in(ych, u, d):
    return jax.nn.gelu(ych + d * u)


def _f_glu(yg, pre, b):
    return yg * jax.nn.sigmoid(pre + b)


def _f_merge(ga, gb, ya, yb):
    return jax.nn.sigmoid(ga) * ya + jax.nn.sigmoid(gb) * yb


def _f_res(x, t, g):
    return x + g * t


def _f_swiglu(a, b):
    return jax.nn.silu(a) * b


def _f_rope_q(qa, qb, cos, sin):
    h = qa.shape[1] // HEAD_PAD
    return qa * jnp.tile(cos, (1, h)) + qb * jnp.tile(sin, (1, h))


def _f_rope_k(kn, kra, krb, cos, sin):
    h = kn.shape[1] // HEAD_PAD
    return kn + jnp.tile(kra * cos + krb * sin, (1, h))


SUBLANES = 8
S5_SUPER = 8


def _cpow(ar, ai, log2n):
    for _ in range(log2n):
        ar, ai = ar * ar - ai * ai, 2.0 * ar * ai
    return ar, ai


def _to_segments(x):
    L, w = x.shape
    return x.reshape(SUBLANES, L // SUBLANES, w).transpose(1, 0, 2).reshape(L, w)


def _from_segments(x):
    L, w = x.shape
    return x.reshape(L // SUBLANES, SUBLANES, w).transpose(1, 0, 2).reshape(L, w)


def _s5_pass(name, xin, a_re, a_im, w_in, *, reverse, nsup, sc, ends=None, w_out=None, u=None, h=None, tb=512):
    L, sw = xin.shape
    gp2 = nsup * 2 * sc
    gp = gp2 // 2
    seg = L // SUBLANES
    assert sw == nsup * LANES and L % SUBLANES == 0 and seg & (seg - 1) == 0, (L, sw)
    tb = _pick(L, tb, SUBLANES)
    nb, nt = L // tb, tb // SUBLANES
    first_pass = ends is None
    sign = -1.0 if reverse else 1.0
    blk = (lambda i: (nb - 1 - i, 0)) if reverse else (lambda i: (i, 0))
    order = (lambda s: nt - 1 - s) if reverse else (lambda s: s)
    tile = lambda s: pl.ds(pl.multiple_of(s * SUBLANES, SUBLANES), SUBLANES)
    const = lambda shape: pl.BlockSpec(shape, lambda i: (0, 0))
    rows_in = pl.BlockSpec((tb, sw), blk)
    rows_st = pl.BlockSpec((tb, gp2), blk)

    def coeffs(ar_ref, ai_ref, cc):
        ar1, ai1 = ar_ref[:, cc], sign * ai_ref[:, cc]
        return ar1, ai1, jnp.broadcast_to(ar1, (SUBLANES, sc)), jnp.broadcast_to(ai1, (SUBLANES, sc))

    def drive(x_ref, w_ref, xs_ref, c):
        lanes = pl.ds(c * LANES, LANES)
        xs_ref[...] = jnp.dot(x_ref[:, lanes].astype(BF16), w_ref[lanes, :], preferred_element_type=F32)

    def body(*refs):
        it = iter(refs)
        x_ref, w_ref, ar_ref, ai_ref = next(it), next(it), next(it), next(it)
        if first_pass:
            e_ref, xs_ref = next(it), next(it)
        elif not reverse:
            e_ref, wo_ref, o_ref, y_ref, st_ref, xs_ref = (next(it) for _ in range(6))
        else:
            (e_ref, wo_ref, u_ref, h_ref, hp_ref, hl_ref, du_ref, gb_ref, gc_ref, dar_ref, dai_ref,
             st_ref, xs_ref, g_ref) = (next(it) for _ in range(14))
        i = pl.program_id(0)

        @pl.when(i == 0)
        def _():
            if first_pass:
                e_ref[...] = jnp.zeros_like(e_ref)
                return
            for c in range(nsup):
                cc, re, im = pl.ds(c * sc, sc), pl.ds(2 * c * sc, sc), pl.ds((2 * c + 1) * sc, sc)
                ar1, ai1, _, _ = coeffs(ar_ref, ai_ref, cc)
                pr, pi = _cpow(ar1, ai1, seg.bit_length() - 1)
                cr = jnp.zeros((1, sc), F32)
                ci = jnp.zeros((1, sc), F32)
                for j in (reversed(range(SUBLANES)) if reverse else range(SUBLANES)):
                    st_ref[j:j + 1, re] = cr
                    st_ref[j:j + 1, im] = ci
                    cr, ci = (e_ref[j:j + 1, re] + pr * cr - pi * ci, e_ref[j:j + 1, im] + pr * ci + pi * cr)
            if reverse:
                for r in (gb_ref, gc_ref, dar_ref, dai_ref):
                    r[...] = jnp.zeros_like(r)

        sub = lax.broadcasted_iota(jnp.int32, (SUBLANES, sc), 0)
        state = e_ref if first_pass else st_ref
        xr, xi = pl.ds(0, sc), pl.ds(sc, sc)
        for c in range(nsup):
            cc, re, im = pl.ds(c * sc, sc), pl.ds(2 * c * sc, sc), pl.ds((2 * c + 1) * sc, sc)
            both = pl.ds(2 * c * sc, 2 * sc)
            lanes = pl.ds(c * LANES, LANES)
            _, _, ar, ai = coeffs(ar_ref, ai_ref, cc)
            drive(x_ref, w_ref, xs_ref, c)

            def advance(rows, sr, si):
                return ar * sr - ai * si + xs_ref[rows, xr], ar * si + ai * sr + xs_ref[rows, xi]

            if first_pass:
                def step(s, st):
                    return advance(tile(order(s)), *st)

                sr, si = lax.fori_loop(0, nt, step, (state[:, re], state[:, im]), unroll=4)
            elif not reverse:
                def step(s, st):
                    rows = tile(s)
                    sr, si = advance(rows, *st)
                    o_ref[rows, re] = sr
                    o_ref[rows, im] = si
                    return sr, si

                sr, si = lax.fori_loop(0, nt, step, (state[:, re], state[:, im]), unroll=4)
                y_ref[:, lanes] = jnp.dot(o_ref[:, both].astype(BF16), wo_ref[both, :], preferred_element_type=F32)
            else:
                def emit(rows, sr, si):
                    sr, si = advance(rows, sr, si)
                    g_ref[rows, xr] = sr
                    g_ref[rows, xi] = si
                    return sr, si

                def grad(sr, si, hpr, hpi, accr, acci):
                    return accr + sr * hpr + si * hpi, acci + si * hpr - sr * hpi

                def step(s, st):
                    sr, si, accr, acci = st
                    t = nt - 1 - s
                    sr, si = emit(tile(t), sr, si)
                    prev = tile(t - 1)
                    return (sr, si) + grad(sr, si, h_ref[prev, re], h_ref[prev, im], accr, acci)

                zero = jnp.zeros((SUBLANES, sc), F32)
                sr, si, accr, acci = lax.fori_loop(0, nt - 1, step, (state[:, re], state[:, im], zero, zero),
                                                   unroll=4)
                sr, si = emit(tile(0), sr, si)
                first = (i == nb - 1)
                wrap_r = jnp.where(sub == 0, 0.0, pltpu.roll(hl_ref[:, re], 1, 0))
                wrap_i = jnp.where(sub == 0, 0.0, pltpu.roll(hl_ref[:, im], 1, 0))
                accr, acci = grad(sr, si, jnp.where(first, wrap_r, hp_ref[:, re]),
                                  jnp.where(first, wrap_i, hp_ref[:, im]), accr, acci)
                dar_ref[:, cc] += jnp.sum(accr, axis=0, keepdims=True)
                dai_ref[:, cc] += jnp.sum(acci, axis=0, keepdims=True)
                gb = g_ref[...].astype(BF16)
                du_ref[:, lanes] = jnp.dot(gb, wo_ref[both, :], preferred_element_type=F32)
                gb_ref[lanes, :] += lax.dot_general(u_ref[:, lanes].astype(BF16), gb, _TN,
                                                    preferred_element_type=F32)
                gc_ref[both, :] += lax.dot_general(h_ref[:, both].astype(BF16), x_ref[:, lanes].astype(BF16), _TN,
                                                   preferred_element_type=F32)
            state[:, re] = sr
            state[:, im] = si

    vec = const((1, gp))
    in_specs = [rows_in, const(w_in.shape), vec, vec]
    operands = [xin, w_in, a_re, a_im]
    xs_scratch = pltpu.VMEM((tb, 2 * sc), F32)
    st_scratch = pltpu.VMEM((SUBLANES, gp2), F32)
    if first_pass:
        out_shape = jax.ShapeDtypeStruct((SUBLANES, gp2), F32)
        out_specs = const((SUBLANES, gp2))
        scratch = [xs_scratch]
    elif not reverse:
        in_specs += [const((SUBLANES, gp2)), const(w_out.shape)]
        operands += [ends, w_out]
        out_shape = [jax.ShapeDtypeStruct((L, gp2), F32), jax.ShapeDtypeStruct((L, sw), F32)]
        out_specs = [rows_st, rows_in]
        scratch = [st_scratch, xs_scratch]
    else:
        in_specs += [const((SUBLANES, gp2)), const(w_out.shape), rows_in, rows_st,
                     pl.BlockSpec((SUBLANES, gp2), lambda i: (jnp.maximum((nb - 1 - i) * nt - 1, 0), 0)),
                     pl.BlockSpec((SUBLANES, gp2), lambda i: (seg - 1, 0))]
        operands += [ends, w_out, u, h, h, h]
        out_shape = [jax.ShapeDtypeStruct((L, sw), F32), jax.ShapeDtypeStruct((sw, 2 * sc), F32),
                     jax.ShapeDtypeStruct((gp2, LANES), F32), jax.ShapeDtypeStruct((1, gp), F32),
                     jax.ShapeDtypeStruct((1, gp), F32)]
        out_specs = [rows_in, const((sw, 2 * sc)), const((gp2, LANES)), vec, vec]
        scratch = [st_scratch, xs_scratch, pltpu.VMEM((tb, 2 * sc), F32)]
    return pl.pallas_call(
        body, name=name, out_shape=out_shape, grid=(nb,), in_specs=in_specs, out_specs=out_specs,
        scratch_shapes=scratch, compiler_params=_cparams(("arbitrary",)),
    )(*operands)


_NT = (((1,), (1,)), ((), ()))
_TN = (((0,), (0,)), ((), ()))


def _causal(s, t, k_major=False):
    row = lax.broadcasted_iota(jnp.int32, (t, t), 0)
    col = lax.broadcasted_iota(jnp.int32, (t, t), 1)
    return jnp.where(row <= col if k_major else col <= row, s, NEG)


def _pair_tables(n, k_major):
    if k_major:
        pairs = [(qi, ki) for ki in range(n) for qi in range(ki, n)]
    else:
        pairs = [(qi, ki) for qi in range(n) for ki in range(qi + 1)]
    return (jnp.asarray([p[0] for p in pairs], jnp.int32), jnp.asarray([p[1] for p in pairs], jnp.int32))


def _flash_fwd(q, k, knv, *, heads, t):
    L = q.shape[0]
    n = L // t
    rep = t // LANES
    qtab, ktab = _pair_tables(n, k_major=False)

    def body(qt_ref, kt_ref, q_ref, k_ref, v_ref, o_ref, lse_ref, m_s, acc_s):
        step = pl.program_id(1)
        qi, ki = qt_ref[step], kt_ref[step]
        lane = lax.broadcasted_iota(jnp.int32, (t, HEAD_PAD), 1)

        @pl.when(ki == 0)
        def _():
            m_s[...] = jnp.full(m_s.shape, NEG, F32)
            acc_s[...] = jnp.zeros_like(acc_s)

        def update(diagonal):
            s = lax.dot_general(q_ref[...], k_ref[...], _NT, preferred_element_type=F32)
            if diagonal:
                s = _causal(s, t)
            m_prev = m_s[...]
            m_next = jnp.maximum(m_prev, jnp.max(s, axis=1, keepdims=True))
            p = jnp.exp(s - jnp.tile(m_next, (1, rep)))
            vb = jnp.where(lane == V_DIM, jnp.ones((), BF16), v_ref[...])
            acc_s[...] = jnp.exp(m_prev - m_next) * acc_s[...] + jnp.dot(p.astype(BF16), vb,
                                                                         preferred_element_type=F32)
            m_s[...] = m_next

        @pl.when(ki < qi)
        def _():
            update(False)

        @pl.when(ki == qi)
        def _():
            update(True)
            acc = acc_s[...]
            l = jnp.sum(jnp.where(lane == V_DIM, acc, 0.0), axis=1, keepdims=True)
            o_ref[...] = jnp.where(lane == V_DIM, 0.0, acc * (1.0 / l)).astype(o_ref.dtype)
            lse_ref[0] = jnp.max(m_s[...], axis=1, keepdims=True) + jnp.log(l)

    q_map = lambda h, s, qt, kt: (qt[s], h)
    kv_map = lambda h, s, qt, kt: (kt[s], h)
    v_map = lambda h, s, qt, kt: (kt[s], h + heads)
    return pl.pallas_call(
        body, name="mla_flash_fwd",
        out_shape=[jax.ShapeDtypeStruct((L, heads * HEAD_PAD), BF16),
                   jax.ShapeDtypeStruct((heads, L, 1), F32)],
        grid_spec=pltpu.PrefetchScalarGridSpec(
            num_scalar_prefetch=2, grid=(heads, qtab.shape[0]),
            in_specs=[pl.BlockSpec((t, HEAD_PAD), q_map),
                      pl.BlockSpec((t, HEAD_PAD), kv_map),
                      pl.BlockSpec((t, HEAD_PAD), v_map)],
            out_specs=[pl.BlockSpec((t, HEAD_PAD), q_map),
                       pl.BlockSpec((1, t, 1), lambda h, s, qt, kt: (h, qt[s], 0))],
            scratch_shapes=[pltpu.VMEM((t, LANES), F32), pltpu.VMEM((t, HEAD_PAD), F32)]),
        compiler_params=_cparams(("parallel", "arbitrary")),
    )(qtab, ktab, q, k, knv)


def _flash_delta(do, o, *, heads, t):
    L = do.shape[0]

    hw = heads * HEAD_PAD

    def body(do_ref, o_ref, d_ref):
        prod = do_ref[...].astype(F32) * o_ref[...].astype(F32)
        for h in range(heads):
            d_ref[h] = jnp.sum(prod[:, h * HEAD_PAD:(h + 1) * HEAD_PAD], axis=1, keepdims=True)

    return pl.pallas_call(
        body, name="mla_flash_delta",
        out_shape=jax.ShapeDtypeStruct((heads, L, 1), F32),
        grid=(L // t,),
        in_specs=[pl.BlockSpec((t, hw), lambda i: (i, 0)),
                  pl.BlockSpec((t, hw), lambda i: (i, 0))],
        out_specs=pl.BlockSpec((heads, t, 1), lambda i: (0, i, 0)),
        compiler_params=_cparams(("parallel",)),
    )(do, o)


def _flash_bwd(q, k, knv, do, lse, delta, *, heads, t):
    L = q.shape[0]
    n = L // t
    qtab, ktab = _pair_tables(n, k_major=True)

    def body(qt_ref, kt_ref, q_ref, k_ref, v_ref, do_ref, lse_ref, dl_ref, dq_ref, dk_ref, dv_ref, dk_s, dv_s):
        step = pl.program_id(1)
        qi, ki = qt_ref[step], kt_ref[step]

        @pl.when(qi == ki)
        def _():
            dk_s[...] = jnp.zeros_like(dk_s)
            dv_s[...] = jnp.zeros_like(dv_s)

        def update(diagonal):
            qb, kb, vb, dob = q_ref[...], k_ref[...], v_ref[...], do_ref[...]
            st = lax.dot_general(kb, qb, _NT, preferred_element_type=F32)
            if diagonal:
                st = _causal(st, t, k_major=True)
            pt = jnp.exp(st - lse_ref[0])
            dv_s[...] += jnp.dot(pt.astype(BF16), dob, preferred_element_type=F32)
            dpt = lax.dot_general(vb, dob, _NT, preferred_element_type=F32)
            dst = (pt * (dpt - dl_ref[0])).astype(BF16)
            dk_s[...] += jnp.dot(dst, qb, preferred_element_type=F32)
            dqb = lax.dot_general(dst, kb, _TN, preferred_element_type=F32)
            rows = pl.ds(pl.multiple_of(qi * t, t), t)

            @pl.when(ki == 0)
            def _():
                dq_ref[rows, :] = dqb

            @pl.when(ki > 0)
            def _():
                dq_ref[rows, :] += dqb

        @pl.when(qi > ki)
        def _():
            update(False)

        @pl.when(qi == ki)
        def _():
            update(True)

        @pl.when(qi == n - 1)
        def _():
            dk_ref[...] = dk_s[...].astype(dk_ref.dtype)
            dv_ref[...] = dv_s[...].astype(dv_ref.dtype)

    q_map = lambda h, s, qt, kt: (qt[s], h)
    stat_map = lambda h, s, qt, kt: (h, 0, qt[s])
    kv_map = lambda h, s, qt, kt: (kt[s], h)
    v_map = lambda h, s, qt, kt: (kt[s], h + heads)
    return pl.pallas_call(
        body, name="mla_flash_bwd",
        out_shape=[jax.ShapeDtypeStruct((L, heads * HEAD_PAD), F32),
                   jax.ShapeDtypeStruct((L, heads * HEAD_PAD), BF16),
                   jax.ShapeDtypeStruct((L, heads * HEAD_PAD), BF16)],
        grid_spec=pltpu.PrefetchScalarGridSpec(
            num_scalar_prefetch=2, grid=(heads, qtab.shape[0]),
            in_specs=[pl.BlockSpec((t, HEAD_PAD), q_map),
                      pl.BlockSpec((t, HEAD_PAD), kv_map),
                      pl.BlockSpec((t, HEAD_PAD), v_map),
                      pl.BlockSpec((t, HEAD_PAD), q_map),
                      pl.BlockSpec((1, 1, t), stat_map),
                      pl.BlockSpec((1, 1, t), stat_map)],
            out_specs=[pl.BlockSpec((L, HEAD_PAD), lambda h, s, qt, kt: (0, h)),
                       pl.BlockSpec((t, HEAD_PAD), kv_map),
                       pl.BlockSpec((t, HEAD_PAD), kv_map)],
            scratch_shapes=[pltpu.VMEM((t, HEAD_PAD), F32), pltpu.VMEM((t, HEAD_PAD), F32)]),
        compiler_params=_cparams(("parallel", "arbitrary")),
    )(qtab, ktab, q, k, knv, do, lse, delta)


def _peer(k):
    mx, my, mc = lax.axis_index("x"), lax.axis_index("y"), lax.axis_index("c")
    px = 1 - mx if (k >> 2) & 1 else mx
    py = 1 - my if (k >> 1) & 1 else my
    pc = 1 - mc if k & 1 else mc
    return (px, py, pc), 4 * px + 2 * py + pc


def _exchange(name, xs, all_to_all):
    n = len(xs)
    any_spec = pl.BlockSpec(memory_space=pl.ANY)
    npeer = N_DEV - 1

    def body(*refs):
        x_refs, o_refs = refs[:n], refs[n:2 * n]
        send_sems, recv_sems, local_sems = refs[2 * n:]
        _, me = _peer(0)
        mine = [x.at[me] if all_to_all else x for x in x_refs]
        local = [pltpu.make_async_copy(mine[i], o_refs[i].at[me], local_sems.at[i]) for i in range(n)]
        for cp in local:
            cp.start()
        sends = []
        for k in range(1, N_DEV):
            dev, idx = _peer(k)
            for i in range(n):
                cp = pltpu.make_async_remote_copy(
                    src_ref=x_refs[i].at[idx] if all_to_all else x_refs[i], dst_ref=o_refs[i].at[me],
                    send_sem=send_sems.at[i * npeer + k - 1], recv_sem=recv_sems.at[i * npeer + k - 1],
                    device_id=dev, device_id_type=pl.DeviceIdType.MESH)
                cp.start()
                sends.append(cp)
        for k in range(1, N_DEV):
            dev, idx = _peer(k)
            for i in range(n):
                pltpu.make_async_remote_copy(
                    src_ref=mine[i], dst_ref=o_refs[i].at[idx],
                    send_sem=send_sems.at[i * npeer + k - 1], recv_sem=recv_sems.at[i * npeer + k - 1],
                    device_id=dev, device_id_type=pl.DeviceIdType.MESH).wait_recv()
        for cp in sends:
            cp.wait_send()
        for cp in local:
            cp.wait()

    return pl.pallas_call(
        body, name=name,
        out_shape=[jax.ShapeDtypeStruct((N_DEV,) + tuple(x.shape[1:] if all_to_all else x.shape), x.dtype)
                   for x in xs],
        in_specs=[any_spec] * n, out_specs=[any_spec] * n,
        scratch_shapes=[pltpu.SemaphoreType.DMA((n * npeer,)), pltpu.SemaphoreType.DMA((n * npeer,)),
                        pltpu.SemaphoreType.DMA((n,))],
    )(*xs)


def _gather_two_level(name, xs):
    n = len(xs)
    any_spec = pl.BlockSpec(memory_space=pl.ANY)
    nslot = N_DEV - 1
    chips = (2, 4, 6)

    def body(*refs):
        x_refs, o_refs = refs[:n], refs[n:2 * n]
        send_sems, recv_sems, local_sems = refs[2 * n:]
        _, me = _peer(0)
        sibling, sib_idx = _peer(1)

        def copy(i, slot, src, block, dev):
            return pltpu.make_async_remote_copy(
                src_ref=src, dst_ref=o_refs[i].at[block],
                send_sem=send_sems.at[i * nslot + slot], recv_sem=recv_sems.at[i * nslot + slot],
                device_id=dev, device_id_type=pl.DeviceIdType.MESH)

        local = [pltpu.make_async_copy(x_refs[i], o_refs[i].at[me], local_sems.at[i]) for i in range(n)]
        for cp in local:
            cp.start()
        sends = []
        for i in range(n):
            sends.append(copy(i, 0, x_refs[i], me, sibling))
            for j, k in enumerate(chips):
                sends.append(copy(i, 1 + j, x_refs[i], me, _peer(k)[0]))
        for cp in sends:
            cp.start()
        for j, k in enumerate(chips):
            _, idx = _peer(k)
            for i in range(n):
                copy(i, 1 + j, x_refs[i], idx, _peer(k)[0]).wait_recv()
                fwd = copy(i, 4 + j, o_refs[i].at[idx], idx, sibling)
                fwd.start()
                sends.append(fwd)
        for i in range(n):
            copy(i, 0, x_refs[i], sib_idx, sibling).wait_recv()
        for j, k in enumerate(chips):
            _, idx = _peer(k | 1)
            for i in range(n):
                copy(i, 4 + j, x_refs[i], idx, sibling).wait_recv()
        for cp in sends:
            cp.wait_send()
        for cp in local:
            cp.wait()

    return pl.pallas_call(
        body, name=name,
        out_shape=[jax.ShapeDtypeStruct((N_DEV,) + tuple(x.shape), x.dtype) for x in xs],
        in_specs=[any_spec] * n, out_specs=[any_spec] * n,
        scratch_shapes=[pltpu.SemaphoreType.DMA((n * nslot,)), pltpu.SemaphoreType.DMA((n * nslot,)),
                        pltpu.SemaphoreType.DMA((n,))],
    )(*xs)


def _sum_slabs(name, x, *, tr=128):
    _, r, w = x.shape
    tr = _pick(r, tr, 8)

    def body(x_ref, o_ref):
        acc = x_ref[0]
        for j in range(1, N_DEV):
            acc = acc + x_ref[j]
        o_ref[...] = acc

    return pl.pallas_call(
        body, name=name, out_shape=jax.ShapeDtypeStruct((r, w), F32), grid=(r // tr,),
        in_specs=[pl.BlockSpec((N_DEV, tr, w), lambda i: (0, i, 0))],
        out_specs=pl.BlockSpec((tr, w), lambda i: (i, 0)),
        compiler_params=_cparams(("parallel",)),
    )(x)


def _adamw_fn(w, g, m, v):
    m = ADAM_B1 * m + (1.0 - ADAM_B1) * g
    v = ADAM_B2 * v + (1.0 - ADAM_B2) * jnp.square(g)
    m_hat = m / (1.0 - ADAM_B1 ** ADAM_STEP)
    v_hat = v / (1.0 - ADAM_B2 ** ADAM_STEP)
    delta = -ADAM_LR * (m_hat / (jnp.sqrt(v_hat) + ADAM_EPS) + ADAM_WD * w)
    return delta, m, v


def _adamw_sharded(name, w, recv, m, v, *, tr=128):
    rows, c = w.shape
    tr = _pick(rows, tr, 8)

    def body(w_ref, r_ref, m_ref, v_ref, g_ref, d_ref, mo_ref, vo_ref):
        g = r_ref[0].astype(F32)
        for j in range(1, N_DEV):
            g = g + r_ref[j].astype(F32)
        d, mn, vn = _adamw_fn(w_ref[...], g, m_ref[...], v_ref[...])
        g_ref[...] = g
        d_ref[...] = d
        mo_ref[...] = mn
        vo_ref[...] = vn

    blk = pl.BlockSpec((tr, c), lambda i: (i, 0))
    return pl.pallas_call(
        body, name=name, out_shape=[jax.ShapeDtypeStruct((rows, c), F32)] * 4, grid=(rows // tr,),
        in_specs=[blk, pl.BlockSpec((N_DEV, tr, c), lambda i: (0, i, 0)), blk, blk],
        out_specs=[blk] * 4,
        compiler_params=_cparams(("parallel",)),
    )(w, recv, m, v)


def _piece_rows(shape):
    return -(-math.prod(shape) // (8 * FLAT_W)) * 8


def _pack_rows(arrs):
    out = []
    for a in arrs:
        flat = a.reshape(-1)
        rows = _piece_rows(a.shape)
        out.append(jnp.pad(flat, (0, rows * FLAT_W - flat.shape[0])).reshape(rows, FLAT_W))
    return jnp.concatenate(out, axis=0)


def _unpack_rows(packed, shapes):
    out, r0 = [], 0
    for s in shapes:
        rows = _piece_rows(s)
        out.append(packed[r0:r0 + rows].reshape(-1)[:math.prod(s)].reshape(s))
        r0 += rows
    return out


def _pack_slabs(a):
    n = a.shape[1]
    rows = _piece_rows((n,))
    return jnp.pad(a, ((0, 0), (0, rows * FLAT_W - n))).reshape(N_DEV, rows, FLAT_W)


def _unpack_slabs(a, n):
    return a.reshape(N_DEV, -1)[:, :n]


def _s5_operators(a_re, a_im, log_dt, b_re, b_im, c_re, c_im):
    g, p, m = b_re.shape
    dt = jnp.exp(log_dt)[:, None]
    mag = jnp.exp(a_re * dt)
    abar_re = mag * jnp.cos(a_im * dt)
    abar_im = mag * jnp.sin(a_im * dt)
    den = a_re * a_re + a_im * a_im
    nr = abar_re - 1.0
    ni = abar_im
    coef_re = ((nr * a_re + ni * a_im) / den)[..., None]
    coef_im = ((ni * a_re - nr * a_im) / den)[..., None]
    bbar_re = coef_re * b_re - coef_im * b_im
    bbar_im = coef_re * b_im + coef_im * b_re
    sup, ns = S5_SUPER, g // S5_SUPER
    eye = jnp.eye(sup, dtype=F32)

    def b_blocks(bb):
        return jnp.einsum('cgpm,gh->cgmhp', bb.reshape(ns, sup, p, m), eye).reshape(ns, sup * m, sup * p)

    def c_blocks(cb):
        return jnp.einsum('cgmp,gh->chpgm', cb.reshape(ns, sup, m, p), eye).reshape(ns, sup * p, sup * m)

    bmat = jnp.concatenate([b_blocks(bbar_re), b_blocks(bbar_im)], axis=2).reshape(g * m, 2 * sup * p)
    cmat = jnp.concatenate([c_blocks(c_re), c_blocks(-c_im)], axis=1).reshape(ns * 2 * sup * p, sup * m)
    return abar_re.reshape(1, g * p), abar_im.reshape(1, g * p), bmat, cmat


def _rot_cols(w):
    half = w.shape[-1] // 2
    return jnp.concatenate([-w[..., half:], w[..., :half]], axis=-1)


def _layer_operators(w, dims):
    d, sw, ql, kvl, heads, dff = dims['d'], dims['sw'], dims['ql'], dims['kvl'], dims['heads'], dims['dff']
    w_in = w['w_in']
    o = 0
    parts = {}
    for nm, sz in (('u', sw), ('cq', ql), ('ckv', kvl), ('kr', QK_ROPE), ('ga', d), ('gb', d)):
        parts[nm] = w_in[:, o:o + sz]
        o += sz
    zpad = lambda n: jnp.zeros((d, n), w_in.dtype)
    kra = jnp.concatenate([zpad(QK_NOPE), parts['kr'], zpad(HEAD_PAD - QK_NOPE - QK_ROPE)], axis=1)
    krb = jnp.concatenate([zpad(QK_NOPE), _rot_cols(parts['kr']), zpad(HEAD_PAD - QK_NOPE - QK_ROPE)], axis=1)
    w_in_x = jnp.concatenate([parts['ga'], parts['gb'], parts['u'], parts['cq'], parts['ckv'], kra, krb], axis=1)

    wq = w['w_uq'].reshape(ql, heads, QK_NOPE + QK_ROPE)
    qz = lambda n: jnp.zeros((ql, heads, n), wq.dtype)
    wq_a = jnp.concatenate([wq, qz(HEAD_PAD - QK_NOPE - QK_ROPE)], axis=2)
    wq_b = jnp.concatenate([qz(QK_NOPE), _rot_cols(wq[:, :, QK_NOPE:]), qz(HEAD_PAD - QK_NOPE - QK_ROPE)], axis=2)
    wq_x = jnp.concatenate([wq_a.reshape(ql, -1), wq_b.reshape(ql, -1)], axis=1)

    kz = lambda n: jnp.zeros((kvl, heads, n), w['w_uk'].dtype)
    wk = jnp.concatenate([w['w_uk'].reshape(kvl, heads, QK_NOPE), kz(HEAD_PAD - QK_NOPE)], axis=2)
    wv = jnp.concatenate([w['w_uv'].reshape(kvl, heads, V_DIM), kz(HEAD_PAD - V_DIM)], axis=2)
    wkv_x = jnp.concatenate([wk.reshape(kvl, -1), wv.reshape(kvl, -1)], axis=1)

    wbo = w['w_b_out'].reshape(heads, V_DIM, d)
    wbo_x = jnp.concatenate([wbo, jnp.zeros((heads, HEAD_PAD - V_DIM, d), wbo.dtype)], axis=1).reshape(-1, d)
    wgu = jnp.concatenate([w['w_gate'], w['w_up']], axis=1)
    return dict(w_in=w_in_x, w_glu=w['w_glu'], w_a_out=w['w_a_out'], wq=wq_x, wkv=wkv_x, wbo=wbo_x,
                w_out=w['w_out'], wgu=wgu, w_down=w['w_down'])


def _gathered_to_full(gathered):
    full = {}
    for n, pc in zip(SHARDED, gathered):
        dep, r, c = pc.shape[1:]
        if n in COL_SHARDED:
            full[n] = pc.transpose(1, 2, 0, 3).reshape(dep, r, N_DEV * c)
        else:
            full[n] = pc.transpose(1, 0, 2, 3).reshape(dep, N_DEV * r, c)
    return full


def _layer_fwd(x, mod, ops, s5, small, rope, dims):
    d, sw, ql, kvl, heads, dff = dims['d'], dims['sw'], dims['ql'], dims['kvl'], dims['heads'], dims['dff']
    zo = dims['zoff']
    hw = heads * HEAD_PAD
    cos, sin = rope
    sh1, sc1, g1, sh2, sc2, g2 = mod
    bf = lambda a: a.astype(BF16)
    sv = dict(x=x)
    (h1,) = _rowwise("norm1_fwd", _f_norm_mod, [x], [small['norm1_g'], sc1, sh1], [(d, BF16)])
    z = _mm("w_in_fwd", h1, bf(ops['w_in']), out_dtype=BF16)
    sv.update(h1=h1, z=z)
    a_re, a_im, bmat, cmat = s5
    u_seg = _to_segments(z[:, zo['u']:zo['u'] + sw])
    nsup, sc = dims['nsup'], dims['s5_chunk']
    s5kw = dict(nsup=nsup, sc=sc)
    ends = _s5_pass("s5_ends_fwd", u_seg, a_re, a_im, bf(bmat), reverse=False, **s5kw)
    hst, ych = _s5_pass("s5_scan_fwd", u_seg, a_re, a_im, bf(bmat), reverse=False, ends=ends, w_out=bf(cmat), **s5kw)
    ych = _from_segments(ych)
    (yg,) = _rowwise("s5_gelu_fwd", _f_gelu_in, [ych, (z, zo['u'], sw)], [small['ssm_d']], [(sw, F32)])
    pre = _mm("s5_glu_mm_fwd", yg, bf(ops['w_glu']))
    (s5o,) = _rowwise("s5_glu_fwd", _f_glu, [yg, pre], [small['b_glu']], [(sw, BF16)])
    ya = _mm("s5_out_fwd", s5o, bf(ops['w_a_out']), out_dtype=BF16)
    sv.update(u_seg=u_seg, hst=hst, ych=ych, yg=yg, pre=pre, s5o=s5o, ya=ya)
    (cq,) = _rowwise("q_norm_fwd", _f_rms, [(z, zo['cq'], ql)], [small['q_norm_g']], [(ql, BF16)])
    qab = _mm("q_up_fwd", cq, bf(ops['wq']))
    scale = dims['scale']
    (q,) = _rowwise("q_rope_fwd", lambda a, b, cb, sb: _f_rope_q(a, b, cb, sb) * scale,
                    [(qab, 0, hw), (qab, hw, hw), cos, sin], [], [(hw, BF16)])
    (ckv,) = _rowwise("kv_norm_fwd", _f_rms, [(z, zo['ckv'], kvl)], [small['kv_norm_g']], [(kvl, BF16)])
    knv = _mm("kv_up_fwd", ckv, bf(ops['wkv']), out_dtype=BF16)
    (k,) = _rowwise("k_rope_fwd", _f_rope_k,
                    [(knv, 0, hw), (z, zo['kra'], HEAD_PAD), (z, zo['krb'], HEAD_PAD), cos, sin], [], [(hw, BF16)])
    o, lse = _flash_fwd(q, k, knv, heads=heads, t=dims['tq'])
    yb = _mm("mla_out_fwd", o, bf(ops['wbo']), out_dtype=BF16)
    sv.update(cq=cq, q=q, ckv=ckv, knv=knv, k=k, o=o, lse=lse, yb=yb)
    mix = [(z, zo['ga'], d), (z, zo['gb'], d), (ya, 0, d), (yb, 0, d)]
    t1, merged = _mm("w_out_fwd", mix, bf(ops['w_out']), a_fn=_f_merge, keep_a=BF16)
    (x1,) = _rowwise("res1_fwd", _f_res, [x, t1], [g1], [(d, F32)])
    sv.update(merged=merged, t1=t1, x1=x1)
    (h2,) = _rowwise("norm2_fwd", _f_norm_mod, [x1], [small['norm2_g'], sc2, sh2], [(d, BF16)])
    ab = _mm("ffn_up_fwd", h2, bf(ops['wgu']), out_dtype=BF16)
    ffn_act = [(ab, 0, dff), (ab, dff, dff)]
    t2, f = _mm("ffn_down_fwd", ffn_act, bf(ops['w_down']), a_fn=_f_swiglu, keep_a=BF16, tm=256, tk=dff)
    (x2,) = _rowwise("res2_fwd", _f_res, [x1, t2], [g2], [(d, F32)])
    sv.update(h2=h2, ab=ab, f=f, t2=t2)
    return x2, sv


def _layer_bwd(dx2, sv, mod, ops, s5, small, rope, dims):
    d, sw, ql, kvl, heads, dff = dims['d'], dims['sw'], dims['ql'], dims['kvl'], dims['heads'], dims['dff']
    zo = dims['zoff']
    hw = heads * HEAD_PAD
    cos, sin = rope
    sh1, sc1, g1, sh2, sc2, g2 = mod
    a_re, a_im, bmat, cmat = s5
    z = sv['z']
    tr = lambda a: a.T.astype(BF16)
    gops, gsm = {}, {}
    dt2, dg2 = _res_bwd("res2_bwd", sv['t2'], g2, dx2)
    gops['w_down'] = _mm("ffn_down_dw", sv['f'], dt2, ta=True)
    df = _mm("ffn_down_dx", dt2, tr(ops['w_down']), out_dtype=BF16)
    ab = sv['ab']
    swiglu_vjp = _vjp_fn(_f_swiglu, 2, 1, (0, 1))
    (dab,) = _rowwise("swiglu_bwd", lambda a, b, c: jnp.concatenate(swiglu_vjp(a, b, c), axis=1),
                      [(ab, 0, dff), (ab, dff, dff), df], [], [(2 * dff, BF16)], tm=256)
    gops['wgu'] = _mm("ffn_up_dw", sv['h2'], dab, ta=True)
    dh2 = _mm("ffn_up_dx", dab, tr(ops['wgu']), out_dtype=BF16)
    dx1, dn2, dsc2, dsh2 = _norm_mod_bwd("norm2_bwd", sv['x1'], small['norm2_g'], sc2, sh2, dh2, dx2)
    gsm['norm2_g'] = dn2
    dt1, dg1 = _res_bwd("res1_bwd", sv['t1'], g1, dx1)
    gops['w_out'] = _mm("w_out_dw", sv['merged'], dt1, ta=True)
    dmerged = _mm("w_out_dx", dt1, tr(ops['w_out']), out_dtype=BF16)
    dga, dgb, dya, dyb = _rowwise(
        "merge_bwd", _vjp_fn(_f_merge, 4, 1, (0, 1, 2, 3)),
        [(z, zo['ga'], d), (z, zo['gb'], d), sv['ya'], sv['yb'], dmerged], [],
        [(d, BF16), (d, BF16), (d, BF16), (d, BF16)])
    gops['wbo'] = _mm("mla_out_dw", sv['o'], dyb, ta=True)
    do = _mm("mla_out_dx", dyb, tr(ops['wbo']), out_dtype=BF16)
    delta = _flash_delta(do, sv['o'], heads=heads, t=dims['tq'])
    as_rows = lambda a: a.reshape(heads, 1, -1)
    dq, dk, dv = _flash_bwd(sv['q'], sv['k'], sv['knv'], do, as_rows(sv['lse']), as_rows(delta),
                            heads=heads, t=dims['tq'])
    def k_bwd(dkb, cosb, sinb):
        dkb = _f32(dkb)
        dkpe = dkb[:, 0:HEAD_PAD]
        for h in range(1, heads):
            dkpe = dkpe + dkb[:, h * HEAD_PAD:(h + 1) * HEAD_PAD]
        return dkpe * cosb, dkpe * sinb
    dkra, dkrb = _rowwise("k_rope_bwd", k_bwd, [dk, cos, sin], [], [(HEAD_PAD, BF16), (HEAD_PAD, BF16)])
    dknv = jnp.concatenate([dk, dv], axis=1)
    gops['wkv'] = _mm("kv_up_dw", sv['ckv'], dknv, ta=True)
    dckv = _mm("kv_up_dx", dknv, tr(ops['wkv']))
    dckv_in, dkvg = _rms_bwd("kv_norm_bwd", z, zo['ckv'], kvl, small['kv_norm_g'], dckv)
    gsm['kv_norm_g'] = dkvg
    def q_bwd(dqb, cosb, sinb):
        dqb = _f32(dqb) * dims['scale']
        return jnp.concatenate([dqb * jnp.tile(cosb, (1, heads)), dqb * jnp.tile(sinb, (1, heads))], axis=1)
    (dqab,) = _rowwise("q_rope_bwd", q_bwd, [dq, cos, sin], [], [(2 * hw, BF16)])
    gops['wq'] = _mm("q_up_dw", sv['cq'], dqab, ta=True)
    dcq = _mm("q_up_dx", dqab, tr(ops['wq']))
    dcq_in, dqg = _rms_bwd("q_norm_bwd", z, zo['cq'], ql, small['q_norm_g'], dcq)
    gsm['q_norm_g'] = dqg
    gops['w_a_out'] = _mm("s5_out_dw", sv['s5o'], dya, ta=True)
    ds5o = _mm("s5_out_dx", dya, tr(ops['w_a_out']))

    def glu_bwd(yg, pre, ds, b):
        _, vjp = jax.vjp(_f_glu, _f32(yg), _f32(pre), b)
        dyg, dpre, db = vjp(_f32(ds))
        return dyg, dpre, db
    dyg_a, dpre, dbglu = _rowwise("s5_glu_bwd", glu_bwd, [sv['yg'], sv['pre'], ds5o], [small['b_glu']],
                                  [(sw, F32), (sw, BF16)], [sw])
    gsm['b_glu'] = dbglu
    gops['w_glu'] = _mm("s5_glu_mm_dw", sv['yg'], dpre, ta=True)
    dyg_b = _mm("s5_glu_mm_dx", dpre, tr(ops['w_glu']))

    def gelu_bwd(ych, u, dya_, dyb_, dvec):
        _, vjp = jax.vjp(_f_gelu_in, _f32(ych), _f32(u), dvec)
        dych, du, dd = vjp(_f32(dya_) + _f32(dyb_))
        return dych, du, dd
    dy, du_skip, dssm_d = _rowwise("s5_gelu_bwd", gelu_bwd, [sv['ych'], (z, zo['u'], sw), dyg_a, dyg_b],
                                   [small['ssm_d']], [(sw, BF16), (sw, F32)], [sw])
    gsm['ssm_d'] = dssm_d
    dy_seg = _to_segments(dy)
    nsup, sc = dims['nsup'], dims['s5_chunk']
    tr_blocks = lambda a: a.reshape(nsup, -1, a.shape[1]).transpose(0, 2, 1).reshape(-1, a.shape[0] // nsup)
    s5kw = dict(nsup=nsup, sc=sc)
    c_t, b_t = tr_blocks(cmat).astype(BF16), tr_blocks(bmat).astype(BF16)
    ends = _s5_pass("s5_ends_bwd", dy_seg, a_re, a_im, c_t, reverse=True, **s5kw)
    du_scan, g_bmat, g_cmat, dar, dai = _s5_pass("s5_scan_bwd", dy_seg, a_re, a_im, c_t, reverse=True, ends=ends,
                                                 w_out=b_t, u=sv['u_seg'], h=sv['hst'], **s5kw)
    du_scan = _from_segments(du_scan)
    (du,) = _rowwise("s5_du_sum", lambda a, b: _f32(a) + _f32(b), [du_skip, du_scan], [], [(sw, BF16)])
    gs5 = (dar, dai, g_bmat, g_cmat)
    dz = jnp.concatenate([dga, dgb, du, dcq_in, dckv_in, dkra, dkrb], axis=1)
    gops['w_in'] = _mm("w_in_dw", sv['h1'], dz, ta=True)
    dh1 = _mm("w_in_dx", dz, tr(ops['w_in']), out_dtype=BF16)
    dx, dn1, dsc1, dsh1 = _norm_mod_bwd("norm1_bwd", sv['x'], small['norm1_g'], sc1, sh1, dh1, dx1)
    gsm['norm1_g'] = dn1
    dmod = (dsh1, dsc1, dg1, dsh2, dsc2, dg2)
    return dx, gops, gs5, gsm, dmod


def _res_bwd(name, t, g, dxo):
    def fn(tb, db, gb):
        db = _f32(db)
        return gb * db, jnp.sum(db * _f32(tb), axis=0, keepdims=True)
    return _rowwise(name, fn, [t, dxo], [g], [(t.shape[1], BF16)], [t.shape[1]])


def _norm_mod_bwd(name, x, g, sc, sh, dh, dres):
    def fn(xb, dhb, dresb, gb, scb, shb):
        _, vjp = jax.vjp(_f_norm_mod, _f32(xb), gb, scb, shb)
        dx, dg, dsc, dsh = vjp(_f32(dhb))
        return dx + _f32(dresb), dg, dsc, dsh
    w = x.shape[1]
    return _rowwise(name, fn, [x, dh, dres], [g, sc, sh], [(w, F32)], [w, w, w])


def _rms_bwd(name, z, off, w, g, dy):
    def fn(xb, dyb, gb):
        _, vjp = jax.vjp(_f_rms, _f32(xb), gb)
        dx, dg = vjp(_f32(dyb))
        return dx, dg
    return _rowwise(name, fn, [(z, off, w), dy], [g], [(w, BF16)], [w])


S5_NAMES = ('ssm_a_re', 'ssm_a_im', 'ssm_log_dt', 'ssm_b_re', 'ssm_b_im', 'ssm_c_re', 'ssm_c_im')
LAYER_VECS = ('norm1_g', 'ssm_d', 'b_glu', 'q_norm_g', 'kv_norm_g', 'norm2_g')


def _step(p, mom_m, mom_v, x, c, positions, loss_target):
    depth, d = p['norm1_g'].shape
    L = x.shape[1]
    sw = p['ssm_d'].shape[1]
    ql, kvl = p['q_norm_g'].shape[1], p['kv_norm_g'].shape[1]
    heads = p['w_uk'].shape[2] * N_DEV // QK_NOPE
    dff = p['w_gate'].shape[2] * N_DEV
    ada_w = p['w_ada'].shape[2]
    zoff, o = {}, 0
    for nm, sz in (('ga', d), ('gb', d), ('u', sw), ('cq', ql), ('ckv', kvl), ('kra', HEAD_PAD), ('krb', HEAD_PAD)):
        assert o % sz == 0, (nm, o, sz)
        zoff[nm] = o
        o += sz
    groups, states = p['ssm_a_re'].shape[1:]
    assert groups % S5_SUPER == 0 and p['ssm_b_re'].shape[3] * S5_SUPER == LANES
    dims = dict(d=d, sw=sw, ql=ql, kvl=kvl, heads=heads, dff=dff, zoff=zoff,
                nsup=groups // S5_SUPER, s5_chunk=S5_SUPER * states,
                tq=1024 if L >= 4096 else 128, scale=(QK_NOPE + QK_ROPE) ** -0.5)
    x = x.reshape(L, d)
    tgt = loss_target.reshape(L, d)

    posf = positions.reshape(L).astype(F32)
    inv_freq = ROPE_BASE ** (-jnp.arange(0, QK_ROPE, 2, dtype=F32) / QK_ROPE)
    ang = posf[:, None] * inv_freq
    cs, sn = jnp.cos(ang), jnp.sin(ang)
    padr = HEAD_PAD - QK_NOPE - QK_ROPE
    cos = jnp.concatenate([jnp.ones((L, QK_NOPE), F32), cs, cs, jnp.zeros((L, padr), F32)], axis=1)
    sin = jnp.concatenate([jnp.zeros((L, QK_NOPE), F32), sn, sn, jnp.zeros((L, padr), F32)], axis=1)
    rope = (cos, sin)

    gathered = _gather_two_level("gather_weights", [p[n].astype(BF16) for n in SHARDED])

    def make_ops(gl):
        return jax.vmap(lambda w: _layer_operators(w, dims))(_gathered_to_full(gl))

    ops_all, ops_vjp = jax.vjp(make_ops, [g.astype(F32) for g in gathered])
    ops = [{k: v[l] for k, v in ops_all.items()} for l in range(depth)]

    def make_s5(sp):
        return jax.vmap(_s5_operators)(*[sp[n] for n in S5_NAMES])

    s5_all, s5_vjp = jax.vjp(make_s5, {n: p[n] for n in S5_NAMES})
    s5ops = [tuple(a[l] for a in s5_all) for l in range(depth)]

    (c_slabs,) = _exchange("gather_c", [jnp.pad(c, ((0, 7), (0, 0)))], all_to_all=False)
    c_all = c_slabs[:, 0, :]
    (c_act,) = _rowwise("c_silu", lambda a: jax.nn.silu(a), [jnp.pad(c_all, ((0, 8), (0, 0)))], [], [(d, F32)])
    w_ada_cat = p['w_ada'].transpose(1, 0, 2).reshape(d, depth * ada_w)
    mod_cols = _mm("ada_fwd", c_act, w_ada_cat)[:N_DEV]
    (mod_rows,) = _exchange("a2a_mod", [_pack_slabs(mod_cols)], all_to_all=True)
    mod_mine = _unpack_slabs(mod_rows, depth * ada_w).reshape(N_DEV, depth, ada_w)
    mod_mine = mod_mine.transpose(1, 0, 2).reshape(depth, N_DEV * ada_w)
    (mod_full,) = _rowwise("ada_bias", lambda a, b: a + b, [mod_mine, p['b_ada']], [], [(6 * d, F32)])
    mods = [tuple(mod_full[l:l + 1, i * d:(i + 1) * d] for i in range(6)) for l in range(depth)]

    saved = []
    xl = x
    for l in range(depth):
        small = {n: p[n][l:l + 1] for n in LAYER_VECS}
        xl, sv = _layer_fwd(xl, mods[l], ops[l], s5ops[l], small, rope, dims)
        saved.append((sv, small))

    def final_fn(xb, tb, gb):
        def lossf(xv, gv):
            e = _f_rms(xv, gv) - tb
            per_row = 0.5 * jnp.mean(e * e, axis=-1, keepdims=True)
            return jnp.sum(per_row, axis=0, keepdims=True)
        lv, vjp = jax.vjp(lossf, xb, gb)
        dxb, dgb = vjp(jnp.ones((1, 1), F32))
        return dxb, jnp.broadcast_to(lv, (1, LANES)), dgb
    dx, loss_vec, dfinal_g = _rowwise("final_loss", final_fn, [xl, tgt], [p['final_g'].reshape(1, d)],
                                      [(d, F32)], [LANES, d])

    g_ops, g_s5, g_small, dmods = [None] * depth, [None] * depth, [None] * depth, [None] * depth
    for l in reversed(range(depth)):
        sv, small = saved[l]
        dx, g_ops[l], g_s5[l], g_small[l], dmods[l] = _layer_bwd(dx, sv, mods[l], ops[l], s5ops[l], small,
                                                                rope, dims)
    grad_x = dx.reshape(1, L, d)

    dmod_mine = jnp.stack([jnp.concatenate(dm, axis=1)[0] for dm in dmods])
    dmod_slabs = dmod_mine.reshape(depth, N_DEV, ada_w).transpose(1, 0, 2).reshape(N_DEV, depth * ada_w)
    (dmod_recv,) = _exchange("a2a_dmod", [_pack_slabs(dmod_slabs)], all_to_all=True)
    dmod_cols = _unpack_slabs(dmod_recv, depth * ada_w)
    g_ada = _mm("ada_dw", c_act, jnp.pad(dmod_cols, ((0, 8), (0, 0))), ta=True)
    grads, delta, new_m, new_v = {}, {}, {}, {}
    grads['w_ada'] = g_ada.reshape(d, depth, ada_w).transpose(1, 0, 2)
    two_d = lambda a: a.reshape(-1, a.shape[-1])
    res = _rowwise("adamw_w_ada", _adamw_fn, [two_d(a) for a in (p['w_ada'], grads['w_ada'], mom_m['w_ada'],
                                                                  mom_v['w_ada'])], [], [(ada_w, F32)] * 3)
    delta['w_ada'], new_m['w_ada'], new_v['w_ada'] = [r.reshape(p['w_ada'].shape) for r in res]

    (g_slabs,) = ops_vjp({k: jnp.stack([g[k] for g in g_ops]) for k in ops_all})
    g_recv = _exchange("a2a_grads", [g.astype(BF16) for g in g_slabs], all_to_all=True)
    for n, rv in zip(SHARDED, g_recv):
        shp = p[n].shape
        res = _adamw_sharded("adamw_" + n, two_d(p[n]), rv.reshape(N_DEV, -1, shp[-1]), two_d(mom_m[n]),
                             two_d(mom_v[n]))
        grads[n], delta[n], new_m[n], new_v[n] = [r.reshape(shp) for r in res]

    (g_s5p,) = s5_vjp(tuple(jnp.stack([g[i] for g in g_s5]) for i in range(len(s5_all))))
    part = dict(g_s5p)
    part['b_ada'] = dmod_mine
    for n in LAYER_VECS:
        part[n] = jnp.concatenate([g_small[l][n] for l in range(depth)], axis=0)
    part['final_g'] = dfinal_g.reshape(d)
    small_shapes = [p[n].shape for n in SMALL] + [(1,)]
    (small_recv,) = _exchange("gather_small", [_pack_rows([part[n] for n in SMALL] + [loss_vec[0, 0:1]])],
                              all_to_all=False)
    small_sum = _sum_slabs("sum_small", small_recv)
    small_list = _unpack_rows(small_sum, small_shapes)
    grads.update(zip(SMALL, small_list[:-1]))
    loss = small_list[-1].reshape(())
    dummy = [jnp.zeros((1,), F32)]
    res = _rowwise("adamw_small", _adamw_fn,
                   [_pack_rows([src[n] for n in SMALL] + dummy) for src in (p, )] + [small_sum] +
                   [_pack_rows([src[n] for n in SMALL] + dummy) for src in (mom_m, mom_v)], [], [(FLAT_W, F32)] * 3)
    for dst, r in zip((delta, new_m, new_v), res):
        dst.update(zip(SMALL, _unpack_rows(r, small_shapes)[:-1]))
    return (loss, grad_x, *[grads[n] for n in WEIGHTS], *[delta[n] for n in WEIGHTS],
            *[new_m[n] for n in WEIGHTS], *[new_v[n] for n in WEIGHTS])


def kernel(x, c, positions, w_ada, b_ada, norm1_g, w_in, ssm_a_re, ssm_a_im, ssm_log_dt, ssm_b_re, ssm_b_im, ssm_c_re, ssm_c_im, ssm_d, w_glu, b_glu, w_a_out, q_norm_g, w_uq, kv_norm_g, w_uk, w_uv, w_b_out, w_out, norm2_g, w_gate, w_up, w_down, final_g, loss_target, m_w_ada, m_b_ada, m_norm1_g, m_w_in, m_ssm_a_re, m_ssm_a_im, m_ssm_log_dt, m_ssm_b_re, m_ssm_b_im, m_ssm_c_re, m_ssm_c_im, m_ssm_d, m_w_glu, m_b_glu, m_w_a_out, m_q_norm_g, m_w_uq, m_kv_norm_g, m_w_uk, m_w_uv, m_w_b_out, m_w_out, m_norm2_g, m_w_gate, m_w_up, m_w_down, m_final_g, v_w_ada, v_b_ada, v_norm1_g, v_w_in, v_ssm_a_re, v_ssm_a_im, v_ssm_log_dt, v_ssm_b_re, v_ssm_b_im, v_ssm_c_re, v_ssm_c_im, v_ssm_d, v_w_glu, v_b_glu, v_w_a_out, v_q_norm_g, v_w_uq, v_kv_norm_g, v_w_uk, v_w_uv, v_w_b_out, v_w_out, v_norm2_g, v_w_gate, v_w_up, v_w_down, v_final_g):
    given = dict(locals())
    p = {n: given[n] for n in WEIGHTS}
    mom_m = {n: given["m_" + n] for n in WEIGHTS}
    mom_v = {n: given["v_" + n] for n in WEIGHTS}
    return _step(p, mom_m, mom_v, x, c, positions, loss_target)
```

```python
import functools
import math

import jax
import jax.numpy as jnp
from jax import lax
from jax.experimental import pallas as pl
from jax.experimental.pallas import tpu as pltpu

F32 = jnp.float32
BF16 = jnp.bfloat16

N_DEV = 8
LANES = 128
FLAT_W = 1024
VMEM_LIMIT = 48 * 1024 * 1024
MM_VMEM_BUDGET = 36 * 1024 * 1024
QK_NOPE, QK_ROPE, V_DIM = 64, 32, 64
HEAD_PAD = LANES
ROPE_BASE = 10000.0
EPS = 1e-6
ADAM_LR, ADAM_B1, ADAM_B2, ADAM_EPS, ADAM_WD, ADAM_STEP = 0.001, 0.9, 0.999, 1e-08, 0.01, 10
NEG = float(jnp.finfo(jnp.float32).min)

WEIGHTS = ['w_ada', 'b_ada', 'norm1_g', 'w_in', 'ssm_a_re', 'ssm_a_im', 'ssm_log_dt', 'ssm_b_re', 'ssm_b_im',
           'ssm_c_re', 'ssm_c_im', 'ssm_d', 'w_glu', 'b_glu', 'w_a_out', 'q_norm_g', 'w_uq', 'kv_norm_g', 'w_uk',
           'w_uv', 'w_b_out', 'w_out', 'norm2_g', 'w_gate', 'w_up', 'w_down', 'final_g']
COL_SHARDED = ['w_in', 'w_a_out', 'w_uq', 'w_uk', 'w_uv', 'w_b_out', 'w_gate', 'w_up']
ROW_SHARDED = ['w_glu', 'w_out', 'w_down']
SHARDED = COL_SHARDED + ROW_SHARDED
SMALL = ['b_ada', 'norm1_g', 'ssm_a_re', 'ssm_a_im', 'ssm_log_dt', 'ssm_b_re', 'ssm_b_im', 'ssm_c_re', 'ssm_c_im',
         'ssm_d', 'b_glu', 'q_norm_g', 'kv_norm_g', 'norm2_g', 'final_g']


def _cparams(sem):
    return pltpu.CompilerParams(dimension_semantics=sem, vmem_limit_bytes=VMEM_LIMIT)


def _pick(dim, pref, quantum=LANES):
    if dim <= pref:
        return dim
    t = (pref // quantum) * quantum
    while t >= quantum:
        if dim % t == 0:
            return t
        t -= quantum
    return dim


def _mm(name, a, b, *, ta=False, out_dtype=F32, a_col=None, b_col=None, a_fn=None, keep_a=None, tm=1408, tn=1408,
        tk=None):
    if a_fn is None:
        a_off, a_w = a_col if a_col is not None else (0, a.shape[1])
        a_parts = [(a, a_off, a_w)]
    else:
        a_parts = a
        a_w = a_parts[0][2]
        assert all(w == a_w for _, _, w in a_parts)
    a0 = a_parts[0][0]
    b_off, b_w = b_col if b_col is not None else (0, b.shape[1])
    if ta:
        kdim, m = a0.shape[0], a_w
        assert b.shape[0] == kdim
    else:
        m, kdim = a0.shape[0], a_w
        assert b.shape[0] == kdim, (name, a0.shape, b.shape)
    n = b_w
    tm = _pick(m, tm, LANES if ta else 8)
    tn = _pick(n, tn)
    a_bytes = sum(arr.dtype.itemsize for arr, _, _ in a_parts) + (8 if a_fn is not None else 0)
    if tk is None:
        def fits(tm_, tk_):
            return (tm_ * tn * (2 * jnp.dtype(out_dtype).itemsize + 4)
                    + tk_ * (2 * (tm_ * a_bytes + tn * b.dtype.itemsize) + 2 * (tm_ + tn))) <= MM_VMEM_BUDGET

        tk = kdim
        while not fits(tm, tk):
            if not ta and tm % 16 == 0 and tm >= 1024 and fits(tm // 2, tk):
                tm //= 2
                break
            if tk % (2 * (8 if ta else LANES)):
                break
            tk //= 2
    tk = _pick(kdim, tk, 8 if ta else LANES)
    nk = kdim // tk
    assert m % tm == 0 and n % tn == 0 and kdim % tk == 0, (name, m, n, kdim, tm, tn, tk)
    a_specs = []
    for _, off, _ in a_parts:
        if ta:
            assert off % tm == 0
            a_specs.append(pl.BlockSpec((tk, tm), functools.partial(lambda i, j, k, o: (k, i + o), o=off // tm)))
        else:
            assert off % tk == 0
            a_specs.append(pl.BlockSpec((tm, tk), functools.partial(lambda i, j, k, o: (i, k + o), o=off // tk)))
    dims = _TN if ta else (((1,), (0,)), ((), ()))
    assert b_off % tn == 0
    b_spec = pl.BlockSpec((tk, tn), lambda i, j, k: (k, j + b_off // tn))
    na = len(a_parts)

    def prod(refs):
        if a_fn is None:
            av = refs[0][...]
        else:
            av = a_fn(*[r[...].astype(F32) for r in refs[:na]])
        return lax.dot_general(av.astype(BF16), refs[na][...].astype(BF16), dims, preferred_element_type=F32)

    def body_one(*refs):
        refs[na + 1][...] = prod(refs).astype(refs[na + 1].dtype)

    def body_keep(*refs):
        av = a_fn(*[r[...].astype(F32) for r in refs[:na]]).astype(keep_a)
        refs[na + 2][...] = av
        refs[na + 1][...] = lax.dot_general(av.astype(BF16), refs[na][...].astype(BF16), dims,
                                            preferred_element_type=F32).astype(refs[na + 1].dtype)

    def body_acc(*refs):
        o_ref, acc_ref = refs[na + 1], refs[na + 2]

        @pl.when(pl.program_id(2) == 0)
        def _():
            acc_ref[...] = jnp.zeros_like(acc_ref)

        acc_ref[...] += prod(refs)

        @pl.when(pl.program_id(2) == nk - 1)
        def _():
            o_ref[...] = acc_ref[...].astype(o_ref.dtype)

    out_shape = jax.ShapeDtypeStruct((m, n), out_dtype)
    out_specs = pl.BlockSpec((tm, tn), lambda i, j, k: (i, j))
    if keep_a is not None:
        assert a_fn is not None and not ta and nk == 1 and n == tn, (name, nk, n, tn)
        out_shape = [out_shape, jax.ShapeDtypeStruct((m, kdim), keep_a)]
        out_specs = [out_specs, pl.BlockSpec((tm, tk), lambda i, j, k: (i, k))]
    return pl.pallas_call(
        body_keep if keep_a is not None else (body_one if nk == 1 else body_acc), name=name,
        out_shape=out_shape,
        grid=(m // tm, n // tn, nk),
        in_specs=a_specs + [b_spec],
        out_specs=out_specs,
        scratch_shapes=[] if nk == 1 else [pltpu.VMEM((tm, tn), F32)],
        compiler_params=_cparams(("parallel", "parallel", "arbitrary")),
    )(*[arr for arr, _, _ in a_parts], b)


def _rowwise(name, fn, rows, vecs, outs, reds=(), *, tm=512):
    rows = [(r, 0, r.shape[1]) if not isinstance(r, tuple) else r for r in rows]
    nrows = rows[0][0].shape[0]
    tm = _pick(nrows, tm, 8)
    assert nrows % tm == 0, (name, nrows, tm)
    nr, nv, no = len(rows), len(vecs), len(outs)
    in_specs = []
    for arr, off, w in rows:
        assert arr.shape[0] == nrows and off % w == 0, (name, arr.shape, off, w)
        in_specs.append(pl.BlockSpec((tm, w), functools.partial(lambda i, cb: (i, cb), cb=off // w)))
    for v in vecs:
        assert v.ndim == 2 and v.shape[0] == 1, (name, v.shape)
        in_specs.append(pl.BlockSpec(v.shape, lambda i: (0, 0)))
    out_shape = [jax.ShapeDtypeStruct((nrows, w), dt) for w, dt in outs]
    out_specs = [pl.BlockSpec((tm, w), lambda i: (i, 0)) for w, dt in outs]
    out_shape += [jax.ShapeDtypeStruct((1, w), F32) for w in reds]
    out_specs += [pl.BlockSpec((1, w), lambda i: (0, 0)) for w in reds]

    def body(*refs):
        rin, vin = refs[:nr], refs[nr:nr + nv]
        rout, rred = refs[nr + nv:nr + nv + no], refs[nr + nv + no:]
        res = fn(*[r[...].astype(F32) for r in rin], *[v[...] for v in vin])
        if not isinstance(res, (tuple, list)):
            res = (res,)
        assert len(res) == no + len(reds), (name, len(res))
        for r, val in zip(rout, res[:no]):
            r[...] = val.astype(r.dtype)
        if reds:
            @pl.when(pl.program_id(0) == 0)
            def _():
                for r in rred:
                    r[...] = jnp.zeros_like(r)

            for r, val in zip(rred, res[no:]):
                r[...] += val.astype(F32)

    res = pl.pallas_call(
        body, name=name, out_shape=out_shape, grid=(nrows // tm,),
        in_specs=in_specs, out_specs=out_specs,
        compiler_params=_cparams(("arbitrary",) if reds else ("parallel",)),
    )(*[r[0] for r in rows], *vecs)
    return res


def _f32(x):
    return x.astype(F32)


def _vjp_fn(f, n_in, n_cot, want):
    def fn(*args):
        ins = [_f32(a) for a in args[:n_in]]
        cots = tuple(_f32(a) for a in args[n_in:n_in + n_cot])
        _, vjp = jax.vjp(f, *ins)
        grads = vjp(cots if n_cot > 1 else cots[0])
        return tuple(grads[i] for i in want)
    return fn


def _f_rms(x, g):
    return (x * lax.rsqrt(jnp.mean(x * x, axis=-1, keepdims=True) + EPS)) * g


def _f_norm_mod(x, g, sc, sh):
    return _f_rms(x, g) * (1.0 + sc) + sh


def _f_gelu_in(ych, u, d):
    return jax.nn.gelu(ych + d * u)


def _f_glu(yg, pre, b):
    return yg * jax.nn.sigmoid(pre + b)


def _f_merge(ga, gb, ya, yb):
    return jax.nn.sigmoid(ga) * ya + jax.nn.sigmoid(gb) * yb


def _f_res(x, t, g):
    return x + g * t


def _f_swiglu(a, b):
    return jax.nn.silu(a) * b


def _f_rope_q(qa, qb, cos, sin):
    h = qa.shape[1] // HEAD_PAD
    return qa * jnp.tile(cos, (1, h)) + qb * jnp.tile(sin, (1, h))


def _f_rope_k(kn, kra, krb, cos, sin):
    h = kn.shape[1] // HEAD_PAD
    return kn + jnp.tile(kra * cos + krb * sin, (1, h))


SUBLANES = 8
S5_SUPER = 8


def _cpow(ar, ai, log2n):
    for _ in range(log2n):
        ar, ai = ar * ar - ai * ai, 2.0 * ar * ai
    return ar, ai


def _to_segments(x):
    L, w = x.shape
    return x.reshape(SUBLANES, L // SUBLANES, w).transpose(1, 0, 2).reshape(L, w)


def _from_segments(x):
    L, w = x.shape
    return x.reshape(L // SUBLANES, SUBLANES, w).transpose(1, 0, 2).reshape(L, w)


def _s5_pass(name, xin, a_re, a_im, w_in, *, reverse, nsup, sc, ends=None, w_out=None, u=None, h=None, tb=512):
    L, sw = xin.shape
    gp2 = nsup * 2 * sc
    gp = gp2 // 2
    seg = L // SUBLANES
    assert sw == nsup * LANES and L % SUBLANES == 0 and seg & (seg - 1) == 0, (L, sw)
    tb = _pick(L, tb, SUBLANES)
    nb, nt = L // tb, tb // SUBLANES
    first_pass = ends is None
    sign = -1.0 if reverse else 1.0
    blk = (lambda i: (nb - 1 - i, 0)) if reverse else (lambda i: (i, 0))
    order = (lambda s: nt - 1 - s) if reverse else (lambda s: s)
    tile = lambda s: pl.ds(pl.multiple_of(s * SUBLANES, SUBLANES), SUBLANES)
    const = lambda shape: pl.BlockSpec(shape, lambda i: (0, 0))
    rows_in = pl.BlockSpec((tb, sw), blk)
    rows_st = pl.BlockSpec((tb, gp2), blk)

    def coeffs(ar_ref, ai_ref, cc):
        ar1, ai1 = ar_ref[:, cc], sign * ai_ref[:, cc]
        return ar1, ai1, jnp.broadcast_to(ar1, (SUBLANES, sc)), jnp.broadcast_to(ai1, (SUBLANES, sc))

    def drive(x_ref, w_ref, xs_ref, c):
        lanes = pl.ds(c * LANES, LANES)
        xs_ref[...] = jnp.dot(x_ref[:, lanes].astype(BF16), w_ref[lanes, :], preferred_element_type=F32)

    def body(*refs):
        it = iter(refs)
        x_ref, w_ref, ar_ref, ai_ref = next(it), next(it), next(it), next(it)
        if first_pass:
            e_ref, xs_ref = next(it), next(it)
        elif not reverse:
            e_ref, wo_ref, o_ref, y_ref, st_ref, xs_ref = (next(it) for _ in range(6))
        else:
            (e_ref, wo_ref, u_ref, h_ref, hp_ref, hl_ref, du_ref, gb_ref, gc_ref, dar_ref, dai_ref,
             st_ref, xs_ref, g_ref) = (next(it) for _ in range(14))
        i = pl.program_id(0)

        @pl.when(i == 0)
        def _():
            if first_pass:
                e_ref[...] = jnp.zeros_like(e_ref)
                return
            for c in range(nsup):
                cc, re, im = pl.ds(c * sc, sc), pl.ds(2 * c * sc, sc), pl.ds((2 * c + 1) * sc, sc)
                ar1, ai1, _, _ = coeffs(ar_ref, ai_ref, cc)
                pr, pi = _cpow(ar1, ai1, seg.bit_length() - 1)
                cr = jnp.zeros((1, sc), F32)
                ci = jnp.zeros((1, sc), F32)
                for j in (reversed(range(SUBLANES)) if reverse else range(SUBLANES)):
                    st_ref[j:j + 1, re] = cr
                    st_ref[j:j + 1, im] = ci
                    cr, ci = (e_ref[j:j + 1, re] + pr * cr - pi * ci, e_ref[j:j + 1, im] + pr * ci + pi * cr)
            if reverse:
                for r in (gb_ref, gc_ref, dar_ref, dai_ref):
                    r[...] = jnp.zeros_like(r)

        sub = lax.broadcasted_iota(jnp.int32, (SUBLANES, sc), 0)
        state = e_ref if first_pass else st_ref
        xr, xi = pl.ds(0, sc), pl.ds(sc, sc)
        for c in range(nsup):
            cc, re, im = pl.ds(c * sc, sc), pl.ds(2 * c * sc, sc), pl.ds((2 * c + 1) * sc, sc)
            both = pl.ds(2 * c * sc, 2 * sc)
            lanes = pl.ds(c * LANES, LANES)
            _, _, ar, ai = coeffs(ar_ref, ai_ref, cc)
            drive(x_ref, w_ref, xs_ref, c)

            def advance(rows, sr, si):
                return ar * sr - ai * si + xs_ref[rows, xr], ar * si + ai * sr + xs_ref[rows, xi]

            if first_pass:
                def step(s, st):
                    return advance(tile(order(s)), *st)

                sr, si = lax.fori_loop(0, nt, step, (state[:, re], state[:, im]), unroll=4)
            elif not reverse:
                def step(s, st):
                    rows = tile(s)
                    sr, si = advance(rows, *st)
                    o_ref[rows, re] = sr
                    o_ref[rows, im] = si
                    return sr, si

                sr, si = lax.fori_loop(0, nt, step, (state[:, re], state[:, im]), unroll=4)
                y_ref[:, lanes] = jnp.dot(o_ref[:, both].astype(BF16), wo_ref[both, :], preferred_element_type=F32)
            else:
                def emit(rows, sr, si):
                    sr, si = advance(rows, sr, si)
                    g_ref[rows, xr] = sr
                    g_ref[rows, xi] = si
                    return sr, si

                def grad(sr, si, hpr, hpi, accr, acci):
                    return accr + sr * hpr + si * hpi, acci + si * hpr - sr * hpi

                def step(s, st):
                    sr, si, accr, acci = st
                    t = nt - 1 - s
                    sr, si = emit(tile(t), sr, si)
                    prev = tile(t - 1)
                    return (sr, si) + grad(sr, si, h_ref[prev, re], h_ref[prev, im], accr, acci)

                zero = jnp.zeros((SUBLANES, sc), F32)
                sr, si, accr, acci = lax.fori_loop(0, nt - 1, step, (state[:, re], state[:, im], zero, zero),
                                                   unroll=4)
                sr, si = emit(tile(0), sr, si)
                first = (i == nb - 1)
                wrap_r = jnp.where(sub == 0, 0.0, pltpu.roll(hl_ref[:, re], 1, 0))
                wrap_i = jnp.where(sub == 0, 0.0, pltpu.roll(hl_ref[:, im], 1, 0))
                accr, acci = grad(sr, si, jnp.where(first, wrap_r, hp_ref[:, re]),
                                  jnp.where(first, wrap_i, hp_ref[:, im]), accr, acci)
                dar_ref[:, cc] += jnp.sum(accr, axis=0, keepdims=True)
                dai_ref[:, cc] += jnp.sum(acci, axis=0, keepdims=True)
                gb = g_ref[...].astype(BF16)
                du_ref[:, lanes] = jnp.dot(gb, wo_ref[both, :], preferred_element_type=F32)
                gb_ref[lanes, :] += lax.dot_general(u_ref[:, lanes].astype(BF16), gb, _TN,
                                                    preferred_element_type=F32)
                gc_ref[both, :] += lax.dot_general(h_ref[:, both].astype(BF16), x_ref[:, lanes].astype(BF16), _TN,
                                                   preferred_element_type=F32)
            state[:, re] = sr
            state[:, im] = si

    vec = const((1, gp))
    in_specs = [rows_in, const(w_in.shape), vec, vec]
    operands = [xin, w_in, a_re, a_im]
    xs_scratch = pltpu.VMEM((tb, 2 * sc), F32)
    st_scratch = pltpu.VMEM((SUBLANES, gp2), F32)
    if first_pass:
        out_shape = jax.ShapeDtypeStruct((SUBLANES, gp2), F32)
        out_specs = const((SUBLANES, gp2))
        scratch = [xs_scratch]
    elif not reverse:
        in_specs += [const((SUBLANES, gp2)), const(w_out.shape)]
        operands += [ends, w_out]
        out_shape = [jax.ShapeDtypeStruct((L, gp2), F32), jax.ShapeDtypeStruct((L, sw), F32)]
        out_specs = [rows_st, rows_in]
        scratch = [st_scratch, xs_scratch]
    else:
        in_specs += [const((SUBLANES, gp2)), const(w_out.shape), rows_in, rows_st,
                     pl.BlockSpec((SUBLANES, gp2), lambda i: (jnp.maximum((nb - 1 - i) * nt - 1, 0), 0)),
                     pl.BlockSpec((SUBLANES, gp2), lambda i: (seg - 1, 0))]
        operands += [ends, w_out, u, h, h, h]
        out_shape = [jax.ShapeDtypeStruct((L, sw), F32), jax.ShapeDtypeStruct((sw, 2 * sc), F32),
                     jax.ShapeDtypeStruct((gp2, LANES), F32), jax.ShapeDtypeStruct((1, gp), F32),
                     jax.ShapeDtypeStruct((1, gp), F32)]
        out_specs = [rows_in, const((sw, 2 * sc)), const((gp2, LANES)), vec, vec]
        scratch = [st_scratch, xs_scratch, pltpu.VMEM((tb, 2 * sc), F32)]
    return pl.pallas_call(
        body, name=name, out_shape=out_shape, grid=(nb,), in_specs=in_specs, out_specs=out_specs,
        scratch_shapes=scratch, compiler_params=_cparams(("arbitrary",)),
    )(*operands)


_NT = (((1,), (1,)), ((), ()))
_TN = (((0,), (0,)), ((), ()))


def _causal(s, t, k_major=False):
    row = lax.broadcasted_iota(jnp.int32, (t, t), 0)
    col = lax.broadcasted_iota(jnp.int32, (t, t), 1)
    return jnp.where(row <= col if k_major else col <= row, s, NEG)


def _pair_tables(n, k_major):
    if k_major:
        pairs = [(qi, ki) for ki in range(n) for qi in range(ki, n)]
    else:
        pairs = [(qi, ki) for qi in range(n) for ki in range(qi + 1)]
    return (jnp.asarray([p[0] for p in pairs], jnp.int32), jnp.asarray([p[1] for p in pairs], jnp.int32))


def _flash_fwd(q, k, knv, *, heads, t):
    L = q.shape[0]
    n = L // t
    rep = t // LANES
    qtab, ktab = _pair_tables(n, k_major=False)

    def body(qt_ref, kt_ref, q_ref, k_ref, v_ref, o_ref, lse_ref, m_s, acc_s):
        step = pl.program_id(1)
        qi, ki = qt_ref[step], kt_ref[step]
        lane = lax.broadcasted_iota(jnp.int32, (t, HEAD_PAD), 1)

        @pl.when(ki == 0)
        def _():
            m_s[...] = jnp.full(m_s.shape, NEG, F32)
            acc_s[...] = jnp.zeros_like(acc_s)

        def update(diagonal):
            s = lax.dot_general(q_ref[...], k_ref[...], _NT, preferred_element_type=F32)
            if diagonal:
                s = _causal(s, t)
            m_prev = m_s[...]
            m_next = jnp.maximum(m_prev, jnp.max(s, axis=1, keepdims=True))
            p = jnp.exp(s - jnp.tile(m_next, (1, rep)))
            vb = jnp.where(lane == V_DIM, jnp.ones((), BF16), v_ref[...])
            acc_s[...] = jnp.exp(m_prev - m_next) * acc_s[...] + jnp.dot(p.astype(BF16), vb,
                                                                         preferred_element_type=F32)
            m_s[...] = m_next

        @pl.when(ki < qi)
        def _():
            update(False)

        @pl.when(ki == qi)
        def _():
            update(True)
            acc = acc_s[...]
            l = jnp.sum(jnp.where(lane == V_DIM, acc, 0.0), axis=1, keepdims=True)
            o_ref[...] = jnp.where(lane == V_DIM, 0.0, acc * (1.0 / l)).astype(o_ref.dtype)
            lse_ref[0] = jnp.max(m_s[...], axis=1, keepdims=True) + jnp.log(l)

    q_map = lambda h, s, qt, kt: (qt[s], h)
    kv_map = lambda h, s, qt, kt: (kt[s], h)
    v_map = lambda h, s, qt, kt: (kt[s], h + heads)
    return pl.pallas_call(
        body, name="mla_flash_fwd",
        out_shape=[jax.ShapeDtypeStruct((L, heads * HEAD_PAD), BF16),
                   jax.ShapeDtypeStruct((heads, L, 1), F32)],
        grid_spec=pltpu.PrefetchScalarGridSpec(
            num_scalar_prefetch=2, grid=(heads, qtab.shape[0]),
            in_specs=[pl.BlockSpec((t, HEAD_PAD), q_map),
                      pl.BlockSpec((t, HEAD_PAD), kv_map),
                      pl.BlockSpec((t, HEAD_PAD), v_map)],
            out_specs=[pl.BlockSpec((t, HEAD_PAD), q_map),
                       pl.BlockSpec((1, t, 1), lambda h, s, qt, kt: (h, qt[s], 0))],
            scratch_shapes=[pltpu.VMEM((t, LANES), F32), pltpu.VMEM((t, HEAD_PAD), F32)]),
        compiler_params=_cparams(("parallel", "arbitrary")),
    )(qtab, ktab, q, k, knv)


def _flash_delta(do, o, *, heads, t):
    L = do.shape[0]

    hw = heads * HEAD_PAD

    def body(do_ref, o_ref, d_ref):
        prod = do_ref[...].astype(F32) * o_ref[...].astype(F32)
        for h in range(heads):
            d_ref[h] = jnp.sum(prod[:, h * HEAD_PAD:(h + 1) * HEAD_PAD], axis=1, keepdims=True)

    return pl.pallas_call(
        body, name="mla_flash_delta",
        out_shape=jax.ShapeDtypeStruct((heads, L, 1), F32),
        grid=(L // t,),
        in_specs=[pl.BlockSpec((t, hw), lambda i: (i, 0)),
                  pl.BlockSpec((t, hw), lambda i: (i, 0))],
        out_specs=pl.BlockSpec((heads, t, 1), lambda i: (0, i, 0)),
        compiler_params=_cparams(("parallel",)),
    )(do, o)


def _flash_bwd(q, k, knv, do, lse, delta, *, heads, t):
    L = q.shape[0]
    n = L // t
    qtab, ktab = _pair_tables(n, k_major=True)

    def body(qt_ref, kt_ref, q_ref, k_ref, v_ref, do_ref, lse_ref, dl_ref, dq_ref, dk_ref, dv_ref, dk_s, dv_s):
        step = pl.program_id(1)
        qi, ki = qt_ref[step], kt_ref[step]

        @pl.when(qi == ki)
        def _():
            dk_s[...] = jnp.zeros_like(dk_s)
            dv_s[...] = jnp.zeros_like(dv_s)

        def update(diagonal):
            qb, kb, vb, dob = q_ref[...], k_ref[...], v_ref[...], do_ref[...]
            st = lax.dot_general(kb, qb, _NT, preferred_element_type=F32)
            if diagonal:
                st = _causal(st, t, k_major=True)
            pt = jnp.exp(st - lse_ref[0])
            dv_s[...] += jnp.dot(pt.astype(BF16), dob, preferred_element_type=F32)
            dpt = lax.dot_general(vb, dob, _NT, preferred_element_type=F32)
            dst = (pt * (dpt - dl_ref[0])).astype(BF16)
            dk_s[...] += jnp.dot(dst, qb, preferred_element_type=F32)
            dqb = lax.dot_general(dst, kb, _TN, preferred_element_type=F32)
            rows = pl.ds(pl.multiple_of(qi * t, t), t)

            @pl.when(ki == 0)
            def _():
                dq_ref[rows, :] = dqb

            @pl.when(ki > 0)
            def _():
                dq_ref[rows, :] += dqb

        @pl.when(qi > ki)
        def _():
            update(False)

        @pl.when(qi == ki)
        def _():
            update(True)

        @pl.when(qi == n - 1)
        def _():
            dk_ref[...] = dk_s[...].astype(dk_ref.dtype)
            dv_ref[...] = dv_s[...].astype(dv_ref.dtype)

    q_map = lambda h, s, qt, kt: (qt[s], h)
    stat_map = lambda h, s, qt, kt: (h, 0, qt[s])
    kv_map = lambda h, s, qt, kt: (kt[s], h)
    v_map = lambda h, s, qt, kt: (kt[s], h + heads)
    return pl.pallas_call(
        body, name="mla_flash_bwd",
        out_shape=[jax.ShapeDtypeStruct((L, heads * HEAD_PAD), F32),
                   jax.ShapeDtypeStruct((L, heads * HEAD_PAD), BF16),
                   jax.ShapeDtypeStruct((L, heads * HEAD_PAD), BF16)],
        grid_spec=pltpu.PrefetchScalarGridSpec(
            num_scalar_prefetch=2, grid=(heads, qtab.shape[0]),
            in_specs=[pl.BlockSpec((t, HEAD_PAD), q_map),
                      pl.BlockSpec((t, HEAD_PAD), kv_map),
                      pl.BlockSpec((t, HEAD_PAD), v_map),
                      pl.BlockSpec((t, HEAD_PAD), q_map),
                      pl.BlockSpec((1, 1, t), stat_map),
                      pl.BlockSpec((1, 1, t), stat_map)],
            out_specs=[pl.BlockSpec((L, HEAD_PAD), lambda h, s, qt, kt: (0, h)),
                       pl.BlockSpec((t, HEAD_PAD), kv_map),
                       pl.BlockSpec((t, HEAD_PAD), kv_map)],
            scratch_shapes=[pltpu.VMEM((t, HEAD_PAD), F32), pltpu.VMEM((t, HEAD_PAD), F32)]),
        compiler_params=_cparams(("parallel", "arbitrary")),
    )(qtab, ktab, q, k, knv, do, lse, delta)


def _peer(k):
    mx, my, mc = lax.axis_index("x"), lax.axis_index("y"), lax.axis_index("c")
    px = 1 - mx if (k >> 2) & 1 else mx
    py = 1 - my if (k >> 1) & 1 else my
    pc = 1 - mc if k & 1 else mc
    return (px, py, pc), 4 * px + 2 * py + pc


def _exchange(name, xs, all_to_all):
    n = len(xs)
    any_spec = pl.BlockSpec(memory_space=pl.ANY)
    npeer = N_DEV - 1

    def body(*refs):
        x_refs, o_refs = refs[:n], refs[n:2 * n]
        send_sems, recv_sems, local_sems = refs[2 * n:]
        _, me = _peer(0)
        mine = [x.at[me] if all_to_all else x for x in x_refs]
        local = [pltpu.make_async_copy(mine[i], o_refs[i].at[me], local_sems.at[i]) for i in range(n)]
        for cp in local:
            cp.start()
        sends = []
        for k in range(1, N_DEV):
            dev, idx = _peer(k)
            for i in range(n):
                cp = pltpu.make_async_remote_copy(
                    src_ref=x_refs[i].at[idx] if all_to_all else x_refs[i], dst_ref=o_refs[i].at[me],
                    send_sem=send_sems.at[i * npeer + k - 1], recv_sem=recv_sems.at[i * npeer + k - 1],
                    device_id=dev, device_id_type=pl.DeviceIdType.MESH)
                cp.start()
                sends.append(cp)
        for k in range(1, N_DEV):
            dev, idx = _peer(k)
            for i in range(n):
                pltpu.make_async_remote_copy(
                    src_ref=mine[i], dst_ref=o_refs[i].at[idx],
                    send_sem=send_sems.at[i * npeer + k - 1], recv_sem=recv_sems.at[i * npeer + k - 1],
                    device_id=dev, device_id_type=pl.DeviceIdType.MESH).wait_recv()
        for cp in sends:
            cp.wait_send()
        for cp in local:
            cp.wait()

    return pl.pallas_call(
        body, name=name,
        out_shape=[jax.ShapeDtypeStruct((N_DEV,) + tuple(x.shape[1:] if all_to_all else x.shape), x.dtype)
                   for x in xs],
        in_specs=[any_spec] * n, out_specs=[any_spec] * n,
        scratch_shapes=[pltpu.SemaphoreType.DMA((n * npeer,)), pltpu.SemaphoreType.DMA((n * npeer,)),
                        pltpu.SemaphoreType.DMA((n,))],
    )(*xs)


def _gather_two_level(name, xs):
    n = len(xs)
    any_spec = pl.BlockSpec(memory_space=pl.ANY)
    nslot = N_DEV - 1
    chips = (2, 4, 6)

    def body(*refs):
        x_refs, o_refs = refs[:n], refs[n:2 * n]
        send_sems, recv_sems, local_sems = refs[2 * n:]
        _, me = _peer(0)
        sibling, sib_idx = _peer(1)

        def copy(i, slot, src, block, dev):
            return pltpu.make_async_remote_copy(
                src_ref=src, dst_ref=o_refs[i].at[block],
                send_sem=send_sems.at[i * nslot + slot], recv_sem=recv_sems.at[i * nslot + slot],
                device_id=dev, device_id_type=pl.DeviceIdType.MESH)

        local = [pltpu.make_async_copy(x_refs[i], o_refs[i].at[me], local_sems.at[i]) for i in range(n)]
        for cp in local:
            cp.start()
        sends = []
        for i in range(n):
            sends.append(copy(i, 0, x_refs[i], me, sibling))
            for j, k in enumerate(chips):
                sends.append(copy(i, 1 + j, x_refs[i], me, _peer(k)[0]))
        for cp in sends:
            cp.start()
        for j, k in enumerate(chips):
            _, idx = _peer(k)
            for i in range(n):
                copy(i, 1 + j, x_refs[i], idx, _peer(k)[0]).wait_recv()
                fwd = copy(i, 4 + j, o_refs[i].at[idx], idx, sibling)
                fwd.start()
                sends.append(fwd)
        for i in range(n):
            copy(i, 0, x_refs[i], sib_idx, sibling).wait_recv()
        for j, k in enumerate(chips):
            _, idx = _peer(k | 1)
            for i in range(n):
                copy(i, 4 + j, x_refs[i], idx, sibling).wait_recv()
        for cp in sends:
            cp.wait_send()
        for cp in local:
            cp.wait()

    return pl.pallas_call(
        body, name=name,
        out_shape=[jax.ShapeDtypeStruct((N_DEV,) + tuple(x.shape), x.dtype) for x in xs],
        in_specs=[any_spec] * n, out_specs=[any_spec] * n,
        scratch_shapes=[pltpu.SemaphoreType.DMA((n * nslot,)), pltpu.SemaphoreType.DMA((n * nslot,)),
                        pltpu.SemaphoreType.DMA((n,))],
    )(*xs)


def _sum_slabs(name, x, *, tr=128):
    _, r, w = x.shape
    tr = _pick(r, tr, 8)

    def body(x_ref, o_ref):
        acc = x_ref[0]
        for j in range(1, N_DEV):
            acc = acc + x_ref[j]
        o_ref[...] = acc

    return pl.pallas_call(
        body, name=name, out_shape=jax.ShapeDtypeStruct((r, w), F32), grid=(r // tr,),
        in_specs=[pl.BlockSpec((N_DEV, tr, w), lambda i: (0, i, 0))],
        out_specs=pl.BlockSpec((tr, w), lambda i: (i, 0)),
        compiler_params=_cparams(("parallel",)),
    )(x)


def _adamw_fn(w, g, m, v):
    m = ADAM_B1 * m + (1.0 - ADAM_B1) * g
    v = ADAM_B2 * v + (1.0 - ADAM_B2) * jnp.square(g)
    m_hat = m / (1.0 - ADAM_B1 ** ADAM_STEP)
    v_hat = v / (1.0 - ADAM_B2 ** ADAM_STEP)
    delta = -ADAM_LR * (m_hat / (jnp.sqrt(v_hat) + ADAM_EPS) + ADAM_WD * w)
    return delta, m, v


def _adamw_sharded(name, w, recv, m, v, *, tr=128):
    rows, c = w.shape
    tr = _pick(rows, tr, 8)

    def body(w_ref, r_ref, m_ref, v_ref, g_ref, d_ref, mo_ref, vo_ref):
        g = r_ref[0].astype(F32)
        for j in range(1, N_DEV):
            g = g + r_ref[j].astype(F32)
        d, mn, vn = _adamw_fn(w_ref[...], g, m_ref[...], v_ref[...])
        g_ref[...] = g
        d_ref[...] = d
        mo_ref[...] = mn
        vo_ref[...] = vn

    blk = pl.BlockSpec((tr, c), lambda i: (i, 0))
    return pl.pallas_call(
        body, name=name, out_shape=[jax.ShapeDtypeStruct((rows, c), F32)] * 4, grid=(rows // tr,),
        in_specs=[blk, pl.BlockSpec((N_DEV, tr, c), lambda i: (0, i, 0)), blk, blk],
        out_specs=[blk] * 4,
        compiler_params=_cparams(("parallel",)),
    )(w, recv, m, v)


def _piece_rows(shape):
    return -(-math.prod(shape) // (8 * FLAT_W)) * 8


def _pack_rows(arrs):
    out = []
    for a in arrs:
        flat = a.reshape(-1)
        rows = _piece_rows(a.shape)
        out.append(jnp.pad(flat, (0, rows * FLAT_W - flat.shape[0])).reshape(rows, FLAT_W))
    return jnp.concatenate(out, axis=0)


def _unpack_rows(packed, shapes):
    out, r0 = [], 0
    for s in shapes:
        rows = _piece_rows(s)
        out.append(packed[r0:r0 + rows].reshape(-1)[:math.prod(s)].reshape(s))
        r0 += rows
    return out


def _pack_slabs(a):
    n = a.shape[1]
    rows = _piece_rows((n,))
    return jnp.pad(a, ((0, 0), (0, rows * FLAT_W - n))).reshape(N_DEV, rows, FLAT_W)


def _unpack_slabs(a, n):
    return a.reshape(N_DEV, -1)[:, :n]


def _s5_operators(a_re, a_im, log_dt, b_re, b_im, c_re, c_im):
    g, p, m = b_re.shape
    dt = jnp.exp(log_dt)[:, None]
    mag = jnp.exp(a_re * dt)
    abar_re = mag * jnp.cos(a_im * dt)
    abar_im = mag * jnp.sin(a_im * dt)
    den = a_re * a_re + a_im * a_im
    nr = abar_re - 1.0
    ni = abar_im
    coef_re = ((nr * a_re + ni * a_im) / den)[..., None]
    coef_im = ((ni * a_re - nr * a_im) / den)[..., None]
    bbar_re = coef_re * b_re - coef_im * b_im
    bbar_im = coef_re * b_im + coef_im * b_re
    sup, ns = S5_SUPER, g // S5_SUPER
    eye = jnp.eye(sup, dtype=F32)

    def b_blocks(bb):
        return jnp.einsum('cgpm,gh->cgmhp', bb.reshape(ns, sup, p, m), eye).reshape(ns, sup * m, sup * p)

    def c_blocks(cb):
        return jnp.einsum('cgmp,gh->chpgm', cb.reshape(ns, sup, m, p), eye).reshape(ns, sup * p, sup * m)

    bmat = jnp.concatenate([b_blocks(bbar_re), b_blocks(bbar_im)], axis=2).reshape(g * m, 2 * sup * p)
    cmat = jnp.concatenate([c_blocks(c_re), c_blocks(-c_im)], axis=1).reshape(ns * 2 * sup * p, sup * m)
    return abar_re.reshape(1, g * p), abar_im.reshape(1, g * p), bmat, cmat


def _rot_cols(w):
    half = w.shape[-1] // 2
    return jnp.concatenate([-w[..., half:], w[..., :half]], axis=-1)


def _layer_operators(w, dims):
    d, sw, ql, kvl, heads, dff = dims['d'], dims['sw'], dims['ql'], dims['kvl'], dims['heads'], dims['dff']
    w_in = w['w_in']
    o = 0
    parts = {}
    for nm, sz in (('u', sw), ('cq', ql), ('ckv', kvl), ('kr', QK_ROPE), ('ga', d), ('gb', d)):
        parts[nm] = w_in[:, o:o + sz]
        o += sz
    zpad = lambda n: jnp.zeros((d, n), w_in.dtype)
    kra = jnp.concatenate([zpad(QK_NOPE), parts['kr'], zpad(HEAD_PAD - QK_NOPE - QK_ROPE)], axis=1)
    krb = jnp.concatenate([zpad(QK_NOPE), _rot_cols(parts['kr']), zpad(HEAD_PAD - QK_NOPE - QK_ROPE)], axis=1)
    w_in_x = jnp.concatenate([parts['ga'], parts['gb'], parts['u'], parts['cq'], parts['ckv'], kra, krb], axis=1)

    wq = w['w_uq'].reshape(ql, heads, QK_NOPE + QK_ROPE)
    qz = lambda n: jnp.zeros((ql, heads, n), wq.dtype)
    wq_a = jnp.concatenate([wq, qz(HEAD_PAD - QK_NOPE - QK_ROPE)], axis=2)
    wq_b = jnp.concatenate([qz(QK_NOPE), _rot_cols(wq[:, :, QK_NOPE:]), qz(HEAD_PAD - QK_NOPE - QK_ROPE)], axis=2)
    wq_x = jnp.concatenate([wq_a.reshape(ql, -1), wq_b.reshape(ql, -1)], axis=1)

    kz = lambda n: jnp.zeros((kvl, heads, n), w['w_uk'].dtype)
    wk = jnp.concatenate([w['w_uk'].reshape(kvl, heads, QK_NOPE), kz(HEAD_PAD - QK_NOPE)], axis=2)
    wv = jnp.concatenate([w['w_uv'].reshape(kvl, heads, V_DIM), kz(HEAD_PAD - V_DIM)], axis=2)
    wkv_x = jnp.concatenate([wk.reshape(kvl, -1), wv.reshape(kvl, -1)], axis=1)

    wbo = w['w_b_out'].reshape(heads, V_DIM, d)
    wbo_x = jnp.concatenate([wbo, jnp.zeros((heads, HEAD_PAD - V_DIM, d), wbo.dtype)], axis=1).reshape(-1, d)
    wgu = jnp.concatenate([w['w_gate'], w['w_up']], axis=1)
    return dict(w_in=w_in_x, w_glu=w['w_glu'], w_a_out=w['w_a_out'], wq=wq_x, wkv=wkv_x, wbo=wbo_x,
                w_out=w['w_out'], wgu=wgu, w_down=w['w_down'])


def _gathered_to_full(gathered):
    full = {}
    for n, pc in zip(SHARDED, gathered):
        dep, r, c = pc.shape[1:]
        if n in COL_SHARDED:
            full[n] = pc.transpose(1, 2, 0, 3).reshape(dep, r, N_DEV * c)
        else:
            full[n] = pc.transpose(1, 0, 2, 3).reshape(dep, N_DEV * r, c)
    return full


def _layer_fwd(x, mod, ops, s5, small, rope, dims):
    d, sw, ql, kvl, heads, dff = dims['d'], dims['sw'], dims['ql'], dims['kvl'], dims['heads'], dims['dff']
    zo = dims['zoff']
    hw = heads * HEAD_PAD
    cos, sin = rope
    sh1, sc1, g1, sh2, sc2, g2 = mod
    bf = lambda a: a.astype(BF16)
    sv = dict(x=x)
    (h1,) = _rowwise("norm1_fwd", _f_norm_mod, [x], [small['norm1_g'], sc1, sh1], [(d, BF16)])
    z = _mm("w_in_fwd", h1, bf(ops['w_in']), out_dtype=BF16, tn=ops['w_in'].shape[1])
    sv.update(h1=h1, z=z)
    a_re, a_im, bmat, cmat = s5
    u_seg = _to_segments(z[:, zo['u']:zo['u'] + sw])
    nsup, sc = dims['nsup'], dims['s5_chunk']
    s5kw = dict(nsup=nsup, sc=sc)
    ends = _s5_pass("s5_ends_fwd", u_seg, a_re, a_im, bf(bmat), reverse=False, **s5kw)
    hst, ych = _s5_pass("s5_scan_fwd", u_seg, a_re, a_im, bf(bmat), reverse=False, ends=ends, w_out=bf(cmat), **s5kw)
    ych = _from_segments(ych)
    (yg,) = _rowwise("s5_gelu_fwd", _f_gelu_in, [ych, (z, zo['u'], sw)], [small['ssm_d']], [(sw, F32)])
    pre = _mm("s5_glu_mm_fwd", yg, bf(ops['w_glu']))
    (s5o,) = _rowwise("s5_glu_fwd", _f_glu, [yg, pre], [small['b_glu']], [(sw, BF16)])
    ya = _mm("s5_out_fwd", s5o, bf(ops['w_a_out']), out_dtype=BF16)
    sv.update(u_seg=u_seg, hst=hst, ych=ych, yg=yg, pre=pre, s5o=s5o, ya=ya)
    (cq,) = _rowwise("q_norm_fwd", _f_rms, [(z, zo['cq'], ql)], [small['q_norm_g']], [(ql, BF16)])
    qab = _mm("q_up_fwd", cq, bf(ops['wq']))
    scale = dims['scale']
    (q,) = _rowwise("q_rope_fwd", lambda a, b, cb, sb: _f_rope_q(a, b, cb, sb) * scale,
                    [(qab, 0, hw), (qab, hw, hw), cos, sin], [], [(hw, BF16)])
    (ckv,) = _rowwise("kv_norm_fwd", _f_rms, [(z, zo['ckv'], kvl)], [small['kv_norm_g']], [(kvl, BF16)])
    knv = _mm("kv_up_fwd", ckv, bf(ops['wkv']), out_dtype=BF16)
    (k,) = _rowwise("k_rope_fwd", _f_rope_k,
                    [(knv, 0, hw), (z, zo['kra'], HEAD_PAD), (z, zo['krb'], HEAD_PAD), cos, sin], [], [(hw, BF16)])
    o, lse = _flash_fwd(q, k, knv, heads=heads, t=dims['tq'])
    yb = _mm("mla_out_fwd", o, bf(ops['wbo']), out_dtype=BF16)
    sv.update(cq=cq, q=q, ckv=ckv, knv=knv, k=k, o=o, lse=lse, yb=yb)
    mix = [(z, zo['ga'], d), (z, zo['gb'], d), (ya, 0, d), (yb, 0, d)]
    t1, merged = _mm("w_out_fwd", mix, bf(ops['w_out']), a_fn=_f_merge, keep_a=BF16)
    (x1,) = _rowwise("res1_fwd", _f_res, [x, t1], [g1], [(d, F32)])
    sv.update(merged=merged, t1=t1, x1=x1)
    (h2,) = _rowwise("norm2_fwd", _f_norm_mod, [x1], [small['norm2_g'], sc2, sh2], [(d, BF16)])
    ab = _mm("ffn_up_fwd", h2, bf(ops['wgu']), out_dtype=BF16)
    ffn_act = [(ab, 0, dff), (ab, dff, dff)]
    t2, f = _mm("ffn_down_fwd", ffn_act, bf(ops['w_down']), a_fn=_f_swiglu, keep_a=BF16, tm=256, tk=dff)
    (x2,) = _rowwise("res2_fwd", _f_res, [x1, t2], [g2], [(d, F32)])
    sv.update(h2=h2, ab=ab, f=f, t2=t2)
    return x2, sv


def _layer_bwd(dx2, sv, mod, ops, s5, small, rope, dims):
    d, sw, ql, kvl, heads, dff = dims['d'], dims['sw'], dims['ql'], dims['kvl'], dims['heads'], dims['dff']
    zo = dims['zoff']
    hw = heads * HEAD_PAD
    cos, sin = rope
    sh1, sc1, g1, sh2, sc2, g2 = mod
    a_re, a_im, bmat, cmat = s5
    z = sv['z']
    tr = lambda a: a.T.astype(BF16)
    gops, gsm = {}, {}
    dt2, dg2 = _res_bwd("res2_bwd", sv['t2'], g2, dx2)
    gops['w_down'] = _mm("ffn_down_dw", sv['f'], dt2, ta=True)
    df = _mm("ffn_down_dx", dt2, tr(ops['w_down']), out_dtype=BF16)
    ab = sv['ab']
    swiglu_vjp = _vjp_fn(_f_swiglu, 2, 1, (0, 1))
    (dab,) = _rowwise("swiglu_bwd", lambda a, b, c: jnp.concatenate(swiglu_vjp(a, b, c), axis=1),
                      [(ab, 0, dff), (ab, dff, dff), df], [], [(2 * dff, BF16)], tm=256)
    gops['wgu'] = _mm("ffn_up_dw", sv['h2'], dab, ta=True)
    dh2 = _mm("ffn_up_dx", dab, tr(ops['wgu']), out_dtype=BF16)
    dx1, dn2, dsc2, dsh2 = _norm_mod_bwd("norm2_bwd", sv['x1'], small['norm2_g'], sc2, sh2, dh2, dx2)
    gsm['norm2_g'] = dn2
    dt1, dg1 = _res_bwd("res1_bwd", sv['t1'], g1, dx1)
    gops['w_out'] = _mm("w_out_dw", sv['merged'], dt1, ta=True)
    dmerged = _mm("w_out_dx", dt1, tr(ops['w_out']), out_dtype=BF16)
    dga, dgb, dya, dyb = _rowwise(
        "merge_bwd", _vjp_fn(_f_merge, 4, 1, (0, 1, 2, 3)),
        [(z, zo['ga'], d), (z, zo['gb'], d), sv['ya'], sv['yb'], dmerged], [],
        [(d, BF16), (d, BF16), (d, BF16), (d, BF16)])
    gops['wbo'] = _mm("mla_out_dw", sv['o'], dyb, ta=True)
    do = _mm("mla_out_dx", dyb, tr(ops['wbo']), out_dtype=BF16)
    delta = _flash_delta(do, sv['o'], heads=heads, t=dims['tq'])
    as_rows = lambda a: a.reshape(heads, 1, -1)
    dq, dk, dv = _flash_bwd(sv['q'], sv['k'], sv['knv'], do, as_rows(sv['lse']), as_rows(delta),
                            heads=heads, t=dims['tq'])
    def k_bwd(dkb, cosb, sinb):
        dkb = _f32(dkb)
        dkpe = dkb[:, 0:HEAD_PAD]
        for h in range(1, heads):
            dkpe = dkpe + dkb[:, h * HEAD_PAD:(h + 1) * HEAD_PAD]
        return dkpe * cosb, dkpe * sinb
    dkra, dkrb = _rowwise("k_rope_bwd", k_bwd, [dk, cos, sin], [], [(HEAD_PAD, BF16), (HEAD_PAD, BF16)])
    dknv = jnp.concatenate([dk, dv], axis=1)
    gops['wkv'] = _mm("kv_up_dw", sv['ckv'], dknv, ta=True)
    dckv = _mm("kv_up_dx", dknv, tr(ops['wkv']))
    dckv_in, dkvg = _rms_bwd("kv_norm_bwd", z, zo['ckv'], kvl, small['kv_norm_g'], dckv)
    gsm['kv_norm_g'] = dkvg
    def q_bwd(dqb, cosb, sinb):
        dqb = _f32(dqb) * dims['scale']
        return jnp.concatenate([dqb * jnp.tile(cosb, (1, heads)), dqb * jnp.tile(sinb, (1, heads))], axis=1)
    (dqab,) = _rowwise("q_rope_bwd", q_bwd, [dq, cos, sin], [], [(2 * hw, BF16)])
    gops['wq'] = _mm("q_up_dw", sv['cq'], dqab, ta=True)
    dcq = _mm("q_up_dx", dqab, tr(ops['wq']))
    dcq_in, dqg = _rms_bwd("q_norm_bwd", z, zo['cq'], ql, small['q_norm_g'], dcq)
    gsm['q_norm_g'] = dqg
    gops['w_a_out'] = _mm("s5_out_dw", sv['s5o'], dya, ta=True)
    ds5o = _mm("s5_out_dx", dya, tr(ops['w_a_out']))

    def glu_bwd(yg, pre, ds, b):
        _, vjp = jax.vjp(_f_glu, _f32(yg), _f32(pre), b)
        dyg, dpre, db = vjp(_f32(ds))
        return dyg, dpre, db
    dyg_a, dpre, dbglu = _rowwise("s5_glu_bwd", glu_bwd, [sv['yg'], sv['pre'], ds5o], [small['b_glu']],
                                  [(sw, F32), (sw, BF16)], [sw])
    gsm['b_glu'] = dbglu
    gops['w_glu'] = _mm("s5_glu_mm_dw", sv['yg'], dpre, ta=True)
    dyg_b = _mm("s5_glu_mm_dx", dpre, tr(ops['w_glu']))

    def gelu_bwd(ych, u, dya_, dyb_, dvec):
        _, vjp = jax.vjp(_f_gelu_in, _f32(ych), _f32(u), dvec)
        dych, du, dd = vjp(_f32(dya_) + _f32(dyb_))
        return dych, du, dd
    dy, du_skip, dssm_d = _rowwise("s5_gelu_bwd", gelu_bwd, [sv['ych'], (z, zo['u'], sw), dyg_a, dyg_b],
                                   [small['ssm_d']], [(sw, BF16), (sw, F32)], [sw])
    gsm['ssm_d'] = dssm_d
    dy_seg = _to_segments(dy)
    nsup, sc = dims['nsup'], dims['s5_chunk']
    tr_blocks = lambda a: a.reshape(nsup, -1, a.shape[1]).transpose(0, 2, 1).reshape(-1, a.shape[0] // nsup)
    s5kw = dict(nsup=nsup, sc=sc)
    c_t, b_t = tr_blocks(cmat).astype(BF16), tr_blocks(bmat).astype(BF16)
    ends = _s5_pass("s5_ends_bwd", dy_seg, a_re, a_im, c_t, reverse=True, **s5kw)
    du_scan, g_bmat, g_cmat, dar, dai = _s5_pass("s5_scan_bwd", dy_seg, a_re, a_im, c_t, reverse=True, ends=ends,
                                                 w_out=b_t, u=sv['u_seg'], h=sv['hst'], **s5kw)
    du_scan = _from_segments(du_scan)
    (du,) = _rowwise("s5_du_sum", lambda a, b: _f32(a) + _f32(b), [du_skip, du_scan], [], [(sw, BF16)])
    gs5 = (dar, dai, g_bmat, g_cmat)
    dz = jnp.concatenate([dga, dgb, du, dcq_in, dckv_in, dkra, dkrb], axis=1)
    gops['w_in'] = _mm("w_in_dw", sv['h1'], dz, ta=True)
    dh1 = _mm("w_in_dx", dz, tr(ops['w_in']), out_dtype=BF16)
    dx, dn1, dsc1, dsh1 = _norm_mod_bwd("norm1_bwd", sv['x'], small['norm1_g'], sc1, sh1, dh1, dx1)
    gsm['norm1_g'] = dn1
    dmod = (dsh1, dsc1, dg1, dsh2, dsc2, dg2)
    return dx, gops, gs5, gsm, dmod


def _res_bwd(name, t, g, dxo):
    def fn(tb, db, gb):
        db = _f32(db)
        return gb * db, jnp.sum(db * _f32(tb), axis=0, keepdims=True)
    return _rowwise(name, fn, [t, dxo], [g], [(t.shape[1], BF16)], [t.shape[1]])


def _norm_mod_bwd(name, x, g, sc, sh, dh, dres):
    def fn(xb, dhb, dresb, gb, scb, shb):
        _, vjp = jax.vjp(_f_norm_mod, _f32(xb), gb, scb, shb)
        dx, dg, dsc, dsh = vjp(_f32(dhb))
        return dx + _f32(dresb), dg, dsc, dsh
    w = x.shape[1]
    return _rowwise(name, fn, [x, dh, dres], [g, sc, sh], [(w, F32)], [w, w, w])


def _rms_bwd(name, z, off, w, g, dy):
    def fn(xb, dyb, gb):
        _, vjp = jax.vjp(_f_rms, _f32(xb), gb)
        dx, dg = vjp(_f32(dyb))
        return dx, dg
    return _rowwise(name, fn, [(z, off, w), dy], [g], [(w, BF16)], [w])


S5_NAMES = ('ssm_a_re', 'ssm_a_im', 'ssm_log_dt', 'ssm_b_re', 'ssm_b_im', 'ssm_c_re', 'ssm_c_im')
LAYER_VECS = ('norm1_g', 'ssm_d', 'b_glu', 'q_norm_g', 'kv_norm_g', 'norm2_g')


def _step(p, mom_m, mom_v, x, c, positions, loss_target):
    depth, d = p['norm1_g'].shape
    L = x.shape[1]
    sw = p['ssm_d'].shape[1]
    ql, kvl = p['q_norm_g'].shape[1], p['kv_norm_g'].shape[1]
    heads = p['w_uk'].shape[2] * N_DEV // QK_NOPE
    dff = p['w_gate'].shape[2] * N_DEV
    ada_w = p['w_ada'].shape[2]
    zoff, o = {}, 0
    for nm, sz in (('ga', d), ('gb', d), ('u', sw), ('cq', ql), ('ckv', kvl), ('kra', HEAD_PAD), ('krb', HEAD_PAD)):
        assert o % sz == 0, (nm, o, sz)
        zoff[nm] = o
        o += sz
    groups, states = p['ssm_a_re'].shape[1:]
    assert groups % S5_SUPER == 0 and p['ssm_b_re'].shape[3] * S5_SUPER == LANES
    dims = dict(d=d, sw=sw, ql=ql, kvl=kvl, heads=heads, dff=dff, zoff=zoff,
                nsup=groups // S5_SUPER, s5_chunk=S5_SUPER * states,
                tq=1024 if L >= 4096 else 128, scale=(QK_NOPE + QK_ROPE) ** -0.5)
    x = x.reshape(L, d)
    tgt = loss_target.reshape(L, d)

    posf = positions.reshape(L).astype(F32)
    inv_freq = ROPE_BASE ** (-jnp.arange(0, QK_ROPE, 2, dtype=F32) / QK_ROPE)
    ang = posf[:, None] * inv_freq
    cs, sn = jnp.cos(ang), jnp.sin(ang)
    padr = HEAD_PAD - QK_NOPE - QK_ROPE
    cos = jnp.concatenate([jnp.ones((L, QK_NOPE), F32), cs, cs, jnp.zeros((L, padr), F32)], axis=1)
    sin = jnp.concatenate([jnp.zeros((L, QK_NOPE), F32), sn, sn, jnp.zeros((L, padr), F32)], axis=1)
    rope = (cos, sin)

    gathered = _gather_two_level("gather_weights", [p[n].astype(BF16) for n in SHARDED])

    def make_ops(gl):
        return jax.vmap(lambda w: _layer_operators(w, dims))(_gathered_to_full(gl))

    ops_all, ops_vjp = jax.vjp(make_ops, [g.astype(F32) for g in gathered])
    ops = [{k: v[l] for k, v in ops_all.items()} for l in range(depth)]

    def make_s5(sp):
        return jax.vmap(_s5_operators)(*[sp[n] for n in S5_NAMES])

    s5_all, s5_vjp = jax.vjp(make_s5, {n: p[n] for n in S5_NAMES})
    s5ops = [tuple(a[l] for a in s5_all) for l in range(depth)]

    (c_slabs,) = _exchange("gather_c", [jnp.pad(c, ((0, 7), (0, 0)))], all_to_all=False)
    c_all = c_slabs[:, 0, :]
    (c_act,) = _rowwise("c_silu", lambda a: jax.nn.silu(a), [jnp.pad(c_all, ((0, 8), (0, 0)))], [], [(d, F32)])
    w_ada_cat = p['w_ada'].transpose(1, 0, 2).reshape(d, depth * ada_w)
    mod_cols = _mm("ada_fwd", c_act, w_ada_cat)[:N_DEV]
    (mod_rows,) = _exchange("a2a_mod", [_pack_slabs(mod_cols)], all_to_all=True)
    mod_mine = _unpack_slabs(mod_rows, depth * ada_w).reshape(N_DEV, depth, ada_w)
    mod_mine = mod_mine.transpose(1, 0, 2).reshape(depth, N_DEV * ada_w)
    (mod_full,) = _rowwise("ada_bias", lambda a, b: a + b, [mod_mine, p['b_ada']], [], [(6 * d, F32)])
    mods = [tuple(mod_full[l:l + 1, i * d:(i + 1) * d] for i in range(6)) for l in range(depth)]

    saved = []
    xl = x
    for l in range(depth):
        small = {n: p[n][l:l + 1] for n in LAYER_VECS}
        xl, sv = _layer_fwd(xl, mods[l], ops[l], s5ops[l], small, rope, dims)
        saved.append((sv, small))

    def final_fn(xb, tb, gb):
        def lossf(xv, gv):
            e = _f_rms(xv, gv) - tb
            per_row = 0.5 * jnp.mean(e * e, axis=-1, keepdims=True)
            return jnp.sum(per_row, axis=0, keepdims=True)
        lv, vjp = jax.vjp(lossf, xb, gb)
        dxb, dgb = vjp(jnp.ones((1, 1), F32))
        return dxb, jnp.broadcast_to(lv, (1, LANES)), dgb
    dx, loss_vec, dfinal_g = _rowwise("final_loss", final_fn, [xl, tgt], [p['final_g'].reshape(1, d)],
                                      [(d, F32)], [LANES, d])

    g_ops, g_s5, g_small, dmods = [None] * depth, [None] * depth, [None] * depth, [None] * depth
    for l in reversed(range(depth)):
        sv, small = saved[l]
        dx, g_ops[l], g_s5[l], g_small[l], dmods[l] = _layer_bwd(dx, sv, mods[l], ops[l], s5ops[l], small,
                                                                rope, dims)
    grad_x = dx.reshape(1, L, d)

    dmod_mine = jnp.stack([jnp.concatenate(dm, axis=1)[0] for dm in dmods])
    dmod_slabs = dmod_mine.reshape(depth, N_DEV, ada_w).transpose(1, 0, 2).reshape(N_DEV, depth * ada_w)
    (dmod_recv,) = _exchange("a2a_dmod", [_pack_slabs(dmod_slabs)], all_to_all=True)
    dmod_cols = _unpack_slabs(dmod_recv, depth * ada_w)
    g_ada = _mm("ada_dw", c_act, jnp.pad(dmod_cols, ((0, 8), (0, 0))), ta=True)
    grads, delta, new_m, new_v = {}, {}, {}, {}
    grads['w_ada'] = g_ada.reshape(d, depth, ada_w).transpose(1, 0, 2)
    two_d = lambda a: a.reshape(-1, a.shape[-1])
    res = _rowwise("adamw_w_ada", _adamw_fn, [two_d(a) for a in (p['w_ada'], grads['w_ada'], mom_m['w_ada'],
                                                                  mom_v['w_ada'])], [], [(ada_w, F32)] * 3)
    delta['w_ada'], new_m['w_ada'], new_v['w_ada'] = [r.reshape(p['w_ada'].shape) for r in res]

    (g_slabs,) = ops_vjp({k: jnp.stack([g[k] for g in g_ops]) for k in ops_all})
    g_recv = _exchange("a2a_grads", [g.astype(BF16) for g in g_slabs], all_to_all=True)
    for n, rv in zip(SHARDED, g_recv):
        shp = p[n].shape
        res = _adamw_sharded("adamw_" + n, two_d(p[n]), rv.reshape(N_DEV, -1, shp[-1]), two_d(mom_m[n]),
                             two_d(mom_v[n]))
        grads[n], delta[n], new_m[n], new_v[n] = [r.reshape(shp) for r in res]

    (g_s5p,) = s5_vjp(tuple(jnp.stack([g[i] for g in g_s5]) for i in range(len(s5_all))))
    part = dict(g_s5p)
    part['b_ada'] = dmod_mine
    for n in LAYER_VECS:
        part[n] = jnp.concatenate([g_small[l][n] for l in range(depth)], axis=0)
    part['final_g'] = dfinal_g.reshape(d)
    small_shapes = [p[n].shape for n in SMALL] + [(1,)]
    (small_recv,) = _exchange("gather_small", [_pack_rows([part[n] for n in SMALL] + [loss_vec[0, 0:1]])],
                              all_to_all=False)
    small_sum = _sum_slabs("sum_small", small_recv)
    small_list = _unpack_rows(small_sum, small_shapes)
    grads.update(zip(SMALL, small_list[:-1]))
    loss = small_list[-1].reshape(())
    dummy = [jnp.zeros((1,), F32)]
    res = _rowwise("adamw_small", _adamw_fn,
                   [_pack_rows([src[n] for n in SMALL] + dummy) for src in (p, )] + [small_sum] +
                   [_pack_rows([src[n] for n in SMALL] + dummy) for src in (mom_m, mom_v)], [], [(FLAT_W, F32)] * 3)
    for dst, r in zip((delta, new_m, new_v), res):
        dst.update(zip(SMALL, _unpack_rows(r, small_shapes)[:-1]))
    return (loss, grad_x, *[grads[n] for n in WEIGHTS], *[delta[n] for n in WEIGHTS],
            *[new_m[n] for n in WEIGHTS], *[new_v[n] for n in WEIGHTS])


def kernel(x, c, positions, w_ada, b_ada, norm1_g, w_in, ssm_a_re, ssm_a_im, ssm_log_dt, ssm_b_re, ssm_b_im, ssm_c_re, ssm_c_im, ssm_d, w_glu, b_glu, w_a_out, q_norm_g, w_uq, kv_norm_g, w_uk, w_uv, w_b_out, w_out, norm2_g, w_gate, w_up, w_down, final_g, loss_target, m_w_ada, m_b_ada, m_norm1_g, m_w_in, m_ssm_a_re, m_ssm_a_im, m_ssm_log_dt, m_ssm_b_re, m_ssm_b_im, m_ssm_c_re, m_ssm_c_im, m_ssm_d, m_w_glu, m_b_glu, m_w_a_out, m_q_norm_g, m_w_uq, m_kv_norm_g, m_w_uk, m_w_uv, m_w_b_out, m_w_out, m_norm2_g, m_w_gate, m_w_up, m_w_down, m_final_g, v_w_ada, v_b_ada, v_norm1_g, v_w_in, v_ssm_a_re, v_ssm_a_im, v_ssm_log_dt, v_ssm_b_re, v_ssm_b_im, v_ssm_c_re, v_ssm_c_im, v_ssm_d, v_w_glu, v_b_glu, v_w_a_out, v_q_norm_g, v_w_uq, v_kv_norm_g, v_w_uk, v_w_uv, v_w_b_out, v_w_out, v_norm2_g, v_w_gate, v_w_up, v_w_down, v_final_g):
    given = dict(locals())
    p = {n: given[n] for n in WEIGHTS}
    mom_m = {n: given["m_" + n] for n in WEIGHTS}
    mom_v = {n: given["v_" + n] for n in WEIGHTS}
    return _step(p, mom_m, mom_v, x, c, positions, loss_target)
```

```python
import functools
import math

import jax
import jax.numpy as jnp
from jax import lax
from jax.experimental import pallas as pl
from jax.experimental.pallas import tpu as pltpu

F32 = jnp.float32
BF16 = jnp.bfloat16

N_DEV = 8
LANES = 128
FLAT_W = 1024
VMEM_LIMIT = 48 * 1024 * 1024
MM_VMEM_BUDGET = 36 * 1024 * 1024
QK_NOPE, QK_ROPE, V_DIM = 64, 32, 64
HEAD_PAD = LANES
ROPE_BASE = 10000.0
EPS = 1e-6
ADAM_LR, ADAM_B1, ADAM_B2, ADAM_EPS, ADAM_WD, ADAM_STEP = 0.001, 0.9, 0.999, 1e-08, 0.01, 10
NEG = float(jnp.finfo(jnp.float32).min)

WEIGHTS = ['w_ada', 'b_ada', 'norm1_g', 'w_in', 'ssm_a_re', 'ssm_a_im', 'ssm_log_dt', 'ssm_b_re', 'ssm_b_im',
           'ssm_c_re', 'ssm_c_im', 'ssm_d', 'w_glu', 'b_glu', 'w_a_out', 'q_norm_g', 'w_uq', 'kv_norm_g', 'w_uk',
           'w_uv', 'w_b_out', 'w_out', 'norm2_g', 'w_gate', 'w_up', 'w_down', 'final_g']
COL_SHARDED = ['w_in', 'w_a_out', 'w_uq', 'w_uk', 'w_uv', 'w_b_out', 'w_gate', 'w_up']
ROW_SHARDED = ['w_glu', 'w_out', 'w_down']
SHARDED = COL_SHARDED + ROW_SHARDED
SMALL = ['b_ada', 'norm1_g', 'ssm_a_re', 'ssm_a_im', 'ssm_log_dt', 'ssm_b_re', 'ssm_b_im', 'ssm_c_re', 'ssm_c_im',
         'ssm_d', 'b_glu', 'q_norm_g', 'kv_norm_g', 'norm2_g', 'final_g']


def _cparams(sem):
    return pltpu.CompilerParams(dimension_semantics=sem, vmem_limit_bytes=VMEM_LIMIT)


def _pick(dim, pref, quantum=LANES):
    if dim <= pref:
        return dim
    t = (pref // quantum) * quantum
    while t >= quantum:
        if dim % t == 0:
            return t
        t -= quantum
    return dim


def _mm(name, a, b, *, ta=False, out_dtype=F32, a_col=None, b_col=None, a_fn=None, keep_a=None, tm=1408, tn=1408,
        tk=None):
    if a_fn is None:
        a_off, a_w = a_col if a_col is not None else (0, a.shape[1])
        a_parts = [(a, a_off, a_w)]
    else:
        a_parts = a
        a_w = a_parts[0][2]
        assert all(w == a_w for _, _, w in a_parts)
    a0 = a_parts[0][0]
    b_off, b_w = b_col if b_col is not None else (0, b.shape[1])
    if ta:
        kdim, m = a0.shape[0], a_w
        assert b.shape[0] == kdim
    else:
        m, kdim = a0.shape[0], a_w
        assert b.shape[0] == kdim, (name, a0.shape, b.shape)
    n = b_w
    tm = _pick(m, tm, LANES if ta else 8)
    tn = _pick(n, tn)
    a_bytes = sum(arr.dtype.itemsize for arr, _, _ in a_parts) + (8 if a_fn is not None else 0)
    if tk is None:
        def fits(tm_, tk_):
            return (tm_ * tn * (2 * jnp.dtype(out_dtype).itemsize + 4)
                    + tk_ * (2 * (tm_ * a_bytes + tn * b.dtype.itemsize) + 2 * (tm_ + tn))) <= MM_VMEM_BUDGET

        tk = kdim
        while not fits(tm, tk):
            if not ta and tm % 16 == 0 and tm >= 1024 and fits(tm // 2, tk):
                tm //= 2
                break
            if tk % (2 * (8 if ta else LANES)):
                break
            tk //= 2
    tk = _pick(kdim, tk, 8 if ta else LANES)
    nk = kdim // tk
    assert m % tm == 0 and n % tn == 0 and kdim % tk == 0, (name, m, n, kdim, tm, tn, tk)
    a_specs = []
    for _, off, _ in a_parts:
        if ta:
            assert off % tm == 0
            a_specs.append(pl.BlockSpec((tk, tm), functools.partial(lambda i, j, k, o: (k, i + o), o=off // tm)))
        else:
            assert off % tk == 0
            a_specs.append(pl.BlockSpec((tm, tk), functools.partial(lambda i, j, k, o: (i, k + o), o=off // tk)))
    dims = _TN if ta else (((1,), (0,)), ((), ()))
    assert b_off % tn == 0
    b_spec = pl.BlockSpec((tk, tn), lambda i, j, k: (k, j + b_off // tn))
    na = len(a_parts)

    def prod(refs):
        if a_fn is None:
            av = refs[0][...]
        else:
            av = a_fn(*[r[...].astype(F32) for r in refs[:na]])
        return lax.dot_general(av.astype(BF16), refs[na][...].astype(BF16), dims, preferred_element_type=F32)

    def body_one(*refs):
        refs[na + 1][...] = prod(refs).astype(refs[na + 1].dtype)

    def body_keep(*refs):
        av = a_fn(*[r[...].astype(F32) for r in refs[:na]]).astype(keep_a)
        refs[na + 2][...] = av
        refs[na + 1][...] = lax.dot_general(av.astype(BF16), refs[na][...].astype(BF16), dims,
                                            preferred_element_type=F32).astype(refs[na + 1].dtype)

    def body_acc(*refs):
        o_ref, acc_ref = refs[na + 1], refs[na + 2]

        @pl.when(pl.program_id(2) == 0)
        def _():
            acc_ref[...] = jnp.zeros_like(acc_ref)

        acc_ref[...] += prod(refs)

        @pl.when(pl.program_id(2) == nk - 1)
        def _():
            o_ref[...] = acc_ref[...].astype(o_ref.dtype)

    out_shape = jax.ShapeDtypeStruct((m, n), out_dtype)
    out_specs = pl.BlockSpec((tm, tn), lambda i, j, k: (i, j))
    if keep_a is not None:
        assert a_fn is not None and not ta and nk == 1 and n == tn, (name, nk, n, tn)
        out_shape = [out_shape, jax.ShapeDtypeStruct((m, kdim), keep_a)]
        out_specs = [out_specs, pl.BlockSpec((tm, tk), lambda i, j, k: (i, k))]
    return pl.pallas_call(
        body_keep if keep_a is not None else (body_one if nk == 1 else body_acc), name=name,
        out_shape=out_shape,
        grid=(m // tm, n // tn, nk),
        in_specs=a_specs + [b_spec],
        out_specs=out_specs,
        scratch_shapes=[] if nk == 1 else [pltpu.VMEM((tm, tn), F32)],
        compiler_params=_cparams(("parallel", "parallel", "arbitrary")),
    )(*[arr for arr, _, _ in a_parts], b)


def _rowwise(name, fn, rows, vecs, outs, reds=(), *, tm=512):
    rows = [(r, 0, r.shape[1]) if not isinstance(r, tuple) else r for r in rows]
    nrows = rows[0][0].shape[0]
    tm = _pick(nrows, tm, 8)
    assert nrows % tm == 0, (name, nrows, tm)
    nr, nv, no = len(rows), len(vecs), len(outs)
    in_specs = []
    for arr, off, w in rows:
        assert arr.shape[0] == nrows and off % w == 0, (name, arr.shape, off, w)
        in_specs.append(pl.BlockSpec((tm, w), functools.partial(lambda i, cb: (i, cb), cb=off // w)))
    for v in vecs:
        assert v.ndim == 2 and v.shape[0] == 1, (name, v.shape)
        in_specs.append(pl.BlockSpec(v.shape, lambda i: (0, 0)))
    out_shape = [jax.ShapeDtypeStruct((nrows, w), dt) for w, dt in outs]
    out_specs = [pl.BlockSpec((tm, w), lambda i: (i, 0)) for w, dt in outs]
    out_shape += [jax.ShapeDtypeStruct((1, w), F32) for w in reds]
    out_specs += [pl.BlockSpec((1, w), lambda i: (0, 0)) for w in reds]

    def body(*refs):
        rin, vin = refs[:nr], refs[nr:nr + nv]
        rout, rred = refs[nr + nv:nr + nv + no], refs[nr + nv + no:]
        res = fn(*[r[...].astype(F32) for r in rin], *[v[...] for v in vin])
        if not isinstance(res, (tuple, list)):
            res = (res,)
        assert len(res) == no + len(reds), (name, len(res))
        for r, val in zip(rout, res[:no]):
            r[...] = val.astype(r.dtype)
        if reds:
            @pl.when(pl.program_id(0) == 0)
            def _():
                for r in rred:
                    r[...] = jnp.zeros_like(r)

            for r, val in zip(rred, res[no:]):
                r[...] += val.astype(F32)

    res = pl.pallas_call(
        body, name=name, out_shape=out_shape, grid=(nrows // tm,),
        in_specs=in_specs, out_specs=out_specs,
        compiler_params=_cparams(("arbitrary",) if reds else ("parallel",)),
    )(*[r[0] for r in rows], *vecs)
    return res


def _f32(x):
    return x.astype(F32)


def _vjp_fn(f, n_in, n_cot, want):
    def fn(*args):
        ins = [_f32(a) for a in args[:n_in]]
        cots = tuple(_f32(a) for a in args[n_in:n_in + n_cot])
        _, vjp = jax.vjp(f, *ins)
        grads = vjp(cots if n_cot > 1 else cots[0])
        return tuple(grads[i] for i in want)
    return fn


def _f_rms(x, g):
    return (x * lax.rsqrt(jnp.mean(x * x, axis=-1, keepdims=True) + EPS)) * g


def _f_norm_mod(x, g, sc, sh):
    return _f_rms(x, g) * (1.0 + sc) + sh


def _f_gelu_in(ych, u, d):
    return jax.nn.gelu(ych + d * u)


def _f_glu(yg, pre, b):
    return yg * jax.nn.sigmoid(pre + b)


def _f_merge(ga, gb, ya, yb):
    return jax.nn.sigmoid(ga) * ya + jax.nn.sigmoid(gb) * yb


def _f_res(x, t, g):
    return x + g * t


def _f_swiglu(a, b):
    return jax.nn.silu(a) * b


def _f_rope_q(qa, qb, cos, sin):
    h = qa.shape[1] // HEAD_PAD
    return qa * jnp.tile(cos, (1, h)) + qb * jnp.tile(sin, (1, h))


def _f_rope_k(kn, kra, krb, cos, sin):
    h = kn.shape[1] // HEAD_PAD
    return kn + jnp.tile(kra * cos + krb * sin, (1, h))


SUBLANES = 8
S5_SUPER = 8


def _cpow(ar, ai, log2n):
    for _ in range(log2n):
        ar, ai = ar * ar - ai * ai, 2.0 * ar * ai
    return ar, ai


def _to_segments(x):
    L, w = x.shape
    return x.reshape(SUBLANES, L // SUBLANES, w).transpose(1, 0, 2).reshape(L, w)


def _from_segments(x):
    L, w = x.shape
    return x.reshape(L // SUBLANES, SUBLANES, w).transpose(1, 0, 2).reshape(L, w)


def _s5_pass(name, xin, a_re, a_im, w_in, *, reverse, nsup, sc, ends=None, w_out=None, u=None, h=None, tb=512):
    L, sw = xin.shape
    gp2 = nsup * 2 * sc
    gp = gp2 // 2
    seg = L // SUBLANES
    assert sw == nsup * LANES and L % SUBLANES == 0 and seg & (seg - 1) == 0, (L, sw)
    tb = _pick(L, tb, SUBLANES)
    nb, nt = L // tb, tb // SUBLANES
    first_pass = ends is None
    sign = -1.0 if reverse else 1.0
    blk = (lambda i: (nb - 1 - i, 0)) if reverse else (lambda i: (i, 0))
    order = (lambda s: nt - 1 - s) if reverse else (lambda s: s)
    tile = lambda s: pl.ds(pl.multiple_of(s * SUBLANES, SUBLANES), SUBLANES)
    const = lambda shape: pl.BlockSpec(shape, lambda i: (0, 0))
    rows_in = pl.BlockSpec((tb, sw), blk)
    rows_st = pl.BlockSpec((tb, gp2), blk)

    def coeffs(ar_ref, ai_ref, cc):
        ar1, ai1 = ar_ref[:, cc], sign * ai_ref[:, cc]
        return ar1, ai1, jnp.broadcast_to(ar1, (SUBLANES, sc)), jnp.broadcast_to(ai1, (SUBLANES, sc))

    def drive(x_ref, w_ref, xs_ref, c):
        lanes = pl.ds(c * LANES, LANES)
        xs_ref[...] = jnp.dot(x_ref[:, lanes].astype(BF16), w_ref[lanes, :], preferred_element_type=F32)

    def body(*refs):
        it = iter(refs)
        x_ref, w_ref, ar_ref, ai_ref = next(it), next(it), next(it), next(it)
        if first_pass:
            e_ref, xs_ref = next(it), next(it)
        elif not reverse:
            e_ref, wo_ref, o_ref, y_ref, st_ref, xs_ref = (next(it) for _ in range(6))
        else:
            (e_ref, wo_ref, u_ref, h_ref, hp_ref, hl_ref, du_ref, gb_ref, gc_ref, dar_ref, dai_ref,
             st_ref, xs_ref, g_ref) = (next(it) for _ in range(14))
        i = pl.program_id(0)

        @pl.when(i == 0)
        def _():
            if first_pass:
                e_ref[...] = jnp.zeros_like(e_ref)
                return
            for c in range(nsup):
                cc, re, im = pl.ds(c * sc, sc), pl.ds(2 * c * sc, sc), pl.ds((2 * c + 1) * sc, sc)
                ar1, ai1, _, _ = coeffs(ar_ref, ai_ref, cc)
                pr, pi = _cpow(ar1, ai1, seg.bit_length() - 1)
                cr = jnp.zeros((1, sc), F32)
                ci = jnp.zeros((1, sc), F32)
                for j in (reversed(range(SUBLANES)) if reverse else range(SUBLANES)):
                    st_ref[j:j + 1, re] = cr
                    st_ref[j:j + 1, im] = ci
                    cr, ci = (e_ref[j:j + 1, re] + pr * cr - pi * ci, e_ref[j:j + 1, im] + pr * ci + pi * cr)
            if reverse:
                for r in (gb_ref, gc_ref, dar_ref, dai_ref):
                    r[...] = jnp.zeros_like(r)

        sub = lax.broadcasted_iota(jnp.int32, (SUBLANES, sc), 0)
        state = e_ref if first_pass else st_ref
        xr, xi = pl.ds(0, sc), pl.ds(sc, sc)
        for c in range(nsup):
            cc, re, im = pl.ds(c * sc, sc), pl.ds(2 * c * sc, sc), pl.ds((2 * c + 1) * sc, sc)
            both = pl.ds(2 * c * sc, 2 * sc)
            lanes = pl.ds(c * LANES, LANES)
            _, _, ar, ai = coeffs(ar_ref, ai_ref, cc)
            drive(x_ref, w_ref, xs_ref, c)

            def advance(rows, sr, si):
                return ar * sr - ai * si + xs_ref[rows, xr], ar * si + ai * sr + xs_ref[rows, xi]

            if first_pass:
                def step(s, st):
                    return advance(tile(order(s)), *st)

                sr, si = lax.fori_loop(0, nt, step, (state[:, re], state[:, im]), unroll=4)
            elif not reverse:
                def step(s, st):
                    rows = tile(s)
                    sr, si = advance(rows, *st)
                    o_ref[rows, re] = sr
                    o_ref[rows, im] = si
                    return sr, si

                sr, si = lax.fori_loop(0, nt, step, (state[:, re], state[:, im]), unroll=4)
                y_ref[:, lanes] = jnp.dot(o_ref[:, both].astype(BF16), wo_ref[both, :], preferred_element_type=F32)
            else:
                def emit(rows, sr, si):
                    sr, si = advance(rows, sr, si)
                    g_ref[rows, xr] = sr
                    g_ref[rows, xi] = si
                    return sr, si

                def grad(sr, si, hpr, hpi, accr, acci):
                    return accr + sr * hpr + si * hpi, acci + si * hpr - sr * hpi

                def step(s, st):
                    sr, si, accr, acci = st
                    t = nt - 1 - s
                    sr, si = emit(tile(t), sr, si)
                    prev = tile(t - 1)
                    return (sr, si) + grad(sr, si, h_ref[prev, re], h_ref[prev, im], accr, acci)

                zero = jnp.zeros((SUBLANES, sc), F32)
                sr, si, accr, acci = lax.fori_loop(0, nt - 1, step, (state[:, re], state[:, im], zero, zero),
                                                   unroll=4)
                sr, si = emit(tile(0), sr, si)
                first = (i == nb - 1)
                wrap_r = jnp.where(sub == 0, 0.0, pltpu.roll(hl_ref[:, re], 1, 0))
                wrap_i = jnp.where(sub == 0, 0.0, pltpu.roll(hl_ref[:, im], 1, 0))
                accr, acci = grad(sr, si, jnp.where(first, wrap_r, hp_ref[:, re]),
                                  jnp.where(first, wrap_i, hp_ref[:, im]), accr, acci)
                dar_ref[:, cc] += jnp.sum(accr, axis=0, keepdims=True)
                dai_ref[:, cc] += jnp.sum(acci, axis=0, keepdims=True)
                gb = g_ref[...].astype(BF16)
                du_ref[:, lanes] = jnp.dot(gb, wo_ref[both, :], preferred_element_type=F32)
                gb_ref[lanes, :] += lax.dot_general(u_ref[:, lanes].astype(BF16), gb, _TN,
                                                    preferred_element_type=F32)
                gc_ref[both, :] += lax.dot_general(h_ref[:, both].astype(BF16), x_ref[:, lanes].astype(BF16), _TN,
                                                   preferred_element_type=F32)
            state[:, re] = sr
            state[:, im] = si

    vec = const((1, gp))
    in_specs = [rows_in, const(w_in.shape), vec, vec]
    operands = [xin, w_in, a_re, a_im]
    xs_scratch = pltpu.VMEM((tb, 2 * sc), F32)
    st_scratch = pltpu.VMEM((SUBLANES, gp2), F32)
    if first_pass:
        out_shape = jax.ShapeDtypeStruct((SUBLANES, gp2), F32)
        out_specs = const((SUBLANES, gp2))
        scratch = [xs_scratch]
    elif not reverse:
        in_specs += [const((SUBLANES, gp2)), const(w_out.shape)]
        operands += [ends, w_out]
        out_shape = [jax.ShapeDtypeStruct((L, gp2), F32), jax.ShapeDtypeStruct((L, sw), F32)]
        out_specs = [rows_st, rows_in]
        scratch = [st_scratch, xs_scratch]
    else:
        in_specs += [const((SUBLANES, gp2)), const(w_out.shape), rows_in, rows_st,
                     pl.BlockSpec((SUBLANES, gp2), lambda i: (jnp.maximum((nb - 1 - i) * nt - 1, 0), 0)),
                     pl.BlockSpec((SUBLANES, gp2), lambda i: (seg - 1, 0))]
        operands += [ends, w_out, u, h, h, h]
        out_shape = [jax.ShapeDtypeStruct((L, sw), F32), jax.ShapeDtypeStruct((sw, 2 * sc), F32),
                     jax.ShapeDtypeStruct((gp2, LANES), F32), jax.ShapeDtypeStruct((1, gp), F32),
                     jax.ShapeDtypeStruct((1, gp), F32)]
        out_specs = [rows_in, const((sw, 2 * sc)), const((gp2, LANES)), vec, vec]
        scratch = [st_scratch, xs_scratch, pltpu.VMEM((tb, 2 * sc), F32)]
    return pl.pallas_call(
        body, name=name, out_shape=out_shape, grid=(nb,), in_specs=in_specs, out_specs=out_specs,
        scratch_shapes=scratch, compiler_params=_cparams(("arbitrary",)),
    )(*operands)


_NT = (((1,), (1,)), ((), ()))
_TN = (((0,), (0,)), ((), ()))


def _causal(s, t, k_major=False):
    row = lax.broadcasted_iota(jnp.int32, (t, t), 0)
    col = lax.broadcasted_iota(jnp.int32, (t, t), 1)
    return jnp.where(row <= col if k_major else col <= row, s, NEG)


def _pair_tables(n, k_major):
    if k_major:
        pairs = [(qi, ki) for ki in range(n) for qi in range(ki, n)]
    else:
        pairs = [(qi, ki) for qi in range(n) for ki in range(qi + 1)]
    return (jnp.asarray([p[0] for p in pairs], jnp.int32), jnp.asarray([p[1] for p in pairs], jnp.int32))


def _flash_fwd(q, k, knv, *, heads, t):
    L = q.shape[0]
    n = L // t
    rep = t // LANES
    qtab, ktab = _pair_tables(n, k_major=False)

    def body(qt_ref, kt_ref, q_ref, k_ref, v_ref, o_ref, lse_ref, m_s, acc_s):
        step = pl.program_id(1)
        qi, ki = qt_ref[step], kt_ref[step]
        lane = lax.broadcasted_iota(jnp.int32, (t, HEAD_PAD), 1)

        @pl.when(ki == 0)
        def _():
            m_s[...] = jnp.full(m_s.shape, NEG, F32)
            acc_s[...] = jnp.zeros_like(acc_s)

        def update(diagonal):
            s = lax.dot_general(q_ref[...], k_ref[...], _NT, preferred_element_type=F32)
            if diagonal:
                s = _causal(s, t)
            m_prev = m_s[...]
            m_next = jnp.maximum(m_prev, jnp.max(s, axis=1, keepdims=True))
            p = jnp.exp(s - jnp.tile(m_next, (1, rep)))
            vb = jnp.where(lane == V_DIM, jnp.ones((), BF16), v_ref[...])
            acc_s[...] = jnp.exp(m_prev - m_next) * acc_s[...] + jnp.dot(p.astype(BF16), vb,
                                                                         preferred_element_type=F32)
            m_s[...] = m_next

        @pl.when(ki < qi)
        def _():
            update(False)

        @pl.when(ki == qi)
        def _():
            update(True)
            acc = acc_s[...]
            l = jnp.sum(jnp.where(lane == V_DIM, acc, 0.0), axis=1, keepdims=True)
            o_ref[...] = jnp.where(lane == V_DIM, 0.0, acc * (1.0 / l)).astype(o_ref.dtype)
            lse_ref[0] = jnp.max(m_s[...], axis=1, keepdims=True) + jnp.log(l)

    q_map = lambda h, s, qt, kt: (qt[s], h)
    kv_map = lambda h, s, qt, kt: (kt[s], h)
    v_map = lambda h, s, qt, kt: (kt[s], h + heads)
    return pl.pallas_call(
        body, name="mla_flash_fwd",
        out_shape=[jax.ShapeDtypeStruct((L, heads * HEAD_PAD), BF16),
                   jax.ShapeDtypeStruct((heads, L, 1), F32)],
        grid_spec=pltpu.PrefetchScalarGridSpec(
            num_scalar_prefetch=2, grid=(heads, qtab.shape[0]),
            in_specs=[pl.BlockSpec((t, HEAD_PAD), q_map),
                      pl.BlockSpec((t, HEAD_PAD), kv_map),
                      pl.BlockSpec((t, HEAD_PAD), v_map)],
            out_specs=[pl.BlockSpec((t, HEAD_PAD), q_map),
                       pl.BlockSpec((1, t, 1), lambda h, s, qt, kt: (h, qt[s], 0))],
            scratch_shapes=[pltpu.VMEM((t, LANES), F32), pltpu.VMEM((t, HEAD_PAD), F32)]),
        compiler_params=_cparams(("parallel", "arbitrary")),
    )(qtab, ktab, q, k, knv)


def _flash_delta(do, o, *, heads, t):
    L = do.shape[0]

    hw = heads * HEAD_PAD

    def body(do_ref, o_ref, d_ref):
        prod = do_ref[...].astype(F32) * o_ref[...].astype(F32)
        for h in range(heads):
            d_ref[h] = jnp.sum(prod[:, h * HEAD_PAD:(h + 1) * HEAD_PAD], axis=1, keepdims=True)

    return pl.pallas_call(
        body, name="mla_flash_delta",
        out_shape=jax.ShapeDtypeStruct((heads, L, 1), F32),
        grid=(L // t,),
        in_specs=[pl.BlockSpec((t, hw), lambda i: (i, 0)),
                  pl.BlockSpec((t, hw), lambda i: (i, 0))],
        out_specs=pl.BlockSpec((heads, t, 1), lambda i: (0, i, 0)),
        compiler_params=_cparams(("parallel",)),
    )(do, o)


def _flash_bwd(q, k, knv, do, lse, delta, *, heads, t):
    L = q.shape[0]
    n = L // t
    qtab, ktab = _pair_tables(n, k_major=True)
    kc = min(t, 256)

    def body(qt_ref, kt_ref, q_ref, k_ref, v_ref, do_ref, lse_ref, dl_ref, dq_ref, dk_ref, dv_ref, dk_s, dv_s):
        step = pl.program_id(1)
        qi, ki = qt_ref[step], kt_ref[step]

        @pl.when(qi == ki)
        def _():
            dk_s[...] = jnp.zeros_like(dk_s)
            dv_s[...] = jnp.zeros_like(dv_s)

        def update(diagonal):
            qb, dob = q_ref[...], do_ref[...]
            dqb = None
            for c in range(t // kc):
                keys = pl.ds(c * kc, kc)
                kb, vb = k_ref[keys, :], v_ref[keys, :]
                st = lax.dot_general(kb, qb, _NT, preferred_element_type=F32)
                if diagonal:
                    key = lax.broadcasted_iota(jnp.int32, (kc, t), 0) + c * kc
                    st = jnp.where(key <= lax.broadcasted_iota(jnp.int32, (kc, t), 1), st, NEG)
                pt = jnp.exp(st - lse_ref[0])
                dv_s[keys, :] += jnp.dot(pt.astype(BF16), dob, preferred_element_type=F32)
                dpt = lax.dot_general(vb, dob, _NT, preferred_element_type=F32)
                dst = (pt * (dpt - dl_ref[0])).astype(BF16)
                dk_s[keys, :] += jnp.dot(dst, qb, preferred_element_type=F32)
                part = lax.dot_general(dst, kb, _TN, preferred_element_type=F32)
                dqb = part if dqb is None else dqb + part
            rows = pl.ds(pl.multiple_of(qi * t, t), t)

            @pl.when(ki == 0)
            def _():
                dq_ref[rows, :] = dqb

            @pl.when(ki > 0)
            def _():
                dq_ref[rows, :] += dqb

        @pl.when(qi > ki)
        def _():
            update(False)

        @pl.when(qi == ki)
        def _():
            update(True)

        @pl.when(qi == n - 1)
        def _():
            dk_ref[...] = dk_s[...].astype(dk_ref.dtype)
            dv_ref[...] = dv_s[...].astype(dv_ref.dtype)

    q_map = lambda h, s, qt, kt: (qt[s], h)
    stat_map = lambda h, s, qt, kt: (h, 0, qt[s])
    kv_map = lambda h, s, qt, kt: (kt[s], h)
    v_map = lambda h, s, qt, kt: (kt[s], h + heads)
    return pl.pallas_call(
        body, name="mla_flash_bwd",
        out_shape=[jax.ShapeDtypeStruct((L, heads * HEAD_PAD), F32),
                   jax.ShapeDtypeStruct((L, heads * HEAD_PAD), BF16),
                   jax.ShapeDtypeStruct((L, heads * HEAD_PAD), BF16)],
        grid_spec=pltpu.PrefetchScalarGridSpec(
            num_scalar_prefetch=2, grid=(heads, qtab.shape[0]),
            in_specs=[pl.BlockSpec((t, HEAD_PAD), q_map),
                      pl.BlockSpec((t, HEAD_PAD), kv_map),
                      pl.BlockSpec((t, HEAD_PAD), v_map),
                      pl.BlockSpec((t, HEAD_PAD), q_map),
                      pl.BlockSpec((1, 1, t), stat_map),
                      pl.BlockSpec((1, 1, t), stat_map)],
            out_specs=[pl.BlockSpec((L, HEAD_PAD), lambda h, s, qt, kt: (0, h)),
                       pl.BlockSpec((t, HEAD_PAD), kv_map),
                       pl.BlockSpec((t, HEAD_PAD), kv_map)],
            scratch_shapes=[pltpu.VMEM((t, HEAD_PAD), F32), pltpu.VMEM((t, HEAD_PAD), F32)]),
        compiler_params=_cparams(("parallel", "arbitrary")),
    )(qtab, ktab, q, k, knv, do, lse, delta)


def _peer(k):
    mx, my, mc = lax.axis_index("x"), lax.axis_index("y"), lax.axis_index("c")
    px = 1 - mx if (k >> 2) & 1 else mx
    py = 1 - my if (k >> 1) & 1 else my
    pc = 1 - mc if k & 1 else mc
    return (px, py, pc), 4 * px + 2 * py + pc


def _exchange(name, xs, all_to_all):
    n = len(xs)
    any_spec = pl.BlockSpec(memory_space=pl.ANY)
    npeer = N_DEV - 1

    def body(*refs):
        x_refs, o_refs = refs[:n], refs[n:2 * n]
        send_sems, recv_sems, local_sems = refs[2 * n:]
        _, me = _peer(0)
        mine = [x.at[me] if all_to_all else x for x in x_refs]
        local = [pltpu.make_async_copy(mine[i], o_refs[i].at[me], local_sems.at[i]) for i in range(n)]
        for cp in local:
            cp.start()
        sends = []
        for k in range(1, N_DEV):
            dev, idx = _peer(k)
            for i in range(n):
                cp = pltpu.make_async_remote_copy(
                    src_ref=x_refs[i].at[idx] if all_to_all else x_refs[i], dst_ref=o_refs[i].at[me],
                    send_sem=send_sems.at[i * npeer + k - 1], recv_sem=recv_sems.at[i * npeer + k - 1],
                    device_id=dev, device_id_type=pl.DeviceIdType.MESH)
                cp.start()
                sends.append(cp)
        for k in range(1, N_DEV):
            dev, idx = _peer(k)
            for i in range(n):
                pltpu.make_async_remote_copy(
                    src_ref=mine[i], dst_ref=o_refs[i].at[idx],
                    send_sem=send_sems.at[i * npeer + k - 1], recv_sem=recv_sems.at[i * npeer + k - 1],
                    device_id=dev, device_id_type=pl.DeviceIdType.MESH).wait_recv()
        for cp in sends:
            cp.wait_send()
        for cp in local:
            cp.wait()

    return pl.pallas_call(
        body, name=name,
        out_shape=[jax.ShapeDtypeStruct((N_DEV,) + tuple(x.shape[1:] if all_to_all else x.shape), x.dtype)
                   for x in xs],
        in_specs=[any_spec] * n, out_specs=[any_spec] * n,
        scratch_shapes=[pltpu.SemaphoreType.DMA((n * npeer,)), pltpu.SemaphoreType.DMA((n * npeer,)),
                        pltpu.SemaphoreType.DMA((n,))],
    )(*xs)


def _gather_two_level(name, xs):
    n = len(xs)
    any_spec = pl.BlockSpec(memory_space=pl.ANY)
    nslot = N_DEV - 1
    chips = (2, 4, 6)

    def body(*refs):
        x_refs, o_refs = refs[:n], refs[n:2 * n]
        send_sems, recv_sems, local_sems = refs[2 * n:]
        _, me = _peer(0)
        sibling, sib_idx = _peer(1)

        def copy(i, slot, src, block, dev):
            return pltpu.make_async_remote_copy(
                src_ref=src, dst_ref=o_refs[i].at[block],
                send_sem=send_sems.at[i * nslot + slot], recv_sem=recv_sems.at[i * nslot + slot],
                device_id=dev, device_id_type=pl.DeviceIdType.MESH)

        local = [pltpu.make_async_copy(x_refs[i], o_refs[i].at[me], local_sems.at[i]) for i in range(n)]
        for cp in local:
            cp.start()
        sends = []
        for i in range(n):
            sends.append(copy(i, 0, x_refs[i], me, sibling))
            for j, k in enumerate(chips):
                sends.append(copy(i, 1 + j, x_refs[i], me, _peer(k)[0]))
        for cp in sends:
            cp.start()
        for j, k in enumerate(chips):
            _, idx = _peer(k)
            for i in range(n):
                copy(i, 1 + j, x_refs[i], idx, _peer(k)[0]).wait_recv()
                fwd = copy(i, 4 + j, o_refs[i].at[idx], idx, sibling)
                fwd.start()
                sends.append(fwd)
        for i in range(n):
            copy(i, 0, x_refs[i], sib_idx, sibling).wait_recv()
        for j, k in enumerate(chips):
            _, idx = _peer(k | 1)
            for i in range(n):
                copy(i, 4 + j, x_refs[i], idx, sibling).wait_recv()
        for cp in sends:
            cp.wait_send()
        for cp in local:
            cp.wait()

    return pl.pallas_call(
        body, name=name,
        out_shape=[jax.ShapeDtypeStruct((N_DEV,) + tuple(x.shape), x.dtype) for x in xs],
        in_specs=[any_spec] * n, out_specs=[any_spec] * n,
        scratch_shapes=[pltpu.SemaphoreType.DMA((n * nslot,)), pltpu.SemaphoreType.DMA((n * nslot,)),
                        pltpu.SemaphoreType.DMA((n,))],
    )(*xs)


def _sum_slabs(name, x, *, tr=128):
    _, r, w = x.shape
    tr = _pick(r, tr, 8)

    def body(x_ref, o_ref):
        acc = x_ref[0]
        for j in range(1, N_DEV):
            acc = acc + x_ref[j]
        o_ref[...] = acc

    return pl.pallas_call(
        body, name=name, out_shape=jax.ShapeDtypeStruct((r, w), F32), grid=(r // tr,),
        in_specs=[pl.BlockSpec((N_DEV, tr, w), lambda i: (0, i, 0))],
        out_specs=pl.BlockSpec((tr, w), lambda i: (i, 0)),
        compiler_params=_cparams(("parallel",)),
    )(x)


def _adamw_fn(w, g, m, v):
    m = ADAM_B1 * m + (1.0 - ADAM_B1) * g
    v = ADAM_B2 * v + (1.0 - ADAM_B2) * jnp.square(g)
    m_hat = m / (1.0 - ADAM_B1 ** ADAM_STEP)
    v_hat = v / (1.0 - ADAM_B2 ** ADAM_STEP)
    delta = -ADAM_LR * (m_hat / (jnp.sqrt(v_hat) + ADAM_EPS) + ADAM_WD * w)
    return delta, m, v


def _adamw_sharded(name, w, recv, m, v, *, tr=128):
    rows, c = w.shape
    tr = _pick(rows, tr, 8)

    def body(w_ref, r_ref, m_ref, v_ref, g_ref, d_ref, mo_ref, vo_ref):
        g = r_ref[0].astype(F32)
        for j in range(1, N_DEV):
            g = g + r_ref[j].astype(F32)
        d, mn, vn = _adamw_fn(w_ref[...], g, m_ref[...], v_ref[...])
        g_ref[...] = g
        d_ref[...] = d
        mo_ref[...] = mn
        vo_ref[...] = vn

    blk = pl.BlockSpec((tr, c), lambda i: (i, 0))
    return pl.pallas_call(
        body, name=name, out_shape=[jax.ShapeDtypeStruct((rows, c), F32)] * 4, grid=(rows // tr,),
        in_specs=[blk, pl.BlockSpec((N_DEV, tr, c), lambda i: (0, i, 0)), blk, blk],
        out_specs=[blk] * 4,
        compiler_params=_cparams(("parallel",)),
    )(w, recv, m, v)


def _piece_rows(shape):
    return -(-math.prod(shape) // (8 * FLAT_W)) * 8


def _pack_rows(arrs):
    out = []
    for a in arrs:
        flat = a.reshape(-1)
        rows = _piece_rows(a.shape)
        out.append(jnp.pad(flat, (0, rows * FLAT_W - flat.shape[0])).reshape(rows, FLAT_W))
    return jnp.concatenate(out, axis=0)


def _unpack_rows(packed, shapes):
    out, r0 = [], 0
    for s in shapes:
        rows = _piece_rows(s)
        out.append(packed[r0:r0 + rows].reshape(-1)[:math.prod(s)].reshape(s))
        r0 += rows
    return out


def _pack_slabs(a):
    n = a.shape[1]
    rows = _piece_rows((n,))
    return jnp.pad(a, ((0, 0), (0, rows * FLAT_W - n))).reshape(N_DEV, rows, FLAT_W)


def _unpack_slabs(a, n):
    return a.reshape(N_DEV, -1)[:, :n]


def _s5_operators(a_re, a_im, log_dt, b_re, b_im, c_re, c_im):
    g, p, m = b_re.shape
    dt = jnp.exp(log_dt)[:, None]
    mag = jnp.exp(a_re * dt)
    abar_re = mag * jnp.cos(a_im * dt)
    abar_im = mag * jnp.sin(a_im * dt)
    den = a_re * a_re + a_im * a_im
    nr = abar_re - 1.0
    ni = abar_im
    coef_re = ((nr * a_re + ni * a_im) / den)[..., None]
    coef_im = ((ni * a_re - nr * a_im) / den)[..., None]
    bbar_re = coef_re * b_re - coef_im * b_im
    bbar_im = coef_re * b_im + coef_im * b_re
    sup, ns = S5_SUPER, g // S5_SUPER
    eye = jnp.eye(sup, dtype=F32)

    def b_blocks(bb):
        return jnp.einsum('cgpm,gh->cgmhp', bb.reshape(ns, sup, p, m), eye).reshape(ns, sup * m, sup * p)

    def c_blocks(cb):
        return jnp.einsum('cgmp,gh->chpgm', cb.reshape(ns, sup, m, p), eye).reshape(ns, sup * p, sup * m)

    bmat = jnp.concatenate([b_blocks(bbar_re), b_blocks(bbar_im)], axis=2).reshape(g * m, 2 * sup * p)
    cmat = jnp.concatenate([c_blocks(c_re), c_blocks(-c_im)], axis=1).reshape(ns * 2 * sup * p, sup * m)
    return abar_re.reshape(1, g * p), abar_im.reshape(1, g * p), bmat, cmat


def _rot_cols(w):
    half = w.shape[-1] // 2
    return jnp.concatenate([-w[..., half:], w[..., :half]], axis=-1)


def _layer_operators(w, dims):
    d, sw, ql, kvl, heads, dff = dims['d'], dims['sw'], dims['ql'], dims['kvl'], dims['heads'], dims['dff']
    w_in = w['w_in']
    o = 0
    parts = {}
    for nm, sz in (('u', sw), ('cq', ql), ('ckv', kvl), ('kr', QK_ROPE), ('ga', d), ('gb', d)):
        parts[nm] = w_in[:, o:o + sz]
        o += sz
    zpad = lambda n: jnp.zeros((d, n), w_in.dtype)
    kra = jnp.concatenate([zpad(QK_NOPE), parts['kr'], zpad(HEAD_PAD - QK_NOPE - QK_ROPE)], axis=1)
    krb = jnp.concatenate([zpad(QK_NOPE), _rot_cols(parts['kr']), zpad(HEAD_PAD - QK_NOPE - QK_ROPE)], axis=1)
    w_in_x = jnp.concatenate([parts['ga'], parts['gb'], parts['u'], parts['cq'], parts['ckv'], kra, krb], axis=1)

    wq = w['w_uq'].reshape(ql, heads, QK_NOPE + QK_ROPE)
    qz = lambda n: jnp.zeros((ql, heads, n), wq.dtype)
    wq_a = jnp.concatenate([wq, qz(HEAD_PAD - QK_NOPE - QK_ROPE)], axis=2)
    wq_b = jnp.concatenate([qz(QK_NOPE), _rot_cols(wq[:, :, QK_NOPE:]), qz(HEAD_PAD - QK_NOPE - QK_ROPE)], axis=2)
    wq_x = jnp.concatenate([wq_a.reshape(ql, -1), wq_b.reshape(ql, -1)], axis=1)

    kz = lambda n: jnp.zeros((kvl, heads, n), w['w_uk'].dtype)
    wk = jnp.concatenate([w['w_uk'].reshape(kvl, heads, QK_NOPE), kz(HEAD_PAD - QK_NOPE)], axis=2)
    wv = jnp.concatenate([w['w_uv'].reshape(kvl, heads, V_DIM), kz(HEAD_PAD - V_DIM)], axis=2)
    wkv_x = jnp.concatenate([wk.reshape(kvl, -1), wv.reshape(kvl, -1)], axis=1)

    wbo = w['w_b_out'].reshape(heads, V_DIM, d)
    wbo_x = jnp.concatenate([wbo, jnp.zeros((heads, HEAD_PAD - V_DIM, d), wbo.dtype)], axis=1).reshape(-1, d)
    wgu = jnp.concatenate([w['w_gate'], w['w_up']], axis=1)
    return dict(w_in=w_in_x, w_glu=w['w_glu'], w_a_out=w['w_a_out'], wq=wq_x, wkv=wkv_x, wbo=wbo_x,
                w_out=w['w_out'], wgu=wgu, w_down=w['w_down'])


def _gathered_to_full(gathered):
    full = {}
    for n, pc in zip(SHARDED, gathered):
        dep, r, c = pc.shape[1:]
        if n in COL_SHARDED:
            full[n] = pc.transpose(1, 2, 0, 3).reshape(dep, r, N_DEV * c)
        else:
            full[n] = pc.transpose(1, 0, 2, 3).reshape(dep, N_DEV * r, c)
    return full


def _layer_fwd(x, mod, ops, s5, small, rope, dims):
    d, sw, ql, kvl, heads, dff = dims['d'], dims['sw'], dims['ql'], dims['kvl'], dims['heads'], dims['dff']
    zo = dims['zoff']
    hw = heads * HEAD_PAD
    cos, sin = rope
    sh1, sc1, g1, sh2, sc2, g2 = mod
    bf = lambda a: a.astype(BF16)
    sv = dict(x=x)
    (h1,) = _rowwise("norm1_fwd", _f_norm_mod, [x], [small['norm1_g'], sc1, sh1], [(d, BF16)])
    z = _mm("w_in_fwd", h1, bf(ops['w_in']), out_dtype=BF16, tn=ops['w_in'].shape[1])
    sv.update(h1=h1, z=z)
    a_re, a_im, bmat, cmat = s5
    u_seg = _to_segments(z[:, zo['u']:zo['u'] + sw])
    nsup, sc = dims['nsup'], dims['s5_chunk']
    s5kw = dict(nsup=nsup, sc=sc)
    ends = _s5_pass("s5_ends_fwd", u_seg, a_re, a_im, bf(bmat), reverse=False, **s5kw)
    hst, ych = _s5_pass("s5_scan_fwd", u_seg, a_re, a_im, bf(bmat), reverse=False, ends=ends, w_out=bf(cmat), **s5kw)
    ych = _from_segments(ych)
    (yg,) = _rowwise("s5_gelu_fwd", _f_gelu_in, [ych, (z, zo['u'], sw)], [small['ssm_d']], [(sw, F32)])
    pre = _mm("s5_glu_mm_fwd", yg, bf(ops['w_glu']))
    (s5o,) = _rowwise("s5_glu_fwd", _f_glu, [yg, pre], [small['b_glu']], [(sw, BF16)])
    ya = _mm("s5_out_fwd", s5o, bf(ops['w_a_out']), out_dtype=BF16)
    sv.update(u_seg=u_seg, hst=hst, ych=ych, yg=yg, pre=pre, s5o=s5o, ya=ya)
    (cq,) = _rowwise("q_norm_fwd", _f_rms, [(z, zo['cq'], ql)], [small['q_norm_g']], [(ql, BF16)])
    qab = _mm("q_up_fwd", cq, bf(ops['wq']))
    scale = dims['scale']
    (q,) = _rowwise("q_rope_fwd", lambda a, b, cb, sb: _f_rope_q(a, b, cb, sb) * scale,
                    [(qab, 0, hw), (qab, hw, hw), cos, sin], [], [(hw, BF16)])
    (ckv,) = _rowwise("kv_norm_fwd", _f_rms, [(z, zo['ckv'], kvl)], [small['kv_norm_g']], [(kvl, BF16)])
    knv = _mm("kv_up_fwd", ckv, bf(ops['wkv']), out_dtype=BF16)
    (k,) = _rowwise("k_rope_fwd", _f_rope_k,
                    [(knv, 0, hw), (z, zo['kra'], HEAD_PAD), (z, zo['krb'], HEAD_PAD), cos, sin], [], [(hw, BF16)])
    o, lse = _flash_fwd(q, k, knv, heads=heads, t=dims['tq'])
    yb = _mm("mla_out_fwd", o, bf(ops['wbo']), out_dtype=BF16)
    sv.update(cq=cq, q=q, ckv=ckv, knv=knv, k=k, o=o, lse=lse, yb=yb)
    mix = [(z, zo['ga'], d), (z, zo['gb'], d), (ya, 0, d), (yb, 0, d)]
    t1, merged = _mm("w_out_fwd", mix, bf(ops['w_out']), a_fn=_f_merge, keep_a=BF16)
    (x1,) = _rowwise("res1_fwd", _f_res, [x, t1], [g1], [(d, F32)])
    sv.update(merged=merged, t1=t1, x1=x1)
    (h2,) = _rowwise("norm2_fwd", _f_norm_mod, [x1], [small['norm2_g'], sc2, sh2], [(d, BF16)])
    ab = _mm("ffn_up_fwd", h2, bf(ops['wgu']), out_dtype=BF16)
    ffn_act = [(ab, 0, dff), (ab, dff, dff)]
    t2, f = _mm("ffn_down_fwd", ffn_act, bf(ops['w_down']), a_fn=_f_swiglu, keep_a=BF16, tm=256, tk=dff)
    (x2,) = _rowwise("res2_fwd", _f_res, [x1, t2], [g2], [(d, F32)])
    sv.update(h2=h2, ab=ab, f=f, t2=t2)
    return x2, sv


def _layer_bwd(dx2, sv, mod, ops, s5, small, rope, dims):
    d, sw, ql, kvl, heads, dff = dims['d'], dims['sw'], dims['ql'], dims['kvl'], dims['heads'], dims['dff']
    zo = dims['zoff']
    hw = heads * HEAD_PAD
    cos, sin = rope
    sh1, sc1, g1, sh2, sc2, g2 = mod
    a_re, a_im, bmat, cmat = s5
    z = sv['z']
    tr = lambda a: a.T.astype(BF16)
    gops, gsm = {}, {}
    dt2, dg2 = _res_bwd("res2_bwd", sv['t2'], g2, dx2)
    gops['w_down'] = _mm("ffn_down_dw", sv['f'], dt2, ta=True)
    df = _mm("ffn_down_dx", dt2, tr(ops['w_down']), out_dtype=BF16)
    ab = sv['ab']
    swiglu_vjp = _vjp_fn(_f_swiglu, 2, 1, (0, 1))
    (dab,) = _rowwise("swiglu_bwd", lambda a, b, c: jnp.concatenate(swiglu_vjp(a, b, c), axis=1),
                      [(ab, 0, dff), (ab, dff, dff), df], [], [(2 * dff, BF16)], tm=256)
    gops['wgu'] = _mm("ffn_up_dw", sv['h2'], dab, ta=True)
    dh2 = _mm("ffn_up_dx", dab, tr(ops['wgu']), out_dtype=BF16)
    dx1, dn2, dsc2, dsh2 = _norm_mod_bwd("norm2_bwd", sv['x1'], small['norm2_g'], sc2, sh2, dh2, dx2)
    gsm['norm2_g'] = dn2
    dt1, dg1 = _res_bwd("res1_bwd", sv['t1'], g1, dx1)
    gops['w_out'] = _mm("w_out_dw", sv['merged'], dt1, ta=True)
    dmerged = _mm("w_out_dx", dt1, tr(ops['w_out']), out_dtype=BF16)
    dga, dgb, dya, dyb = _rowwise(
        "merge_bwd", _vjp_fn(_f_merge, 4, 1, (0, 1, 2, 3)),
        [(z, zo['ga'], d), (z, zo['gb'], d), sv['ya'], sv['yb'], dmerged], [],
        [(d, BF16), (d, BF16), (d, BF16), (d, BF16)])
    gops['wbo'] = _mm("mla_out_dw", sv['o'], dyb, ta=True)
    do = _mm("mla_out_dx", dyb, tr(ops['wbo']), out_dtype=BF16)
    delta = _flash_delta(do, sv['o'], heads=heads, t=dims['tq'])
    as_rows = lambda a: a.reshape(heads, 1, -1)
    dq, dk, dv = _flash_bwd(sv['q'], sv['k'], sv['knv'], do, as_rows(sv['lse']), as_rows(delta),
                            heads=heads, t=dims['tq'])
    def k_bwd(dkb, cosb, sinb):
        dkb = _f32(dkb)
        dkpe = dkb[:, 0:HEAD_PAD]
        for h in range(1, heads):
            dkpe = dkpe + dkb[:, h * HEAD_PAD:(h + 1) * HEAD_PAD]
        return dkpe * cosb, dkpe * sinb
    dkra, dkrb = _rowwise("k_rope_bwd", k_bwd, [dk, cos, sin], [], [(HEAD_PAD, BF16), (HEAD_PAD, BF16)])
    dknv = jnp.concatenate([dk, dv], axis=1)
    gops['wkv'] = _mm("kv_up_dw", sv['ckv'], dknv, ta=True)
    dckv = _mm("kv_up_dx", dknv, tr(ops['wkv']))
    dckv_in, dkvg = _rms_bwd("kv_norm_bwd", z, zo['ckv'], kvl, small['kv_norm_g'], dckv)
    gsm['kv_norm_g'] = dkvg
    def q_bwd(dqb, cosb, sinb):
        dqb = _f32(dqb) * dims['scale']
        return jnp.concatenate([dqb * jnp.tile(cosb, (1, heads)), dqb * jnp.tile(sinb, (1, heads))], axis=1)
    (dqab,) = _rowwise("q_rope_bwd", q_bwd, [dq, cos, sin], [], [(2 * hw, BF16)])
    gops['wq'] = _mm("q_up_dw", sv['cq'], dqab, ta=True)
    dcq = _mm("q_up_dx", dqab, tr(ops['wq']))
    dcq_in, dqg = _rms_bwd("q_norm_bwd", z, zo['cq'], ql, small['q_norm_g'], dcq)
    gsm['q_norm_g'] = dqg
    gops['w_a_out'] = _mm("s5_out_dw", sv['s5o'], dya, ta=True)
    ds5o = _mm("s5_out_dx", dya, tr(ops['w_a_out']))

    def glu_bwd(yg, pre, ds, b):
        _, vjp = jax.vjp(_f_glu, _f32(yg), _f32(pre), b)
        dyg, dpre, db = vjp(_f32(ds))
        return dyg, dpre, db
    dyg_a, dpre, dbglu = _rowwise("s5_glu_bwd", glu_bwd, [sv['yg'], sv['pre'], ds5o], [small['b_glu']],
                                  [(sw, F32), (sw, BF16)], [sw])
    gsm['b_glu'] = dbglu
    gops['w_glu'] = _mm("s5_glu_mm_dw", sv['yg'], dpre, ta=True)
    dyg_b = _mm("s5_glu_mm_dx", dpre, tr(ops['w_glu']))

    def gelu_bwd(ych, u, dya_, dyb_, dvec):
        _, vjp = jax.vjp(_f_gelu_in, _f32(ych), _f32(u), dvec)
        dych, du, dd = vjp(_f32(dya_) + _f32(dyb_))
        return dych, du, dd
    dy, du_skip, dssm_d = _rowwise("s5_gelu_bwd", gelu_bwd, [sv['ych'], (z, zo['u'], sw), dyg_a, dyg_b],
                                   [small['ssm_d']], [(sw, BF16), (sw, F32)], [sw])
    gsm['ssm_d'] = dssm_d
    dy_seg = _to_segments(dy)
    nsup, sc = dims['nsup'], dims['s5_chunk']
    tr_blocks = lambda a: a.reshape(nsup, -1, a.shape[1]).transpose(0, 2, 1).reshape(-1, a.shape[0] // nsup)
    s5kw = dict(nsup=nsup, sc=sc)
    c_t, b_t = tr_blocks(cmat).astype(BF16), tr_blocks(bmat).astype(BF16)
    ends = _s5_pass("s5_ends_bwd", dy_seg, a_re, a_im, c_t, reverse=True, **s5kw)
    du_scan, g_bmat, g_cmat, dar, dai = _s5_pass("s5_scan_bwd", dy_seg, a_re, a_im, c_t, reverse=True, ends=ends,
                                                 w_out=b_t, u=sv['u_seg'], h=sv['hst'], **s5kw)
    du_scan = _from_segments(du_scan)
    (du,) = _rowwise("s5_du_sum", lambda a, b: _f32(a) + _f32(b), [du_skip, du_scan], [], [(sw, BF16)])
    gs5 = (dar, dai, g_bmat, g_cmat)
    dz = jnp.concatenate([dga, dgb, du, dcq_in, dckv_in, dkra, dkrb], axis=1)
    gops['w_in'] = _mm("w_in_dw", sv['h1'], dz, ta=True)
    dh1 = _mm("w_in_dx", dz, tr(ops['w_in']), out_dtype=BF16)
    dx, dn1, dsc1, dsh1 = _norm_mod_bwd("norm1_bwd", sv['x'], small['norm1_g'], sc1, sh1, dh1, dx1)
    gsm['norm1_g'] = dn1
    dmod = (dsh1, dsc1, dg1, dsh2, dsc2, dg2)
    return dx, gops, gs5, gsm, dmod


def _res_bwd(name, t, g, dxo):
    def fn(tb, db, gb):
        db = _f32(db)
        return gb * db, jnp.sum(db * _f32(tb), axis=0, keepdims=True)
    return _rowwise(name, fn, [t, dxo], [g], [(t.shape[1], BF16)], [t.shape[1]])


def _norm_mod_bwd(name, x, g, sc, sh, dh, dres):
    def fn(xb, dhb, dresb, gb, scb, shb):
        _, vjp = jax.vjp(_f_norm_mod, _f32(xb), gb, scb, shb)
        dx, dg, dsc, dsh = vjp(_f32(dhb))
        return dx + _f32(dresb), dg, dsc, dsh
    w = x.shape[1]
    return _rowwise(name, fn, [x, dh, dres], [g, sc, sh], [(w, F32)], [w, w, w])


def _rms_bwd(name, z, off, w, g, dy):
    def fn(xb, dyb, gb):
        _, vjp = jax.vjp(_f_rms, _f32(xb), gb)
        dx, dg = vjp(_f32(dyb))
        return dx, dg
    return _rowwise(name, fn, [(z, off, w), dy], [g], [(w, BF16)], [w])


S5_NAMES = ('ssm_a_re', 'ssm_a_im', 'ssm_log_dt', 'ssm_b_re', 'ssm_b_im', 'ssm_c_re', 'ssm_c_im')
LAYER_VECS = ('norm1_g', 'ssm_d', 'b_glu', 'q_norm_g', 'kv_norm_g', 'norm2_g')


def _step(p, mom_m, mom_v, x, c, positions, loss_target):
    depth, d = p['norm1_g'].shape
    L = x.shape[1]
    sw = p['ssm_d'].shape[1]
    ql, kvl = p['q_norm_g'].shape[1], p['kv_norm_g'].shape[1]
    heads = p['w_uk'].shape[2] * N_DEV // QK_NOPE
    dff = p['w_gate'].shape[2] * N_DEV
    ada_w = p['w_ada'].shape[2]
    zoff, o = {}, 0
    for nm, sz in (('ga', d), ('gb', d), ('u', sw), ('cq', ql), ('ckv', kvl), ('kra', HEAD_PAD), ('krb', HEAD_PAD)):
        assert o % sz == 0, (nm, o, sz)
        zoff[nm] = o
        o += sz
    groups, states = p['ssm_a_re'].shape[1:]
    assert groups % S5_SUPER == 0 and p['ssm_b_re'].shape[3] * S5_SUPER == LANES
    dims = dict(d=d, sw=sw, ql=ql, kvl=kvl, heads=heads, dff=dff, zoff=zoff,
                nsup=groups // S5_SUPER, s5_chunk=S5_SUPER * states,
                tq=1024 if L >= 4096 else 128, scale=(QK_NOPE + QK_ROPE) ** -0.5)
    x = x.reshape(L, d)
    tgt = loss_target.reshape(L, d)

    posf = positions.reshape(L).astype(F32)
    inv_freq = ROPE_BASE ** (-jnp.arange(0, QK_ROPE, 2, dtype=F32) / QK_ROPE)
    ang = posf[:, None] * inv_freq
    cs, sn = jnp.cos(ang), jnp.sin(ang)
    padr = HEAD_PAD - QK_NOPE - QK_ROPE
    cos = jnp.concatenate([jnp.ones((L, QK_NOPE), F32), cs, cs, jnp.zeros((L, padr), F32)], axis=1)
    sin = jnp.concatenate([jnp.zeros((L, QK_NOPE), F32), sn, sn, jnp.zeros((L, padr), F32)], axis=1)
    rope = (cos, sin)

    gathered = _gather_two_level("gather_weights", [p[n].astype(BF16) for n in SHARDED])

    def make_ops(gl):
        return jax.vmap(lambda w: _layer_operators(w, dims))(_gathered_to_full(gl))

    ops_all, ops_vjp = jax.vjp(make_ops, [g.astype(F32) for g in gathered])
    ops = [{k: v[l] for k, v in ops_all.items()} for l in range(depth)]

    def make_s5(sp):
        return jax.vmap(_s5_operators)(*[sp[n] for n in S5_NAMES])

    s5_all, s5_vjp = jax.vjp(make_s5, {n: p[n] for n in S5_NAMES})
    s5ops = [tuple(a[l] for a in s5_all) for l in range(depth)]

    (c_slabs,) = _exchange("gather_c", [jnp.pad(c, ((0, 7), (0, 0)))], all_to_all=False)
    c_all = c_slabs[:, 0, :]
    (c_act,) = _rowwise("c_silu", lambda a: jax.nn.silu(a), [jnp.pad(c_all, ((0, 8), (0, 0)))], [], [(d, F32)])
    w_ada_cat = p['w_ada'].transpose(1, 0, 2).reshape(d, depth * ada_w)
    mod_cols = _mm("ada_fwd", c_act, w_ada_cat)[:N_DEV]
    (mod_rows,) = _exchange("a2a_mod", [_pack_slabs(mod_cols)], all_to_all=True)
    mod_mine = _unpack_slabs(mod_rows, depth * ada_w).reshape(N_DEV, depth, ada_w)
    mod_mine = mod_mine.transpose(1, 0, 2).reshape(depth, N_DEV * ada_w)
    (mod_full,) = _rowwise("ada_bias", lambda a, b: a + b, [mod_mine, p['b_ada']], [], [(6 * d, F32)])
    mods = [tuple(mod_full[l:l + 1, i * d:(i + 1) * d] for i in range(6)) for l in range(depth)]

    saved = []
    xl = x
    for l in range(depth):
        small = {n: p[n][l:l + 1] for n in LAYER_VECS}
        xl, sv = _layer_fwd(xl, mods[l], ops[l], s5ops[l], small, rope, dims)
        saved.append((sv, small))

    def final_fn(xb, tb, gb):
        def lossf(xv, gv):
            e = _f_rms(xv, gv) - tb
            per_row = 0.5 * jnp.mean(e * e, axis=-1, keepdims=True)
            return jnp.sum(per_row, axis=0, keepdims=True)
        lv, vjp = jax.vjp(lossf, xb, gb)
        dxb, dgb = vjp(jnp.ones((1, 1), F32))
        return dxb, jnp.broadcast_to(lv, (1, LANES)), dgb
    dx, loss_vec, dfinal_g = _rowwise("final_loss", final_fn, [xl, tgt], [p['final_g'].reshape(1, d)],
                                      [(d, F32)], [LANES, d])

    g_ops, g_s5, g_small, dmods = [None] * depth, [None] * depth, [None] * depth, [None] * depth
    for l in reversed(range(depth)):
        sv, small = saved[l]
        dx, g_ops[l], g_s5[l], g_small[l], dmods[l] = _layer_bwd(dx, sv, mods[l], ops[l], s5ops[l], small,
                                                                rope, dims)
    grad_x = dx.reshape(1, L, d)

    dmod_mine = jnp.stack([jnp.concatenate(dm, axis=1)[0] for dm in dmods])
    dmod_slabs = dmod_mine.reshape(depth, N_DEV, ada_w).transpose(1, 0, 2).reshape(N_DEV, depth * ada_w)
    (dmod_recv,) = _exchange("a2a_dmod", [_pack_slabs(dmod_slabs)], all_to_all=True)
    dmod_cols = _unpack_slabs(dmod_recv, depth * ada_w)
    g_ada = _mm("ada_dw", c_act, jnp.pad(dmod_cols, ((0, 8), (0, 0))), ta=True)
    grads, delta, new_m, new_v = {}, {}, {}, {}
    grads['w_ada'] = g_ada.reshape(d, depth, ada_w).transpose(1, 0, 2)
    two_d = lambda a: a.reshape(-1, a.shape[-1])
    res = _rowwise("adamw_w_ada", _adamw_fn, [two_d(a) for a in (p['w_ada'], grads['w_ada'], mom_m['w_ada'],
                                                                  mom_v['w_ada'])], [], [(ada_w, F32)] * 3)
    delta['w_ada'], new_m['w_ada'], new_v['w_ada'] = [r.reshape(p['w_ada'].shape) for r in res]

    (g_slabs,) = ops_vjp({k: jnp.stack([g[k] for g in g_ops]) for k in ops_all})
    g_recv = _exchange("a2a_grads", [g.astype(BF16) for g in g_slabs], all_to_all=True)
    for n, rv in zip(SHARDED, g_recv):
        shp = p[n].shape
        res = _adamw_sharded("adamw_" + n, two_d(p[n]), rv.reshape(N_DEV, -1, shp[-1]), two_d(mom_m[n]),
                             two_d(mom_v[n]))
        grads[n], delta[n], new_m[n], new_v[n] = [r.reshape(shp) for r in res]

    (g_s5p,) = s5_vjp(tuple(jnp.stack([g[i] for g in g_s5]) for i in range(len(s5_all))))
    part = dict(g_s5p)
    part['b_ada'] = dmod_mine
    for n in LAYER_VECS:
        part[n] = jnp.concatenate([g_small[l][n] for l in range(depth)], axis=0)
    part['final_g'] = dfinal_g.reshape(d)
    small_shapes = [p[n].shape for n in SMALL] + [(1,)]
    (small_recv,) = _exchange("gather_small", [_pack_rows([part[n] for n in SMALL] + [loss_vec[0, 0:1]])],
                              all_to_all=False)
    small_sum = _sum_slabs("sum_small", small_recv)
    small_list = _unpack_rows(small_sum, small_shapes)
    grads.update(zip(SMALL, small_list[:-1]))
    loss = small_list[-1].reshape(())
    dummy = [jnp.zeros((1,), F32)]
    res = _rowwise("adamw_small", _adamw_fn,
                   [_pack_rows([src[n] for n in SMALL] + dummy) for src in (p, )] + [small_sum] +
                   [_pack_rows([src[n] for n in SMALL] + dummy) for src in (mom_m, mom_v)], [], [(FLAT_W, F32)] * 3)
    for dst, r in zip((delta, new_m, new_v), res):
        dst.update(zip(SMALL, _unpack_rows(r, small_shapes)[:-1]))
    return (loss, grad_x, *[grads[n] for n in WEIGHTS], *[delta[n] for n in WEIGHTS],
            *[new_m[n] for n in WEIGHTS], *[new_v[n] for n in WEIGHTS])


def kernel(x, c, positions, w_ada, b_ada, norm1_g, w_in, ssm_a_re, ssm_a_im, ssm_log_dt, ssm_b_re, ssm_b_im, ssm_c_re, ssm_c_im, ssm_d, w_glu, b_glu, w_a_out, q_norm_g, w_uq, kv_norm_g, w_uk, w_uv, w_b_out, w_out, norm2_g, w_gate, w_up, w_down, final_g, loss_target, m_w_ada, m_b_ada, m_norm1_g, m_w_in, m_ssm_a_re, m_ssm_a_im, m_ssm_log_dt, m_ssm_b_re, m_ssm_b_im, m_ssm_c_re, m_ssm_c_im, m_ssm_d, m_w_glu, m_b_glu, m_w_a_out, m_q_norm_g, m_w_uq, m_kv_norm_g, m_w_uk, m_w_uv, m_w_b_out, m_w_out, m_norm2_g, m_w_gate, m_w_up, m_w_down, m_final_g, v_w_ada, v_b_ada, v_norm1_g, v_w_in, v_ssm_a_re, v_ssm_a_im, v_ssm_log_dt, v_ssm_b_re, v_ssm_b_im, v_ssm_c_re, v_ssm_c_im, v_ssm_d, v_w_glu, v_b_glu, v_w_a_out, v_q_norm_g, v_w_uq, v_kv_norm_g, v_w_uk, v_w_uv, v_w_b_out, v_w_out, v_norm2_g, v_w_gate, v_w_up, v_w_down, v_final_g):
    given = dict(locals())
    p = {n: given[n] for n in WEIGHTS}
    mom_m = {n: given["m_" + n] for n in WEIGHTS}
    mom_v = {n: given["v_" + n] for n in WEIGHTS}
    return _step(p, mom_m, mom_v, x, c, positions, loss_target)
```
